```python
import math
import jax, jax.numpy as jnp
from jax import lax
import numpy as np

D_MODEL = 1024
BATCH = 4
SEQ = 8192
DEPTH = 1
DEC_BATCH = 16
DEC_SEQ = 16
PAST_LEN = 2048

CHUNK = 64
RET_HEADS = 8
RET_DK = 64
RET_DV = 64
ATT_HEADS = 8
ATT_KV_HEADS = 2
ATT_HD = 64
IDX_HEADS = 8
IDX_DIM = 64
TOPK_MAX = 256
N_BUCKETS = 32
MAX_DISTANCE = 128
D_FF = 4 * D_MODEL
Q_BLOCK = 128
ROPE_BASE = 10000.0
EPS = 1e-6

RET_W = RET_HEADS * RET_DV
ATT_W = ATT_HEADS * ATT_HD
MIX_W = RET_W + ATT_W
SPLIT_SIZES = [RET_HEADS * RET_DK, RET_HEADS * RET_DK, RET_W, RET_W,
               ATT_W, ATT_KV_HEADS * ATT_HD, ATT_KV_HEADS * ATT_HD,
               IDX_HEADS * IDX_DIM, IDX_DIM, IDX_HEADS]
D_IN = sum(SPLIT_SIZES)
SPLIT_OFFSETS = [int(o) for o in np.cumsum(SPLIT_SIZES)[:-1]]

kernel_name = "hybrid_retention_dsa_stream_step"


def rms_norm(x, gain):
    xf = x.astype(jnp.float32)
    y = xf * lax.rsqrt(jnp.mean(xf * xf, axis=-1, keepdims=True) + EPS)
    return (y * gain.astype(jnp.float32)).astype(x.dtype)


def head_norm(x):
    xf = x.astype(jnp.float32)
    return xf * lax.rsqrt(jnp.mean(xf * xf, axis=-1, keepdims=True) + EPS)


def rotary(x, pos):
    half = x.shape[-1] // 2
    inv = ROPE_BASE ** (-jnp.arange(half, dtype=jnp.float32) / half)
    ang = pos.astype(jnp.float32)[:, None] * inv[None, :]
    cos = jnp.cos(ang)[:, None, :]
    sin = jnp.sin(ang)[:, None, :]
    x1, x2 = x[..., :half], x[..., half:]
    return jnp.concatenate([x1 * cos - x2 * sin, x2 * cos + x1 * sin], axis=-1).astype(x.dtype)


def retention_log_decay():
    h = jnp.arange(RET_HEADS, dtype=jnp.float32)
    return jnp.log(1.0 - jnp.exp2(-5.0 - h))


def retention_chunks(q, k, v, s0):
    C = q.shape[2]
    log_g = retention_log_decay()
    pos = jnp.arange(C, dtype=jnp.float32)
    dist = jnp.abs(pos[:, None] - pos[None, :])
    d_intra = jnp.exp(log_g[:, None, None] * dist)
    scores = jnp.einsum('bnihd,bnjhd->bnhij', q, k) * d_intra
    o = jnp.einsum('bnhij,bnjhe->bnihe', scores, v)
    k_dec = jnp.exp(log_g[None, :] * (C - 1 - pos)[:, None])
    u = jnp.einsum('bnjhd,bnjhe->nbhde', k * k_dec[..., None], v).astype(jnp.float32)
    g_chunk = jnp.exp(log_g * C)[None, :, None, None]

    def step(r, u_n):
        return g_chunk * r + u_n, r

    s_final, r_before = lax.scan(step, s0.astype(jnp.float32), u)
    q_dec = jnp.exp(log_g[None, :] * (pos + 1.0)[:, None])
    o = o + jnp.einsum('bnihd,nbhde->bnihe', q * q_dec[..., None], r_before)
    return o, s_final


def retention_mixer(q, k, v, g, pos, s0, chunk):
    B, T = q.shape[0], q.shape[1]
    nc = T // chunk
    q = rotary(q, pos)
    k = rotary(k, pos) * (RET_DK ** -0.5)
    o, s_new = retention_chunks(q.reshape(B, nc, chunk, RET_HEADS, RET_DK),
                                k.reshape(B, nc, chunk, RET_HEADS, RET_DK),
                                v.reshape(B, nc, chunk, RET_HEADS, RET_DV), s0)
    o = head_norm(o.reshape(B, T, RET_HEADS, RET_DV))
    o = (jax.nn.silu(g.astype(jnp.float32)) * o).astype(v.dtype)
    return o.reshape(B, T, RET_W), s_new


def t5_bucket(rel):
    half = N_BUCKETS // 2
    max_exact = half // 2
    ret = jnp.where(rel > 0, half, 0)
    n = jnp.abs(rel)
    nf = jnp.maximum(n, 1).astype(jnp.float32)
    large = max_exact + (jnp.log(nf / max_exact) / math.log(MAX_DISTANCE / max_exact)
                         * (half - max_exact)).astype(jnp.int32)
    large = jnp.minimum(large, half - 1)
    return ret + jnp.where(n < max_exact, n, large)


def sparse_attend(q, qi, w, q_pos, k, v, ki, rel_bias, n_sel):
    B, Tq, H, HD = q.shape
    L, KV = k.shape[1], k.shape[2]
    R = H // KV
    k_pos = jnp.arange(L, dtype=jnp.int32)
    idx = jax.nn.relu(jnp.einsum('bqhd,bsd->bqhs', qi, ki).astype(jnp.float32) * (IDX_DIM ** -0.5))
    score = jnp.einsum('bqhs,bqh->bqs', idx, w.astype(jnp.float32) * (IDX_HEADS ** -0.5))
    admissible = (k_pos[None, :] // CHUNK) <= (q_pos[:, None] // CHUNK)
    score = jnp.where(admissible[None], score, -jnp.inf)
    sel_score, sel = lax.top_k(score, n_sel)
    valid = jnp.isfinite(sel_score)
    gather = jax.vmap(lambda a, i: a[i])
    k_sel = gather(k, sel)
    v_sel = gather(v, sel)
    qg = q.reshape(B, Tq, KV, R, HD)
    logits = jnp.einsum('bqgrd,bqkgd->bqgrk', qg, k_sel).astype(jnp.float32) * (HD ** -0.5)
    rel = sel - q_pos[None, :, None]
    bias = rel_bias.astype(jnp.float32)[t5_bucket(rel)]
    bias = bias.reshape(B, Tq, n_sel, KV, R).transpose(0, 1, 3, 4, 2)
    logits = jnp.where(valid[:, :, None, None, :], logits + bias, -jnp.inf)
    p = jax.nn.softmax(logits, axis=-1)
    out = jnp.einsum('bqgrk,bqkgd->bqgrd', p.astype(v.dtype), v_sel)
    return out.reshape(B, Tq, H * HD)


def hybrid_layer(x, ret_s0, past_k, past_v, past_ki, w_in, w_out, w_up, w_down,
                 g_pre_mix, g_post_mix, g_pre_ffn, g_post_ffn, rel_bias):
    B, T, _ = x.shape
    P = past_k.shape[1]
    pos = P + jnp.arange(T, dtype=jnp.int32)
    a = rms_norm(x, g_pre_mix)
    proj = a @ w_in
    rq, rk, rv, rg, aq, ak, av, iq, ik, iw = jnp.split(proj, SPLIT_OFFSETS, axis=-1)
    chunk = min(CHUNK, T)
    o_ret, s_new = retention_mixer(rq.reshape(B, T, RET_HEADS, RET_DK), rk.reshape(B, T, RET_HEADS, RET_DK),
                                   rv.reshape(B, T, RET_HEADS, RET_DV), rg.reshape(B, T, RET_HEADS, RET_DV),
                                   pos, ret_s0, chunk)
    ak = ak.reshape(B, T, ATT_KV_HEADS, ATT_HD)
    av = av.reshape(B, T, ATT_KV_HEADS, ATT_HD)
    k_all = jnp.concatenate([past_k.astype(ak.dtype), ak], axis=1)
    v_all = jnp.concatenate([past_v.astype(av.dtype), av], axis=1)
    ki_all = jnp.concatenate([past_ki.astype(ik.dtype), ik], axis=1)
    L = k_all.shape[1]
    n_sel = min(TOPK_MAX, L // 4)
    aq = aq.reshape(B, T, ATT_HEADS, ATT_HD)
    iq = iq.reshape(B, T, IDX_HEADS, IDX_DIM)
    if T > Q_BLOCK and T % Q_BLOCK == 0:
        nb = T // Q_BLOCK
        blk = lambda t: jnp.moveaxis(t.reshape((B, nb, Q_BLOCK) + t.shape[2:]), 1, 0)
        pos_b = pos.reshape(nb, Q_BLOCK)
        o_att = lax.map(lambda args: sparse_attend(args[0], args[1], args[2], args[3],
                                                   k_all, v_all, ki_all, rel_bias, n_sel),
                        (blk(aq), blk(iq), blk(iw), pos_b))
        o_att = jnp.moveaxis(o_att, 0, 1).reshape(B, T, ATT_W)
    else:
        o_att = sparse_attend(aq, iq, iw, pos, k_all, v_all, ki_all, rel_bias, n_sel)
    mix = jnp.concatenate([o_ret.astype(x.dtype), o_att.astype(x.dtype)], axis=-1) @ w_out
    h = x + rms_norm(mix, g_post_mix)
    f = rms_norm(h, g_pre_ffn) @ w_up
    f = jnp.square(jax.nn.relu(f)) @ w_down
    y = h + rms_norm(f, g_post_ffn)
    return y, s_new, ak, av, ik


def setup_inputs(seed: int = 0) -> dict:
    key = jax.random.key(seed)
    ks = jax.random.split(key, 16)
    nrm = lambda k, shape, s: jax.random.normal(k, shape, jnp.float32) * s
    return {
        "x_prompt": nrm(ks[0], (BATCH, SEQ, D_MODEL), 1.0),
        "x_sample": nrm(ks[1], (DEC_BATCH, DEC_SEQ, D_MODEL), 1.0),
        "state_ret": nrm(ks[2], (DEPTH, DEC_BATCH, RET_HEADS, RET_DK, RET_DV), 1.0),
        "cache_k": nrm(ks[3], (DEPTH, DEC_BATCH, PAST_LEN, ATT_KV_HEADS, ATT_HD), 1.0),
        "cache_v": nrm(ks[4], (DEPTH, DEC_BATCH, PAST_LEN, ATT_KV_HEADS, ATT_HD), 1.0),
        "cache_kidx": nrm(ks[5], (DEPTH, DEC_BATCH, PAST_LEN, IDX_DIM), 1.0),
        "w_in": nrm(ks[6], (DEPTH, D_MODEL, D_IN), D_MODEL ** -0.5),
        "w_out": nrm(ks[7], (DEPTH, MIX_W, D_MODEL), MIX_W ** -0.5),
        "w_up": nrm(ks[8], (DEPTH, D_MODEL, D_FF), D_MODEL ** -0.5),
        "w_down": nrm(ks[9], (DEPTH, D_FF, D_MODEL), D_FF ** -0.5),
        "g_pre_mix": 1.0 + nrm(ks[10], (DEPTH, D_MODEL), 0.05),
        "g_post_mix": 1.0 + nrm(ks[11], (DEPTH, D_MODEL), 0.05),
        "g_pre_ffn": 1.0 + nrm(ks[12], (DEPTH, D_MODEL), 0.05),
        "g_post_ffn": 1.0 + nrm(ks[13], (DEPTH, D_MODEL), 0.05),
        "rel_bias": nrm(ks[14], (N_BUCKETS, ATT_HEADS), 0.5),
    }


def reference(x_prompt, x_sample, state_ret, cache_k, cache_v, cache_kidx,
              w_in, w_out, w_up, w_down, g_pre_mix, g_post_mix, g_pre_ffn, g_post_ffn, rel_bias):
    Bp = x_prompt.shape[0]
    empty_kv = jnp.zeros((Bp, 0, ATT_KV_HEADS, ATT_HD), x_prompt.dtype)
    empty_ki = jnp.zeros((Bp, 0, IDX_DIM), x_prompt.dtype)
    zero_state = jnp.zeros((Bp, RET_HEADS, RET_DK, RET_DV), jnp.float32)
    yp, ys = x_prompt, x_sample
    sp_l, kp_l, vp_l, ip_l = [], [], [], []
    ss_l, ks_l, vs_l, is_l = [], [], [], []
    for l in range(DEPTH):
        lw = (w_in[l], w_out[l], w_up[l], w_down[l], g_pre_mix[l], g_post_mix[l],
              g_pre_ffn[l], g_post_ffn[l], rel_bias)
        yp, sp, kp, vp, ip = hybrid_layer(yp, zero_state, empty_kv, empty_kv, empty_ki, *lw)
        ys, ss, k_s, v_s, i_s = hybrid_layer(ys, state_ret[l], cache_k[l], cache_v[l], cache_kidx[l], *lw)
        sp_l.append(sp); kp_l.append(kp); vp_l.append(vp); ip_l.append(ip)
        ss_l.append(ss); ks_l.append(k_s); vs_l.append(v_s); is_l.append(i_s)
    return (yp, ys,
            jnp.stack(sp_l), jnp.stack(kp_l), jnp.stack(vp_l), jnp.stack(ip_l),
            jnp.stack(ss_l), jnp.stack(ks_l), jnp.stack(vs_l), jnp.stack(is_l))
```

```python
import functools
import math

import jax
import jax.numpy as jnp
import numpy as np
from jax import lax
from jax.experimental import pallas as pl
from jax.experimental.pallas import tpu as pltpu

D_MODEL = 1024
CHUNK = 64
RET_HEADS = 8
RET_DK = 64
RET_DV = 64
ATT_HEADS = 8
ATT_KV_HEADS = 2
ATT_HD = 64
IDX_HEADS = 8
IDX_DIM = 64
TOPK_MAX = 256
N_BUCKETS = 32
MAX_DISTANCE = 128
D_FF = 4 * D_MODEL
ROPE_BASE = 10000.0
EPS = 1e-6

RET_W = RET_HEADS * RET_DV
ATT_W = ATT_HEADS * ATT_HD
KV_W = ATT_KV_HEADS * ATT_HD
IDX_W = IDX_HEADS * IDX_DIM
HEADS_PER_KV = ATT_HEADS // ATT_KV_HEADS

MAIN_W = 4 * RET_W + ATT_W + IDX_W
TAIL_W = 3 * 128
IW_OFF = IDX_DIM

LANE = 128
VT_ROWS = 80
VT_ALL = ATT_KV_HEADS * VT_ROWS
TK = 256
NEG = -1e30
VMEM_LIMIT = 56 * 1024 * 1024

F32 = jnp.float32
BF16 = jnp.bfloat16
NT_DIMS = (((1,), (1,)), ((), ()))
TN_DIMS = (((0,), (0,)), ((), ()))


def _const_spec(shape):
    nd = len(shape)
    return pl.BlockSpec(shape, lambda *_: (0,) * nd, pipeline_mode=pl.Buffered(1))


def _rms(x, gain):
    return x * lax.rsqrt(jnp.mean(x * x, axis=-1, keepdims=True) + EPS) * gain


def _inproj_kernel(x_ref, g_ref, w_ref, main_ref, kc_ref, vc_ref, ic_ref, iw_ref,
                   *tile_refs, tm):
    a = _rms(x_ref[...], g_ref[...]).astype(BF16)
    for c in range(MAIN_W // 512):
        main_ref[:, c * 512:(c + 1) * 512] = jnp.dot(
            a, w_ref[:, c * 512:(c + 1) * 512], preferred_element_type=F32)
    tail = jnp.dot(a, w_ref[:, MAIN_W:MAIN_W + TAIL_W], preferred_element_type=F32)
    ak = tail[:, 0:KV_W]
    av = tail[:, KV_W:2 * KV_W]
    last = tail[:, 2 * KV_W:3 * KV_W]
    kc_ref[...] = ak
    vc_ref[...] = av
    ic_ref[...] = last[:, :IDX_DIM]
    iw_ref[...] = last
    if not tile_refs:
        return
    k3_ref, ki3_ref, vt3_ref = tile_refs
    avt = av.T
    row = lax.broadcasted_iota(jnp.int32, (VT_ROWS - ATT_HD, tm), 0)
    ones_rows = jnp.where(row == 0, 1.0, 0.0).astype(F32)
    vt = jnp.concatenate([avt[:ATT_HD], ones_rows, avt[ATT_HD:], ones_rows], axis=0).astype(BF16)
    for j in range(tm // TK):
        k3_ref[j] = ak[j * TK:(j + 1) * TK].astype(BF16)
        ki3_ref[j] = last[j * TK:(j + 1) * TK, :IDX_DIM].astype(BF16)
        vt3_ref[j] = vt[:, j * TK:(j + 1) * TK]


def _inproj(x2, gain, w_perm, tm, key_tiles):
    rows = x2.shape[0]
    grid = (rows // tm,)
    row_spec = lambda w: pl.BlockSpec((tm, w), lambda i: (i, 0))
    out_shape = (
        jax.ShapeDtypeStruct((rows, MAIN_W), F32),
        jax.ShapeDtypeStruct((rows, KV_W), F32),
        jax.ShapeDtypeStruct((rows, KV_W), F32),
        jax.ShapeDtypeStruct((rows, IDX_DIM), F32),
        jax.ShapeDtypeStruct((rows, LANE), F32),
    )
    out_specs = (row_spec(MAIN_W), row_spec(KV_W), row_spec(KV_W), row_spec(IDX_DIM), row_spec(LANE))
    if key_tiles:
        t3 = lambda a, b: pl.BlockSpec((tm // TK, a, b), lambda i: (i, 0, 0))
        out_shape += (jax.ShapeDtypeStruct((rows // TK, TK, KV_W), BF16),
                      jax.ShapeDtypeStruct((rows // TK, TK, IDX_DIM), BF16),
                      jax.ShapeDtypeStruct((rows // TK, VT_ALL, TK), BF16))
        out_specs += (t3(TK, KV_W), t3(TK, IDX_DIM), t3(VT_ALL, TK))
    return pl.pallas_call(
        functools.partial(_inproj_kernel, tm=tm),
        grid=grid,
        in_specs=[row_spec(D_MODEL), _const_spec((1, D_MODEL)), _const_spec(w_perm.shape)],
        out_specs=out_specs,
        out_shape=out_shape,
        compiler_params=pltpu.CompilerParams(dimension_semantics=("parallel",),
                                             vmem_limit_bytes=VMEM_LIMIT),
        name="inproj",
    )(x2, gain, w_perm)


def _ret_gammas():
    return [1.0 - 2.0 ** (-5.0 - h) for h in range(RET_HEADS)]


def _ret_tables(sb, chunk):
    lg = np.log(np.array(_ret_gammas(), np.float64))
    t = np.arange(sb)
    ci = t // chunk
    diff = t[:, None] - t[None, :]
    same = ci[:, None] == ci[None, :]
    below = ci[None, :] < ci[:, None]
    expo = np.where(same, np.abs(diff), np.where(below, diff, 0)).astype(np.float64)
    dmat = np.exp(lg[:, None, None] * expo[None]) * (same | below)[None]
    qd = np.exp(lg[None, :] * (t + 1.0)[:, None])
    kd = np.exp(lg[None, :] * (sb - 1.0 - t)[:, None])
    qd = np.repeat(qd, RET_DK, axis=1)
    kd = np.repeat(kd, RET_DK, axis=1)
    return (jnp.asarray(dmat, F32), jnp.asarray(qd, F32), jnp.asarray(kd, F32),
            [float(math.exp(v * sb)) for v in lg])


def _ret_kernel(q_ref, k_ref, v_ref, g_ref, cos_ref, sin_ref, d_ref, qd_ref, kd_ref, s0_ref,
                o_ref, sfin_ref, s_scr, *, g_block):
    n = pl.program_id(1)

    @pl.when(n == 0)
    def _():
        s_scr[...] = s0_ref[0]

    reps = RET_W // LANE
    cos = jnp.concatenate([cos_ref[...]] * reps, axis=1)
    sin = jnp.concatenate([sin_ref[...]] * reps, axis=1)
    lane = lax.broadcasted_iota(jnp.int32, cos.shape, 1)
    first_half = (lane & (RET_DK - 1)) < RET_DK // 2

    def rot(x):
        partner = jnp.where(first_half, pltpu.roll(x, RET_W - RET_DK // 2, 1),
                            pltpu.roll(x, RET_DK // 2, 1))
        return x * cos + partner * sin

    q = rot(q_ref[...])
    k = rot(k_ref[...]) * (RET_DK ** -0.5)
    v = v_ref[...].astype(BF16)
    gate = g_ref[...]
    qb = q.astype(BF16)
    kb = k.astype(BF16)
    qx = (q * qd_ref[...]).astype(BF16)
    kx = (k * kd_ref[...]).astype(BF16)
    outs = []
    for h in range(RET_HEADS):
        sl = slice(h * RET_DK, (h + 1) * RET_DK)
        s = lax.dot_general(qb[:, sl], kb[:, sl], NT_DIMS, preferred_element_type=F32)
        p = (s * d_ref[h]).astype(BF16)
        st = s_scr[h]
        o = jnp.dot(p, v[:, sl], preferred_element_type=F32)
        o = o + jnp.dot(qx[:, sl], st.astype(BF16), preferred_element_type=F32)
        s_scr[h] = g_block[h] * st + lax.dot_general(kx[:, sl], v[:, sl], TN_DIMS,
                                                     preferred_element_type=F32)
        o = o * lax.rsqrt(jnp.mean(o * o, axis=-1, keepdims=True) + EPS)
        outs.append(o)
    o_all = jnp.concatenate(outs, axis=1)
    o_ref[...] = gate * (1.0 / (1.0 + jnp.exp(-gate))) * o_all

    @pl.when(n == pl.num_programs(1) - 1)
    def _():
        sfin_ref[0] = s_scr[...]


def _retention(main, s0, cos_t, sin_t, batch, t_len, sb, chunk):
    nsb = t_len // sb
    dmat, qd, kd, g_block = _ret_tables(sb, chunk)
    col = lambda c: pl.BlockSpec((sb, RET_W), lambda b, n, c=c: (b * nsb + n, c))
    tab = pl.BlockSpec((sb, LANE), lambda b, n: (n, 0))
    st_spec = pl.BlockSpec((1, RET_HEADS, RET_DK, RET_DV), lambda b, n: (b, 0, 0, 0))
    return pl.pallas_call(
        functools.partial(_ret_kernel, g_block=g_block),
        grid=(batch, nsb),
        in_specs=[col(0), col(1), col(2), col(3), tab, tab,
                  _const_spec(dmat.shape), _const_spec(qd.shape), _const_spec(kd.shape), st_spec],
        out_specs=(pl.BlockSpec((sb, RET_W), lambda b, n: (b * nsb + n, 0)), st_spec),
        out_shape=(jax.ShapeDtypeStruct((batch * t_len, RET_W), F32),
                   jax.ShapeDtypeStruct((batch, RET_HEADS, RET_DK, RET_DV), F32)),
        scratch_shapes=[pltpu.VMEM((RET_HEADS, RET_DK, RET_DV), F32)],
        compiler_params=pltpu.CompilerParams(dimension_semantics=("parallel", "arbitrary"),
                                             vmem_limit_bytes=VMEM_LIMIT),
        name="retention",
    )(main, main, main, main, cos_t, sin_t, dmat, qd, kd, s0)


def _attn_kernel(aq_ref, iq_ref, iw_ref, ki_ref, k_ref, vt_ref, bt_ref, o_ref,
                 s_scr, acc_scr, m_scr, *, tq, nkt_static, last_valid, n_sel, n_bisect, n_index):
    qblk = pl.program_id(1)
    nkt = qblk + 1 if nkt_static is None else nkt_static
    lane_f = lax.broadcasted_iota(jnp.int32, (1, tq), 1).astype(F32)
    klim = jnp.minimum((jnp.floor(lane_f * (1.0 / CHUNK)) + 1.0) * CHUNK, float(last_valid))
    krow = lax.broadcasted_iota(jnp.int32, (TK, tq), 0).astype(F32)
    adm_last = krow < klim
    nkt_f = nkt.astype(F32) if nkt_static is None else float(nkt)
    n_adm = (nkt_f - 1.0) * TK + klim
    k_target = jnp.minimum(float(n_sel), n_adm)

    iq = iq_ref[...]
    w_t = iw_ref[...].T[IW_OFF:IW_OFF + IDX_HEADS, :] * (IDX_HEADS ** -0.5 * IDX_DIM ** -0.5)
    iq_h = [iq[:, h * IDX_DIM:(h + 1) * IDX_DIM].astype(BF16) for h in range(IDX_HEADS)]

    def score_tile(kt):
        ki_t = ki_ref[kt]
        acc = jnp.zeros((TK, tq), F32)
        for h in range(IDX_HEADS):
            s = lax.dot_general(ki_t, iq_h[h], NT_DIMS, preferred_element_type=F32)
            acc = acc + jnp.maximum(s, 0.0) * w_t[h:h + 1, :]
        return acc

    def p1_body(kt, carry):
        rmax, rmin = carry
        sc = score_tile(kt)
        s_scr[kt] = sc
        return (jnp.maximum(rmax, jnp.max(sc, axis=0, keepdims=True)),
                jnp.minimum(rmin, jnp.min(sc, axis=0, keepdims=True)))

    init = (jnp.full((1, tq), -jnp.inf, F32), jnp.full((1, tq), jnp.inf, F32))
    rmax, rmin = lax.fori_loop(0, nkt - 1, p1_body, init)
    sc = score_tile(nkt - 1)
    s_scr[nkt - 1] = jnp.where(adm_last, sc, -jnp.inf)
    rmax = jnp.maximum(rmax, jnp.max(jnp.where(adm_last, sc, -jnp.inf), axis=0, keepdims=True))
    rmin = jnp.minimum(rmin, jnp.min(jnp.where(adm_last, sc, jnp.inf), axis=0, keepdims=True))

    def count_ge(thr):
        def body(kt, c):
            return c + jnp.sum(jnp.where(s_scr[kt] >= thr, 1.0, 0.0), axis=0, keepdims=True)
        return lax.fori_loop(0, nkt, body, jnp.zeros((1, tq), F32))

    span = jnp.maximum(jnp.maximum(rmax - rmin, jnp.abs(rmax)), 1e-30)
    hi0 = rmax + span * (2.0 ** -10)

    def bis_body(_, carry):
        lo, hi, cnt_lo = carry
        mid = lo + 0.5 * (hi - lo)
        c = count_ge(mid)
        ge = c >= k_target
        return jnp.where(ge, mid, lo), jnp.where(ge, hi, mid), jnp.where(ge, c, cnt_lo)

    lo, hi, cnt_lo = lax.fori_loop(0, n_bisect, bis_body, (rmin, hi0, n_adm))

    excess = jnp.max(cnt_lo - k_target)

    @pl.when(excess > 0.0)
    def _():
        need = k_target - count_ge(hi)

        def count_tie_upto(jm):
            def body(kt, c):
                s = s_scr[kt]
                idx = krow + kt.astype(F32) * TK
                hit = (s >= lo) & (s < hi) & (idx <= jm)
                return c + jnp.sum(jnp.where(hit, 1.0, 0.0), axis=0, keepdims=True)
            return lax.fori_loop(0, nkt, body, jnp.zeros((1, tq), F32))

        def idx_body(_, carry):
            jlo, jhi = carry
            jm = jnp.floor(0.5 * (jlo + jhi))
            ok = count_tie_upto(jm) >= need
            return jnp.where(ok, jlo, jm), jnp.where(ok, jm, jhi)

        last_idx = jnp.zeros((1, tq), F32) + (nkt_f * TK - 1.0)
        _, jstar = lax.fori_loop(0, n_index, idx_body, (jnp.full((1, tq), -1.0, F32), last_idx))

        def drop_body(kt, c):
            s = s_scr[kt]
            idx = krow + kt.astype(F32) * TK
            s_scr[kt] = jnp.where((s >= lo) & (s < hi) & (idx > jstar), -jnp.inf, s)
            return c
        lax.fori_loop(0, nkt, drop_body, 0)

    aq = aq_ref[...] * (ATT_HD ** -0.5)
    zeros_q = jnp.zeros((tq, ATT_HD), F32)
    q_pad = []
    for h in range(ATT_HEADS):
        qh = aq[:, h * ATT_HD:(h + 1) * ATT_HD]
        parts = [zeros_q] * ATT_KV_HEADS
        parts[h // HEADS_PER_KV] = qh
        q_pad.append(jnp.concatenate(parts, axis=1).astype(BF16))

    m_scr[...] = jnp.full(m_scr.shape, NEG, F32)
    acc_scr[...] = jnp.zeros(acc_scr.shape, F32)

    def attend(kt, bias_idx):
        k_t = k_ref[kt]
        sel = s_scr[kt] >= lo
        for h in range(ATT_HEADS):
            g = h // HEADS_PER_KV
            lg = lax.dot_general(k_t, q_pad[h], NT_DIMS, preferred_element_type=F32)
            if bias_idx is not None:
                lg = lg + bt_ref[h, bias_idx]
            lg = jnp.where(sel, lg, NEG)
            m_old = m_scr[h:h + 1, :]
            m_new = jnp.maximum(m_old, jnp.max(lg, axis=0, keepdims=True))
            alpha = jnp.exp(m_old - m_new)
            p = jnp.exp(lg - m_new).astype(BF16)
            vt_g = vt_ref[kt, g * VT_ROWS:(g + 1) * VT_ROWS, :]
            acc_scr[h] = acc_scr[h] * alpha + jnp.dot(vt_g, p, preferred_element_type=F32)
            m_scr[h:h + 1, :] = m_new

    def far_body(kt, c):
        attend(kt, None)
        return c
    lax.fori_loop(0, nkt - 2, far_body, 0)

    if nkt_static is None:
        @pl.when(nkt >= 2)
        def _():
            attend(nkt - 2, 0)
    elif nkt_static >= 2:
        attend(nkt - 2, 0)
    attend(nkt - 1, 1)

    outs = []
    for h in range(ATT_HEADS):
        a = acc_scr[h]
        outs.append(a[:ATT_HD] * (1.0 / a[ATT_HD:ATT_HD + 1]))
    o_ref[...] = jnp.concatenate(outs, axis=0).T


def _attention(aq_src, iq_src, iw, ki3, k3, vt3, bt, batch, n_qblk, tq, nkt_total, nkt_static,
               last_valid, n_sel, aq_col, iq_col):
    kern = functools.partial(_attn_kernel, tq=tq, nkt_static=nkt_static, last_valid=last_valid,
                             n_sel=n_sel, n_bisect=36, n_index=15)
    qspec = lambda c: pl.BlockSpec((tq, ATT_W), lambda b, q, c=c: (b * n_qblk + q, c))
    kspec = lambda a, c: pl.BlockSpec((nkt_total, a, c), lambda b, q: (b, 0, 0))
    return pl.pallas_call(
        kern,
        grid=(batch, n_qblk),
        in_specs=[qspec(aq_col), qspec(iq_col),
                  pl.BlockSpec((tq, LANE), lambda b, q: (b * n_qblk + q, 0)),
                  kspec(TK, IDX_DIM), kspec(TK, KV_W), kspec(VT_ALL, TK),
                  _const_spec(bt.shape)],
        out_specs=pl.BlockSpec((tq, ATT_W), lambda b, q: (b * n_qblk + q, 0)),
        out_shape=jax.ShapeDtypeStruct((batch * n_qblk * tq, ATT_W), F32),
        scratch_shapes=[pltpu.VMEM((nkt_total, TK, tq), F32),
                        pltpu.VMEM((ATT_HEADS, VT_ROWS, tq), F32),
                        pltpu.VMEM((ATT_HEADS, tq), F32)],
        compiler_params=pltpu.CompilerParams(dimension_semantics=("parallel", "arbitrary"),
                                             vmem_limit_bytes=VMEM_LIMIT),
        name="attention",
    )(aq_src, iq_src, iw, ki3, k3, vt3, bt)


def _ffn_kernel(x_ref, oret_ref, oatt_ref, wout_ref, wup_ref, wdown_ref,
                gpost_ref, gpre_ref, gffn_ref, y_ref, *, ff_chunk):
    mix = jnp.dot(oret_ref[...].astype(BF16), wout_ref[:RET_W, :], preferred_element_type=F32)
    mix = mix + jnp.dot(oatt_ref[...].astype(BF16), wout_ref[RET_W:, :], preferred_element_type=F32)
    h = x_ref[...] + _rms(mix, gpost_ref[...])
    a = _rms(h, gpre_ref[...]).astype(BF16)
    f = jnp.zeros(h.shape, F32)
    for c in range(D_FF // ff_chunk):
        sl = slice(c * ff_chunk, (c + 1) * ff_chunk)
        u = jnp.dot(a, wup_ref[:, sl], preferred_element_type=F32)
        u = jnp.square(jnp.maximum(u, 0.0)).astype(BF16)
        f = f + jnp.dot(u, wdown_ref[sl, :], preferred_element_type=F32)
    y_ref[...] = h + _rms(f, gffn_ref[...])


def _out_ffn(x2, o_ret, o_att, w_out, w_up, w_down, g_post, g_pre, g_ffn, tm):
    rows = x2.shape[0]
    row_spec = lambda w: pl.BlockSpec((tm, w), lambda i: (i, 0))
    return pl.pallas_call(
        functools.partial(_ffn_kernel, ff_chunk=512),
        grid=(rows // tm,),
        in_specs=[row_spec(D_MODEL), row_spec(RET_W), row_spec(ATT_W),
                  _const_spec(w_out.shape), _const_spec(w_up.shape), _const_spec(w_down.shape),
                  _const_spec((1, D_MODEL)), _const_spec((1, D_MODEL)), _const_spec((1, D_MODEL))],
        out_specs=row_spec(D_MODEL),
        out_shape=jax.ShapeDtypeStruct((rows, D_MODEL), F32),
        compiler_params=pltpu.CompilerParams(dimension_semantics=("parallel",),
                                             vmem_limit_bytes=VMEM_LIMIT),
        name="out_ffn",
    )(x2, o_ret, o_att, w_out, w_up, w_down, g_post, g_pre, g_ffn)


def _t5_bucket(rel):
    half = N_BUCKETS // 2
    max_exact = half // 2
    ret = jnp.where(rel > 0, half, 0)
    n = jnp.abs(rel)
    nf = jnp.maximum(n, 1).astype(F32)
    large = max_exact + (jnp.log(nf / max_exact) / math.log(MAX_DISTANCE / max_exact)
                         * (half - max_exact)).astype(jnp.int32)
    large = jnp.minimum(large, half - 1)
    return ret + jnp.where(n < max_exact, n, large)


def _bias_tiles(rel_bias, tq):
    j = jnp.arange(TK, dtype=jnp.int32)[:, None]
    i = jnp.arange(tq, dtype=jnp.int32)[None, :]
    rel = jnp.stack([j - TK - i, j - i])
    rel = jnp.clip(rel, -MAX_DISTANCE, MAX_DISTANCE)
    table = rel_bias.astype(F32)
    far = table[_t5_bucket(jnp.int32(-MAX_DISTANCE))]
    bias = table[_t5_bucket(rel)] - far
    return jnp.transpose(bias, (3, 0, 1, 2))


def _rope_tables(pos):
    half = RET_DK // 2
    inv = ROPE_BASE ** (-jnp.arange(half, dtype=F32) / half)
    ang = pos.astype(F32)[:, None] * inv[None, :]
    cos, sin = jnp.cos(ang), jnp.sin(ang)
    cos_h = jnp.concatenate([cos, cos], axis=1)
    sin_h = jnp.concatenate([-sin, sin], axis=1)
    reps = LANE // RET_DK
    return jnp.tile(cos_h, (1, reps)), jnp.tile(sin_h, (1, reps))


def _permute_w_in(w_in):
    offs = np.cumsum([0, 512, 512, 512, 512, 512, 128, 128, 512, 64, 8])
    seg = lambda i: w_in[:, offs[i]:offs[i + 1]]
    pad = jnp.zeros((D_MODEL, LANE - IDX_DIM - IDX_HEADS), w_in.dtype)
    return jnp.concatenate([seg(0), seg(1), seg(2), seg(3), seg(4), seg(7),
                            seg(5), seg(6), seg(8), seg(9), pad], axis=1).astype(BF16)


def _layer(x, s0, past, weights, rel_bias):
    w_perm, w_out, w_up, w_down, g_pre_mix, g_post_mix, g_pre_ffn, g_post_ffn = weights
    batch, t_len, _ = x.shape
    rows = batch * t_len
    x2 = x.reshape(rows, D_MODEL)
    tm = min(512, rows)
    main, kc, vc, ic, iw, *key_tiles = _inproj(x2, g_pre_mix, w_perm, tm, key_tiles=past is None)

    p_len = 0 if past is None else past[0].shape[1]
    pos = p_len + jnp.arange(t_len, dtype=jnp.int32)
    cos_t, sin_t = _rope_tables(pos)
    chunk = min(CHUNK, t_len)
    sb = min(256, t_len)
    o_ret, s_new = _retention(main, s0, cos_t, sin_t, batch, t_len, sb, chunk)

    l_all = p_len + t_len
    n_sel = min(TOPK_MAX, l_all // 4)
    if past is None:
        tq = TK
        n_qblk = t_len // tq
        nkt_total = t_len // TK
        bt = _bias_tiles(rel_bias, tq)
        k3, ki3, vt3 = key_tiles
        o_att = _attention(main, main, iw, ki3, k3, vt3, bt, batch, n_qblk, tq, nkt_total, None,
                           TK, n_sel, aq_col=4, iq_col=5)
    else:
        tq = LANE
        pk, pv, pi = past
        assert p_len % TK == 0 and t_len <= min(tq, TK), (p_len, t_len)
        nkt_total = p_len // TK + 1
        padk = nkt_total * TK - l_all
        cat = lambda old, new: jnp.concatenate(
            [old.astype(BF16), new.reshape(batch, t_len, -1).astype(BF16),
             jnp.zeros((batch, padk, new.shape[-1]), BF16)], axis=1)
        k_all = cat(pk.reshape(batch, p_len, KV_W), kc)
        v_all = cat(pv.reshape(batch, p_len, KV_W), vc)
        i_all = cat(pi, ic)
        v_t = jnp.transpose(v_all.reshape(batch, nkt_total, TK, ATT_KV_HEADS, ATT_HD), (0, 1, 3, 4, 2))
        ones = jnp.zeros((batch, nkt_total, ATT_KV_HEADS, VT_ROWS - ATT_HD, TK), BF16).at[:, :, :, 0, :].set(1.0)
        vt_s = jnp.concatenate([v_t, ones], axis=3).reshape(batch * nkt_total, VT_ALL, TK)
        k_s = k_all.reshape(batch * nkt_total, TK, KV_W)
        i_s = i_all.reshape(batch * nkt_total, TK, IDX_DIM)
        padq = lambda a: jnp.pad(a.reshape(batch, t_len, -1), ((0, 0), (0, tq - t_len), (0, 0))
                                 ).reshape(batch * tq, -1)
        aq_s = padq(main[:, 4 * RET_W:4 * RET_W + ATT_W])
        iq_s = padq(main[:, 4 * RET_W + ATT_W:])
        iw_s = padq(iw)
        bt = _bias_tiles(rel_bias, tq)
        o_pad = _attention(aq_s, iq_s, iw_s, i_s, k_s, vt_s, bt, batch, 1, tq, nkt_total, nkt_total,
                           l_all - p_len, n_sel, aq_col=0, iq_col=0)
        o_att = o_pad.reshape(batch, tq, ATT_W)[:, :t_len].reshape(rows, ATT_W)

    y = _out_ffn(x2, o_ret, o_att, w_out, w_up, w_down, g_post_mix, g_pre_ffn, g_post_ffn, tm)
    return (y.reshape(batch, t_len, D_MODEL), s_new,
            kc.reshape(batch, t_len, ATT_KV_HEADS, ATT_HD),
            vc.reshape(batch, t_len, ATT_KV_HEADS, ATT_HD),
            ic.reshape(batch, t_len, IDX_DIM))


def kernel(x_prompt, x_sample, state_ret, cache_k, cache_v, cache_kidx, w_in, w_out, w_up, w_down,
           g_pre_mix, g_post_mix, g_pre_ffn, g_post_ffn, rel_bias):
    depth = w_in.shape[0]
    bp = x_prompt.shape[0]
    zero_state = jnp.zeros((bp, RET_HEADS, RET_DK, RET_DV), F32)
    yp, ys = x_prompt, x_sample
    outs_p, outs_s = [], []
    for l in range(depth):
        row = lambda g: g[l].reshape(1, D_MODEL).astype(F32)
        weights = (_permute_w_in(w_in[l]), w_out[l].astype(BF16), w_up[l].astype(BF16),
                   w_down[l].astype(BF16), row(g_pre_mix), row(g_post_mix), row(g_pre_ffn),
                   row(g_post_ffn))
        yp, *rest_p = _layer(yp, zero_state, None, weights, rel_bias)
        ys, *rest_s = _layer(ys, state_ret[l], (cache_k[l], cache_v[l], cache_kidx[l]), weights, rel_bias)
        outs_p.append(rest_p)
        outs_s.append(rest_s)
    stack = lambda outs, i: jnp.stack([o[i] for o in outs])
    return (yp, ys,
            stack(outs_p, 0), stack(outs_p, 1), stack(outs_p, 2), stack(outs_p, 3),
            stack(outs_s, 0), stack(outs_s, 1), stack(outs_s, 2), stack(outs_s, 3))
```

```python
import functools
import math

import jax
import jax.numpy as jnp
import numpy as np
from jax import lax
from jax.experimental import pallas as pl
from jax.experimental.pallas import tpu as pltpu

D_MODEL = 1024
CHUNK = 64
RET_HEADS = 8
RET_DK = 64
RET_DV = 64
ATT_HEADS = 8
ATT_KV_HEADS = 2
ATT_HD = 64
IDX_HEADS = 8
IDX_DIM = 64
TOPK_MAX = 256
N_BUCKETS = 32
MAX_DISTANCE = 128
D_FF = 4 * D_MODEL
ROPE_BASE = 10000.0
EPS = 1e-6

RET_W = RET_HEADS * RET_DV
ATT_W = ATT_HEADS * ATT_HD
KV_W = ATT_KV_HEADS * ATT_HD
IDX_W = IDX_HEADS * IDX_DIM
HEADS_PER_KV = ATT_HEADS // ATT_KV_HEADS

MAIN_W = 4 * RET_W + ATT_W + IDX_W
TAIL_W = 3 * 128
IW_OFF = IDX_DIM

LANE = 128
VT_ROWS = 80
VT_ALL = ATT_KV_HEADS * VT_ROWS
TK = 256
NEG = -1e30
CNT_ROWS = 32
VMEM_LIMIT = 56 * 1024 * 1024

F32 = jnp.float32
BF16 = jnp.bfloat16
NT_DIMS = (((1,), (1,)), ((), ()))
TN_DIMS = (((0,), (0,)), ((), ()))


def _const_spec(shape):
    nd = len(shape)
    return pl.BlockSpec(shape, lambda *_: (0,) * nd, pipeline_mode=pl.Buffered(1))


def _rms(x, gain):
    return x * lax.rsqrt(jnp.mean(x * x, axis=-1, keepdims=True) + EPS) * gain


def _inproj_kernel(x_ref, g_ref, w_ref, main_ref, kc_ref, vc_ref, ic_ref, iw_ref,
                   *tile_refs, tm):
    a = _rms(x_ref[...], g_ref[...]).astype(BF16)
    for c in range(MAIN_W // 512):
        main_ref[:, c * 512:(c + 1) * 512] = jnp.dot(
            a, w_ref[:, c * 512:(c + 1) * 512], preferred_element_type=F32)
    tail = jnp.dot(a, w_ref[:, MAIN_W:MAIN_W + TAIL_W], preferred_element_type=F32)
    ak = tail[:, 0:KV_W]
    av = tail[:, KV_W:2 * KV_W]
    last = tail[:, 2 * KV_W:3 * KV_W]
    kc_ref[...] = ak
    vc_ref[...] = av
    ic_ref[...] = last[:, :IDX_DIM]
    iw_ref[...] = last
    if not tile_refs:
        return
    k3_ref, ki3_ref, vt3_ref = tile_refs
    avt = av.T
    row = lax.broadcasted_iota(jnp.int32, (VT_ROWS - ATT_HD, tm), 0)
    ones_rows = jnp.where(row == 0, 1.0, 0.0).astype(F32)
    vt = jnp.concatenate([avt[:ATT_HD], ones_rows, avt[ATT_HD:], ones_rows], axis=0).astype(BF16)
    for j in range(tm // TK):
        k3_ref[j] = ak[j * TK:(j + 1) * TK].astype(BF16)
        ki3_ref[j] = last[j * TK:(j + 1) * TK, :IDX_DIM].astype(BF16)
        vt3_ref[j] = vt[:, j * TK:(j + 1) * TK]


def _inproj(x2, gain, w_perm, tm, key_tiles):
    rows = x2.shape[0]
    grid = (rows // tm,)
    row_spec = lambda w: pl.BlockSpec((tm, w), lambda i: (i, 0))
    out_shape = (
        jax.ShapeDtypeStruct((rows, MAIN_W), F32),
        jax.ShapeDtypeStruct((rows, KV_W), F32),
        jax.ShapeDtypeStruct((rows, KV_W), F32),
        jax.ShapeDtypeStruct((rows, IDX_DIM), F32),
        jax.ShapeDtypeStruct((rows, LANE), F32),
    )
    out_specs = (row_spec(MAIN_W), row_spec(KV_W), row_spec(KV_W), row_spec(IDX_DIM), row_spec(LANE))
    if key_tiles:
        t3 = lambda a, b: pl.BlockSpec((tm // TK, a, b), lambda i: (i, 0, 0))
        out_shape += (jax.ShapeDtypeStruct((rows // TK, TK, KV_W), BF16),
                      jax.ShapeDtypeStruct((rows // TK, TK, IDX_DIM), BF16),
                      jax.ShapeDtypeStruct((rows // TK, VT_ALL, TK), BF16))
        out_specs += (t3(TK, KV_W), t3(TK, IDX_DIM), t3(VT_ALL, TK))
    return pl.pallas_call(
        functools.partial(_inproj_kernel, tm=tm),
        grid=grid,
        in_specs=[row_spec(D_MODEL), _const_spec((1, D_MODEL)), _const_spec(w_perm.shape)],
        out_specs=out_specs,
        out_shape=out_shape,
        compiler_params=pltpu.CompilerParams(dimension_semantics=("parallel",),
                                             vmem_limit_bytes=VMEM_LIMIT),
        name="inproj",
    )(x2, gain, w_perm)


def _ret_gammas():
    return [1.0 - 2.0 ** (-5.0 - h) for h in range(RET_HEADS)]


def _ret_tables(sb, chunk):
    lg = np.log(np.array(_ret_gammas(), np.float64))
    t = np.arange(sb)
    ci = t // chunk
    diff = t[:, None] - t[None, :]
    same = ci[:, None] == ci[None, :]
    below = ci[None, :] < ci[:, None]
    expo = np.where(same, np.abs(diff), np.where(below, diff, 0)).astype(np.float64)
    dmat = np.exp(lg[:, None, None] * expo[None]) * (same | below)[None]
    qd = np.exp(lg[None, :] * (t + 1.0)[:, None])
    kd = np.exp(lg[None, :] * (sb - 1.0 - t)[:, None])
    qd = np.repeat(qd, RET_DK, axis=1)
    kd = np.repeat(kd, RET_DK, axis=1)
    return (jnp.asarray(dmat, F32), jnp.asarray(qd, F32), jnp.asarray(kd, F32),
            [float(math.exp(v * sb)) for v in lg])


def _ret_kernel(q_ref, k_ref, v_ref, g_ref, cos_ref, sin_ref, d_ref, qd_ref, kd_ref, s0_ref,
                o_ref, sfin_ref, s_scr, *, g_block):
    n = pl.program_id(1)

    @pl.when(n == 0)
    def _():
        s_scr[...] = s0_ref[0]

    reps = RET_W // LANE
    cos = jnp.concatenate([cos_ref[...]] * reps, axis=1)
    sin = jnp.concatenate([sin_ref[...]] * reps, axis=1)
    lane = lax.broadcasted_iota(jnp.int32, cos.shape, 1)
    first_half = (lane & (RET_DK - 1)) < RET_DK // 2

    def rot(x):
        partner = jnp.where(first_half, pltpu.roll(x, RET_W - RET_DK // 2, 1),
                            pltpu.roll(x, RET_DK // 2, 1))
        return x * cos + partner * sin

    q = rot(q_ref[...])
    k = rot(k_ref[...]) * (RET_DK ** -0.5)
    v = v_ref[...].astype(BF16)
    gate = g_ref[...]
    qb = q.astype(BF16)
    kb = k.astype(BF16)
    qx = (q * qd_ref[...]).astype(BF16)
    kx = (k * kd_ref[...]).astype(BF16)
    outs = []
    for h in range(RET_HEADS):
        sl = slice(h * RET_DK, (h + 1) * RET_DK)
        s = lax.dot_general(qb[:, sl], kb[:, sl], NT_DIMS, preferred_element_type=F32)
        p = (s * d_ref[h]).astype(BF16)
        st = s_scr[h]
        o = jnp.dot(p, v[:, sl], preferred_element_type=F32)
        o = o + jnp.dot(qx[:, sl], st.astype(BF16), preferred_element_type=F32)
        s_scr[h] = g_block[h] * st + lax.dot_general(kx[:, sl], v[:, sl], TN_DIMS,
                                                     preferred_element_type=F32)
        o = o * lax.rsqrt(jnp.mean(o * o, axis=-1, keepdims=True) + EPS)
        outs.append(o)
    o_all = jnp.concatenate(outs, axis=1)
    o_ref[...] = gate * (1.0 / (1.0 + jnp.exp(-gate))) * o_all

    @pl.when(n == pl.num_programs(1) - 1)
    def _():
        sfin_ref[0] = s_scr[...]


def _retention(main, s0, cos_t, sin_t, batch, t_len, sb, chunk):
    nsb = t_len // sb
    dmat, qd, kd, g_block = _ret_tables(sb, chunk)
    col = lambda c: pl.BlockSpec((sb, RET_W), lambda b, n, c=c: (b * nsb + n, c))
    tab = pl.BlockSpec((sb, LANE), lambda b, n: (n, 0))
    st_spec = pl.BlockSpec((1, RET_HEADS, RET_DK, RET_DV), lambda b, n: (b, 0, 0, 0))
    return pl.pallas_call(
        functools.partial(_ret_kernel, g_block=g_block),
        grid=(batch, nsb),
        in_specs=[col(0), col(1), col(2), col(3), tab, tab,
                  _const_spec(dmat.shape), _const_spec(qd.shape), _const_spec(kd.shape), st_spec],
        out_specs=(pl.BlockSpec((sb, RET_W), lambda b, n: (b * nsb + n, 0)), st_spec),
        out_shape=(jax.ShapeDtypeStruct((batch * t_len, RET_W), F32),
                   jax.ShapeDtypeStruct((batch, RET_HEADS, RET_DK, RET_DV), F32)),
        scratch_shapes=[pltpu.VMEM((RET_HEADS, RET_DK, RET_DV), F32)],
        compiler_params=pltpu.CompilerParams(dimension_semantics=("parallel", "arbitrary"),
                                             vmem_limit_bytes=VMEM_LIMIT),
        name="retention",
    )(main, main, main, main, cos_t, sin_t, dmat, qd, kd, s0)


def _attn_kernel(aq_ref, iq_ref, iw_ref, ki_ref, k_ref, vt_ref, bt_ref, o_ref,
                 s_scr, acc_scr, m_scr, *, tq, nkt_static, last_valid, n_valid_q, n_sel,
                 max_search, n_index):
    qblk = pl.program_id(1)
    nkt = qblk + 1 if nkt_static is None else nkt_static
    lane_f = lax.broadcasted_iota(jnp.int32, (1, tq), 1).astype(F32)
    lane_ok = lane_f < float(n_valid_q)
    klim = jnp.minimum((jnp.floor(lane_f * (1.0 / CHUNK)) + 1.0) * CHUNK, float(last_valid))
    krow = lax.broadcasted_iota(jnp.int32, (TK, tq), 0).astype(F32)
    adm_last = krow < klim
    nkt_f = nkt.astype(F32) if nkt_static is None else float(nkt)
    n_adm = (nkt_f - 1.0) * TK + klim
    k_target = jnp.minimum(float(n_sel), n_adm)

    iq = iq_ref[...]
    w_t = iw_ref[...].T[IW_OFF:IW_OFF + IDX_HEADS, :] * (IDX_HEADS ** -0.5 * IDX_DIM ** -0.5)
    iq_all = jnp.concatenate(
        [iq[:, h * IDX_DIM:(h + 1) * IDX_DIM].astype(BF16) for h in range(IDX_HEADS)], axis=0)

    def score_tile(kt):
        s_all = lax.dot_general(ki_ref[kt], iq_all, NT_DIMS, preferred_element_type=F32)
        acc = jnp.maximum(s_all[:, :tq], 0.0) * w_t[0:1, :]
        for h in range(1, IDX_HEADS):
            acc = acc + jnp.maximum(s_all[:, h * tq:(h + 1) * tq], 0.0) * w_t[h:h + 1, :]
        return acc

    def p1_body(kt, carry):
        rmax, rmin = carry
        sc = score_tile(kt)
        s_scr[kt] = sc
        return (jnp.maximum(rmax, jnp.max(sc, axis=0, keepdims=True)),
                jnp.minimum(rmin, jnp.min(sc, axis=0, keepdims=True)))

    init = (jnp.full((1, tq), -jnp.inf, F32), jnp.full((1, tq), jnp.inf, F32))
    rmax, rmin = lax.fori_loop(0, nkt - 1, p1_body, init)
    sc = score_tile(nkt - 1)
    s_scr[nkt - 1] = jnp.where(adm_last, sc, -jnp.inf)
    rmax = jnp.maximum(rmax, jnp.max(jnp.where(adm_last, sc, -jnp.inf), axis=0, keepdims=True))
    rmin = jnp.minimum(rmin, jnp.min(jnp.where(adm_last, sc, jnp.inf), axis=0, keepdims=True))

    def fold_rows(x):
        parts = [x[i * CNT_ROWS:(i + 1) * CNT_ROWS] for i in range(TK // CNT_ROWS)]
        while len(parts) > 1:
            parts = [a + b for a, b in zip(parts[0::2], parts[1::2])]
        return parts[0]

    def count_where(pred):
        def body(kt, c):
            return c + fold_rows(jnp.where(pred(kt), 1.0, 0.0))
        c = lax.fori_loop(0, nkt, body, jnp.zeros((CNT_ROWS, tq), F32))
        return jnp.sum(c, axis=0, keepdims=True)

    def count_ge(thr):
        return count_where(lambda kt: s_scr[kt] >= thr)

    span = jnp.maximum(jnp.maximum(rmax - rmin, jnp.abs(rmax)), 1e-30)
    hi0 = rmax + span * (2.0 ** -10)

    def search_cond(st):
        return (st[0] < max_search) & (st[1] > 0.0)

    def search_body(st):
        it, _, lo, hi, c_lo, c_hi = st
        width = hi - lo
        frac = (c_lo - k_target - 0.5) / (c_lo - c_hi)
        interp = ((it + 1) % 2).astype(F32)
        mid_i = lo + width * (0.5 + interp * (frac - 0.5))
        mid_b = lo + 0.5 * width
        mid = jnp.where((mid_i > lo) & (mid_i < hi), mid_i, mid_b)
        open_ = (mid > lo) & (mid < hi)
        active = open_ & (c_lo != k_target) & lane_ok
        c = count_ge(mid)
        ge = c >= k_target
        up = active & ge
        dn = active & jnp.logical_not(ge)
        lo = jnp.where(up, mid, lo)
        c_lo = jnp.where(up, c, c_lo)
        hi = jnp.where(dn, mid, hi)
        c_hi = jnp.where(dn, c, c_hi)
        n_active = jnp.max(jnp.where(active & (c_lo != k_target), 1.0, 0.0))
        return it + 1, n_active, lo, hi, c_lo, c_hi

    st0 = (jnp.int32(0), jnp.float32(1.0), rmin, hi0, n_adm, jnp.zeros((1, tq), F32))
    _, _, lo, hi, cnt_lo, _ = lax.while_loop(search_cond, search_body, st0)

    excess = jnp.max(jnp.where(lane_ok, cnt_lo - k_target, 0.0))

    @pl.when(excess > 0.0)
    def _():
        need = k_target - count_ge(hi)

        def tie_upto(jm):
            def pred(kt):
                s = s_scr[kt]
                return (s >= lo) & (s < hi) & (krow + kt.astype(F32) * TK <= jm)
            return pred

        def idx_body(_, carry):
            jlo, jhi = carry
            jm = jnp.floor(0.5 * (jlo + jhi))
            ok = count_where(tie_upto(jm)) >= need
            return jnp.where(ok, jlo, jm), jnp.where(ok, jm, jhi)

        last_idx = jnp.zeros((1, tq), F32) + (nkt_f * TK - 1.0)
        _, jstar = lax.fori_loop(0, n_index, idx_body, (jnp.full((1, tq), -1.0, F32), last_idx))

        def drop_body(kt, c):
            s = s_scr[kt]
            idx = krow + kt.astype(F32) * TK
            s_scr[kt] = jnp.where((s >= lo) & (s < hi) & (idx > jstar), -jnp.inf, s)
            return c
        lax.fori_loop(0, nkt, drop_body, 0)

    aq = aq_ref[...] * (ATT_HD ** -0.5)
    zeros_q = jnp.zeros((tq, ATT_HD), F32)
    q_pad = []
    for h in range(ATT_HEADS):
        qh = aq[:, h * ATT_HD:(h + 1) * ATT_HD]
        parts = [zeros_q] * ATT_KV_HEADS
        parts[h // HEADS_PER_KV] = qh
        q_pad.append(jnp.concatenate(parts, axis=1).astype(BF16))
    q_all = jnp.concatenate(q_pad, axis=0)

    m_scr[...] = jnp.full(m_scr.shape, NEG, F32)
    acc_scr[...] = jnp.zeros(acc_scr.shape, F32)

    def attend(kt, bias_idx):
        neg_mask = jnp.where(s_scr[kt] >= lo, 0.0, NEG)
        lg_all = lax.dot_general(k_ref[kt], q_all, NT_DIMS, preferred_element_type=F32)
        p_parts, alphas = [], []
        for h in range(ATT_HEADS):
            add = neg_mask if bias_idx is None else neg_mask + bt_ref[h, bias_idx]
            lg = lg_all[:, h * tq:(h + 1) * tq] + add
            m_old = m_scr[h:h + 1, :]
            m_new = jnp.maximum(m_old, jnp.max(lg, axis=0, keepdims=True))
            alphas.append(jnp.exp(m_old - m_new))
            p_parts.append(jnp.exp(lg - m_new).astype(BF16))
            m_scr[h:h + 1, :] = m_new
        for g in range(ATT_KV_HEADS):
            hs = range(g * HEADS_PER_KV, (g + 1) * HEADS_PER_KV)
            p_g = jnp.concatenate([p_parts[h] for h in hs], axis=1)
            a_g = jnp.concatenate([alphas[h] for h in hs], axis=1)
            vt_g = vt_ref[kt, g * VT_ROWS:(g + 1) * VT_ROWS, :]
            acc_scr[g] = acc_scr[g] * a_g + jnp.dot(vt_g, p_g, preferred_element_type=F32)

    def far_body(kt, c):
        attend(kt, None)
        return c
    lax.fori_loop(0, nkt - 2, far_body, 0)

    if nkt_static is None:
        @pl.when(nkt >= 2)
        def _():
            attend(nkt - 2, 0)
    elif nkt_static >= 2:
        attend(nkt - 2, 0)
    attend(nkt - 1, 1)

    outs = []
    for g in range(ATT_KV_HEADS):
        a = acc_scr[g]
        o_g = a[:ATT_HD] * (1.0 / a[ATT_HD:ATT_HD + 1])
        outs += [o_g[:, r * tq:(r + 1) * tq] for r in range(HEADS_PER_KV)]
    o_ref[...] = jnp.concatenate(outs, axis=0).T


def _attention(aq_src, iq_src, iw, ki3, k3, vt3, bt, batch, n_qblk, tq, nkt_total, nkt_static,
               last_valid, n_valid_q, n_sel, aq_col, iq_col):
    kern = functools.partial(_attn_kernel, tq=tq, nkt_static=nkt_static, last_valid=last_valid,
                             n_valid_q=n_valid_q, n_sel=n_sel, max_search=96,
                             n_index=int(math.log2(nkt_total * TK)) + 2)
    qspec = lambda c: pl.BlockSpec((tq, ATT_W), lambda b, q, c=c: (b * n_qblk + q, c))
    kspec = lambda a, c: pl.BlockSpec((nkt_total, a, c), lambda b, q: (b, 0, 0))
    return pl.pallas_call(
        kern,
        grid=(batch, n_qblk),
        in_specs=[qspec(aq_col), qspec(iq_col),
                  pl.BlockSpec((tq, LANE), lambda b, q: (b * n_qblk + q, 0)),
                  kspec(TK, IDX_DIM), kspec(TK, KV_W), kspec(VT_ALL, TK),
                  _const_spec(bt.shape)],
        out_specs=pl.BlockSpec((tq, ATT_W), lambda b, q: (b * n_qblk + q, 0)),
        out_shape=jax.ShapeDtypeStruct((batch * n_qblk * tq, ATT_W), F32),
        scratch_shapes=[pltpu.VMEM((nkt_total, TK, tq), F32),
                        pltpu.VMEM((ATT_KV_HEADS, VT_ROWS, HEADS_PER_KV * tq), F32),
                        pltpu.VMEM((ATT_HEADS, tq), F32)],
        compiler_params=pltpu.CompilerParams(dimension_semantics=("parallel", "arbitrary"),
                                             vmem_limit_bytes=VMEM_LIMIT),
        name="attention",
    )(aq_src, iq_src, iw, ki3, k3, vt3, bt)


def _ffn_kernel(x_ref, oret_ref, oatt_ref, wout_ref, wup_ref, wdown_ref,
                gpost_ref, gpre_ref, gffn_ref, y_ref, *, ff_chunk):
    mix = jnp.dot(oret_ref[...].astype(BF16), wout_ref[:RET_W, :], preferred_element_type=F32)
    mix = mix + jnp.dot(oatt_ref[...].astype(BF16), wout_ref[RET_W:, :], preferred_element_type=F32)
    h = x_ref[...] + _rms(mix, gpost_ref[...])
    a = _rms(h, gpre_ref[...]).astype(BF16)
    f = jnp.zeros(h.shape, F32)
    for c in range(D_FF // ff_chunk):
        sl = slice(c * ff_chunk, (c + 1) * ff_chunk)
        u = jnp.dot(a, wup_ref[:, sl], preferred_element_type=F32)
        u = jnp.square(jnp.maximum(u, 0.0)).astype(BF16)
        f = f + jnp.dot(u, wdown_ref[sl, :], preferred_element_type=F32)
    y_ref[...] = h + _rms(f, gffn_ref[...])


def _out_ffn(x2, o_ret, o_att, w_out, w_up, w_down, g_post, g_pre, g_ffn, tm):
    rows = x2.shape[0]
    row_spec = lambda w: pl.BlockSpec((tm, w), lambda i: (i, 0))
    return pl.pallas_call(
        functools.partial(_ffn_kernel, ff_chunk=512),
        grid=(rows // tm,),
        in_specs=[row_spec(D_MODEL), row_spec(RET_W), row_spec(ATT_W),
                  _const_spec(w_out.shape), _const_spec(w_up.shape), _const_spec(w_down.shape),
                  _const_spec((1, D_MODEL)), _const_spec((1, D_MODEL)), _const_spec((1, D_MODEL))],
        out_specs=row_spec(D_MODEL),
        out_shape=jax.ShapeDtypeStruct((rows, D_MODEL), F32),
        compiler_params=pltpu.CompilerParams(dimension_semantics=("parallel",),
                                             vmem_limit_bytes=VMEM_LIMIT),
        name="out_ffn",
    )(x2, o_ret, o_att, w_out, w_up, w_down, g_post, g_pre, g_ffn)


def _t5_bucket(rel):
    half = N_BUCKETS // 2
    max_exact = half // 2
    ret = jnp.where(rel > 0, half, 0)
    n = jnp.abs(rel)
    nf = jnp.maximum(n, 1).astype(F32)
    large = max_exact + (jnp.log(nf / max_exact) / math.log(MAX_DISTANCE / max_exact)
                         * (half - max_exact)).astype(jnp.int32)
    large = jnp.minimum(large, half - 1)
    return ret + jnp.where(n < max_exact, n, large)


def _bias_tiles(rel_bias, tq):
    period = 2 * TK + tq
    rel = jnp.arange(period, dtype=jnp.int32) - (TK + tq - 1)
    table = rel_bias.astype(F32)
    far = table[_t5_bucket(jnp.int32(-MAX_DISTANCE))]
    u = (table[_t5_bucket(jnp.clip(rel, -MAX_DISTANCE, MAX_DISTANCE))] - far).T
    n = jnp.tile(u, (1, tq))[:, :tq * (period - 1)].reshape(ATT_HEADS, tq, period - 1)
    m = n[:, :, tq - 1:tq - 1 + 2 * TK]
    return jnp.transpose(m.reshape(ATT_HEADS, tq, 2, TK), (0, 2, 3, 1))


def _rope_tables(pos):
    half = RET_DK // 2
    inv = ROPE_BASE ** (-jnp.arange(half, dtype=F32) / half)
    ang = pos.astype(F32)[:, None] * inv[None, :]
    cos, sin = jnp.cos(ang), jnp.sin(ang)
    cos_h = jnp.concatenate([cos, cos], axis=1)
    sin_h = jnp.concatenate([-sin, sin], axis=1)
    reps = LANE // RET_DK
    return jnp.tile(cos_h, (1, reps)), jnp.tile(sin_h, (1, reps))


def _permute_w_in(w_in):
    offs = np.cumsum([0, 512, 512, 512, 512, 512, 128, 128, 512, 64, 8])
    seg = lambda i: w_in[:, offs[i]:offs[i + 1]]
    pad = jnp.zeros((D_MODEL, LANE - IDX_DIM - IDX_HEADS), w_in.dtype)
    return jnp.concatenate([seg(0), seg(1), seg(2), seg(3), seg(4), seg(7),
                            seg(5), seg(6), seg(8), seg(9), pad], axis=1).astype(BF16)


def _layer(x, s0, past, weights, rel_bias):
    w_perm, w_out, w_up, w_down, g_pre_mix, g_post_mix, g_pre_ffn, g_post_ffn = weights
    batch, t_len, _ = x.shape
    rows = batch * t_len
    x2 = x.reshape(rows, D_MODEL)
    tm = min(512, rows)
    main, kc, vc, ic, iw, *key_tiles = _inproj(x2, g_pre_mix, w_perm, tm, key_tiles=past is None)

    p_len = 0 if past is None else past[0].shape[1]
    pos = p_len + jnp.arange(t_len, dtype=jnp.int32)
    cos_t, sin_t = _rope_tables(pos)
    chunk = min(CHUNK, t_len)
    sb = min(256, t_len)
    o_ret, s_new = _retention(main, s0, cos_t, sin_t, batch, t_len, sb, chunk)

    l_all = p_len + t_len
    n_sel = min(TOPK_MAX, l_all // 4)
    if past is None:
        tq = TK
        n_qblk = t_len // tq
        nkt_total = t_len // TK
        bt = _bias_tiles(rel_bias, tq)
        k3, ki3, vt3 = key_tiles
        o_att = _attention(main, main, iw, ki3, k3, vt3, bt, batch, n_qblk, tq, nkt_total, None,
                           TK, tq, n_sel, aq_col=4, iq_col=5)
    else:
        tq = LANE
        pk, pv, pi = past
        assert p_len % TK == 0 and t_len <= min(tq, TK), (p_len, t_len)
        nkt_total = p_len // TK + 1
        padk = nkt_total * TK - l_all
        cat = lambda old, new: jnp.concatenate(
            [old.astype(BF16), new.reshape(batch, t_len, -1).astype(BF16),
             jnp.zeros((batch, padk, new.shape[-1]), BF16)], axis=1)
        k_all = cat(pk.reshape(batch, p_len, KV_W), kc)
        v_all = cat(pv.reshape(batch, p_len, KV_W), vc)
        i_all = cat(pi, ic)
        v_t = jnp.transpose(v_all.reshape(batch, nkt_total, TK, ATT_KV_HEADS, ATT_HD), (0, 1, 3, 4, 2))
        ones = jnp.zeros((batch, nkt_total, ATT_KV_HEADS, VT_ROWS - ATT_HD, TK), BF16).at[:, :, :, 0, :].set(1.0)
        vt_s = jnp.concatenate([v_t, ones], axis=3).reshape(batch * nkt_total, VT_ALL, TK)
        k_s = k_all.reshape(batch * nkt_total, TK, KV_W)
        i_s = i_all.reshape(batch * nkt_total, TK, IDX_DIM)
        padq = lambda a: jnp.pad(a.reshape(batch, t_len, -1), ((0, 0), (0, tq - t_len), (0, 0))
                                 ).reshape(batch * tq, -1)
        aq_s = padq(main[:, 4 * RET_W:4 * RET_W + ATT_W])
        iq_s = padq(main[:, 4 * RET_W + ATT_W:])
        iw_s = padq(iw)
        bt = _bias_tiles(rel_bias, tq)
        o_pad = _attention(aq_s, iq_s, iw_s, i_s, k_s, vt_s, bt, batch, 1, tq, nkt_total, nkt_total,
                           l_all - p_len, t_len, n_sel, aq_col=0, iq_col=0)
        o_att = o_pad.reshape(batch, tq, ATT_W)[:, :t_len].reshape(rows, ATT_W)

    y = _out_ffn(x2, o_ret, o_att, w_out, w_up, w_down, g_post_mix, g_pre_ffn, g_post_ffn, tm)
    return (y.reshape(batch, t_len, D_MODEL), s_new,
            kc.reshape(batch, t_len, ATT_KV_HEADS, ATT_HD),
            vc.reshape(batch, t_len, ATT_KV_HEADS, ATT_HD),
            ic.reshape(batch, t_len, IDX_DIM))


def kernel(x_prompt, x_sample, state_ret, cache_k, cache_v, cache_kidx, w_in, w_out, w_up, w_down,
           g_pre_mix, g_post_mix, g_pre_ffn, g_post_ffn, rel_bias):
    depth = w_in.shape[0]
    bp = x_prompt.shape[0]
    zero_state = jnp.zeros((bp, RET_HEADS, RET_DK, RET_DV), F32)
    yp, ys = x_prompt, x_sample
    outs_p, outs_s = [], []
    for l in range(depth):
        row = lambda g: g[l].reshape(1, D_MODEL).astype(F32)
        weights = (_permute_w_in(w_in[l]), w_out[l].astype(BF16), w_up[l].astype(BF16),
                   w_down[l].astype(BF16), row(g_pre_mix), row(g_post_mix), row(g_pre_ffn),
                   row(g_post_ffn))
        yp, *rest_p = _layer(yp, zero_state, None, weights, rel_bias)
        ys, *rest_s = _layer(ys, state_ret[l], (cache_k[l], cache_v[l], cache_kidx[l]), weights, rel_bias)
        outs_p.append(rest_p)
        outs_s.append(rest_s)
    stack = lambda outs, i: jnp.stack([o[i] for o in outs])
    return (yp, ys,
            stack(outs_p, 0), stack(outs_p, 1), stack(outs_p, 2), stack(outs_p, 3),
            stack(outs_s, 0), stack(outs_s, 1), stack(outs_s, 2), stack(outs_s, 3))
```

```python
import functools
import math

import jax
import jax.numpy as jnp
import numpy as np
from jax import lax
from jax.experimental import pallas as pl
from jax.experimental.pallas import tpu as pltpu

D_MODEL = 1024
CHUNK = 64
RET_HEADS = 8
RET_DK = 64
RET_DV = 64
ATT_HEADS = 8
ATT_KV_HEADS = 2
ATT_HD = 64
IDX_HEADS = 8
IDX_DIM = 64
TOPK_MAX = 256
N_BUCKETS = 32
MAX_DISTANCE = 128
D_FF = 4 * D_MODEL
ROPE_BASE = 10000.0
EPS = 1e-6

RET_W = RET_HEADS * RET_DV
ATT_W = ATT_HEADS * ATT_HD
KV_W = ATT_KV_HEADS * ATT_HD
IDX_W = IDX_HEADS * IDX_DIM
HEADS_PER_KV = ATT_HEADS // ATT_KV_HEADS

MAIN_W = 4 * RET_W + ATT_W + IDX_W
TAIL_W = 3 * 128
IW_OFF = IDX_DIM

LANE = 128
VT_ROWS = 80
VT_ALL = ATT_KV_HEADS * VT_ROWS
TK = 256
NEG = -1e30
CNT_ROWS = 32
NARROW = 2.0 ** -20
VMEM_LIMIT = 56 * 1024 * 1024

F32 = jnp.float32
BF16 = jnp.bfloat16
NT_DIMS = (((1,), (1,)), ((), ()))
TN_DIMS = (((0,), (0,)), ((), ()))


def _const_spec(shape):
    nd = len(shape)
    return pl.BlockSpec(shape, lambda *_: (0,) * nd, pipeline_mode=pl.Buffered(1))


def _rms(x, gain):
    return x * lax.rsqrt(jnp.mean(x * x, axis=-1, keepdims=True) + EPS) * gain


def _inproj_kernel(x_ref, g_ref, w_ref, main_ref, kc_ref, vc_ref, ic_ref, iw_ref,
                   *tile_refs, tm):
    a = _rms(x_ref[...], g_ref[...]).astype(BF16)
    for c in range(MAIN_W // 512):
        main_ref[:, c * 512:(c + 1) * 512] = jnp.dot(
            a, w_ref[:, c * 512:(c + 1) * 512], preferred_element_type=F32)
    tail = jnp.dot(a, w_ref[:, MAIN_W:MAIN_W + TAIL_W], preferred_element_type=F32)
    ak = tail[:, 0:KV_W]
    av = tail[:, KV_W:2 * KV_W]
    last = tail[:, 2 * KV_W:3 * KV_W]
    kc_ref[...] = ak
    vc_ref[...] = av
    ic_ref[...] = last[:, :IDX_DIM]
    iw_ref[...] = last
    if not tile_refs:
        return
    k3_ref, ki3_ref, vt3_ref = tile_refs
    avt = av.T
    row = lax.broadcasted_iota(jnp.int32, (VT_ROWS - ATT_HD, tm), 0)
    ones_rows = jnp.where(row == 0, 1.0, 0.0).astype(F32)
    vt = jnp.concatenate([avt[:ATT_HD], ones_rows, avt[ATT_HD:], ones_rows], axis=0).astype(BF16)
    for j in range(tm // TK):
        k3_ref[j] = ak[j * TK:(j + 1) * TK].astype(BF16)
        ki3_ref[j] = last[j * TK:(j + 1) * TK, :IDX_DIM].astype(BF16)
        vt3_ref[j] = vt[:, j * TK:(j + 1) * TK]


def _inproj(x2, gain, w_perm, tm, key_tiles):
    rows = x2.shape[0]
    grid = (rows // tm,)
    row_spec = lambda w: pl.BlockSpec((tm, w), lambda i: (i, 0))
    out_shape = (
        jax.ShapeDtypeStruct((rows, MAIN_W), F32),
        jax.ShapeDtypeStruct((rows, KV_W), F32),
        jax.ShapeDtypeStruct((rows, KV_W), F32),
        jax.ShapeDtypeStruct((rows, IDX_DIM), F32),
        jax.ShapeDtypeStruct((rows, LANE), F32),
    )
    out_specs = (row_spec(MAIN_W), row_spec(KV_W), row_spec(KV_W), row_spec(IDX_DIM), row_spec(LANE))
    if key_tiles:
        t3 = lambda a, b: pl.BlockSpec((tm // TK, a, b), lambda i: (i, 0, 0))
        out_shape += (jax.ShapeDtypeStruct((rows // TK, TK, KV_W), BF16),
                      jax.ShapeDtypeStruct((rows // TK, TK, IDX_DIM), BF16),
                      jax.ShapeDtypeStruct((rows // TK, VT_ALL, TK), BF16))
        out_specs += (t3(TK, KV_W), t3(TK, IDX_DIM), t3(VT_ALL, TK))
    return pl.pallas_call(
        functools.partial(_inproj_kernel, tm=tm),
        grid=grid,
        in_specs=[row_spec(D_MODEL), _const_spec((1, D_MODEL)), _const_spec(w_perm.shape)],
        out_specs=out_specs,
        out_shape=out_shape,
        compiler_params=pltpu.CompilerParams(dimension_semantics=("parallel",),
                                             vmem_limit_bytes=VMEM_LIMIT),
        name="inproj",
    )(x2, gain, w_perm)


def _ret_gammas():
    return [1.0 - 2.0 ** (-5.0 - h) for h in range(RET_HEADS)]


def _ret_tables(sb, chunk):
    lg = np.log(np.array(_ret_gammas(), np.float64))
    t = np.arange(sb)
    ci = t // chunk
    diff = t[:, None] - t[None, :]
    same = ci[:, None] == ci[None, :]
    below = ci[None, :] < ci[:, None]
    expo = np.where(same, np.abs(diff), np.where(below, diff, 0)).astype(np.float64)
    dmat = np.exp(lg[:, None, None] * expo[None]) * (same | below)[None]
    qd = np.exp(lg[None, :] * (t + 1.0)[:, None])
    kd = np.exp(lg[None, :] * (sb - 1.0 - t)[:, None])
    qd = np.repeat(qd, RET_DK, axis=1)
    kd = np.repeat(kd, RET_DK, axis=1)
    return (jnp.asarray(dmat, F32), jnp.asarray(qd, F32), jnp.asarray(kd, F32),
            [float(math.exp(v * sb)) for v in lg])


def _ret_kernel(q_ref, k_ref, v_ref, g_ref, cos_ref, sin_ref, d_ref, qd_ref, kd_ref, s0_ref,
                o_ref, sfin_ref, s_scr, *, g_block):
    n = pl.program_id(1)

    @pl.when(n == 0)
    def _():
        s_scr[...] = s0_ref[0]

    reps = RET_W // LANE
    cos = jnp.concatenate([cos_ref[...]] * reps, axis=1)
    sin = jnp.concatenate([sin_ref[...]] * reps, axis=1)
    lane = lax.broadcasted_iota(jnp.int32, cos.shape, 1)
    first_half = (lane & (RET_DK - 1)) < RET_DK // 2

    def rot(x):
        partner = jnp.where(first_half, pltpu.roll(x, RET_W - RET_DK // 2, 1),
                            pltpu.roll(x, RET_DK // 2, 1))
        return x * cos + partner * sin

    q = rot(q_ref[...])
    k = rot(k_ref[...]) * (RET_DK ** -0.5)
    v = v_ref[...].astype(BF16)
    gate = g_ref[...]
    qb = q.astype(BF16)
    kb = k.astype(BF16)
    qx = (q * qd_ref[...]).astype(BF16)
    kx = (k * kd_ref[...]).astype(BF16)
    outs = []
    for h in range(RET_HEADS):
        sl = slice(h * RET_DK, (h + 1) * RET_DK)
        s = lax.dot_general(qb[:, sl], kb[:, sl], NT_DIMS, preferred_element_type=F32)
        p = (s * d_ref[h]).astype(BF16)
        st = s_scr[h]
        o = jnp.dot(p, v[:, sl], preferred_element_type=F32)
        o = o + jnp.dot(qx[:, sl], st.astype(BF16), preferred_element_type=F32)
        s_scr[h] = g_block[h] * st + lax.dot_general(kx[:, sl], v[:, sl], TN_DIMS,
                                                     preferred_element_type=F32)
        o = o * lax.rsqrt(jnp.mean(o * o, axis=-1, keepdims=True) + EPS)
        outs.append(o)
    o_all = jnp.concatenate(outs, axis=1)
    o_ref[...] = gate * (1.0 / (1.0 + jnp.exp(-gate))) * o_all

    @pl.when(n == pl.num_programs(1) - 1)
    def _():
        sfin_ref[0] = s_scr[...]


def _retention(main, s0, cos_t, sin_t, batch, t_len, sb, chunk):
    nsb = t_len // sb
    dmat, qd, kd, g_block = _ret_tables(sb, chunk)
    col = lambda c: pl.BlockSpec((sb, RET_W), lambda b, n, c=c: (b * nsb + n, c))
    tab = pl.BlockSpec((sb, LANE), lambda b, n: (n, 0))
    st_spec = pl.BlockSpec((1, RET_HEADS, RET_DK, RET_DV), lambda b, n: (b, 0, 0, 0))
    return pl.pallas_call(
        functools.partial(_ret_kernel, g_block=g_block),
        grid=(batch, nsb),
        in_specs=[col(0), col(1), col(2), col(3), tab, tab,
                  _const_spec(dmat.shape), _const_spec(qd.shape), _const_spec(kd.shape), st_spec],
        out_specs=(pl.BlockSpec((sb, RET_W), lambda b, n: (b * nsb + n, 0)), st_spec),
        out_shape=(jax.ShapeDtypeStruct((batch * t_len, RET_W), F32),
                   jax.ShapeDtypeStruct((batch, RET_HEADS, RET_DK, RET_DV), F32)),
        scratch_shapes=[pltpu.VMEM((RET_HEADS, RET_DK, RET_DV), F32)],
        compiler_params=pltpu.CompilerParams(dimension_semantics=("parallel", "arbitrary"),
                                             vmem_limit_bytes=VMEM_LIMIT),
        name="retention",
    )(main, main, main, main, cos_t, sin_t, dmat, qd, kd, s0)


def _attn_kernel(aq_ref, iq_ref, iw_ref, ki_ref, k_ref, vt_ref, bt_ref, o_ref,
                 s_scr, acc_scr, m_scr, *, tq, nkt_static, last_valid, n_valid_q, n_sel,
                 max_search, n_index):
    qblk = pl.program_id(1)
    nkt = qblk + 1 if nkt_static is None else nkt_static
    lane_f = lax.broadcasted_iota(jnp.int32, (1, tq), 1).astype(F32)
    lane_ok = lane_f < float(n_valid_q)
    klim = jnp.minimum((jnp.floor(lane_f * (1.0 / CHUNK)) + 1.0) * CHUNK, float(last_valid))
    krow = lax.broadcasted_iota(jnp.int32, (TK, tq), 0).astype(F32)
    adm_last = krow < klim
    nkt_f = nkt.astype(F32) if nkt_static is None else float(nkt)
    n_adm = (nkt_f - 1.0) * TK + klim
    k_target = jnp.minimum(float(n_sel), n_adm)

    iq = iq_ref[...]
    w_t = iw_ref[...].T[IW_OFF:IW_OFF + IDX_HEADS, :] * (IDX_HEADS ** -0.5 * IDX_DIM ** -0.5)
    iq_all = jnp.concatenate(
        [iq[:, h * IDX_DIM:(h + 1) * IDX_DIM].astype(BF16) for h in range(IDX_HEADS)], axis=0)

    def score_tile(kt):
        s_all = lax.dot_general(ki_ref[kt], iq_all, NT_DIMS, preferred_element_type=F32)
        acc = jnp.maximum(s_all[:, :tq], 0.0) * w_t[0:1, :]
        for h in range(1, IDX_HEADS):
            acc = acc + jnp.maximum(s_all[:, h * tq:(h + 1) * tq], 0.0) * w_t[h:h + 1, :]
        return acc

    def p1_body(kt, carry):
        rmax, rmin = carry
        sc = score_tile(kt)
        s_scr[kt] = sc
        return (jnp.maximum(rmax, jnp.max(sc, axis=0, keepdims=True)),
                jnp.minimum(rmin, jnp.min(sc, axis=0, keepdims=True)))

    init = (jnp.full((1, tq), -jnp.inf, F32), jnp.full((1, tq), jnp.inf, F32))
    rmax, rmin = lax.fori_loop(0, nkt - 1, p1_body, init)
    sc = score_tile(nkt - 1)
    s_scr[nkt - 1] = jnp.where(adm_last, sc, -jnp.inf)
    rmax = jnp.maximum(rmax, jnp.max(jnp.where(adm_last, sc, -jnp.inf), axis=0, keepdims=True))
    rmin = jnp.minimum(rmin, jnp.min(jnp.where(adm_last, sc, jnp.inf), axis=0, keepdims=True))

    def fold_rows(x, op=jnp.add):
        parts = [x[i * CNT_ROWS:(i + 1) * CNT_ROWS] for i in range(TK // CNT_ROWS)]
        while len(parts) > 1:
            parts = [op(a, b) for a, b in zip(parts[0::2], parts[1::2])]
        return parts[0]

    def count_where(pred):
        def body(kt, c):
            return c + fold_rows(jnp.where(pred(kt), 1.0, 0.0))
        c = lax.fori_loop(0, nkt, body, jnp.zeros((CNT_ROWS, tq), F32))
        return jnp.sum(c, axis=0, keepdims=True)

    def count_ge(thr):
        return count_where(lambda kt: s_scr[kt] >= thr)

    span = jnp.maximum(jnp.maximum(rmax - rmin, jnp.abs(rmax)), 1e-30)
    hi0 = rmax + span * (2.0 ** -10)

    def max_below(bound, n_tiles):
        def body(kt, m):
            s = s_scr[kt]
            return jnp.maximum(m, fold_rows(jnp.where(s < bound, s, -jnp.inf), jnp.maximum))
        m = lax.fori_loop(0, n_tiles, body, jnp.full((CNT_ROWS, tq), -jnp.inf, F32))
        return jnp.max(m, axis=0, keepdims=True)

    def search_body(st):
        it, _, lo, hi, c_lo, c_hi, done = st
        width = hi - lo
        frac = (c_lo - k_target - 0.5) / (c_lo - c_hi)
        interp = ((it + 1) % 2).astype(F32)
        mid_i = lo + width * (0.5 + interp * (frac - 0.5))
        mid_b = lo + 0.5 * width
        mid = jnp.where((mid_i > lo) & (mid_i < hi), mid_i, mid_b)
        active = done == 0.0
        narrow = active & (jnp.logical_not((mid > lo) & (mid < hi)) | (width <= span * NARROW))
        any_narrow = jnp.max(jnp.where(narrow, 1.0, 0.0))
        top = max_below(hi, jnp.where(any_narrow > 0.0, nkt, 0))
        mid = jnp.where(narrow, top, mid)
        c = count_ge(mid)
        ge = c >= k_target
        up = active & ge
        dn = active & jnp.logical_not(ge)
        lo = jnp.where(up, mid, lo)
        c_lo = jnp.where(up, c, c_lo)
        hi = jnp.where(dn, mid, hi)
        c_hi = jnp.where(dn, c, c_hi)
        done = jnp.where((c_lo == k_target) | (narrow & up), 1.0, done)
        return it + 1, jnp.min(done), lo, hi, c_lo, c_hi, done

    done0 = jnp.where((n_adm == k_target) | jnp.logical_not(lane_ok), 1.0, 0.0)
    st0 = (jnp.int32(0), jnp.float32(0.0), rmin, hi0, n_adm, jnp.zeros((1, tq), F32), done0)
    _, _, lo, hi, cnt_lo, _, _ = lax.while_loop(
        lambda st: (st[0] < max_search) & (st[1] < 1.0), search_body, st0)

    excess = jnp.max(jnp.where(lane_ok, cnt_lo - k_target, 0.0))

    @pl.when(excess > 0.0)
    def _():
        need = k_target - count_ge(hi)

        def tie_upto(jm):
            def pred(kt):
                s = s_scr[kt]
                return (s >= lo) & (s < hi) & (krow + lax.convert_element_type(kt, F32) * TK <= jm)
            return pred

        def idx_body(_, carry):
            jlo, jhi = carry
            jm = jnp.floor(0.5 * (jlo + jhi))
            ok = count_where(tie_upto(jm)) >= need
            return jnp.where(ok, jlo, jm), jnp.where(ok, jm, jhi)

        last_idx = jnp.zeros((1, tq), F32) + (nkt_f * TK - 1.0)
        _, jstar = lax.fori_loop(0, n_index, idx_body, (jnp.full((1, tq), -1.0, F32), last_idx))

        def drop_body(kt, c):
            s = s_scr[kt]
            idx = krow + lax.convert_element_type(kt, F32) * TK
            s_scr[kt] = jnp.where((s >= lo) & (s < hi) & (idx > jstar), -jnp.inf, s)
            return c
        lax.fori_loop(0, nkt, drop_body, 0)

    aq = aq_ref[...] * (ATT_HD ** -0.5)
    zeros_q = jnp.zeros((tq, ATT_HD), F32)
    q_pad = []
    for h in range(ATT_HEADS):
        qh = aq[:, h * ATT_HD:(h + 1) * ATT_HD]
        parts = [zeros_q] * ATT_KV_HEADS
        parts[h // HEADS_PER_KV] = qh
        q_pad.append(jnp.concatenate(parts, axis=1).astype(BF16))
    q_all = jnp.concatenate(q_pad, axis=0)

    m_scr[...] = jnp.full(m_scr.shape, NEG, F32)
    acc_scr[...] = jnp.zeros(acc_scr.shape, F32)

    def attend(kt, bias_idx):
        neg_mask = jnp.where(s_scr[kt] >= lo, 0.0, NEG)
        lg_all = lax.dot_general(k_ref[kt], q_all, NT_DIMS, preferred_element_type=F32)
        p_parts, alphas = [], []
        for h in range(ATT_HEADS):
            add = neg_mask if bias_idx is None else neg_mask + bt_ref[h, bias_idx]
            lg = lg_all[:, h * tq:(h + 1) * tq] + add
            m_old = m_scr[h:h + 1, :]
            m_new = jnp.maximum(m_old, jnp.max(lg, axis=0, keepdims=True))
            alphas.append(jnp.exp(m_old - m_new))
            p_parts.append(jnp.exp(lg - m_new).astype(BF16))
            m_scr[h:h + 1, :] = m_new
        for g in range(ATT_KV_HEADS):
            hs = range(g * HEADS_PER_KV, (g + 1) * HEADS_PER_KV)
            p_g = jnp.concatenate([p_parts[h] for h in hs], axis=1)
            a_g = jnp.concatenate([alphas[h] for h in hs], axis=1)
            vt_g = vt_ref[kt, g * VT_ROWS:(g + 1) * VT_ROWS, :]
            acc_scr[g] = acc_scr[g] * a_g + jnp.dot(vt_g, p_g, preferred_element_type=F32)

    def far_body(kt, c):
        attend(kt, None)
        return c
    lax.fori_loop(0, nkt - 2, far_body, 0)

    if nkt_static is None:
        @pl.when(nkt >= 2)
        def _():
            attend(nkt - 2, 0)
    elif nkt_static >= 2:
        attend(nkt - 2, 0)
    attend(nkt - 1, 1)

    outs = []
    for g in range(ATT_KV_HEADS):
        a = acc_scr[g]
        o_g = a[:ATT_HD] * (1.0 / a[ATT_HD:ATT_HD + 1])
        outs += [o_g[:, r * tq:(r + 1) * tq] for r in range(HEADS_PER_KV)]
    o_ref[...] = jnp.concatenate(outs, axis=0).T


def _attention(aq_src, iq_src, iw, ki3, k3, vt3, bt, batch, n_qblk, tq, nkt_total, nkt_static,
               last_valid, n_valid_q, n_sel, aq_col, iq_col):
    kern = functools.partial(_attn_kernel, tq=tq, nkt_static=nkt_static, last_valid=last_valid,
                             n_valid_q=n_valid_q, n_sel=n_sel, max_search=96,
                             n_index=int(math.log2(nkt_total * TK)) + 2)
    qspec = lambda c: pl.BlockSpec((tq, ATT_W), lambda b, q, c=c: (b * n_qblk + q, c))
    kspec = lambda a, c: pl.BlockSpec((nkt_total, a, c), lambda b, q: (b, 0, 0))
    return pl.pallas_call(
        kern,
        grid=(batch, n_qblk),
        in_specs=[qspec(aq_col), qspec(iq_col),
                  pl.BlockSpec((tq, LANE), lambda b, q: (b * n_qblk + q, 0)),
                  kspec(TK, IDX_DIM), kspec(TK, KV_W), kspec(VT_ALL, TK),
                  _const_spec(bt.shape)],
        out_specs=pl.BlockSpec((tq, ATT_W), lambda b, q: (b * n_qblk + q, 0)),
        out_shape=jax.ShapeDtypeStruct((batch * n_qblk * tq, ATT_W), F32),
        scratch_shapes=[pltpu.VMEM((nkt_total, TK, tq), F32),
                        pltpu.VMEM((ATT_KV_HEADS, VT_ROWS, HEADS_PER_KV * tq), F32),
                        pltpu.VMEM((ATT_HEADS, tq), F32)],
        compiler_params=pltpu.CompilerParams(dimension_semantics=("parallel", "arbitrary"),
                                             vmem_limit_bytes=VMEM_LIMIT),
        name="attention",
    )(aq_src, iq_src, iw, ki3, k3, vt3, bt)


def _ffn_kernel(x_ref, oret_ref, oatt_ref, wout_ref, wup_ref, wdown_ref,
                gpost_ref, gpre_ref, gffn_ref, y_ref, *, ff_chunk):
    mix = jnp.dot(oret_ref[...].astype(BF16), wout_ref[:RET_W, :], preferred_element_type=F32)
    mix = mix + jnp.dot(oatt_ref[...].astype(BF16), wout_ref[RET_W:, :], preferred_element_type=F32)
    h = x_ref[...] + _rms(mix, gpost_ref[...])
    a = _rms(h, gpre_ref[...]).astype(BF16)
    f = jnp.zeros(h.shape, F32)
    for c in range(D_FF // ff_chunk):
        sl = slice(c * ff_chunk, (c + 1) * ff_chunk)
        u = jnp.dot(a, wup_ref[:, sl], preferred_element_type=F32)
        u = jnp.square(jnp.maximum(u, 0.0)).astype(BF16)
        f = f + jnp.dot(u, wdown_ref[sl, :], preferred_element_type=F32)
    y_ref[...] = h + _rms(f, gffn_ref[...])


def _out_ffn(x2, o_ret, o_att, w_out, w_up, w_down, g_post, g_pre, g_ffn, tm):
    rows = x2.shape[0]
    row_spec = lambda w: pl.BlockSpec((tm, w), lambda i: (i, 0))
    return pl.pallas_call(
        functools.partial(_ffn_kernel, ff_chunk=512),
        grid=(rows // tm,),
        in_specs=[row_spec(D_MODEL), row_spec(RET_W), row_spec(ATT_W),
                  _const_spec(w_out.shape), _const_spec(w_up.shape), _const_spec(w_down.shape),
                  _const_spec((1, D_MODEL)), _const_spec((1, D_MODEL)), _const_spec((1, D_MODEL))],
        out_specs=row_spec(D_MODEL),
        out_shape=jax.ShapeDtypeStruct((rows, D_MODEL), F32),
        compiler_params=pltpu.CompilerParams(dimension_semantics=("parallel",),
                                             vmem_limit_bytes=VMEM_LIMIT),
        name="out_ffn",
    )(x2, o_ret, o_att, w_out, w_up, w_down, g_post, g_pre, g_ffn)


def _t5_bucket(rel):
    half = N_BUCKETS // 2
    max_exact = half // 2
    ret = jnp.where(rel > 0, half, 0)
    n = jnp.abs(rel)
    nf = jnp.maximum(n, 1).astype(F32)
    large = max_exact + (jnp.log(nf / max_exact) / math.log(MAX_DISTANCE / max_exact)
                         * (half - max_exact)).astype(jnp.int32)
    large = jnp.minimum(large, half - 1)
    return ret + jnp.where(n < max_exact, n, large)


def _bias_tiles(rel_bias, tq):
    period = 2 * TK + tq
    rel = jnp.arange(period, dtype=jnp.int32) - (TK + tq - 1)
    table = rel_bias.astype(F32)
    far = table[_t5_bucket(jnp.int32(-MAX_DISTANCE))]
    u = (table[_t5_bucket(jnp.clip(rel, -MAX_DISTANCE, MAX_DISTANCE))] - far).T
    n = jnp.tile(u, (1, tq))[:, :tq * (period - 1)].reshape(ATT_HEADS, tq, period - 1)
    m = n[:, :, tq - 1:tq - 1 + 2 * TK]
    return jnp.transpose(m.reshape(ATT_HEADS, tq, 2, TK), (0, 2, 3, 1))


def _rope_tables(pos):
    half = RET_DK // 2
    inv = ROPE_BASE ** (-jnp.arange(half, dtype=F32) / half)
    ang = pos.astype(F32)[:, None] * inv[None, :]
    cos, sin = jnp.cos(ang), jnp.sin(ang)
    cos_h = jnp.concatenate([cos, cos], axis=1)
    sin_h = jnp.concatenate([-sin, sin], axis=1)
    reps = LANE // RET_DK
    return jnp.tile(cos_h, (1, reps)), jnp.tile(sin_h, (1, reps))


def _permute_w_in(w_in):
    offs = np.cumsum([0, 512, 512, 512, 512, 512, 128, 128, 512, 64, 8])
    seg = lambda i: w_in[:, offs[i]:offs[i + 1]]
    pad = jnp.zeros((D_MODEL, LANE - IDX_DIM - IDX_HEADS), w_in.dtype)
    return jnp.concatenate([seg(0), seg(1), seg(2), seg(3), seg(4), seg(7),
                            seg(5), seg(6), seg(8), seg(9), pad], axis=1).astype(BF16)


def _layer(x, s0, past, weights, rel_bias):
    w_perm, w_out, w_up, w_down, g_pre_mix, g_post_mix, g_pre_ffn, g_post_ffn = weights
    batch, t_len, _ = x.shape
    rows = batch * t_len
    x2 = x.reshape(rows, D_MODEL)
    tm = min(512, rows)
    main, kc, vc, ic, iw, *key_tiles = _inproj(x2, g_pre_mix, w_perm, tm, key_tiles=past is None)

    p_len = 0 if past is None else past[0].shape[1]
    pos = p_len + jnp.arange(t_len, dtype=jnp.int32)
    cos_t, sin_t = _rope_tables(pos)
    chunk = min(CHUNK, t_len)
    sb = min(256, t_len)
    o_ret, s_new = _retention(main, s0, cos_t, sin_t, batch, t_len, sb, chunk)

    l_all = p_len + t_len
    n_sel = min(TOPK_MAX, l_all // 4)
    if past is None:
        tq = TK
        n_qblk = t_len // tq
        nkt_total = t_len // TK
        bt = _bias_tiles(rel_bias, tq)
        k3, ki3, vt3 = key_tiles
        o_att = _attention(main, main, iw, ki3, k3, vt3, bt, batch, n_qblk, tq, nkt_total, None,
                           TK, tq, n_sel, aq_col=4, iq_col=5)
    else:
        tq = LANE
        pk, pv, pi = past
        assert p_len % TK == 0 and t_len <= min(tq, TK), (p_len, t_len)
        nkt_total = p_len // TK + 1
        padk = nkt_total * TK - l_all
        cat = lambda old, new: jnp.concatenate(
            [old.astype(BF16), new.reshape(batch, t_len, -1).astype(BF16),
             jnp.zeros((batch, padk, new.shape[-1]), BF16)], axis=1)
        k_all = cat(pk.reshape(batch, p_len, KV_W), kc)
        v_all = cat(pv.reshape(batch, p_len, KV_W), vc)
        i_all = cat(pi, ic)
        v_t = jnp.transpose(v_all.reshape(batch, nkt_total, TK, ATT_KV_HEADS, ATT_HD), (0, 1, 3, 4, 2))
        ones = jnp.zeros((batch, nkt_total, ATT_KV_HEADS, VT_ROWS - ATT_HD, TK), BF16).at[:, :, :, 0, :].set(1.0)
        vt_s = jnp.concatenate([v_t, ones], axis=3).reshape(batch * nkt_total, VT_ALL, TK)
        k_s = k_all.reshape(batch * nkt_total, TK, KV_W)
        i_s = i_all.reshape(batch * nkt_total, TK, IDX_DIM)
        padq = lambda a: jnp.pad(a.reshape(batch, t_len, -1), ((0, 0), (0, tq - t_len), (0, 0))
                                 ).reshape(batch * tq, -1)
        aq_s = padq(main[:, 4 * RET_W:4 * RET_W + ATT_W])
        iq_s = padq(main[:, 4 * RET_W + ATT_W:])
        iw_s = padq(iw)
        bt = _bias_tiles(rel_bias, tq)
        o_pad = _attention(aq_s, iq_s, iw_s, i_s, k_s, vt_s, bt, batch, 1, tq, nkt_total, nkt_total,
                           l_all - p_len, t_len, n_sel, aq_col=0, iq_col=0)
        o_att = o_pad.reshape(batch, tq, ATT_W)[:, :t_len].reshape(rows, ATT_W)

    y = _out_ffn(x2, o_ret, o_att, w_out, w_up, w_down, g_post_mix, g_pre_ffn, g_post_ffn, tm)
    return (y.reshape(batch, t_len, D_MODEL), s_new,
            kc.reshape(batch, t_len, ATT_KV_HEADS, ATT_HD),
            vc.reshape(batch, t_len, ATT_KV_HEADS, ATT_HD),
            ic.reshape(batch, t_len, IDX_DIM))


def kernel(x_prompt, x_sample, state_ret, cache_k, cache_v, cache_kidx, w_in, w_out, w_up, w_down,
           g_pre_mix, g_post_mix, g_pre_ffn, g_post_ffn, rel_bias):
    depth = w_in.shape[0]
    bp = x_prompt.shape[0]
    zero_state = jnp.zeros((bp, RET_HEADS, RET_DK, RET_DV), F32)
    yp, ys = x_prompt, x_sample
    outs_p, outs_s = [], []
    for l in range(depth):
        row = lambda g: g[l].reshape(1, D_MODEL).astype(F32)
        weights = (_permute_w_in(w_in[l]), w_out[l].astype(BF16), w_up[l].astype(BF16),
                   w_down[l].astype(BF16), row(g_pre_mix), row(g_post_mix), row(g_pre_ffn),
                   row(g_post_ffn))
        yp, *rest_p = _layer(yp, zero_state, None, weights, rel_bias)
        ys, *rest_s = _layer(ys, state_ret[l], (cache_k[l], cache_v[l], cache_kidx[l]), weights, rel_bias)
        outs_p.append(rest_p)
        outs_s.append(rest_s)
    stack = lambda outs, i: jnp.stack([o[i] for o in outs])
    return (yp, ys,
            stack(outs_p, 0), stack(outs_p, 1), stack(outs_p, 2), stack(outs_p, 3),
            stack(outs_s, 0), stack(outs_s, 1), stack(outs_s, 2), stack(outs_s, 3))
```

```python
import functools
import math

import jax
import jax.numpy as jnp
import numpy as np
from jax import lax
from jax.experimental import pallas as pl
from jax.experimental.pallas import tpu as pltpu

D_MODEL = 1024
CHUNK = 64
RET_HEADS = 8
RET_DK = 64
RET_DV = 64
ATT_HEADS = 8
ATT_KV_HEADS = 2
ATT_HD = 64
IDX_HEADS = 8
IDX_DIM = 64
TOPK_MAX = 256
N_BUCKETS = 32
MAX_DISTANCE = 128
D_FF = 4 * D_MODEL
ROPE_BASE = 10000.0
EPS = 1e-6

RET_W = RET_HEADS * RET_DV
ATT_W = ATT_HEADS * ATT_HD
KV_W = ATT_KV_HEADS * ATT_HD
IDX_W = IDX_HEADS * IDX_DIM
HEADS_PER_KV = ATT_HEADS // ATT_KV_HEADS

MAIN_W = 4 * RET_W + ATT_W + IDX_W
TAIL_W = 3 * 128
IW_OFF = IDX_DIM

LANE = 128
VT_ROWS = 80
VT_ALL = ATT_KV_HEADS * VT_ROWS
TK = 256
NEG = -(2.0 ** 100)
BF16_ROWS = 16
CNT_ROWS = 32
NARROW = 2.0 ** -20
VMEM_LIMIT = 56 * 1024 * 1024

F32 = jnp.float32
BF16 = jnp.bfloat16
NT_DIMS = (((1,), (1,)), ((), ()))
TN_DIMS = (((0,), (0,)), ((), ()))


def _const_spec(shape):
    nd = len(shape)
    return pl.BlockSpec(shape, lambda *_: (0,) * nd, pipeline_mode=pl.Buffered(1))


def _rms(x, gain):
    return x * lax.rsqrt(jnp.mean(x * x, axis=-1, keepdims=True) + EPS) * gain


def _inproj_kernel(x_ref, g_ref, w_ref, main_ref, kc_ref, vc_ref, ic_ref, iw_ref,
                   *tile_refs, tm):
    a = _rms(x_ref[...], g_ref[...]).astype(BF16)
    for c in range(MAIN_W // 512):
        main_ref[:, c * 512:(c + 1) * 512] = jnp.dot(
            a, w_ref[:, c * 512:(c + 1) * 512], preferred_element_type=F32)
    tail = jnp.dot(a, w_ref[:, MAIN_W:MAIN_W + TAIL_W], preferred_element_type=F32)
    ak = tail[:, 0:KV_W]
    av = tail[:, KV_W:2 * KV_W]
    last = tail[:, 2 * KV_W:3 * KV_W]
    kc_ref[...] = ak
    vc_ref[...] = av
    ic_ref[...] = last[:, :IDX_DIM]
    iw_ref[...] = last
    if not tile_refs:
        return
    k3_ref, ki3_ref, vt3_ref = tile_refs
    avt = av.T
    row = lax.broadcasted_iota(jnp.int32, (VT_ROWS - ATT_HD, tm), 0)
    ones_rows = jnp.where(row == 0, 1.0, 0.0).astype(F32)
    vt = jnp.concatenate([avt[:ATT_HD], ones_rows, avt[ATT_HD:], ones_rows], axis=0).astype(BF16)
    for j in range(tm // TK):
        k3_ref[j] = ak[j * TK:(j + 1) * TK].astype(BF16)
        ki3_ref[j] = last[j * TK:(j + 1) * TK, :IDX_DIM].astype(BF16)
        vt3_ref[j] = vt[:, j * TK:(j + 1) * TK]


def _inproj(x2, gain, w_perm, tm, key_tiles):
    rows = x2.shape[0]
    grid = (rows // tm,)
    row_spec = lambda w: pl.BlockSpec((tm, w), lambda i: (i, 0))
    out_shape = (
        jax.ShapeDtypeStruct((rows, MAIN_W), F32),
        jax.ShapeDtypeStruct((rows, KV_W), F32),
        jax.ShapeDtypeStruct((rows, KV_W), F32),
        jax.ShapeDtypeStruct((rows, IDX_DIM), F32),
        jax.ShapeDtypeStruct((rows, LANE), F32),
    )
    out_specs = (row_spec(MAIN_W), row_spec(KV_W), row_spec(KV_W), row_spec(IDX_DIM), row_spec(LANE))
    if key_tiles:
        t3 = lambda a, b: pl.BlockSpec((tm // TK, a, b), lambda i: (i, 0, 0))
        out_shape += (jax.ShapeDtypeStruct((rows // TK, TK, KV_W), BF16),
                      jax.ShapeDtypeStruct((rows // TK, TK, IDX_DIM), BF16),
                      jax.ShapeDtypeStruct((rows // TK, VT_ALL, TK), BF16))
        out_specs += (t3(TK, KV_W), t3(TK, IDX_DIM), t3(VT_ALL, TK))
    return pl.pallas_call(
        functools.partial(_inproj_kernel, tm=tm),
        grid=grid,
        in_specs=[row_spec(D_MODEL), _const_spec((1, D_MODEL)), _const_spec(w_perm.shape)],
        out_specs=out_specs,
        out_shape=out_shape,
        compiler_params=pltpu.CompilerParams(dimension_semantics=("parallel",),
                                             vmem_limit_bytes=VMEM_LIMIT),
        name="inproj",
    )(x2, gain, w_perm)


def _ret_gammas():
    return [1.0 - 2.0 ** (-5.0 - h) for h in range(RET_HEADS)]


def _ret_tables(sb, chunk):
    lg = np.log(np.array(_ret_gammas(), np.float64))
    t = np.arange(sb)
    ci = t // chunk
    diff = t[:, None] - t[None, :]
    same = ci[:, None] == ci[None, :]
    below = ci[None, :] < ci[:, None]
    expo = np.where(same, np.abs(diff), np.where(below, diff, 0)).astype(np.float64)
    dmat = np.exp(lg[:, None, None] * expo[None]) * (same | below)[None]
    qd = np.exp(lg[None, :] * (t + 1.0)[:, None])
    kd = np.exp(lg[None, :] * (sb - 1.0 - t)[:, None])
    qd = np.repeat(qd, RET_DK, axis=1)
    kd = np.repeat(kd, RET_DK, axis=1)
    return (jnp.asarray(dmat, F32), jnp.asarray(qd, F32), jnp.asarray(kd, F32),
            [float(math.exp(v * sb)) for v in lg])


def _ret_kernel(q_ref, k_ref, v_ref, g_ref, cos_ref, sin_ref, d_ref, qd_ref, kd_ref, s0_ref,
                o_ref, sfin_ref, s_scr, *, g_block):
    n = pl.program_id(1)

    @pl.when(n == 0)
    def _():
        s_scr[...] = s0_ref[0]

    reps = RET_W // LANE
    cos = jnp.concatenate([cos_ref[...]] * reps, axis=1)
    sin = jnp.concatenate([sin_ref[...]] * reps, axis=1)
    lane = lax.broadcasted_iota(jnp.int32, cos.shape, 1)
    first_half = (lane & (RET_DK - 1)) < RET_DK // 2

    def rot(x):
        partner = jnp.where(first_half, pltpu.roll(x, RET_W - RET_DK // 2, 1),
                            pltpu.roll(x, RET_DK // 2, 1))
        return x * cos + partner * sin

    q = rot(q_ref[...])
    k = rot(k_ref[...]) * (RET_DK ** -0.5)
    v = v_ref[...].astype(BF16)
    gate = g_ref[...]
    qb = q.astype(BF16)
    kb = k.astype(BF16)
    qx = (q * qd_ref[...]).astype(BF16)
    kx = (k * kd_ref[...]).astype(BF16)
    outs = []
    for h in range(RET_HEADS):
        sl = slice(h * RET_DK, (h + 1) * RET_DK)
        s = lax.dot_general(qb[:, sl], kb[:, sl], NT_DIMS, preferred_element_type=F32)
        p = (s * d_ref[h]).astype(BF16)
        st = s_scr[h]
        o = jnp.dot(p, v[:, sl], preferred_element_type=F32)
        o = o + jnp.dot(qx[:, sl], st.astype(BF16), preferred_element_type=F32)
        s_scr[h] = g_block[h] * st + lax.dot_general(kx[:, sl], v[:, sl], TN_DIMS,
                                                     preferred_element_type=F32)
        o = o * lax.rsqrt(jnp.mean(o * o, axis=-1, keepdims=True) + EPS)
        outs.append(o)
    o_all = jnp.concatenate(outs, axis=1)
    o_ref[...] = gate * (1.0 / (1.0 + jnp.exp(-gate))) * o_all

    @pl.when(n == pl.num_programs(1) - 1)
    def _():
        sfin_ref[0] = s_scr[...]


def _retention(main, s0, cos_t, sin_t, batch, t_len, sb, chunk):
    nsb = t_len // sb
    dmat, qd, kd, g_block = _ret_tables(sb, chunk)
    col = lambda c: pl.BlockSpec((sb, RET_W), lambda b, n, c=c: (b * nsb + n, c))
    tab = pl.BlockSpec((sb, LANE), lambda b, n: (n, 0))
    st_spec = pl.BlockSpec((1, RET_HEADS, RET_DK, RET_DV), lambda b, n: (b, 0, 0, 0))
    return pl.pallas_call(
        functools.partial(_ret_kernel, g_block=g_block),
        grid=(batch, nsb),
        in_specs=[col(0), col(1), col(2), col(3), tab, tab,
                  _const_spec(dmat.shape), _const_spec(qd.shape), _const_spec(kd.shape), st_spec],
        out_specs=(pl.BlockSpec((sb, RET_W), lambda b, n: (b * nsb + n, 0)), st_spec),
        out_shape=(jax.ShapeDtypeStruct((batch * t_len, RET_W), F32),
                   jax.ShapeDtypeStruct((batch, RET_HEADS, RET_DK, RET_DV), F32)),
        scratch_shapes=[pltpu.VMEM((RET_HEADS, RET_DK, RET_DV), F32)],
        compiler_params=pltpu.CompilerParams(dimension_semantics=("parallel", "arbitrary"),
                                             vmem_limit_bytes=VMEM_LIMIT),
        name="retention",
    )(main, main, main, main, cos_t, sin_t, dmat, qd, kd, s0)


def _attn_kernel(aq_ref, iq_ref, iw_ref, ki_ref, k_ref, vt_ref, bt_ref, o_ref,
                 s_scr, acc_scr, m_scr, *, tq, nkt_static, last_valid, n_valid_q, n_sel,
                 max_search, n_index):
    qblk = pl.program_id(1)
    nkt = qblk + 1 if nkt_static is None else nkt_static
    lane_f = lax.broadcasted_iota(jnp.int32, (1, tq), 1).astype(F32)
    lane_ok = lane_f < float(n_valid_q)
    klim = jnp.minimum((jnp.floor(lane_f * (1.0 / CHUNK)) + 1.0) * CHUNK, float(last_valid))
    krow = lax.broadcasted_iota(jnp.int32, (TK, tq), 0).astype(F32)
    adm_last = krow < klim
    nkt_f = nkt.astype(F32) if nkt_static is None else float(nkt)
    n_adm = (nkt_f - 1.0) * TK + klim
    k_target = jnp.minimum(float(n_sel), n_adm)

    iq = iq_ref[...]
    w_t = iw_ref[...].T[IW_OFF:IW_OFF + IDX_HEADS, :] * (IDX_HEADS ** -0.5 * IDX_DIM ** -0.5)
    iq_all = jnp.concatenate(
        [iq[:, h * IDX_DIM:(h + 1) * IDX_DIM].astype(BF16) for h in range(IDX_HEADS)], axis=0)

    def score_tile(kt):
        s_all = lax.dot_general(ki_ref[kt], iq_all, NT_DIMS, preferred_element_type=F32)
        acc = jnp.maximum(s_all[:, :tq], 0.0) * w_t[0:1, :]
        for h in range(1, IDX_HEADS):
            acc = acc + jnp.maximum(s_all[:, h * tq:(h + 1) * tq], 0.0) * w_t[h:h + 1, :]
        return acc

    def p1_body(kt, carry):
        rmax, rmin = carry
        sc = score_tile(kt)
        s_scr[kt] = sc
        return (jnp.maximum(rmax, jnp.max(sc, axis=0, keepdims=True)),
                jnp.minimum(rmin, jnp.min(sc, axis=0, keepdims=True)))

    init = (jnp.full((1, tq), -jnp.inf, F32), jnp.full((1, tq), jnp.inf, F32))
    rmax, rmin = lax.fori_loop(0, nkt - 1, p1_body, init)
    sc = score_tile(nkt - 1)
    s_scr[nkt - 1] = jnp.where(adm_last, sc, -jnp.inf)
    rmax = jnp.maximum(rmax, jnp.max(jnp.where(adm_last, sc, -jnp.inf), axis=0, keepdims=True))
    rmin = jnp.minimum(rmin, jnp.min(jnp.where(adm_last, sc, jnp.inf), axis=0, keepdims=True))

    def fold_rows(x, op=jnp.add):
        parts = [x[i * CNT_ROWS:(i + 1) * CNT_ROWS] for i in range(TK // CNT_ROWS)]
        while len(parts) > 1:
            parts = [op(a, b) for a, b in zip(parts[0::2], parts[1::2])]
        return parts[0]

    def count_where(pred):
        def body(kt, c):
            return c + fold_rows(jnp.where(pred(kt), 1.0, 0.0))
        c = lax.fori_loop(0, nkt, body, jnp.zeros((CNT_ROWS, tq), F32))
        return jnp.sum(c, axis=0, keepdims=True)

    def count_ge(thr):
        return count_where(lambda kt: s_scr[kt] >= thr)

    span = jnp.maximum(jnp.maximum(rmax - rmin, jnp.abs(rmax)), 1e-30)
    hi0 = rmax + span * (2.0 ** -10)

    def max_below(bound, n_tiles):
        def body(kt, m):
            s = s_scr[kt]
            return jnp.maximum(m, fold_rows(jnp.where(s < bound, s, -jnp.inf), jnp.maximum))
        m = lax.fori_loop(0, n_tiles, body, jnp.full((CNT_ROWS, tq), -jnp.inf, F32))
        return jnp.max(m, axis=0, keepdims=True)

    def plan(it, lo, hi, c_lo, c_hi, done):
        width = hi - lo
        frac = (c_lo - k_target - 0.5) / (c_lo - c_hi)
        interp = ((it + 1) % 2).astype(F32)
        mid_i = lo + width * (0.5 + interp * (frac - 0.5))
        mid_b = lo + 0.5 * width
        mid = jnp.where((mid_i > lo) & (mid_i < hi), mid_i, mid_b)
        active = done == 0.0
        narrow = active & (jnp.logical_not((mid > lo) & (mid < hi)) | (width <= span * NARROW))
        flag = jnp.max(jnp.where(narrow, 3.0, jnp.where(active, 1.0, 0.0)))
        return mid, jnp.where(narrow, 1.0, 0.0), flag

    def search_body(st):
        it, flag, lo, hi, c_lo, c_hi, done, mid, narrow_f = st
        narrow = narrow_f > 0.0
        top = max_below(hi, jnp.where(flag > 2.0, nkt, 0))
        mid = jnp.where(narrow, top, mid)
        c = count_ge(mid)
        active = done == 0.0
        ge = c >= k_target
        up = active & ge
        dn = active & jnp.logical_not(ge)
        lo = jnp.where(up, mid, lo)
        c_lo = jnp.where(up, c, c_lo)
        hi = jnp.where(dn, mid, hi)
        c_hi = jnp.where(dn, c, c_hi)
        done = jnp.where((c_lo == k_target) | (narrow & up), 1.0, done)
        mid, narrow_f, flag = plan(it + 1, lo, hi, c_lo, c_hi, done)
        return it + 1, flag, lo, hi, c_lo, c_hi, done, mid, narrow_f

    done0 = jnp.where((n_adm == k_target) | jnp.logical_not(lane_ok), 1.0, 0.0)
    c_hi0 = jnp.zeros((1, tq), F32)
    mid0, narrow0, flag0 = plan(jnp.int32(0), rmin, hi0, n_adm, c_hi0, done0)
    st0 = (jnp.int32(0), flag0, rmin, hi0, n_adm, c_hi0, done0, mid0, narrow0)
    _, _, lo, hi, cnt_lo, _, _, _, _ = lax.while_loop(
        lambda st: (st[0] < max_search) & (st[1] > 0.0), search_body, st0)

    excess = jnp.max(jnp.where(lane_ok, cnt_lo - k_target, 0.0))

    @pl.when(excess > 0.0)
    def _():
        need = k_target - count_ge(hi)

        def tie_upto(jm):
            def pred(kt):
                s = s_scr[kt]
                return (s >= lo) & (s < hi) & (krow + lax.convert_element_type(kt, F32) * TK <= jm)
            return pred

        def idx_body(_, carry):
            jlo, jhi = carry
            jm = jnp.floor(0.5 * (jlo + jhi))
            ok = count_where(tie_upto(jm)) >= need
            return jnp.where(ok, jlo, jm), jnp.where(ok, jm, jhi)

        last_idx = jnp.zeros((1, tq), F32) + (nkt_f * TK - 1.0)
        _, jstar = lax.fori_loop(0, n_index, idx_body, (jnp.full((1, tq), -1.0, F32), last_idx))

        def drop_body(kt, c):
            s = s_scr[kt]
            idx = krow + lax.convert_element_type(kt, F32) * TK
            s_scr[kt] = jnp.where((s >= lo) & (s < hi) & (idx > jstar), -jnp.inf, s)
            return c
        lax.fori_loop(0, nkt, drop_body, 0)

    aq = aq_ref[...] * (ATT_HD ** -0.5)
    zeros_q = jnp.zeros((tq, ATT_HD), F32)
    q_pad = []
    for h in range(ATT_HEADS):
        qh = aq[:, h * ATT_HD:(h + 1) * ATT_HD]
        parts = [zeros_q] * ATT_KV_HEADS
        parts[h // HEADS_PER_KV] = qh
        q_pad.append(jnp.concatenate(parts, axis=1).astype(BF16))

    m_scr[...] = jnp.full(m_scr.shape, NEG, F32)
    acc_scr[...] = jnp.zeros(acc_scr.shape, F32)

    def col_max(x):
        parts = [x[i * BF16_ROWS:(i + 1) * BF16_ROWS] for i in range(x.shape[0] // BF16_ROWS)]
        while len(parts) > 1:
            parts = [jnp.maximum(a, b) for a, b in zip(parts[0::2], parts[1::2])]
        return jnp.max(parts[0].astype(F32), axis=0, keepdims=True)

    def attend(kt, n_tiles, near):
        rows = n_tiles * TK
        tiles = [kt + j for j in range(n_tiles)]
        cat = lambda xs, axis: xs[0] if len(xs) == 1 else jnp.concatenate(xs, axis=axis)
        s_t = cat([s_scr[t] for t in tiles], 0)
        k_t = cat([k_ref[t] for t in tiles], 0)
        neg_mask = jnp.where(s_t >= lo, 0.0, NEG).astype(BF16)
        p_parts, alphas = [], []
        for h in range(ATT_HEADS):
            lg = lax.dot_general(k_t, q_pad[h], NT_DIMS, preferred_element_type=F32)
            if near:
                lg = lg + cat([bt_ref[h, 2 - n_tiles + j] for j in range(n_tiles)], 0)
            lg = lg.astype(BF16) + neg_mask
            m_old = m_scr[h:h + 1, :]
            m_new = jnp.maximum(m_old, col_max(lg))
            alphas.append(jnp.exp(m_old - m_new))
            p_parts.append(jnp.exp(lg - m_new.astype(BF16)))
            m_scr[h:h + 1, :] = m_new
        for g in range(ATT_KV_HEADS):
            hs = range(g * HEADS_PER_KV, (g + 1) * HEADS_PER_KV)
            p_g = jnp.concatenate([p_parts[h] for h in hs], axis=1)
            a_g = jnp.concatenate([alphas[h] for h in hs], axis=1)
            vt_g = cat([vt_ref[t, g * VT_ROWS:(g + 1) * VT_ROWS, :] for t in tiles], 1)
            acc_scr[g] = acc_scr[g] * a_g + jnp.dot(vt_g, p_g, preferred_element_type=F32)

    n_far = nkt - 2

    def far_body(i, c):
        attend(2 * i, 2, False)
        return c
    if nkt_static is None:
        lax.fori_loop(0, n_far // 2, far_body, 0)

        @pl.when((n_far > 0) & (n_far % 2 == 1))
        def _():
            attend(n_far - 1, 1, False)

        @pl.when(nkt >= 2)
        def _():
            attend(nkt - 2, 2, True)

        @pl.when(nkt < 2)
        def _():
            attend(0, 1, True)
    else:
        lax.fori_loop(0, max(n_far, 0) // 2, far_body, 0)
        if n_far > 0 and n_far % 2 == 1:
            attend(n_far - 1, 1, False)
        if nkt >= 2:
            attend(nkt - 2, 2, True)
        else:
            attend(0, 1, True)

    outs = []
    for g in range(ATT_KV_HEADS):
        a = acc_scr[g]
        o_g = a[:ATT_HD] * (1.0 / a[ATT_HD:ATT_HD + 1])
        outs += [o_g[:, r * tq:(r + 1) * tq] for r in range(HEADS_PER_KV)]
    o_ref[...] = jnp.concatenate(outs, axis=0).T


def _attention(aq_src, iq_src, iw, ki3, k3, vt3, bt, batch, n_qblk, tq, nkt_total, nkt_static,
               last_valid, n_valid_q, n_sel, aq_col, iq_col):
    kern = functools.partial(_attn_kernel, tq=tq, nkt_static=nkt_static, last_valid=last_valid,
                             n_valid_q=n_valid_q, n_sel=n_sel, max_search=96,
                             n_index=int(math.log2(nkt_total * TK)) + 2)
    qspec = lambda c: pl.BlockSpec((tq, ATT_W), lambda b, q, c=c: (b * n_qblk + q, c))
    kspec = lambda a, c: pl.BlockSpec((nkt_total, a, c), lambda b, q: (b, 0, 0))
    return pl.pallas_call(
        kern,
        grid=(batch, n_qblk),
        in_specs=[qspec(aq_col), qspec(iq_col),
                  pl.BlockSpec((tq, LANE), lambda b, q: (b * n_qblk + q, 0)),
                  kspec(TK, IDX_DIM), kspec(TK, KV_W), kspec(VT_ALL, TK),
                  _const_spec(bt.shape)],
        out_specs=pl.BlockSpec((tq, ATT_W), lambda b, q: (b * n_qblk + q, 0)),
        out_shape=jax.ShapeDtypeStruct((batch * n_qblk * tq, ATT_W), F32),
        scratch_shapes=[pltpu.VMEM((nkt_total, TK, tq), F32),
                        pltpu.VMEM((ATT_KV_HEADS, VT_ROWS, HEADS_PER_KV * tq), F32),
                        pltpu.VMEM((ATT_HEADS, tq), F32)],
        compiler_params=pltpu.CompilerParams(dimension_semantics=("parallel", "arbitrary"),
                                             vmem_limit_bytes=VMEM_LIMIT),
        name="attention",
    )(aq_src, iq_src, iw, ki3, k3, vt3, bt)


def _ffn_kernel(x_ref, oret_ref, oatt_ref, wout_ref, wup_ref, wdown_ref,
                gpost_ref, gpre_ref, gffn_ref, y_ref, *, ff_chunk):
    mix = jnp.dot(oret_ref[...].astype(BF16), wout_ref[:RET_W, :], preferred_element_type=F32)
    mix = mix + jnp.dot(oatt_ref[...].astype(BF16), wout_ref[RET_W:, :], preferred_element_type=F32)
    h = x_ref[...] + _rms(mix, gpost_ref[...])
    a = _rms(h, gpre_ref[...]).astype(BF16)
    f = jnp.zeros(h.shape, F32)
    for c in range(D_FF // ff_chunk):
        sl = slice(c * ff_chunk, (c + 1) * ff_chunk)
        u = jnp.dot(a, wup_ref[:, sl], preferred_element_type=F32)
        u = jnp.square(jnp.maximum(u, 0.0)).astype(BF16)
        f = f + jnp.dot(u, wdown_ref[sl, :], preferred_element_type=F32)
    y_ref[...] = h + _rms(f, gffn_ref[...])


def _out_ffn(x2, o_ret, o_att, w_out, w_up, w_down, g_post, g_pre, g_ffn, tm):
    rows = x2.shape[0]
    row_spec = lambda w: pl.BlockSpec((tm, w), lambda i: (i, 0))
    return pl.pallas_call(
        functools.partial(_ffn_kernel, ff_chunk=512),
        grid=(rows // tm,),
        in_specs=[row_spec(D_MODEL), row_spec(RET_W), row_spec(ATT_W),
                  _const_spec(w_out.shape), _const_spec(w_up.shape), _const_spec(w_down.shape),
                  _const_spec((1, D_MODEL)), _const_spec((1, D_MODEL)), _const_spec((1, D_MODEL))],
        out_specs=row_spec(D_MODEL),
        out_shape=jax.ShapeDtypeStruct((rows, D_MODEL), F32),
        compiler_params=pltpu.CompilerParams(dimension_semantics=("parallel",),
                                             vmem_limit_bytes=VMEM_LIMIT),
        name="out_ffn",
    )(x2, o_ret, o_att, w_out, w_up, w_down, g_post, g_pre, g_ffn)


def _t5_bucket(rel):
    half = N_BUCKETS // 2
    max_exact = half // 2
    ret = jnp.where(rel > 0, half, 0)
    n = jnp.abs(rel)
    nf = jnp.maximum(n, 1).astype(F32)
    large = max_exact + (jnp.log(nf / max_exact) / math.log(MAX_DISTANCE / max_exact)
                         * (half - max_exact)).astype(jnp.int32)
    large = jnp.minimum(large, half - 1)
    return ret + jnp.where(n < max_exact, n, large)


def _bias_tiles(rel_bias, tq):
    period = 2 * TK + tq
    rel = jnp.arange(period, dtype=jnp.int32) - (TK + tq - 1)
    table = rel_bias.astype(F32)
    far = table[_t5_bucket(jnp.int32(-MAX_DISTANCE))]
    u = (table[_t5_bucket(jnp.clip(rel, -MAX_DISTANCE, MAX_DISTANCE))] - far).T
    n = jnp.tile(u, (1, tq))[:, :tq * (period - 1)].reshape(ATT_HEADS, tq, period - 1)
    m = n[:, :, tq - 1:tq - 1 + 2 * TK]
    return jnp.transpose(m.reshape(ATT_HEADS, tq, 2, TK), (0, 2, 3, 1))


def _rope_tables(pos):
    half = RET_DK // 2
    inv = ROPE_BASE ** (-jnp.arange(half, dtype=F32) / half)
    ang = pos.astype(F32)[:, None] * inv[None, :]
    cos, sin = jnp.cos(ang), jnp.sin(ang)
    cos_h = jnp.concatenate([cos, cos], axis=1)
    sin_h = jnp.concatenate([-sin, sin], axis=1)
    reps = LANE // RET_DK
    return jnp.tile(cos_h, (1, reps)), jnp.tile(sin_h, (1, reps))


def _permute_w_in(w_in):
    offs = np.cumsum([0, 512, 512, 512, 512, 512, 128, 128, 512, 64, 8])
    seg = lambda i: w_in[:, offs[i]:offs[i + 1]]
    pad = jnp.zeros((D_MODEL, LANE - IDX_DIM - IDX_HEADS), w_in.dtype)
    return jnp.concatenate([seg(0), seg(1), seg(2), seg(3), seg(4), seg(7),
                            seg(5), seg(6), seg(8), seg(9), pad], axis=1).astype(BF16)


def _layer(x, s0, past, weights, rel_bias):
    w_perm, w_out, w_up, w_down, g_pre_mix, g_post_mix, g_pre_ffn, g_post_ffn = weights
    batch, t_len, _ = x.shape
    rows = batch * t_len
    x2 = x.reshape(rows, D_MODEL)
    tm = min(512, rows)
    main, kc, vc, ic, iw, *key_tiles = _inproj(x2, g_pre_mix, w_perm, tm, key_tiles=past is None)

    p_len = 0 if past is None else past[0].shape[1]
    pos = p_len + jnp.arange(t_len, dtype=jnp.int32)
    cos_t, sin_t = _rope_tables(pos)
    chunk = min(CHUNK, t_len)
    sb = min(256, t_len)
    o_ret, s_new = _retention(main, s0, cos_t, sin_t, batch, t_len, sb, chunk)

    l_all = p_len + t_len
    n_sel = min(TOPK_MAX, l_all // 4)
    if past is None:
        tq = TK
        n_qblk = t_len // tq
        nkt_total = t_len // TK
        bt = _bias_tiles(rel_bias, tq)
        k3, ki3, vt3 = key_tiles
        o_att = _attention(main, main, iw, ki3, k3, vt3, bt, batch, n_qblk, tq, nkt_total, None,
                           TK, tq, n_sel, aq_col=4, iq_col=5)
    else:
        tq = LANE
        pk, pv, pi = past
        assert p_len % TK == 0 and t_len <= min(tq, TK), (p_len, t_len)
        nkt_total = p_len // TK + 1
        padk = nkt_total * TK - l_all
        cat = lambda old, new: jnp.concatenate(
            [old.astype(BF16), new.reshape(batch, t_len, -1).astype(BF16),
             jnp.zeros((batch, padk, new.shape[-1]), BF16)], axis=1)
        k_all = cat(pk.reshape(batch, p_len, KV_W), kc)
        v_all = cat(pv.reshape(batch, p_len, KV_W), vc)
        i_all = cat(pi, ic)
        v_t = jnp.transpose(v_all.reshape(batch, nkt_total, TK, ATT_KV_HEADS, ATT_HD), (0, 1, 3, 4, 2))
        ones = jnp.zeros((batch, nkt_total, ATT_KV_HEADS, VT_ROWS - ATT_HD, TK), BF16).at[:, :, :, 0, :].set(1.0)
        vt_s = jnp.concatenate([v_t, ones], axis=3).reshape(batch * nkt_total, VT_ALL, TK)
        k_s = k_all.reshape(batch * nkt_total, TK, KV_W)
        i_s = i_all.reshape(batch * nkt_total, TK, IDX_DIM)
        padq = lambda a: jnp.pad(a.reshape(batch, t_len, -1), ((0, 0), (0, tq - t_len), (0, 0))
                                 ).reshape(batch * tq, -1)
        aq_s = padq(main[:, 4 * RET_W:4 * RET_W + ATT_W])
        iq_s = padq(main[:, 4 * RET_W + ATT_W:])
        iw_s = padq(iw)
        bt = _bias_tiles(rel_bias, tq)
        o_pad = _attention(aq_s, iq_s, iw_s, i_s, k_s, vt_s, bt, batch, 1, tq, nkt_total, nkt_total,
                           l_all - p_len, t_len, n_sel, aq_col=0, iq_col=0)
        o_att = o_pad.reshape(batch, tq, ATT_W)[:, :t_len].reshape(rows, ATT_W)

    y = _out_ffn(x2, o_ret, o_att, w_out, w_up, w_down, g_post_mix, g_pre_ffn, g_post_ffn, tm)
    return (y.reshape(batch, t_len, D_MODEL), s_new,
            kc.reshape(batch, t_len, ATT_KV_HEADS, ATT_HD),
            vc.reshape(batch, t_len, ATT_KV_HEADS, ATT_HD),
            ic.reshape(batch, t_len, IDX_DIM))


def kernel(x_prompt, x_sample, state_ret, cache_k, cache_v, cache_kidx, w_in, w_out, w_up, w_down,
           g_pre_mix, g_post_mix, g_pre_ffn, g_post_ffn, rel_bias):
    depth = w_in.shape[0]
    bp = x_prompt.shape[0]
    zero_state = jnp.zeros((bp, RET_HEADS, RET_DK, RET_DV), F32)
    yp, ys = x_prompt, x_sample
    outs_p, outs_s = [], []
    for l in range(depth):
        row = lambda g: g[l].reshape(1, D_MODEL).astype(F32)
        weights = (_permute_w_in(w_in[l]), w_out[l].astype(BF16), w_up[l].astype(BF16),
                   w_down[l].astype(BF16), row(g_pre_mix), row(g_post_mix), row(g_pre_ffn),
                   row(g_post_ffn))
        yp, *rest_p = _layer(yp, zero_state, None, weights, rel_bias)
        ys, *rest_s = _layer(ys, state_ret[l], (cache_k[l], cache_v[l], cache_kidx[l]), weights, rel_bias)
        outs_p.append(rest_p)
        outs_s.append(rest_s)
    stack = lambda outs, i: jnp.stack([o[i] for o in outs])
    return (yp, ys,
            stack(outs_p, 0), stack(outs_p, 1), stack(outs_p, 2), stack(outs_p, 3),
            stack(outs_s, 0), stack(outs_s, 1), stack(outs_s, 2), stack(outs_s, 3))
```

```python
import functools
import math

import jax
import jax.numpy as jnp
import numpy as np
from jax import lax
from jax.experimental import pallas as pl
from jax.experimental.pallas import tpu as pltpu

D_MODEL = 1024
CHUNK = 64
RET_HEADS = 8
RET_DK = 64
RET_DV = 64
ATT_HEADS = 8
ATT_KV_HEADS = 2
ATT_HD = 64
IDX_HEADS = 8
IDX_DIM = 64
TOPK_MAX = 256
N_BUCKETS = 32
MAX_DISTANCE = 128
D_FF = 4 * D_MODEL
ROPE_BASE = 10000.0
EPS = 1e-6

RET_W = RET_HEADS * RET_DV
ATT_W = ATT_HEADS * ATT_HD
KV_W = ATT_KV_HEADS * ATT_HD
IDX_W = IDX_HEADS * IDX_DIM
HEADS_PER_KV = ATT_HEADS // ATT_KV_HEADS

MAIN_W = 4 * RET_W + ATT_W + IDX_W
TAIL_W = 3 * 128
IW_OFF = IDX_DIM

LANE = 128
VT_ROWS = 80
VT_ALL = ATT_KV_HEADS * VT_ROWS
TK = 256
NEG = -(2.0 ** 100)
BF16_ROWS = 16
CNT_ROWS = 32
NARROW = 2.0 ** -20
MAX_SEARCH = 96
F32_TINY = float(np.finfo(np.float32).tiny)
VMEM_LIMIT = 56 * 1024 * 1024

F32 = jnp.float32
BF16 = jnp.bfloat16
NT_DIMS = (((1,), (1,)), ((), ()))
TN_DIMS = (((0,), (0,)), ((), ()))


def _const_spec(shape):
    nd = len(shape)
    return pl.BlockSpec(shape, lambda *_: (0,) * nd, pipeline_mode=pl.Buffered(1))


def _rms(x, gain):
    return x * lax.rsqrt(jnp.mean(x * x, axis=-1, keepdims=True) + EPS) * gain


def _inproj_kernel(x_ref, g_ref, w_ref, main_ref, kc_ref, vc_ref, ic_ref, iw_ref,
                   *tile_refs, tm):
    a = _rms(x_ref[...], g_ref[...]).astype(BF16)
    for c in range(MAIN_W // 512):
        main_ref[:, c * 512:(c + 1) * 512] = jnp.dot(
            a, w_ref[:, c * 512:(c + 1) * 512], preferred_element_type=F32)
    tail = jnp.dot(a, w_ref[:, MAIN_W:MAIN_W + TAIL_W], preferred_element_type=F32)
    ak = tail[:, 0:KV_W]
    av = tail[:, KV_W:2 * KV_W]
    last = tail[:, 2 * KV_W:3 * KV_W]
    kc_ref[...] = ak
    vc_ref[...] = av
    ic_ref[...] = last[:, :IDX_DIM]
    iw_ref[...] = last
    if not tile_refs:
        return
    k3_ref, ki3_ref, vt3_ref = tile_refs
    avt = av.T
    row = lax.broadcasted_iota(jnp.int32, (VT_ROWS - ATT_HD, tm), 0)
    ones_rows = jnp.where(row == 0, 1.0, 0.0).astype(F32)
    vt = jnp.concatenate([avt[:ATT_HD], ones_rows, avt[ATT_HD:], ones_rows], axis=0).astype(BF16)
    for j in range(tm // TK):
        k3_ref[j] = ak[j * TK:(j + 1) * TK].astype(BF16)
        ki3_ref[j] = last[j * TK:(j + 1) * TK, :IDX_DIM].astype(BF16)
        vt3_ref[j] = vt[:, j * TK:(j + 1) * TK]


def _inproj(x2, gain, w_perm, tm, key_tiles):
    rows = x2.shape[0]
    grid = (rows // tm,)
    row_spec = lambda w: pl.BlockSpec((tm, w), lambda i: (i, 0))
    out_shape = (
        jax.ShapeDtypeStruct((rows, MAIN_W), F32),
        jax.ShapeDtypeStruct((rows, KV_W), F32),
        jax.ShapeDtypeStruct((rows, KV_W), F32),
        jax.ShapeDtypeStruct((rows, IDX_DIM), F32),
        jax.ShapeDtypeStruct((rows, LANE), F32),
    )
    out_specs = (row_spec(MAIN_W), row_spec(KV_W), row_spec(KV_W), row_spec(IDX_DIM), row_spec(LANE))
    if key_tiles:
        t3 = lambda a, b: pl.BlockSpec((tm // TK, a, b), lambda i: (i, 0, 0))
        out_shape += (jax.ShapeDtypeStruct((rows // TK, TK, KV_W), BF16),
                      jax.ShapeDtypeStruct((rows // TK, TK, IDX_DIM), BF16),
                      jax.ShapeDtypeStruct((rows // TK, VT_ALL, TK), BF16))
        out_specs += (t3(TK, KV_W), t3(TK, IDX_DIM), t3(VT_ALL, TK))
    return pl.pallas_call(
        functools.partial(_inproj_kernel, tm=tm),
        grid=grid,
        in_specs=[row_spec(D_MODEL), _const_spec((1, D_MODEL)), _const_spec(w_perm.shape)],
        out_specs=out_specs,
        out_shape=out_shape,
        compiler_params=pltpu.CompilerParams(dimension_semantics=("parallel",),
                                             vmem_limit_bytes=VMEM_LIMIT),
        name="inproj",
    )(x2, gain, w_perm)


def _ret_gammas():
    return [1.0 - 2.0 ** (-5.0 - h) for h in range(RET_HEADS)]


def _ret_tables(sb, chunk):
    lg = np.log(np.array(_ret_gammas(), np.float64))
    t = np.arange(sb)
    ci = t // chunk
    diff = t[:, None] - t[None, :]
    same = ci[:, None] == ci[None, :]
    below = ci[None, :] < ci[:, None]
    expo = np.where(same, np.abs(diff), np.where(below, diff, 0)).astype(np.float64)
    dmat = np.exp(lg[:, None, None] * expo[None]) * (same | below)[None]
    qd = np.exp(lg[None, :] * (t + 1.0)[:, None])
    kd = np.exp(lg[None, :] * (sb - 1.0 - t)[:, None])
    qd = np.repeat(qd, RET_DK, axis=1)
    kd = np.repeat(kd, RET_DK, axis=1)
    return (jnp.asarray(dmat, F32), jnp.asarray(qd, F32), jnp.asarray(kd, F32),
            [float(math.exp(v * sb)) for v in lg])


def _ret_kernel(q_ref, k_ref, v_ref, g_ref, cos_ref, sin_ref, d_ref, qd_ref, kd_ref, s0_ref,
                o_ref, sfin_ref, s_scr, *, g_block):
    n = pl.program_id(1)

    @pl.when(n == 0)
    def _():
        s_scr[...] = s0_ref[0]

    reps = RET_W // LANE
    cos = jnp.concatenate([cos_ref[...]] * reps, axis=1)
    sin = jnp.concatenate([sin_ref[...]] * reps, axis=1)
    lane = lax.broadcasted_iota(jnp.int32, cos.shape, 1)
    first_half = (lane & (RET_DK - 1)) < RET_DK // 2

    def rot(x):
        partner = jnp.where(first_half, pltpu.roll(x, RET_W - RET_DK // 2, 1),
                            pltpu.roll(x, RET_DK // 2, 1))
        return x * cos + partner * sin

    q = rot(q_ref[...])
    k = rot(k_ref[...]) * (RET_DK ** -0.5)
    v = v_ref[...].astype(BF16)
    gate = g_ref[...]
    qb = q.astype(BF16)
    kb = k.astype(BF16)
    qx = (q * qd_ref[...]).astype(BF16)
    kx = (k * kd_ref[...]).astype(BF16)
    outs = []
    for h in range(RET_HEADS):
        sl = slice(h * RET_DK, (h + 1) * RET_DK)
        s = lax.dot_general(qb[:, sl], kb[:, sl], NT_DIMS, preferred_element_type=F32)
        p = (s * d_ref[h]).astype(BF16)
        st = s_scr[h]
        o = jnp.dot(p, v[:, sl], preferred_element_type=F32)
        o = o + jnp.dot(qx[:, sl], st.astype(BF16), preferred_element_type=F32)
        s_scr[h] = g_block[h] * st + lax.dot_general(kx[:, sl], v[:, sl], TN_DIMS,
                                                     preferred_element_type=F32)
        o = o * lax.rsqrt(jnp.mean(o * o, axis=-1, keepdims=True) + EPS)
        outs.append(o)
    o_all = jnp.concatenate(outs, axis=1)
    o_ref[...] = gate * (1.0 / (1.0 + jnp.exp(-gate))) * o_all

    @pl.when(n == pl.num_programs(1) - 1)
    def _():
        sfin_ref[0] = s_scr[...]


def _retention(main, s0, cos_t, sin_t, batch, t_len, sb, chunk):
    nsb = t_len // sb
    dmat, qd, kd, g_block = _ret_tables(sb, chunk)
    col = lambda c: pl.BlockSpec((sb, RET_W), lambda b, n, c=c: (b * nsb + n, c))
    tab = pl.BlockSpec((sb, LANE), lambda b, n: (n, 0))
    st_spec = pl.BlockSpec((1, RET_HEADS, RET_DK, RET_DV), lambda b, n: (b, 0, 0, 0))
    return pl.pallas_call(
        functools.partial(_ret_kernel, g_block=g_block),
        grid=(batch, nsb),
        in_specs=[col(0), col(1), col(2), col(3), tab, tab,
                  _const_spec(dmat.shape), _const_spec(qd.shape), _const_spec(kd.shape), st_spec],
        out_specs=(pl.BlockSpec((sb, RET_W), lambda b, n: (b * nsb + n, 0)), st_spec),
        out_shape=(jax.ShapeDtypeStruct((batch * t_len, RET_W), F32),
                   jax.ShapeDtypeStruct((batch, RET_HEADS, RET_DK, RET_DV), F32)),
        scratch_shapes=[pltpu.VMEM((RET_HEADS, RET_DK, RET_DV), F32)],
        compiler_params=pltpu.CompilerParams(dimension_semantics=("parallel", "arbitrary"),
                                             vmem_limit_bytes=VMEM_LIMIT),
        name="retention",
    )(main, main, main, main, cos_t, sin_t, dmat, qd, kd, s0)


def _attn_kernel(aq_ref, iq_ref, iw_ref, ki_ref, k_ref, vt_ref, bt_ref, o_ref,
                 s_scr, acc_scr, m_scr, *, tq, nkt_static, last_valid, n_valid_q, n_sel,
                 max_search):
    qblk = pl.program_id(1)
    nkt = qblk + 1 if nkt_static is None else nkt_static
    lane_f = lax.broadcasted_iota(jnp.int32, (1, tq), 1).astype(F32)
    lane_ok = lane_f < float(n_valid_q)
    klim = jnp.minimum((jnp.floor(lane_f * (1.0 / CHUNK)) + 1.0) * CHUNK, float(last_valid))
    krow = lax.broadcasted_iota(jnp.int32, (TK, tq), 0).astype(F32)
    adm_last = krow < klim
    nkt_f = nkt.astype(F32) if nkt_static is None else float(nkt)
    n_adm = (nkt_f - 1.0) * TK + klim
    k_target = jnp.minimum(float(n_sel), n_adm)

    iq = iq_ref[...]
    w_t = iw_ref[...].T[IW_OFF:IW_OFF + IDX_HEADS, :] * (IDX_HEADS ** -0.5 * IDX_DIM ** -0.5)
    iq_all = jnp.concatenate(
        [iq[:, h * IDX_DIM:(h + 1) * IDX_DIM].astype(BF16) for h in range(IDX_HEADS)], axis=0)

    def score_tile(kt):
        s_all = lax.dot_general(ki_ref[kt], iq_all, NT_DIMS, preferred_element_type=F32)
        acc = jnp.maximum(s_all[:, :tq], 0.0) * w_t[0:1, :]
        for h in range(1, IDX_HEADS):
            acc = acc + jnp.maximum(s_all[:, h * tq:(h + 1) * tq], 0.0) * w_t[h:h + 1, :]
        return acc

    def p1_body(kt, carry):
        rmax, rmin = carry
        sc = score_tile(kt)
        s_scr[kt] = sc
        return (jnp.maximum(rmax, jnp.max(sc, axis=0, keepdims=True)),
                jnp.minimum(rmin, jnp.min(sc, axis=0, keepdims=True)))

    init = (jnp.full((1, tq), -jnp.inf, F32), jnp.full((1, tq), jnp.inf, F32))
    rmax, rmin = lax.fori_loop(0, nkt - 1, p1_body, init)
    sc = score_tile(nkt - 1)
    s_scr[nkt - 1] = jnp.where(adm_last, sc, -jnp.inf)
    rmax = jnp.maximum(rmax, jnp.max(jnp.where(adm_last, sc, -jnp.inf), axis=0, keepdims=True))
    rmin = jnp.minimum(rmin, jnp.min(jnp.where(adm_last, sc, jnp.inf), axis=0, keepdims=True))

    def fold_rows(x, op=jnp.add):
        parts = [x[i * CNT_ROWS:(i + 1) * CNT_ROWS] for i in range(TK // CNT_ROWS)]
        while len(parts) > 1:
            parts = [op(a, b) for a, b in zip(parts[0::2], parts[1::2])]
        return parts[0]

    def count_where(pred):
        def body(kt, c):
            return c + fold_rows(jnp.where(pred(kt), 1.0, 0.0))
        c = lax.fori_loop(0, nkt, body, jnp.zeros((CNT_ROWS, tq), F32))
        return jnp.sum(c, axis=0, keepdims=True)

    def count_ge(thr):
        return count_where(lambda kt: s_scr[kt] >= thr)

    span = jnp.maximum(jnp.maximum(rmax - rmin, jnp.abs(rmax)), 1e-30)
    hi0 = rmax + span * (2.0 ** -10)

    def max_below(bound, n_tiles):
        def body(kt, m):
            s = s_scr[kt]
            return jnp.maximum(m, fold_rows(jnp.where(s < bound, s, -jnp.inf), jnp.maximum))
        m = lax.fori_loop(0, n_tiles, body, jnp.full((CNT_ROWS, tq), -jnp.inf, F32))
        return jnp.max(m, axis=0, keepdims=True)

    def plan(it, lo, hi, c_lo, c_hi, done):
        width = hi - lo
        frac = (c_lo - k_target - 0.5) / (c_lo - c_hi)
        interp = ((it + 1) % 2).astype(F32)
        mid_i = lo + width * (0.5 + interp * (frac - 0.5))
        mid_b = lo + 0.5 * width
        mid = jnp.where((mid_i > lo) & (mid_i < hi), mid_i, mid_b)
        active = done == 0.0
        narrow = active & (jnp.logical_not((mid > lo) & (mid < hi)) | (width <= span * NARROW))
        flag = jnp.max(jnp.where(narrow, 3.0, jnp.where(active, 1.0, 0.0)))
        return mid, jnp.where(narrow, 1.0, 0.0), flag

    def search_body(st):
        it, flag, lo, hi, c_lo, c_hi, done, mid, narrow_f = st
        narrow = narrow_f > 0.0
        top = max_below(hi, jnp.where(flag > 2.0, nkt, 0))
        mid = jnp.where(narrow, top, mid)
        c = count_ge(mid)
        active = done == 0.0
        ge = c >= k_target
        up = active & ge
        dn = active & jnp.logical_not(ge)
        lo = jnp.where(up, mid, lo)
        c_lo = jnp.where(up, c, c_lo)
        hi = jnp.where(dn, mid, hi)
        c_hi = jnp.where(dn, c, c_hi)
        done = jnp.where((c_lo == k_target) | (narrow & up), 1.0, done)
        mid, narrow_f, flag = plan(it + 1, lo, hi, c_lo, c_hi, done)
        return it + 1, flag, lo, hi, c_lo, c_hi, done, mid, narrow_f

    c_zero = count_ge(0.0)
    c_pos = count_ge(F32_TINY)
    pos_side = c_pos >= k_target
    at_zero = jnp.logical_not(pos_side) & (c_zero >= k_target)
    pick = lambda p, z, n: jnp.where(pos_side, p, jnp.where(at_zero, z, n))
    lo0 = pick(F32_TINY, 0.0, rmin)
    hi0 = pick(hi0, F32_TINY, 0.0)
    c_lo0 = pick(c_pos, c_zero, n_adm)
    c_hi0 = pick(0.0, c_pos, c_zero)
    done0 = jnp.where((n_adm == k_target) | at_zero | jnp.logical_not(lane_ok), 1.0, 0.0)
    mid0, narrow0, flag0 = plan(jnp.int32(0), lo0, hi0, c_lo0, c_hi0, done0)
    st0 = (jnp.int32(0), flag0, lo0, hi0, c_lo0, c_hi0, done0, mid0, narrow0)
    _, _, lo, hi, cnt_lo, _, _, _, _ = lax.while_loop(
        lambda st: (st[0] < max_search) & (st[1] > 0.0), search_body, st0)

    excess = jnp.max(jnp.where(lane_ok, cnt_lo - k_target, 0.0))

    @pl.when(excess > 0.0)
    def _():
        need = k_target - count_ge(hi)
        r = lax.broadcasted_iota(jnp.int32, (TK, TK), 0)
        c = lax.broadcasted_iota(jnp.int32, (TK, TK), 1)
        prefix = jnp.where(r >= c, 1.0, 0.0).astype(BF16)

        def drop_body(kt, before):
            s = s_scr[kt]
            tie = (s >= lo) & (s < hi)
            tie_f = jnp.where(tie, 1.0, 0.0)
            rank = before + jnp.dot(prefix, tie_f.astype(BF16), preferred_element_type=F32)
            s_scr[kt] = jnp.where(tie & (rank > need), -jnp.inf, s)
            return before + jnp.sum(tie_f, axis=0, keepdims=True)
        lax.fori_loop(0, nkt, drop_body, jnp.zeros((1, tq), F32))

    aq = aq_ref[...] * (ATT_HD ** -0.5)
    zeros_q = jnp.zeros((tq, ATT_HD), F32)
    q_pad = []
    for h in range(ATT_HEADS):
        qh = aq[:, h * ATT_HD:(h + 1) * ATT_HD]
        parts = [zeros_q] * ATT_KV_HEADS
        parts[h // HEADS_PER_KV] = qh
        q_pad.append(jnp.concatenate(parts, axis=1).astype(BF16))

    m_scr[...] = jnp.full(m_scr.shape, NEG, F32)
    acc_scr[...] = jnp.zeros(acc_scr.shape, F32)

    def col_max(x):
        parts = [x[i * BF16_ROWS:(i + 1) * BF16_ROWS] for i in range(x.shape[0] // BF16_ROWS)]
        while len(parts) > 1:
            parts = [jnp.maximum(a, b) for a, b in zip(parts[0::2], parts[1::2])]
        return jnp.max(parts[0].astype(F32), axis=0, keepdims=True)

    def attend(kt, n_tiles, near):
        rows = n_tiles * TK
        tiles = [kt + j for j in range(n_tiles)]
        cat = lambda xs, axis: xs[0] if len(xs) == 1 else jnp.concatenate(xs, axis=axis)
        s_t = cat([s_scr[t] for t in tiles], 0)
        k_t = cat([k_ref[t] for t in tiles], 0)
        neg_mask = jnp.where(s_t >= lo, 0.0, NEG).astype(BF16)
        p_parts, alphas = [], []
        for h in range(ATT_HEADS):
            lg = lax.dot_general(k_t, q_pad[h], NT_DIMS, preferred_element_type=F32)
            if near:
                lg = lg + cat([bt_ref[h, 2 - n_tiles + j] for j in range(n_tiles)], 0)
            lg = lg.astype(BF16) + neg_mask
            m_old = m_scr[h:h + 1, :]
            m_new = jnp.maximum(m_old, col_max(lg))
            alphas.append(jnp.exp(m_old - m_new))
            p_parts.append(jnp.exp(lg - m_new.astype(BF16)))
            m_scr[h:h + 1, :] = m_new
        for g in range(ATT_KV_HEADS):
            hs = range(g * HEADS_PER_KV, (g + 1) * HEADS_PER_KV)
            p_g = jnp.concatenate([p_parts[h] for h in hs], axis=1)
            a_g = jnp.concatenate([alphas[h] for h in hs], axis=1)
            vt_g = cat([vt_ref[t, g * VT_ROWS:(g + 1) * VT_ROWS, :] for t in tiles], 1)
            acc_scr[g] = acc_scr[g] * a_g + jnp.dot(vt_g, p_g, preferred_element_type=F32)

    n_far = nkt - 2

    def far_body(i, c):
        attend(2 * i, 2, False)
        return c
    if nkt_static is None:
        lax.fori_loop(0, n_far // 2, far_body, 0)

        @pl.when((n_far > 0) & (n_far % 2 == 1))
        def _():
            attend(n_far - 1, 1, False)

        @pl.when(nkt >= 2)
        def _():
            attend(nkt - 2, 2, True)

        @pl.when(nkt < 2)
        def _():
            attend(0, 1, True)
    else:
        lax.fori_loop(0, max(n_far, 0) // 2, far_body, 0)
        if n_far > 0 and n_far % 2 == 1:
            attend(n_far - 1, 1, False)
        if nkt >= 2:
            attend(nkt - 2, 2, True)
        else:
            attend(0, 1, True)

    outs = []
    for g in range(ATT_KV_HEADS):
        a = acc_scr[g]
        o_g = a[:ATT_HD] * (1.0 / a[ATT_HD:ATT_HD + 1])
        outs += [o_g[:, r * tq:(r + 1) * tq] for r in range(HEADS_PER_KV)]
    o_ref[...] = jnp.concatenate(outs, axis=0).T


def _attention(aq_src, iq_src, iw, ki3, k3, vt3, bt, batch, n_qblk, tq, nkt_total, nkt_static,
               last_valid, n_valid_q, n_sel, aq_col, iq_col):
    kern = functools.partial(_attn_kernel, tq=tq, nkt_static=nkt_static, last_valid=last_valid,
                             n_valid_q=n_valid_q, n_sel=n_sel, max_search=MAX_SEARCH)
    qspec = lambda c: pl.BlockSpec((tq, ATT_W), lambda b, q, c=c: (b * n_qblk + q, c))
    kspec = lambda a, c: pl.BlockSpec((nkt_total, a, c), lambda b, q: (b, 0, 0))
    return pl.pallas_call(
        kern,
        grid=(batch, n_qblk),
        in_specs=[qspec(aq_col), qspec(iq_col),
                  pl.BlockSpec((tq, LANE), lambda b, q: (b * n_qblk + q, 0)),
                  kspec(TK, IDX_DIM), kspec(TK, KV_W), kspec(VT_ALL, TK),
                  _const_spec(bt.shape)],
        out_specs=pl.BlockSpec((tq, ATT_W), lambda b, q: (b * n_qblk + q, 0)),
        out_shape=jax.ShapeDtypeStruct((batch * n_qblk * tq, ATT_W), F32),
        scratch_shapes=[pltpu.VMEM((nkt_total, TK, tq), F32),
                        pltpu.VMEM((ATT_KV_HEADS, VT_ROWS, HEADS_PER_KV * tq), F32),
                        pltpu.VMEM((ATT_HEADS, tq), F32)],
        compiler_params=pltpu.CompilerParams(dimension_semantics=("parallel", "arbitrary"),
                                             vmem_limit_bytes=VMEM_LIMIT),
        name="attention",
    )(aq_src, iq_src, iw, ki3, k3, vt3, bt)


def _ffn_kernel(x_ref, oret_ref, oatt_ref, wout_ref, wup_ref, wdown_ref,
                gpost_ref, gpre_ref, gffn_ref, y_ref, *, ff_chunk):
    mix = jnp.dot(oret_ref[...].astype(BF16), wout_ref[:RET_W, :], preferred_element_type=F32)
    mix = mix + jnp.dot(oatt_ref[...].astype(BF16), wout_ref[RET_W:, :], preferred_element_type=F32)
    h = x_ref[...] + _rms(mix, gpost_ref[...])
    a = _rms(h, gpre_ref[...]).astype(BF16)
    f = jnp.zeros(h.shape, F32)
    for c in range(D_FF // ff_chunk):
        sl = slice(c * ff_chunk, (c + 1) * ff_chunk)
        u = jnp.dot(a, wup_ref[:, sl], preferred_element_type=F32)
        u = jnp.square(jnp.maximum(u, 0.0)).astype(BF16)
        f = f + jnp.dot(u, wdown_ref[sl, :], preferred_element_type=F32)
    y_ref[...] = h + _rms(f, gffn_ref[...])


def _out_ffn(x2, o_ret, o_att, w_out, w_up, w_down, g_post, g_pre, g_ffn, tm):
    rows = x2.shape[0]
    row_spec = lambda w: pl.BlockSpec((tm, w), lambda i: (i, 0))
    return pl.pallas_call(
        functools.partial(_ffn_kernel, ff_chunk=512),
        grid=(rows // tm,),
        in_specs=[row_spec(D_MODEL), row_spec(RET_W), row_spec(ATT_W),
                  _const_spec(w_out.shape), _const_spec(w_up.shape), _const_spec(w_down.shape),
                  _const_spec((1, D_MODEL)), _const_spec((1, D_MODEL)), _const_spec((1, D_MODEL))],
        out_specs=row_spec(D_MODEL),
        out_shape=jax.ShapeDtypeStruct((rows, D_MODEL), F32),
        compiler_params=pltpu.CompilerParams(dimension_semantics=("parallel",),
                                             vmem_limit_bytes=VMEM_LIMIT),
        name="out_ffn",
    )(x2, o_ret, o_att, w_out, w_up, w_down, g_post, g_pre, g_ffn)


def _t5_bucket(rel):
    half = N_BUCKETS // 2
    max_exact = half // 2
    ret = jnp.where(rel > 0, half, 0)
    n = jnp.abs(rel)
    nf = jnp.maximum(n, 1).astype(F32)
    large = max_exact + (jnp.log(nf / max_exact) / math.log(MAX_DISTANCE / max_exact)
                         * (half - max_exact)).astype(jnp.int32)
    large = jnp.minimum(large, half - 1)
    return ret + jnp.where(n < max_exact, n, large)


def _bias_tiles(rel_bias, tq):
    period = 2 * TK + tq
    rel = jnp.arange(period, dtype=jnp.int32) - (TK + tq - 1)
    table = rel_bias.astype(F32)
    far = table[_t5_bucket(jnp.int32(-MAX_DISTANCE))]
    u = (table[_t5_bucket(jnp.clip(rel, -MAX_DISTANCE, MAX_DISTANCE))] - far).T
    n = jnp.tile(u, (1, tq))[:, :tq * (period - 1)].reshape(ATT_HEADS, tq, period - 1)
    m = n[:, :, tq - 1:tq - 1 + 2 * TK]
    return jnp.transpose(m.reshape(ATT_HEADS, tq, 2, TK), (0, 2, 3, 1))


def _rope_tables(pos):
    half = RET_DK // 2
    inv = ROPE_BASE ** (-jnp.arange(half, dtype=F32) / half)
    ang = pos.astype(F32)[:, None] * inv[None, :]
    cos, sin = jnp.cos(ang), jnp.sin(ang)
    cos_h = jnp.concatenate([cos, cos], axis=1)
    sin_h = jnp.concatenate([-sin, sin], axis=1)
    reps = LANE // RET_DK
    return jnp.tile(cos_h, (1, reps)), jnp.tile(sin_h, (1, reps))


def _permute_w_in(w_in):
    offs = np.cumsum([0, 512, 512, 512, 512, 512, 128, 128, 512, 64, 8])
    seg = lambda i: w_in[:, offs[i]:offs[i + 1]]
    pad = jnp.zeros((D_MODEL, LANE - IDX_DIM - IDX_HEADS), w_in.dtype)
    return jnp.concatenate([seg(0), seg(1), seg(2), seg(3), seg(4), seg(7),
                            seg(5), seg(6), seg(8), seg(9), pad], axis=1).astype(BF16)


def _layer(x, s0, past, weights, rel_bias):
    w_perm, w_out, w_up, w_down, g_pre_mix, g_post_mix, g_pre_ffn, g_post_ffn = weights
    batch, t_len, _ = x.shape
    rows = batch * t_len
    x2 = x.reshape(rows, D_MODEL)
    tm = min(512, rows)
    main, kc, vc, ic, iw, *key_tiles = _inproj(x2, g_pre_mix, w_perm, tm, key_tiles=past is None)

    p_len = 0 if past is None else past[0].shape[1]
    pos = p_len + jnp.arange(t_len, dtype=jnp.int32)
    cos_t, sin_t = _rope_tables(pos)
    chunk = min(CHUNK, t_len)
    sb = min(256, t_len)
    o_ret, s_new = _retention(main, s0, cos_t, sin_t, batch, t_len, sb, chunk)

    l_all = p_len + t_len
    n_sel = min(TOPK_MAX, l_all // 4)
    if past is None:
        tq = TK
        n_qblk = t_len // tq
        nkt_total = t_len // TK
        bt = _bias_tiles(rel_bias, tq)
        k3, ki3, vt3 = key_tiles
        o_att = _attention(main, main, iw, ki3, k3, vt3, bt, batch, n_qblk, tq, nkt_total, None,
                           TK, tq, n_sel, aq_col=4, iq_col=5)
    else:
        tq = LANE
        pk, pv, pi = past
        assert p_len % TK == 0 and t_len <= min(tq, TK), (p_len, t_len)
        nkt_total = p_len // TK + 1
        padk = nkt_total * TK - l_all
        cat = lambda old, new: jnp.concatenate(
            [old.astype(BF16), new.reshape(batch, t_len, -1).astype(BF16),
             jnp.zeros((batch, padk, new.shape[-1]), BF16)], axis=1)
        k_all = cat(pk.reshape(batch, p_len, KV_W), kc)
        v_all = cat(pv.reshape(batch, p_len, KV_W), vc)
        i_all = cat(pi, ic)
        v_t = jnp.transpose(v_all.reshape(batch, nkt_total, TK, ATT_KV_HEADS, ATT_HD), (0, 1, 3, 4, 2))
        ones = jnp.zeros((batch, nkt_total, ATT_KV_HEADS, VT_ROWS - ATT_HD, TK), BF16).at[:, :, :, 0, :].set(1.0)
        vt_s = jnp.concatenate([v_t, ones], axis=3).reshape(batch * nkt_total, VT_ALL, TK)
        k_s = k_all.reshape(batch * nkt_total, TK, KV_W)
        i_s = i_all.reshape(batch * nkt_total, TK, IDX_DIM)
        padq = lambda a: jnp.pad(a.reshape(batch, t_len, -1), ((0, 0), (0, tq - t_len), (0, 0))
                                 ).reshape(batch * tq, -1)
        aq_s = padq(main[:, 4 * RET_W:4 * RET_W + ATT_W])
        iq_s = padq(main[:, 4 * RET_W + ATT_W:])
        iw_s = padq(iw)
        bt = _bias_tiles(rel_bias, tq)
        o_pad = _attention(aq_s, iq_s, iw_s, i_s, k_s, vt_s, bt, batch, 1, tq, nkt_total, nkt_total,
                           l_all - p_len, t_len, n_sel, aq_col=0, iq_col=0)
        o_att = o_pad.reshape(batch, tq, ATT_W)[:, :t_len].reshape(rows, ATT_W)

    y = _out_ffn(x2, o_ret, o_att, w_out, w_up, w_down, g_post_mix, g_pre_ffn, g_post_ffn, tm)
    return (y.reshape(batch, t_len, D_MODEL), s_new,
            kc.reshape(batch, t_len, ATT_KV_HEADS, ATT_HD),
            vc.reshape(batch, t_len, ATT_KV_HEADS, ATT_HD),
            ic.reshape(batch, t_len, IDX_DIM))


def kernel(x_prompt, x_sample, state_ret, cache_k, cache_v, cache_kidx, w_in, w_out, w_up, w_down,
           g_pre_mix, g_post_mix, g_pre_ffn, g_post_ffn, rel_bias):
    depth = w_in.shape[0]
    bp = x_prompt.shape[0]
    zero_state = jnp.zeros((bp, RET_HEADS, RET_DK, RET_DV), F32)
    yp, ys = x_prompt, x_sample
    outs_p, outs_s = [], []
    for l in range(depth):
        row = lambda g: g[l].reshape(1, D_MODEL).astype(F32)
        weights = (_permute_w_in(w_in[l]), w_out[l].astype(BF16), w_up[l].astype(BF16),
                   w_down[l].astype(BF16), row(g_pre_mix), row(g_post_mix), row(g_pre_ffn),
                   row(g_post_ffn))
        yp, *rest_p = _layer(yp, zero_state, None, weights, rel_bias)
        ys, *rest_s = _layer(ys, state_ret[l], (cache_k[l], cache_v[l], cache_kidx[l]), weights, rel_bias)
        outs_p.append(rest_p)
        outs_s.append(rest_s)
    stack = lambda outs, i: jnp.stack([o[i] for o in outs])
    return (yp, ys,
            stack(outs_p, 0), stack(outs_p, 1), stack(outs_p, 2), stack(outs_p, 3),
            stack(outs_s, 0), stack(outs_s, 1), stack(outs_s, 2), stack(outs_s, 3))
```

```python
import functools
import math

import jax
import jax.numpy as jnp
import numpy as np
from jax import lax
from jax.experimental import pallas as pl
from jax.experimental.pallas import tpu as pltpu

D_MODEL = 1024
CHUNK = 64
RET_HEADS = 8
RET_DK = 64
RET_DV = 64
ATT_HEADS = 8
ATT_KV_HEADS = 2
ATT_HD = 64
IDX_HEADS = 8
IDX_DIM = 64
TOPK_MAX = 256
N_BUCKETS = 32
MAX_DISTANCE = 128
D_FF = 4 * D_MODEL
ROPE_BASE = 10000.0
EPS = 1e-6

RET_W = RET_HEADS * RET_DV
ATT_W = ATT_HEADS * ATT_HD
KV_W = ATT_KV_HEADS * ATT_HD
IDX_W = IDX_HEADS * IDX_DIM
HEADS_PER_KV = ATT_HEADS // ATT_KV_HEADS

MAIN_W = 4 * RET_W + ATT_W + IDX_W
TAIL_W = 3 * 128
IW_OFF = IDX_DIM

LANE = 128
VT_ROWS = 80
VT_ALL = ATT_KV_HEADS * VT_ROWS
TK = 256
NEG = -(2.0 ** 100)
BF16_ROWS = 16
CNT_ROWS = 32
NARROW = 2.0 ** -20
MAX_SEARCH = 96
F32_TINY = float(np.finfo(np.float32).tiny)
VMEM_LIMIT = 56 * 1024 * 1024

F32 = jnp.float32
BF16 = jnp.bfloat16
NT_DIMS = (((1,), (1,)), ((), ()))
TN_DIMS = (((0,), (0,)), ((), ()))


def _const_spec(shape):
    nd = len(shape)
    return pl.BlockSpec(shape, lambda *_: (0,) * nd, pipeline_mode=pl.Buffered(1))


def _rms(x, gain):
    return x * lax.rsqrt(jnp.mean(x * x, axis=-1, keepdims=True) + EPS) * gain


def _inproj_kernel(x_ref, g_ref, w_ref, main_ref, kc_ref, vc_ref, ic_ref, iw_ref,
                   *tile_refs, tm):
    a = _rms(x_ref[...], g_ref[...]).astype(BF16)
    for c in range(MAIN_W // 512):
        main_ref[:, c * 512:(c + 1) * 512] = jnp.dot(
            a, w_ref[:, c * 512:(c + 1) * 512], preferred_element_type=F32)
    tail = jnp.dot(a, w_ref[:, MAIN_W:MAIN_W + TAIL_W], preferred_element_type=F32)
    ak = tail[:, 0:KV_W]
    av = tail[:, KV_W:2 * KV_W]
    last = tail[:, 2 * KV_W:3 * KV_W]
    kc_ref[...] = ak
    vc_ref[...] = av
    ic_ref[...] = last[:, :IDX_DIM]
    iw_ref[...] = last
    if not tile_refs:
        return
    k3_ref, ki3_ref, vt3_ref = tile_refs
    avt = av.T
    row = lax.broadcasted_iota(jnp.int32, (VT_ROWS - ATT_HD, tm), 0)
    ones_rows = jnp.where(row == 0, 1.0, 0.0).astype(F32)
    vt = jnp.concatenate([avt[:ATT_HD], ones_rows, avt[ATT_HD:], ones_rows], axis=0).astype(BF16)
    for j in range(tm // TK):
        k3_ref[j] = ak[j * TK:(j + 1) * TK].astype(BF16)
        ki3_ref[j] = last[j * TK:(j + 1) * TK, :IDX_DIM].astype(BF16)
        vt3_ref[j] = vt[:, j * TK:(j + 1) * TK]


def _inproj(x2, gain, w_perm, tm, key_tiles):
    rows = x2.shape[0]
    grid = (rows // tm,)
    row_spec = lambda w: pl.BlockSpec((tm, w), lambda i: (i, 0))
    out_shape = (
        jax.ShapeDtypeStruct((rows, MAIN_W), F32),
        jax.ShapeDtypeStruct((rows, KV_W), F32),
        jax.ShapeDtypeStruct((rows, KV_W), F32),
        jax.ShapeDtypeStruct((rows, IDX_DIM), F32),
        jax.ShapeDtypeStruct((rows, LANE), F32),
    )
    out_specs = (row_spec(MAIN_W), row_spec(KV_W), row_spec(KV_W), row_spec(IDX_DIM), row_spec(LANE))
    if key_tiles:
        t3 = lambda a, b: pl.BlockSpec((tm // TK, a, b), lambda i: (i, 0, 0))
        out_shape += (jax.ShapeDtypeStruct((rows // TK, TK, KV_W), BF16),
                      jax.ShapeDtypeStruct((rows // TK, TK, IDX_DIM), BF16),
                      jax.ShapeDtypeStruct((rows // TK, VT_ALL, TK), BF16))
        out_specs += (t3(TK, KV_W), t3(TK, IDX_DIM), t3(VT_ALL, TK))
    return pl.pallas_call(
        functools.partial(_inproj_kernel, tm=tm),
        grid=grid,
        in_specs=[row_spec(D_MODEL), _const_spec((1, D_MODEL)), _const_spec(w_perm.shape)],
        out_specs=out_specs,
        out_shape=out_shape,
        compiler_params=pltpu.CompilerParams(dimension_semantics=("parallel",),
                                             vmem_limit_bytes=VMEM_LIMIT),
        name="inproj",
    )(x2, gain, w_perm)


def _ret_gammas():
    return [1.0 - 2.0 ** (-5.0 - h) for h in range(RET_HEADS)]


def _ret_tables(sb, chunk):
    lg = np.log(np.array(_ret_gammas(), np.float64))
    t = np.arange(sb)
    ci = t // chunk
    diff = t[:, None] - t[None, :]
    same = ci[:, None] == ci[None, :]
    below = ci[None, :] < ci[:, None]
    expo = np.where(same, np.abs(diff), np.where(below, diff, 0)).astype(np.float64)
    dmat = np.exp(lg[:, None, None] * expo[None]) * (same | below)[None]
    qd = np.exp(lg[None, :] * (t + 1.0)[:, None])
    kd = np.exp(lg[None, :] * (sb - 1.0 - t)[:, None])
    qd = np.repeat(qd, RET_DK, axis=1)
    kd = np.repeat(kd, RET_DK, axis=1)
    return (jnp.asarray(dmat, F32), jnp.asarray(qd, F32), jnp.asarray(kd, F32),
            [float(math.exp(v * sb)) for v in lg])


def _ret_kernel(q_ref, k_ref, v_ref, g_ref, cos_ref, sin_ref, d_ref, qd_ref, kd_ref, s0_ref,
                o_ref, sfin_ref, s_scr, *, g_block):
    n = pl.program_id(1)

    @pl.when(n == 0)
    def _():
        s_scr[...] = s0_ref[0]

    reps = RET_W // LANE
    cos = jnp.concatenate([cos_ref[...]] * reps, axis=1)
    sin = jnp.concatenate([sin_ref[...]] * reps, axis=1)
    lane = lax.broadcasted_iota(jnp.int32, cos.shape, 1)
    first_half = (lane & (RET_DK - 1)) < RET_DK // 2

    def rot(x):
        partner = jnp.where(first_half, pltpu.roll(x, RET_W - RET_DK // 2, 1),
                            pltpu.roll(x, RET_DK // 2, 1))
        return x * cos + partner * sin

    q = rot(q_ref[...])
    k = rot(k_ref[...]) * (RET_DK ** -0.5)
    v = v_ref[...].astype(BF16)
    gate = g_ref[...]
    qb = q.astype(BF16)
    kb = k.astype(BF16)
    qx = (q * qd_ref[...]).astype(BF16)
    kx = (k * kd_ref[...]).astype(BF16)
    outs = []
    for h in range(RET_HEADS):
        sl = slice(h * RET_DK, (h + 1) * RET_DK)
        s = lax.dot_general(qb[:, sl], kb[:, sl], NT_DIMS, preferred_element_type=F32)
        p = (s * d_ref[h]).astype(BF16)
        st = s_scr[h]
        o = jnp.dot(p, v[:, sl], preferred_element_type=F32)
        o = o + jnp.dot(qx[:, sl], st.astype(BF16), preferred_element_type=F32)
        s_scr[h] = g_block[h] * st + lax.dot_general(kx[:, sl], v[:, sl], TN_DIMS,
                                                     preferred_element_type=F32)
        o = o * lax.rsqrt(jnp.mean(o * o, axis=-1, keepdims=True) + EPS)
        outs.append(o)
    o_all = jnp.concatenate(outs, axis=1)
    o_ref[...] = gate * (1.0 / (1.0 + jnp.exp(-gate))) * o_all

    @pl.when(n == pl.num_programs(1) - 1)
    def _():
        sfin_ref[0] = s_scr[...]


def _retention(main, s0, cos_t, sin_t, batch, t_len, sb, chunk):
    nsb = t_len // sb
    dmat, qd, kd, g_block = _ret_tables(sb, chunk)
    col = lambda c: pl.BlockSpec((sb, RET_W), lambda b, n, c=c: (b * nsb + n, c))
    tab = pl.BlockSpec((sb, LANE), lambda b, n: (n, 0))
    st_spec = pl.BlockSpec((1, RET_HEADS, RET_DK, RET_DV), lambda b, n: (b, 0, 0, 0))
    return pl.pallas_call(
        functools.partial(_ret_kernel, g_block=g_block),
        grid=(batch, nsb),
        in_specs=[col(0), col(1), col(2), col(3), tab, tab,
                  _const_spec(dmat.shape), _const_spec(qd.shape), _const_spec(kd.shape), st_spec],
        out_specs=(pl.BlockSpec((sb, RET_W), lambda b, n: (b * nsb + n, 0)), st_spec),
        out_shape=(jax.ShapeDtypeStruct((batch * t_len, RET_W), F32),
                   jax.ShapeDtypeStruct((batch, RET_HEADS, RET_DK, RET_DV), F32)),
        scratch_shapes=[pltpu.VMEM((RET_HEADS, RET_DK, RET_DV), F32)],
        compiler_params=pltpu.CompilerParams(dimension_semantics=("parallel", "arbitrary"),
                                             vmem_limit_bytes=VMEM_LIMIT),
        name="retention",
    )(main, main, main, main, cos_t, sin_t, dmat, qd, kd, s0)


def _attn_kernel(aq_ref, iq_ref, iw_ref, ki_ref, k_ref, vt_ref, bt_ref, o_ref,
                 s_scr, acc_scr, m_scr, p_scr, *, tq, nkt_static, last_valid, n_valid_q, n_sel,
                 max_search):
    qblk = pl.program_id(1)
    nkt = qblk + 1 if nkt_static is None else nkt_static
    lane_f = lax.broadcasted_iota(jnp.int32, (1, tq), 1).astype(F32)
    lane_ok = lane_f < float(n_valid_q)
    klim = jnp.minimum((jnp.floor(lane_f * (1.0 / CHUNK)) + 1.0) * CHUNK, float(last_valid))
    krow = lax.broadcasted_iota(jnp.int32, (TK, tq), 0).astype(F32)
    adm_last = krow < klim
    nkt_f = nkt.astype(F32) if nkt_static is None else float(nkt)
    n_adm = (nkt_f - 1.0) * TK + klim
    k_target = jnp.minimum(float(n_sel), n_adm)

    iq = iq_ref[...]
    w_t = iw_ref[...].T[IW_OFF:IW_OFF + IDX_HEADS, :] * (IDX_HEADS ** -0.5 * IDX_DIM ** -0.5)
    iq_all = jnp.concatenate(
        [iq[:, h * IDX_DIM:(h + 1) * IDX_DIM].astype(BF16) for h in range(IDX_HEADS)], axis=0)

    def score_tile(kt, n_tiles=1):
        ki_t = ki_ref[kt] if n_tiles == 1 else jnp.concatenate(
            [ki_ref[kt + j] for j in range(n_tiles)], axis=0)
        s_all = lax.dot_general(ki_t, iq_all, NT_DIMS, preferred_element_type=F32)
        acc = jnp.maximum(s_all[:, :tq], 0.0) * w_t[0:1, :]
        for h in range(1, IDX_HEADS):
            acc = acc + jnp.maximum(s_all[:, h * tq:(h + 1) * tq], 0.0) * w_t[h:h + 1, :]
        return acc

    def p1_step(kt, n_tiles, carry):
        rmax, rmin = carry
        sc = score_tile(kt, n_tiles)
        for j in range(n_tiles):
            s_scr[kt + j] = sc[j * TK:(j + 1) * TK]
        return (jnp.maximum(rmax, jnp.max(sc, axis=0, keepdims=True)),
                jnp.minimum(rmin, jnp.min(sc, axis=0, keepdims=True)))

    n_full = nkt - 1
    init = (jnp.full((1, tq), -jnp.inf, F32), jnp.full((1, tq), jnp.inf, F32))
    rmax, rmin = lax.fori_loop(0, n_full // 2, lambda i, c: p1_step(2 * i, 2, c), init)
    rmax, rmin = lax.fori_loop(0, n_full % 2, lambda i, c: p1_step(n_full - 1, 1, c), (rmax, rmin))
    sc = score_tile(nkt - 1)
    s_scr[nkt - 1] = jnp.where(adm_last, sc, -jnp.inf)
    rmax = jnp.maximum(rmax, jnp.max(jnp.where(adm_last, sc, -jnp.inf), axis=0, keepdims=True))
    rmin = jnp.minimum(rmin, jnp.min(jnp.where(adm_last, sc, jnp.inf), axis=0, keepdims=True))

    def fold_rows(x, op=jnp.add):
        parts = [x[i * CNT_ROWS:(i + 1) * CNT_ROWS] for i in range(TK // CNT_ROWS)]
        while len(parts) > 1:
            parts = [op(a, b) for a, b in zip(parts[0::2], parts[1::2])]
        return parts[0]

    def count_where(pred):
        def body(kt, c):
            return c + fold_rows(jnp.where(pred(kt), 1.0, 0.0))
        c = lax.fori_loop(0, nkt, body, jnp.zeros((CNT_ROWS, tq), F32))
        return jnp.sum(c, axis=0, keepdims=True)

    def count_ge(thr):
        return count_where(lambda kt: s_scr[kt] >= thr)

    span = jnp.maximum(jnp.maximum(rmax - rmin, jnp.abs(rmax)), 1e-30)
    hi0 = rmax + span * (2.0 ** -10)

    def max_below(bound, n_tiles):
        def body(kt, m):
            s = s_scr[kt]
            return jnp.maximum(m, fold_rows(jnp.where(s < bound, s, -jnp.inf), jnp.maximum))
        m = lax.fori_loop(0, n_tiles, body, jnp.full((CNT_ROWS, tq), -jnp.inf, F32))
        return jnp.max(m, axis=0, keepdims=True)

    def plan(it, lo, hi, c_lo, c_hi, done):
        width = hi - lo
        frac = (c_lo - k_target - 0.5) / (c_lo - c_hi)
        interp = ((it + 1) % 2).astype(F32)
        mid_i = lo + width * (0.5 + interp * (frac - 0.5))
        mid_b = lo + 0.5 * width
        mid = jnp.where((mid_i > lo) & (mid_i < hi), mid_i, mid_b)
        active = done == 0.0
        narrow = active & (jnp.logical_not((mid > lo) & (mid < hi)) | (width <= span * NARROW))
        flag = jnp.max(jnp.where(narrow, 3.0, jnp.where(active, 1.0, 0.0)))
        return mid, jnp.where(narrow, 1.0, 0.0), flag

    def search_body(st):
        it, flag, lo, hi, c_lo, c_hi, done, mid, narrow_f = st
        narrow = narrow_f > 0.0
        top = max_below(hi, jnp.where(flag > 2.0, nkt, 0))
        mid = jnp.where(narrow, top, mid)
        c = count_ge(mid)
        active = done == 0.0
        ge = c >= k_target
        up = active & ge
        dn = active & jnp.logical_not(ge)
        lo = jnp.where(up, mid, lo)
        c_lo = jnp.where(up, c, c_lo)
        hi = jnp.where(dn, mid, hi)
        c_hi = jnp.where(dn, c, c_hi)
        done = jnp.where((c_lo == k_target) | (narrow & up), 1.0, done)
        mid, narrow_f, flag = plan(it + 1, lo, hi, c_lo, c_hi, done)
        return it + 1, flag, lo, hi, c_lo, c_hi, done, mid, narrow_f

    c_zero = count_ge(0.0)
    c_pos = count_ge(F32_TINY)
    pos_side = c_pos >= k_target
    at_zero = jnp.logical_not(pos_side) & (c_zero >= k_target)
    pick = lambda p, z, n: jnp.where(pos_side, p, jnp.where(at_zero, z, n))
    lo0 = pick(F32_TINY, 0.0, rmin)
    hi0 = pick(hi0, F32_TINY, 0.0)
    c_lo0 = pick(c_pos, c_zero, n_adm)
    c_hi0 = pick(0.0, c_pos, c_zero)
    done0 = jnp.where((n_adm == k_target) | at_zero | jnp.logical_not(lane_ok), 1.0, 0.0)
    mid0, narrow0, flag0 = plan(jnp.int32(0), lo0, hi0, c_lo0, c_hi0, done0)
    st0 = (jnp.int32(0), flag0, lo0, hi0, c_lo0, c_hi0, done0, mid0, narrow0)
    _, _, lo, hi, cnt_lo, _, _, _, _ = lax.while_loop(
        lambda st: (st[0] < max_search) & (st[1] > 0.0), search_body, st0)

    excess = jnp.max(jnp.where(lane_ok, cnt_lo - k_target, 0.0))

    @pl.when(excess > 0.0)
    def _():
        need = k_target - count_ge(hi)
        r = lax.broadcasted_iota(jnp.int32, (TK, TK), 0)
        c = lax.broadcasted_iota(jnp.int32, (TK, TK), 1)
        prefix = jnp.where(r >= c, 1.0, 0.0).astype(BF16)

        def drop_body(kt, before):
            s = s_scr[kt]
            tie = (s >= lo) & (s < hi)
            tie_f = jnp.where(tie, 1.0, 0.0)
            rank = before + jnp.dot(prefix, tie_f.astype(BF16), preferred_element_type=F32)
            s_scr[kt] = jnp.where(tie & (rank > need), -jnp.inf, s)
            return before + jnp.sum(tie_f, axis=0, keepdims=True)
        lax.fori_loop(0, nkt, drop_body, jnp.zeros((1, tq), F32))

    aq = aq_ref[...] * (ATT_HD ** -0.5)
    zeros_q = jnp.zeros((tq, ATT_HD), F32)
    q_pad = []
    for h in range(ATT_HEADS):
        qh = aq[:, h * ATT_HD:(h + 1) * ATT_HD]
        parts = [zeros_q] * ATT_KV_HEADS
        parts[h // HEADS_PER_KV] = qh
        q_pad.append(jnp.concatenate(parts, axis=1).astype(BF16))

    m_scr[...] = jnp.full(m_scr.shape, NEG, F32)
    acc_scr[...] = jnp.zeros(acc_scr.shape, F32)

    def col_max(x):
        parts = [x[i * BF16_ROWS:(i + 1) * BF16_ROWS] for i in range(x.shape[0] // BF16_ROWS)]
        while len(parts) > 1:
            parts = [jnp.maximum(a, b) for a, b in zip(parts[0::2], parts[1::2])]
        return jnp.max(parts[0].astype(F32), axis=0, keepdims=True)

    cat = lambda xs, axis: xs[0] if len(xs) == 1 else jnp.concatenate(xs, axis=axis)

    def logits(kt, n_tiles, near):
        k_t = cat([k_ref[kt + j] for j in range(n_tiles)], 0)
        s_t = cat([s_scr[kt + j] for j in range(n_tiles)], 0)
        neg_mask = jnp.where(s_t >= lo, 0.0, NEG).astype(BF16)
        out = []
        for h in range(ATT_HEADS):
            lg = lax.dot_general(k_t, q_pad[h], NT_DIMS, preferred_element_type=F32)
            if near:
                lg = lg + cat([bt_ref[h, 2 - n_tiles + j] for j in range(n_tiles)], 0)
            out.append(lg.astype(BF16) + neg_mask)
        return out

    def softmax_pv(kt, n_tiles, lgs):
        tiles = [kt + j for j in range(n_tiles)]
        rows = n_tiles * TK
        alphas = []
        for h in range(ATT_HEADS):
            g, r = divmod(h, HEADS_PER_KV)
            lg = lgs[h]
            m_old = m_scr[h:h + 1, :]
            m_new = jnp.maximum(m_old, col_max(lg))
            alphas.append(jnp.exp(m_old - m_new))
            p_scr[g, :rows, r * tq:(r + 1) * tq] = jnp.exp(lg - m_new.astype(BF16))
            m_scr[h:h + 1, :] = m_new
        for g in range(ATT_KV_HEADS):
            hs = range(g * HEADS_PER_KV, (g + 1) * HEADS_PER_KV)
            a_g = jnp.concatenate([alphas[h] for h in hs], axis=1)
            vt_g = cat([vt_ref[t, g * VT_ROWS:(g + 1) * VT_ROWS, :] for t in tiles], 1)
            acc_scr[g] = acc_scr[g] * a_g + jnp.dot(vt_g, p_scr[g, :rows, :],
                                                    preferred_element_type=F32)

    def attend(kt, n_tiles, near):
        softmax_pv(kt, n_tiles, logits(kt, n_tiles, near))

    n_far = jnp.maximum(nkt - 2, 0) if nkt_static is None else max(nkt - 2, 0)

    def far_body(i, c):
        attend(2 * i, 2, False)
        return c
    lax.fori_loop(0, n_far // 2, far_body, 0)

    def when(cond):
        return pl.when(cond) if nkt_static is None else (lambda f: f() if cond else None)

    @when(n_far % 2 == 1)
    def _():
        attend(n_far - 1, 1, False)

    @when(nkt >= 2)
    def _():
        attend(nkt - 2, 2, True)

    @when(nkt < 2)
    def _():
        attend(0, 1, True)

    outs = []
    for g in range(ATT_KV_HEADS):
        a = acc_scr[g]
        o_g = a[:ATT_HD] * (1.0 / a[ATT_HD:ATT_HD + 1])
        outs += [o_g[:, r * tq:(r + 1) * tq] for r in range(HEADS_PER_KV)]
    o_ref[...] = jnp.concatenate(outs, axis=0).T


def _attention(aq_src, iq_src, iw, ki3, k3, vt3, bt, batch, n_qblk, tq, nkt_total, nkt_static,
               last_valid, n_valid_q, n_sel, aq_col, iq_col):
    kern = functools.partial(_attn_kernel, tq=tq, nkt_static=nkt_static, last_valid=last_valid,
                             n_valid_q=n_valid_q, n_sel=n_sel, max_search=MAX_SEARCH)
    qspec = lambda c: pl.BlockSpec((tq, ATT_W), lambda b, q, c=c: (b * n_qblk + q, c))
    kspec = lambda a, c: pl.BlockSpec((nkt_total, a, c), lambda b, q: (b, 0, 0))
    return pl.pallas_call(
        kern,
        grid=(batch, n_qblk),
        in_specs=[qspec(aq_col), qspec(iq_col),
                  pl.BlockSpec((tq, LANE), lambda b, q: (b * n_qblk + q, 0)),
                  kspec(TK, IDX_DIM), kspec(TK, KV_W), kspec(VT_ALL, TK),
                  _const_spec(bt.shape)],
        out_specs=pl.BlockSpec((tq, ATT_W), lambda b, q: (b * n_qblk + q, 0)),
        out_shape=jax.ShapeDtypeStruct((batch * n_qblk * tq, ATT_W), F32),
        scratch_shapes=[pltpu.VMEM((nkt_total, TK, tq), F32),
                        pltpu.VMEM((ATT_KV_HEADS, VT_ROWS, HEADS_PER_KV * tq), F32),
                        pltpu.VMEM((ATT_HEADS, tq), F32),
                        pltpu.VMEM((ATT_KV_HEADS, 2 * TK, HEADS_PER_KV * tq), BF16)],
        compiler_params=pltpu.CompilerParams(dimension_semantics=("parallel", "arbitrary"),
                                             vmem_limit_bytes=VMEM_LIMIT),
        name="attention",
    )(aq_src, iq_src, iw, ki3, k3, vt3, bt)


def _ffn_kernel(x_ref, oret_ref, oatt_ref, wout_ref, wup_ref, wdown_ref,
                gpost_ref, gpre_ref, gffn_ref, y_ref, *, ff_chunk):
    mix = jnp.dot(oret_ref[...].astype(BF16), wout_ref[:RET_W, :], preferred_element_type=F32)
    mix = mix + jnp.dot(oatt_ref[...].astype(BF16), wout_ref[RET_W:, :], preferred_element_type=F32)
    h = x_ref[...] + _rms(mix, gpost_ref[...])
    a = _rms(h, gpre_ref[...]).astype(BF16)
    f = jnp.zeros(h.shape, F32)
    for c in range(D_FF // ff_chunk):
        sl = slice(c * ff_chunk, (c + 1) * ff_chunk)
        u = jnp.dot(a, wup_ref[:, sl], preferred_element_type=F32)
        u = jnp.square(jnp.maximum(u, 0.0)).astype(BF16)
        f = f + jnp.dot(u, wdown_ref[sl, :], preferred_element_type=F32)
    y_ref[...] = h + _rms(f, gffn_ref[...])


def _out_ffn(x2, o_ret, o_att, w_out, w_up, w_down, g_post, g_pre, g_ffn, tm):
    rows = x2.shape[0]
    row_spec = lambda w: pl.BlockSpec((tm, w), lambda i: (i, 0))
    return pl.pallas_call(
        functools.partial(_ffn_kernel, ff_chunk=512),
        grid=(rows // tm,),
        in_specs=[row_spec(D_MODEL), row_spec(RET_W), row_spec(ATT_W),
                  _const_spec(w_out.shape), _const_spec(w_up.shape), _const_spec(w_down.shape),
                  _const_spec((1, D_MODEL)), _const_spec((1, D_MODEL)), _const_spec((1, D_MODEL))],
        out_specs=row_spec(D_MODEL),
        out_shape=jax.ShapeDtypeStruct((rows, D_MODEL), F32),
        compiler_params=pltpu.CompilerParams(dimension_semantics=("parallel",),
                                             vmem_limit_bytes=VMEM_LIMIT),
        name="out_ffn",
    )(x2, o_ret, o_att, w_out, w_up, w_down, g_post, g_pre, g_ffn)


def _t5_bucket(rel):
    half = N_BUCKETS // 2
    max_exact = half // 2
    ret = jnp.where(rel > 0, half, 0)
    n = jnp.abs(rel)
    nf = jnp.maximum(n, 1).astype(F32)
    large = max_exact + (jnp.log(nf / max_exact) / math.log(MAX_DISTANCE / max_exact)
                         * (half - max_exact)).astype(jnp.int32)
    large = jnp.minimum(large, half - 1)
    return ret + jnp.where(n < max_exact, n, large)


def _bias_tiles(rel_bias, tq):
    period = 2 * TK + tq
    rel = jnp.arange(period, dtype=jnp.int32) - (TK + tq - 1)
    table = rel_bias.astype(F32)
    far = table[_t5_bucket(jnp.int32(-MAX_DISTANCE))]
    u = (table[_t5_bucket(jnp.clip(rel, -MAX_DISTANCE, MAX_DISTANCE))] - far).T
    n = jnp.tile(u, (1, tq))[:, :tq * (period - 1)].reshape(ATT_HEADS, tq, period - 1)
    m = n[:, :, tq - 1:tq - 1 + 2 * TK]
    return jnp.transpose(m.reshape(ATT_HEADS, tq, 2, TK), (0, 2, 3, 1))


def _rope_tables(pos):
    half = RET_DK // 2
    inv = ROPE_BASE ** (-jnp.arange(half, dtype=F32) / half)
    ang = pos.astype(F32)[:, None] * inv[None, :]
    cos, sin = jnp.cos(ang), jnp.sin(ang)
    cos_h = jnp.concatenate([cos, cos], axis=1)
    sin_h = jnp.concatenate([-sin, sin], axis=1)
    reps = LANE // RET_DK
    return jnp.tile(cos_h, (1, reps)), jnp.tile(sin_h, (1, reps))


def _permute_w_in(w_in):
    offs = np.cumsum([0, 512, 512, 512, 512, 512, 128, 128, 512, 64, 8])
    seg = lambda i: w_in[:, offs[i]:offs[i + 1]]
    pad = jnp.zeros((D_MODEL, LANE - IDX_DIM - IDX_HEADS), w_in.dtype)
    return jnp.concatenate([seg(0), seg(1), seg(2), seg(3), seg(4), seg(7),
                            seg(5), seg(6), seg(8), seg(9), pad], axis=1).astype(BF16)


def _layer(x, s0, past, weights, rel_bias):
    w_perm, w_out, w_up, w_down, g_pre_mix, g_post_mix, g_pre_ffn, g_post_ffn = weights
    batch, t_len, _ = x.shape
    rows = batch * t_len
    x2 = x.reshape(rows, D_MODEL)
    tm = min(512, rows)
    main, kc, vc, ic, iw, *key_tiles = _inproj(x2, g_pre_mix, w_perm, tm, key_tiles=past is None)

    p_len = 0 if past is None else past[0].shape[1]
    pos = p_len + jnp.arange(t_len, dtype=jnp.int32)
    cos_t, sin_t = _rope_tables(pos)
    chunk = min(CHUNK, t_len)
    sb = min(256, t_len)
    o_ret, s_new = _retention(main, s0, cos_t, sin_t, batch, t_len, sb, chunk)

    l_all = p_len + t_len
    n_sel = min(TOPK_MAX, l_all // 4)
    if past is None:
        tq = TK
        n_qblk = t_len // tq
        nkt_total = t_len // TK
        bt = _bias_tiles(rel_bias, tq)
        k3, ki3, vt3 = key_tiles
        o_att = _attention(main, main, iw, ki3, k3, vt3, bt, batch, n_qblk, tq, nkt_total, None,
                           TK, tq, n_sel, aq_col=4, iq_col=5)
    else:
        tq = LANE
        pk, pv, pi = past
        assert p_len % TK == 0 and t_len <= min(tq, TK), (p_len, t_len)
        nkt_total = p_len // TK + 1
        padk = nkt_total * TK - l_all
        cat = lambda old, new: jnp.concatenate(
            [old.astype(BF16), new.reshape(batch, t_len, -1).astype(BF16),
             jnp.zeros((batch, padk, new.shape[-1]), BF16)], axis=1)
        k_all = cat(pk.reshape(batch, p_len, KV_W), kc)
        v_all = cat(pv.reshape(batch, p_len, KV_W), vc)
        i_all = cat(pi, ic)
        v_t = jnp.transpose(v_all.reshape(batch, nkt_total, TK, ATT_KV_HEADS, ATT_HD), (0, 1, 3, 4, 2))
        ones = jnp.zeros((batch, nkt_total, ATT_KV_HEADS, VT_ROWS - ATT_HD, TK), BF16).at[:, :, :, 0, :].set(1.0)
        vt_s = jnp.concatenate([v_t, ones], axis=3).reshape(batch * nkt_total, VT_ALL, TK)
        k_s = k_all.reshape(batch * nkt_total, TK, KV_W)
        i_s = i_all.reshape(batch * nkt_total, TK, IDX_DIM)
        padq = lambda a: jnp.pad(a.reshape(batch, t_len, -1), ((0, 0), (0, tq - t_len), (0, 0))
                                 ).reshape(batch * tq, -1)
        aq_s = padq(main[:, 4 * RET_W:4 * RET_W + ATT_W])
        iq_s = padq(main[:, 4 * RET_W + ATT_W:])
        iw_s = padq(iw)
        bt = _bias_tiles(rel_bias, tq)
        o_pad = _attention(aq_s, iq_s, iw_s, i_s, k_s, vt_s, bt, batch, 1, tq, nkt_total, nkt_total,
                           l_all - p_len, t_len, n_sel, aq_col=0, iq_col=0)
        o_att = o_pad.reshape(batch, tq, ATT_W)[:, :t_len].reshape(rows, ATT_W)

    y = _out_ffn(x2, o_ret, o_att, w_out, w_up, w_down, g_post_mix, g_pre_ffn, g_post_ffn, tm)
    return (y.reshape(batch, t_len, D_MODEL), s_new,
            kc.reshape(batch, t_len, ATT_KV_HEADS, ATT_HD),
            vc.reshape(batch, t_len, ATT_KV_HEADS, ATT_HD),
            ic.reshape(batch, t_len, IDX_DIM))


def kernel(x_prompt, x_sample, state_ret, cache_k, cache_v, cache_kidx, w_in, w_out, w_up, w_down,
           g_pre_mix, g_post_mix, g_pre_ffn, g_post_ffn, rel_bias):
    depth = w_in.shape[0]
    bp = x_prompt.shape[0]
    zero_state = jnp.zeros((bp, RET_HEADS, RET_DK, RET_DV), F32)
    yp, ys = x_prompt, x_sample
    outs_p, outs_s = [], []
    for l in range(depth):
        row = lambda g: g[l].reshape(1, D_MODEL).astype(F32)
        weights = (_permute_w_in(w_in[l]), w_out[l].astype(BF16), w_up[l].astype(BF16),
                   w_down[l].astype(BF16), row(g_pre_mix), row(g_post_mix), row(g_pre_ffn),
                   row(g_post_ffn))
        yp, *rest_p = _layer(yp, zero_state, None, weights, rel_bias)
        ys, *rest_s = _layer(ys, state_ret[l], (cache_k[l], cache_v[l], cache_kidx[l]), weights, rel_bias)
        outs_p.append(rest_p)
        outs_s.append(rest_s)
    stack = lambda outs, i: jnp.stack([o[i] for o in outs])
    return (yp, ys,
            stack(outs_p, 0), stack(outs_p, 1), stack(outs_p, 2), stack(outs_p, 3),
            stack(outs_s, 0), stack(outs_s, 1), stack(outs_s, 2), stack(outs_s, 3))
```

```python
import functools
import math

import jax
import jax.numpy as jnp
import numpy as np
from jax import lax
from jax.experimental import pallas as pl
from jax.experimental.pallas import tpu as pltpu

D_MODEL = 1024
CHUNK = 64
RET_HEADS = 8
RET_DK = 64
RET_DV = 64
ATT_HEADS = 8
ATT_KV_HEADS = 2
ATT_HD = 64
IDX_HEADS = 8
IDX_DIM = 64
TOPK_MAX = 256
N_BUCKETS = 32
MAX_DISTANCE = 128
D_FF = 4 * D_MODEL
ROPE_BASE = 10000.0
EPS = 1e-6

RET_W = RET_HEADS * RET_DV
ATT_W = ATT_HEADS * ATT_HD
KV_W = ATT_KV_HEADS * ATT_HD
IDX_W = IDX_HEADS * IDX_DIM
HEADS_PER_KV = ATT_HEADS // ATT_KV_HEADS

MAIN_W = 4 * RET_W + ATT_W + IDX_W
TAIL_W = 3 * 128
IW_OFF = IDX_DIM

LANE = 128
VT_ROWS = 80
VT_ALL = ATT_KV_HEADS * VT_ROWS
TK = 256
NEG = -(2.0 ** 100)
BF16_ROWS = 16
CNT_ROWS = 32
NARROW = 2.0 ** -20
MAX_SEARCH = 96
F32_TINY = float(np.finfo(np.float32).tiny)
N_COARSE = 10
BF16_STEP = 2.0 ** -7
VMEM_LIMIT = 56 * 1024 * 1024

F32 = jnp.float32
BF16 = jnp.bfloat16
NT_DIMS = (((1,), (1,)), ((), ()))
TN_DIMS = (((0,), (0,)), ((), ()))


def _const_spec(shape):
    nd = len(shape)
    return pl.BlockSpec(shape, lambda *_: (0,) * nd, pipeline_mode=pl.Buffered(1))


def _rms(x, gain):
    return x * lax.rsqrt(jnp.mean(x * x, axis=-1, keepdims=True) + EPS) * gain


def _inproj_kernel(x_ref, g_ref, w_ref, main_ref, kc_ref, vc_ref, ic_ref, iw_ref,
                   *tile_refs, tm):
    a = _rms(x_ref[...], g_ref[...]).astype(BF16)
    for c in range(MAIN_W // 512):
        main_ref[:, c * 512:(c + 1) * 512] = jnp.dot(
            a, w_ref[:, c * 512:(c + 1) * 512], preferred_element_type=F32)
    tail = jnp.dot(a, w_ref[:, MAIN_W:MAIN_W + TAIL_W], preferred_element_type=F32)
    ak = tail[:, 0:KV_W]
    av = tail[:, KV_W:2 * KV_W]
    last = tail[:, 2 * KV_W:3 * KV_W]
    kc_ref[...] = ak
    vc_ref[...] = av
    ic_ref[...] = last[:, :IDX_DIM]
    iw_ref[...] = last
    if not tile_refs:
        return
    k3_ref, ki3_ref, vt3_ref = tile_refs
    avt = av.T
    row = lax.broadcasted_iota(jnp.int32, (VT_ROWS - ATT_HD, tm), 0)
    ones_rows = jnp.where(row == 0, 1.0, 0.0).astype(F32)
    vt = jnp.concatenate([avt[:ATT_HD], ones_rows, avt[ATT_HD:], ones_rows], axis=0).astype(BF16)
    for j in range(tm // TK):
        k3_ref[j] = ak[j * TK:(j + 1) * TK].astype(BF16)
        ki3_ref[j] = last[j * TK:(j + 1) * TK, :IDX_DIM].astype(BF16)
        vt3_ref[j] = vt[:, j * TK:(j + 1) * TK]


def _inproj(x2, gain, w_perm, tm, key_tiles):
    rows = x2.shape[0]
    grid = (rows // tm,)
    row_spec = lambda w: pl.BlockSpec((tm, w), lambda i: (i, 0))
    out_shape = (
        jax.ShapeDtypeStruct((rows, MAIN_W), F32),
        jax.ShapeDtypeStruct((rows, KV_W), F32),
        jax.ShapeDtypeStruct((rows, KV_W), F32),
        jax.ShapeDtypeStruct((rows, IDX_DIM), F32),
        jax.ShapeDtypeStruct((rows, LANE), F32),
    )
    out_specs = (row_spec(MAIN_W), row_spec(KV_W), row_spec(KV_W), row_spec(IDX_DIM), row_spec(LANE))
    if key_tiles:
        t3 = lambda a, b: pl.BlockSpec((tm // TK, a, b), lambda i: (i, 0, 0))
        out_shape += (jax.ShapeDtypeStruct((rows // TK, TK, KV_W), BF16),
                      jax.ShapeDtypeStruct((rows // TK, TK, IDX_DIM), BF16),
                      jax.ShapeDtypeStruct((rows // TK, VT_ALL, TK), BF16))
        out_specs += (t3(TK, KV_W), t3(TK, IDX_DIM), t3(VT_ALL, TK))
    return pl.pallas_call(
        functools.partial(_inproj_kernel, tm=tm),
        grid=grid,
        in_specs=[row_spec(D_MODEL), _const_spec((1, D_MODEL)), _const_spec(w_perm.shape)],
        out_specs=out_specs,
        out_shape=out_shape,
        compiler_params=pltpu.CompilerParams(dimension_semantics=("parallel",),
                                             vmem_limit_bytes=VMEM_LIMIT),
        name="inproj",
    )(x2, gain, w_perm)


def _ret_gammas():
    return [1.0 - 2.0 ** (-5.0 - h) for h in range(RET_HEADS)]


def _ret_tables(sb, chunk):
    lg = np.log(np.array(_ret_gammas(), np.float64))
    t = np.arange(sb)
    ci = t // chunk
    diff = t[:, None] - t[None, :]
    same = ci[:, None] == ci[None, :]
    below = ci[None, :] < ci[:, None]
    expo = np.where(same, np.abs(diff), np.where(below, diff, 0)).astype(np.float64)
    dmat = np.exp(lg[:, None, None] * expo[None]) * (same | below)[None]
    qd = np.exp(lg[None, :] * (t + 1.0)[:, None])
    kd = np.exp(lg[None, :] * (sb - 1.0 - t)[:, None])
    qd = np.repeat(qd, RET_DK, axis=1)
    kd = np.repeat(kd, RET_DK, axis=1)
    return (jnp.asarray(dmat, F32), jnp.asarray(qd, F32), jnp.asarray(kd, F32),
            [float(math.exp(v * sb)) for v in lg])


def _ret_kernel(q_ref, k_ref, v_ref, g_ref, cos_ref, sin_ref, d_ref, qd_ref, kd_ref, s0_ref,
                o_ref, sfin_ref, s_scr, *, g_block):
    n = pl.program_id(1)

    @pl.when(n == 0)
    def _():
        s_scr[...] = s0_ref[0]

    reps = RET_W // LANE
    cos = jnp.concatenate([cos_ref[...]] * reps, axis=1)
    sin = jnp.concatenate([sin_ref[...]] * reps, axis=1)
    lane = lax.broadcasted_iota(jnp.int32, cos.shape, 1)
    first_half = (lane & (RET_DK - 1)) < RET_DK // 2

    def rot(x):
        partner = jnp.where(first_half, pltpu.roll(x, RET_W - RET_DK // 2, 1),
                            pltpu.roll(x, RET_DK // 2, 1))
        return x * cos + partner * sin

    q = rot(q_ref[...])
    k = rot(k_ref[...]) * (RET_DK ** -0.5)
    v = v_ref[...].astype(BF16)
    gate = g_ref[...]
    qb = q.astype(BF16)
    kb = k.astype(BF16)
    qx = (q * qd_ref[...]).astype(BF16)
    kx = (k * kd_ref[...]).astype(BF16)
    outs = []
    for h in range(RET_HEADS):
        sl = slice(h * RET_DK, (h + 1) * RET_DK)
        s = lax.dot_general(qb[:, sl], kb[:, sl], NT_DIMS, preferred_element_type=F32)
        p = (s * d_ref[h]).astype(BF16)
        st = s_scr[h]
        o = jnp.dot(p, v[:, sl], preferred_element_type=F32)
        o = o + jnp.dot(qx[:, sl], st.astype(BF16), preferred_element_type=F32)
        s_scr[h] = g_block[h] * st + lax.dot_general(kx[:, sl], v[:, sl], TN_DIMS,
                                                     preferred_element_type=F32)
        o = o * lax.rsqrt(jnp.mean(o * o, axis=-1, keepdims=True) + EPS)
        outs.append(o)
    o_all = jnp.concatenate(outs, axis=1)
    o_ref[...] = gate * (1.0 / (1.0 + jnp.exp(-gate))) * o_all

    @pl.when(n == pl.num_programs(1) - 1)
    def _():
        sfin_ref[0] = s_scr[...]


def _retention(main, s0, cos_t, sin_t, batch, t_len, sb, chunk):
    nsb = t_len // sb
    dmat, qd, kd, g_block = _ret_tables(sb, chunk)
    col = lambda c: pl.BlockSpec((sb, RET_W), lambda b, n, c=c: (b * nsb + n, c))
    tab = pl.BlockSpec((sb, LANE), lambda b, n: (n, 0))
    st_spec = pl.BlockSpec((1, RET_HEADS, RET_DK, RET_DV), lambda b, n: (b, 0, 0, 0))
    return pl.pallas_call(
        functools.partial(_ret_kernel, g_block=g_block),
        grid=(batch, nsb),
        in_specs=[col(0), col(1), col(2), col(3), tab, tab,
                  _const_spec(dmat.shape), _const_spec(qd.shape), _const_spec(kd.shape), st_spec],
        out_specs=(pl.BlockSpec((sb, RET_W), lambda b, n: (b * nsb + n, 0)), st_spec),
        out_shape=(jax.ShapeDtypeStruct((batch * t_len, RET_W), F32),
                   jax.ShapeDtypeStruct((batch, RET_HEADS, RET_DK, RET_DV), F32)),
        scratch_shapes=[pltpu.VMEM((RET_HEADS, RET_DK, RET_DV), F32)],
        compiler_params=pltpu.CompilerParams(dimension_semantics=("parallel", "arbitrary"),
                                             vmem_limit_bytes=VMEM_LIMIT),
        name="retention",
    )(main, main, main, main, cos_t, sin_t, dmat, qd, kd, s0)


def _attn_kernel(aq_ref, iq_ref, iw_ref, ki_ref, k_ref, vt_ref, bt_ref, o_ref,
                 s_scr, sb_scr, acc_scr, m_scr, p_scr, *, tq, nkt_static, last_valid, n_valid_q, n_sel,
                 max_search):
    qblk = pl.program_id(1)
    nkt = qblk + 1 if nkt_static is None else nkt_static
    lane_f = lax.broadcasted_iota(jnp.int32, (1, tq), 1).astype(F32)
    lane_ok = lane_f < float(n_valid_q)
    klim = jnp.minimum((jnp.floor(lane_f * (1.0 / CHUNK)) + 1.0) * CHUNK, float(last_valid))
    krow = lax.broadcasted_iota(jnp.int32, (TK, tq), 0).astype(F32)
    adm_last = krow < klim
    nkt_f = nkt.astype(F32) if nkt_static is None else float(nkt)
    n_adm = (nkt_f - 1.0) * TK + klim
    k_target = jnp.minimum(float(n_sel), n_adm)

    iq = iq_ref[...]
    w_t = iw_ref[...].T[IW_OFF:IW_OFF + IDX_HEADS, :] * (IDX_HEADS ** -0.5 * IDX_DIM ** -0.5)
    iq_all = jnp.concatenate(
        [iq[:, h * IDX_DIM:(h + 1) * IDX_DIM].astype(BF16) for h in range(IDX_HEADS)], axis=0)

    def score_tile(kt, n_tiles=1):
        ki_t = ki_ref[kt] if n_tiles == 1 else jnp.concatenate(
            [ki_ref[kt + j] for j in range(n_tiles)], axis=0)
        s_all = lax.dot_general(ki_t, iq_all, NT_DIMS, preferred_element_type=F32)
        acc = jnp.maximum(s_all[:, :tq], 0.0) * w_t[0:1, :]
        for h in range(1, IDX_HEADS):
            acc = acc + jnp.maximum(s_all[:, h * tq:(h + 1) * tq], 0.0) * w_t[h:h + 1, :]
        return acc

    def p1_step(kt, n_tiles, carry):
        rmax, rmin = carry
        sc = score_tile(kt, n_tiles)
        for j in range(n_tiles):
            s_scr[kt + j] = sc[j * TK:(j + 1) * TK]
            sb_scr[kt + j] = sc[j * TK:(j + 1) * TK].astype(BF16)
        return (jnp.maximum(rmax, jnp.max(sc, axis=0, keepdims=True)),
                jnp.minimum(rmin, jnp.min(sc, axis=0, keepdims=True)))

    n_full = nkt - 1
    init = (jnp.full((1, tq), -jnp.inf, F32), jnp.full((1, tq), jnp.inf, F32))
    rmax, rmin = lax.fori_loop(0, n_full // 2, lambda i, c: p1_step(2 * i, 2, c), init)
    rmax, rmin = lax.fori_loop(0, n_full % 2, lambda i, c: p1_step(n_full - 1, 1, c), (rmax, rmin))
    sc = score_tile(nkt - 1)
    s_scr[nkt - 1] = jnp.where(adm_last, sc, -jnp.inf)
    sb_scr[nkt - 1] = jnp.where(adm_last, sc, -jnp.inf).astype(BF16)
    rmax =jnp.maximum(rmax, jnp.max(jnp.where(adm_last, sc, -jnp.inf), axis=0, keepdims=True))
    rmin = jnp.minimum(rmin, jnp.min(jnp.where(adm_last, sc, jnp.inf), axis=0, keepdims=True))

    def fold_rows(x, op=jnp.add):
        parts = [x[i * CNT_ROWS:(i + 1) * CNT_ROWS] for i in range(TK // CNT_ROWS)]
        while len(parts) > 1:
            parts = [op(a, b) for a, b in zip(parts[0::2], parts[1::2])]
        return parts[0]

    def count_where(pred):
        def body(kt, c):
            return c + fold_rows(jnp.where(pred(kt), 1.0, 0.0))
        c = lax.fori_loop(0, nkt, body, jnp.zeros((CNT_ROWS, tq), F32))
        return jnp.sum(c, axis=0, keepdims=True)

    def count_ge(thr):
        return count_where(lambda kt: s_scr[kt] >= thr)

    def count_ge_rounded(thr_b):
        one, zero = jnp.ones((), BF16), jnp.zeros((), BF16)

        def body(kt, c):
            ind = jnp.where(sb_scr[kt] >= thr_b, one, zero)
            parts = [ind[i * BF16_ROWS:(i + 1) * BF16_ROWS] for i in range(TK // BF16_ROWS)]
            while len(parts) > 1:
                parts = [a + b for a, b in zip(parts[0::2], parts[1::2])]
            return c + parts[0].astype(F32)
        c = lax.fori_loop(0, nkt, body, jnp.zeros((BF16_ROWS, tq), F32))
        return jnp.sum(c, axis=0, keepdims=True)

    span = jnp.maximum(jnp.maximum(rmax - rmin, jnp.abs(rmax)), 1e-30)
    hi0 = rmax + span * (2.0 ** -10)

    def max_below(bound, n_tiles):
        def body(kt, m):
            s = s_scr[kt]
            return jnp.maximum(m, fold_rows(jnp.where(s < bound, s, -jnp.inf), jnp.maximum))
        m = lax.fori_loop(0, n_tiles, body, jnp.full((CNT_ROWS, tq), -jnp.inf, F32))
        return jnp.max(m, axis=0, keepdims=True)

    def plan(it, lo, hi, c_lo, c_hi, done):
        width = hi - lo
        frac = (c_lo - k_target - 0.5) / (c_lo - c_hi)
        interp = ((it + 1) % 2).astype(F32)
        mid_i = lo + width * (0.5 + interp * (frac - 0.5))
        mid_b = lo + 0.5 * width
        mid = jnp.where((mid_i > lo) & (mid_i < hi), mid_i, mid_b)
        active = done == 0.0
        narrow = active & (jnp.logical_not((mid > lo) & (mid < hi)) | (width <= span * NARROW))
        flag = jnp.max(jnp.where(narrow, 3.0, jnp.where(active, 1.0, 0.0)))
        return mid, jnp.where(narrow, 1.0, 0.0), flag

    def search_body(st):
        it, flag, lo, hi, c_lo, c_hi, done, mid, narrow_f = st
        narrow = narrow_f > 0.0
        top = max_below(hi, jnp.where(flag > 2.0, nkt, 0))
        mid = jnp.where(narrow, top, mid)
        c = count_ge(mid)
        active = done == 0.0
        ge = c >= k_target
        up = active & ge
        dn = active & jnp.logical_not(ge)
        lo = jnp.where(up, mid, lo)
        c_lo = jnp.where(up, c, c_lo)
        hi = jnp.where(dn, mid, hi)
        c_hi = jnp.where(dn, c, c_hi)
        done = jnp.where((c_lo == k_target) | (narrow & up), 1.0, done)
        mid, narrow_f, flag = plan(it + 1, lo, hi, c_lo, c_hi, done)
        return it + 1, flag, lo, hi, c_lo, c_hi, done, mid, narrow_f

    c_zero = count_ge(0.0)
    c_pos = count_ge(F32_TINY)
    pos_side = c_pos >= k_target
    at_zero = jnp.logical_not(pos_side) & (c_zero >= k_target)
    pick = lambda p, z, n: jnp.where(pos_side, p, jnp.where(at_zero, z, n))
    lo0 = pick(F32_TINY, 0.0, rmin)
    hi0 = pick(hi0, F32_TINY, 0.0)
    c_lo0 = pick(c_pos, c_zero, n_adm)
    c_hi0 = pick(0.0, c_pos, c_zero)
    done0 = jnp.where((n_adm == k_target) | at_zero | jnp.logical_not(lane_ok), 1.0, 0.0)

    def coarse_body(it, st):
        lo, hi, c_lo, c_hi, live = st
        width = hi - lo
        frac = (c_lo - k_target - 0.5) / (c_lo - c_hi)
        interp = ((it + 1) % 2).astype(F32)
        grid = lambda x: x.astype(BF16).astype(F32)
        mid_i = grid(lo + width * (0.5 + interp * (frac - 0.5)))
        mid_b = grid(lo + 0.5 * width)
        mid = jnp.where((mid_i > lo) & (mid_i < hi), mid_i, mid_b)
        ok = (mid > lo) & (mid < hi) & (live > 0.0)
        c = count_ge_rounded(mid.astype(BF16))
        ge = c >= k_target
        up = ok & ge
        dn = ok & jnp.logical_not(ge)
        return (jnp.where(up, mid, lo), jnp.where(dn, mid, hi), jnp.where(up, c, c_lo),
                jnp.where(dn, c, c_hi), jnp.where(ok, live, 0.0))

    lo1, hi0, c_lo0, c_hi0, _ = lax.fori_loop(
        0, N_COARSE, coarse_body, (lo0, hi0, c_lo0, c_hi0, 1.0 - done0))
    lo0 = jnp.where(lo1 != lo0, lo1 - jnp.abs(lo1) * BF16_STEP, lo0)
    c_lo0 = count_ge(lo0)
    done0 = jnp.where(c_lo0 == k_target, 1.0, done0)
    mid0, narrow0, flag0 = plan(jnp.int32(0), lo0, hi0, c_lo0, c_hi0, done0)
    st0 = (jnp.int32(0), flag0, lo0, hi0, c_lo0, c_hi0, done0, mid0, narrow0)
    _, _, lo, hi, cnt_lo, _, _, _, _ = lax.while_loop(
        lambda st: (st[0] < max_search) & (st[1] > 0.0), search_body, st0)

    excess = jnp.max(jnp.where(lane_ok, cnt_lo - k_target, 0.0))

    @pl.when(excess > 0.0)
    def _():
        need = k_target - count_ge(hi)
        r = lax.broadcasted_iota(jnp.int32, (TK, TK), 0)
        c = lax.broadcasted_iota(jnp.int32, (TK, TK), 1)
        prefix = jnp.where(r >= c, 1.0, 0.0).astype(BF16)

        def drop_body(kt, before):
            s = s_scr[kt]
            tie = (s >= lo) & (s < hi)
            tie_f = jnp.where(tie, 1.0, 0.0)
            rank = before + jnp.dot(prefix, tie_f.astype(BF16), preferred_element_type=F32)
            s_scr[kt] = jnp.where(tie & (rank > need), -jnp.inf, s)
            return before + jnp.sum(tie_f, axis=0, keepdims=True)
        lax.fori_loop(0, nkt, drop_body, jnp.zeros((1, tq), F32))

    aq = aq_ref[...] * (ATT_HD ** -0.5)
    zeros_q = jnp.zeros((tq, ATT_HD), F32)
    q_pad = []
    for h in range(ATT_HEADS):
        qh = aq[:, h * ATT_HD:(h + 1) * ATT_HD]
        parts = [zeros_q] * ATT_KV_HEADS
        parts[h // HEADS_PER_KV] = qh
        q_pad.append(jnp.concatenate(parts, axis=1).astype(BF16))

    m_scr[...] = jnp.full(m_scr.shape, NEG, F32)
    acc_scr[...] = jnp.zeros(acc_scr.shape, F32)

    def col_max(x):
        parts = [x[i * BF16_ROWS:(i + 1) * BF16_ROWS] for i in range(x.shape[0] // BF16_ROWS)]
        while len(parts) > 1:
            parts = [jnp.maximum(a, b) for a, b in zip(parts[0::2], parts[1::2])]
        return jnp.max(parts[0].astype(F32), axis=0, keepdims=True)

    cat = lambda xs, axis: xs[0] if len(xs) == 1 else jnp.concatenate(xs, axis=axis)

    def logits(kt, n_tiles, near):
        k_t = cat([k_ref[kt + j] for j in range(n_tiles)], 0)
        s_t = cat([s_scr[kt + j] for j in range(n_tiles)], 0)
        neg_mask = jnp.where(s_t >= lo, 0.0, NEG).astype(BF16)
        out = []
        for h in range(ATT_HEADS):
            lg = lax.dot_general(k_t, q_pad[h], NT_DIMS, preferred_element_type=F32)
            if near:
                lg = lg + cat([bt_ref[h, 2 - n_tiles + j] for j in range(n_tiles)], 0)
            out.append(lg.astype(BF16) + neg_mask)
        return out

    def softmax_pv(kt, n_tiles, lgs):
        tiles = [kt + j for j in range(n_tiles)]
        rows = n_tiles * TK
        alphas = []
        for h in range(ATT_HEADS):
            g, r = divmod(h, HEADS_PER_KV)
            lg = lgs[h]
            m_old = m_scr[h:h + 1, :]
            m_new = jnp.maximum(m_old, col_max(lg))
            alphas.append(jnp.exp(m_old - m_new))
            p_scr[g, :rows, r * tq:(r + 1) * tq] = jnp.exp(lg - m_new.astype(BF16))
            m_scr[h:h + 1, :] = m_new
        for g in range(ATT_KV_HEADS):
            hs = range(g * HEADS_PER_KV, (g + 1) * HEADS_PER_KV)
            a_g = jnp.concatenate([alphas[h] for h in hs], axis=1)
            vt_g = cat([vt_ref[t, g * VT_ROWS:(g + 1) * VT_ROWS, :] for t in tiles], 1)
            acc_scr[g] = acc_scr[g] * a_g + jnp.dot(vt_g, p_scr[g, :rows, :],
                                                    preferred_element_type=F32)

    def attend(kt, n_tiles, near):
        softmax_pv(kt, n_tiles, logits(kt, n_tiles, near))

    n_far = jnp.maximum(nkt - 2, 0) if nkt_static is None else max(nkt - 2, 0)

    def far_body(i, c):
        attend(2 * i, 2, False)
        return c
    lax.fori_loop(0, n_far // 2, far_body, 0)

    def when(cond):
        return pl.when(cond) if nkt_static is None else (lambda f: f() if cond else None)

    @when(n_far % 2 == 1)
    def _():
        attend(n_far - 1, 1, False)

    @when(nkt >= 2)
    def _():
        attend(nkt - 2, 2, True)

    @when(nkt < 2)
    def _():
        attend(0, 1, True)

    outs = []
    for g in range(ATT_KV_HEADS):
        a = acc_scr[g]
        o_g = a[:ATT_HD] * (1.0 / a[ATT_HD:ATT_HD + 1])
        outs += [o_g[:, r * tq:(r + 1) * tq] for r in range(HEADS_PER_KV)]
    o_ref[...] = jnp.concatenate(outs, axis=0).T


def _attention(aq_src, iq_src, iw, ki3, k3, vt3, bt, batch, n_qblk, tq, nkt_total, nkt_static,
               last_valid, n_valid_q, n_sel, aq_col, iq_col):
    kern = functools.partial(_attn_kernel, tq=tq, nkt_static=nkt_static, last_valid=last_valid,
                             n_valid_q=n_valid_q, n_sel=n_sel, max_search=MAX_SEARCH)
    qspec = lambda c: pl.BlockSpec((tq, ATT_W), lambda b, q, c=c: (b * n_qblk + q, c))
    kspec = lambda a, c: pl.BlockSpec((nkt_total, a, c), lambda b, q: (b, 0, 0))
    return pl.pallas_call(
        kern,
        grid=(batch, n_qblk),
        in_specs=[qspec(aq_col), qspec(iq_col),
                  pl.BlockSpec((tq, LANE), lambda b, q: (b * n_qblk + q, 0)),
                  kspec(TK, IDX_DIM), kspec(TK, KV_W), kspec(VT_ALL, TK),
                  _const_spec(bt.shape)],
        out_specs=pl.BlockSpec((tq, ATT_W), lambda b, q: (b * n_qblk + q, 0)),
        out_shape=jax.ShapeDtypeStruct((batch * n_qblk * tq, ATT_W), F32),
        scratch_shapes=[pltpu.VMEM((nkt_total, TK, tq), F32),
                        pltpu.VMEM((nkt_total, TK, tq), BF16),
                        pltpu.VMEM((ATT_KV_HEADS, VT_ROWS, HEADS_PER_KV * tq), F32),
                        pltpu.VMEM((ATT_HEADS, tq), F32),
                        pltpu.VMEM((ATT_KV_HEADS, 2 * TK, HEADS_PER_KV * tq), BF16)],
        compiler_params=pltpu.CompilerParams(dimension_semantics=("parallel", "arbitrary"),
                                             vmem_limit_bytes=VMEM_LIMIT),
        name="attention",
    )(aq_src, iq_src, iw, ki3, k3, vt3, bt)


def _ffn_kernel(x_ref, oret_ref, oatt_ref, wout_ref, wup_ref, wdown_ref,
                gpost_ref, gpre_ref, gffn_ref, y_ref, *, ff_chunk):
    mix = jnp.dot(oret_ref[...].astype(BF16), wout_ref[:RET_W, :], preferred_element_type=F32)
    mix = mix + jnp.dot(oatt_ref[...].astype(BF16), wout_ref[RET_W:, :], preferred_element_type=F32)
    h = x_ref[...] + _rms(mix, gpost_ref[...])
    a = _rms(h, gpre_ref[...]).astype(BF16)
    f = jnp.zeros(h.shape, F32)
    for c in range(D_FF // ff_chunk):
        sl = slice(c * ff_chunk, (c + 1) * ff_chunk)
        u = jnp.dot(a, wup_ref[:, sl], preferred_element_type=F32)
        u = jnp.square(jnp.maximum(u, 0.0)).astype(BF16)
        f = f + jnp.dot(u, wdown_ref[sl, :], preferred_element_type=F32)
    y_ref[...] = h + _rms(f, gffn_ref[...])


def _out_ffn(x2, o_ret, o_att, w_out, w_up, w_down, g_post, g_pre, g_ffn, tm):
    rows = x2.shape[0]
    row_spec = lambda w: pl.BlockSpec((tm, w), lambda i: (i, 0))
    return pl.pallas_call(
        functools.partial(_ffn_kernel, ff_chunk=512),
        grid=(rows // tm,),
        in_specs=[row_spec(D_MODEL), row_spec(RET_W), row_spec(ATT_W),
                  _const_spec(w_out.shape), _const_spec(w_up.shape), _const_spec(w_down.shape),
                  _const_spec((1, D_MODEL)), _const_spec((1, D_MODEL)), _const_spec((1, D_MODEL))],
        out_specs=row_spec(D_MODEL),
        out_shape=jax.ShapeDtypeStruct((rows, D_MODEL), F32),
        compiler_params=pltpu.CompilerParams(dimension_semantics=("parallel",),
                                             vmem_limit_bytes=VMEM_LIMIT),
        name="out_ffn",
    )(x2, o_ret, o_att, w_out, w_up, w_down, g_post, g_pre, g_ffn)


def _t5_bucket(rel):
    half = N_BUCKETS // 2
    max_exact = half // 2
    ret = jnp.where(rel > 0, half, 0)
    n = jnp.abs(rel)
    nf = jnp.maximum(n, 1).astype(F32)
    large = max_exact + (jnp.log(nf / max_exact) / math.log(MAX_DISTANCE / max_exact)
                         * (half - max_exact)).astype(jnp.int32)
    large = jnp.minimum(large, half - 1)
    return ret + jnp.where(n < max_exact, n, large)


def _bias_tiles(rel_bias, tq):
    period = 2 * TK + tq
    rel = jnp.arange(period, dtype=jnp.int32) - (TK + tq - 1)
    table = rel_bias.astype(F32)
    far = table[_t5_bucket(jnp.int32(-MAX_DISTANCE))]
    u = (table[_t5_bucket(jnp.clip(rel, -MAX_DISTANCE, MAX_DISTANCE))] - far).T
    n = jnp.tile(u, (1, tq))[:, :tq * (period - 1)].reshape(ATT_HEADS, tq, period - 1)
    m = n[:, :, tq - 1:tq - 1 + 2 * TK]
    return jnp.transpose(m.reshape(ATT_HEADS, tq, 2, TK), (0, 2, 3, 1))


def _rope_tables(pos):
    half = RET_DK // 2
    inv = ROPE_BASE ** (-jnp.arange(half, dtype=F32) / half)
    ang = pos.astype(F32)[:, None] * inv[None, :]
    cos, sin = jnp.cos(ang), jnp.sin(ang)
    cos_h = jnp.concatenate([cos, cos], axis=1)
    sin_h = jnp.concatenate([-sin, sin], axis=1)
    reps = LANE // RET_DK
    return jnp.tile(cos_h, (1, reps)), jnp.tile(sin_h, (1, reps))


def _permute_w_in(w_in):
    offs = np.cumsum([0, 512, 512, 512, 512, 512, 128, 128, 512, 64, 8])
    seg = lambda i: w_in[:, offs[i]:offs[i + 1]]
    pad = jnp.zeros((D_MODEL, LANE - IDX_DIM - IDX_HEADS), w_in.dtype)
    return jnp.concatenate([seg(0), seg(1), seg(2), seg(3), seg(4), seg(7),
                            seg(5), seg(6), seg(8), seg(9), pad], axis=1).astype(BF16)


def _layer(x, s0, past, weights, rel_bias):
    w_perm, w_out, w_up, w_down, g_pre_mix, g_post_mix, g_pre_ffn, g_post_ffn = weights
    batch, t_len, _ = x.shape
    rows = batch * t_len
    x2 = x.reshape(rows, D_MODEL)
    tm = min(512, rows)
    main, kc, vc, ic, iw, *key_tiles = _inproj(x2, g_pre_mix, w_perm, tm, key_tiles=past is None)

    p_len = 0 if past is None else past[0].shape[1]
    pos = p_len + jnp.arange(t_len, dtype=jnp.int32)
    cos_t, sin_t = _rope_tables(pos)
    chunk = min(CHUNK, t_len)
    sb = min(256, t_len)
    o_ret, s_new = _retention(main, s0, cos_t, sin_t, batch, t_len, sb, chunk)

    l_all = p_len + t_len
    n_sel = min(TOPK_MAX, l_all // 4)
    if past is None:
        tq = TK
        n_qblk = t_len // tq
        nkt_total = t_len // TK
        bt = _bias_tiles(rel_bias, tq)
        k3, ki3, vt3 = key_tiles
        o_att = _attention(main, main, iw, ki3, k3, vt3, bt, batch, n_qblk, tq, nkt_total, None,
                           TK, tq, n_sel, aq_col=4, iq_col=5)
    else:
        tq = LANE
        pk, pv, pi = past
        assert p_len % TK == 0 and t_len <= min(tq, TK), (p_len, t_len)
        nkt_total = p_len // TK + 1
        padk = nkt_total * TK - l_all
        cat = lambda old, new: jnp.concatenate(
            [old.astype(BF16), new.reshape(batch, t_len, -1).astype(BF16),
             jnp.zeros((batch, padk, new.shape[-1]), BF16)], axis=1)
        k_all = cat(pk.reshape(batch, p_len, KV_W), kc)
        v_all = cat(pv.reshape(batch, p_len, KV_W), vc)
        i_all = cat(pi, ic)
        v_t = jnp.transpose(v_all.reshape(batch, nkt_total, TK, ATT_KV_HEADS, ATT_HD), (0, 1, 3, 4, 2))
        ones = jnp.zeros((batch, nkt_total, ATT_KV_HEADS, VT_ROWS - ATT_HD, TK), BF16).at[:, :, :, 0, :].set(1.0)
        vt_s = jnp.concatenate([v_t, ones], axis=3).reshape(batch * nkt_total, VT_ALL, TK)
        k_s = k_all.reshape(batch * nkt_total, TK, KV_W)
        i_s = i_all.reshape(batch * nkt_total, TK, IDX_DIM)
        padq = lambda a: jnp.pad(a.reshape(batch, t_len, -1), ((0, 0), (0, tq - t_len), (0, 0))
                                 ).reshape(batch * tq, -1)
        aq_s = padq(main[:, 4 * RET_W:4 * RET_W + ATT_W])
        iq_s = padq(main[:, 4 * RET_W + ATT_W:])
        iw_s = padq(iw)
        bt = _bias_tiles(rel_bias, tq)
        o_pad = _attention(aq_s, iq_s, iw_s, i_s, k_s, vt_s, bt, batch, 1, tq, nkt_total, nkt_total,
                           l_all - p_len, t_len, n_sel, aq_col=0, iq_col=0)
        o_att = o_pad.reshape(batch, tq, ATT_W)[:, :t_len].reshape(rows, ATT_W)

    y = _out_ffn(x2, o_ret, o_att, w_out, w_up, w_down, g_post_mix, g_pre_ffn, g_post_ffn, tm)
    return (y.reshape(batch, t_len, D_MODEL), s_new,
            kc.reshape(batch, t_len, ATT_KV_HEADS, ATT_HD),
            vc.reshape(batch, t_len, ATT_KV_HEADS, ATT_HD),
            ic.reshape(batch, t_len, IDX_DIM))


def kernel(x_prompt, x_sample, state_ret, cache_k, cache_v, cache_kidx, w_in, w_out, w_up, w_down,
           g_pre_mix, g_post_mix, g_pre_ffn, g_post_ffn, rel_bias):
    depth = w_in.shape[0]
    bp = x_prompt.shape[0]
    zero_state = jnp.zeros((bp, RET_HEADS, RET_DK, RET_DV), F32)
    yp, ys = x_prompt, x_sample
    outs_p, outs_s = [], []
    for l in range(depth):
        row = lambda g: g[l].reshape(1, D_MODEL).astype(F32)
        weights = (_permute_w_in(w_in[l]), w_out[l].astype(BF16), w_up[l].astype(BF16),
                   w_down[l].astype(BF16), row(g_pre_mix), row(g_post_mix), row(g_pre_ffn),
                   row(g_post_ffn))
        yp, *rest_p = _layer(yp, zero_state, None, weights, rel_bias)
        ys, *rest_s = _layer(ys, state_ret[l], (cache_k[l], cache_v[l], cache_kidx[l]), weights, rel_bias)
        outs_p.append(rest_p)
        outs_s.append(rest_s)
    stack = lambda outs, i: jnp.stack([o[i] for o in outs])
    return (yp, ys,
            stack(outs_p, 0), stack(outs_p, 1), stack(outs_p, 2), stack(outs_p, 3),
            stack(outs_s, 0), stack(outs_s, 1), stack(outs_s, 2), stack(outs_s, 3))
```

```python
import functools
import math

import jax
import jax.numpy as jnp
import numpy as np
from jax import lax
from jax.experimental import pallas as pl
from jax.experimental.pallas import tpu as pltpu

D_MODEL = 1024
CHUNK = 64
RET_HEADS = 8
RET_DK = 64
RET_DV = 64
ATT_HEADS = 8
ATT_KV_HEADS = 2
ATT_HD = 64
IDX_HEADS = 8
IDX_DIM = 64
TOPK_MAX = 256
N_BUCKETS = 32
MAX_DISTANCE = 128
D_FF = 4 * D_MODEL
ROPE_BASE = 10000.0
EPS = 1e-6

RET_W = RET_HEADS * RET_DV
ATT_W = ATT_HEADS * ATT_HD
KV_W = ATT_KV_HEADS * ATT_HD
IDX_W = IDX_HEADS * IDX_DIM
HEADS_PER_KV = ATT_HEADS // ATT_KV_HEADS

MAIN_W = 4 * RET_W + ATT_W + IDX_W
TAIL_W = 3 * 128
IW_OFF = IDX_DIM

LANE = 128
VT_ROWS = 80
VT_ALL = ATT_KV_HEADS * VT_ROWS
TK = 256
NEG = -(2.0 ** 100)
BF16_ROWS = 16
CNT_ROWS = 32
NARROW = 2.0 ** -20
MAX_SEARCH = 96
F32_TINY = float(np.finfo(np.float32).tiny)
N_COARSE = 10
BF16_STEP = 2.0 ** -7
VMEM_LIMIT = 56 * 1024 * 1024

F32 = jnp.float32
BF16 = jnp.bfloat16
NT_DIMS = (((1,), (1,)), ((), ()))
TN_DIMS = (((0,), (0,)), ((), ()))


def _const_spec(shape):
    nd = len(shape)
    return pl.BlockSpec(shape, lambda *_: (0,) * nd, pipeline_mode=pl.Buffered(1))


def _rms(x, gain):
    return x * lax.rsqrt(jnp.mean(x * x, axis=-1, keepdims=True) + EPS) * gain


def _inproj_kernel(x_ref, g_ref, w_ref, main_ref, kc_ref, vc_ref, ic_ref, iw_ref,
                   *tile_refs, tm):
    a = _rms(x_ref[...], g_ref[...]).astype(BF16)
    for c in range(MAIN_W // 512):
        main_ref[:, c * 512:(c + 1) * 512] = jnp.dot(
            a, w_ref[:, c * 512:(c + 1) * 512], preferred_element_type=F32)
    tail = jnp.dot(a, w_ref[:, MAIN_W:MAIN_W + TAIL_W], preferred_element_type=F32)
    ak = tail[:, 0:KV_W]
    av = tail[:, KV_W:2 * KV_W]
    last = tail[:, 2 * KV_W:3 * KV_W]
    kc_ref[...] = ak
    vc_ref[...] = av
    ic_ref[...] = last[:, :IDX_DIM]
    iw_ref[...] = last
    if not tile_refs:
        return
    k3_ref, ki3_ref, vt3_ref = tile_refs
    avt = av.T
    row = lax.broadcasted_iota(jnp.int32, (VT_ROWS - ATT_HD, tm), 0)
    ones_rows = jnp.where(row == 0, 1.0, 0.0).astype(F32)
    vt = jnp.concatenate([avt[:ATT_HD], ones_rows, avt[ATT_HD:], ones_rows], axis=0).astype(BF16)
    for j in range(tm // TK):
        k3_ref[j] = ak[j * TK:(j + 1) * TK].astype(BF16)
        ki3_ref[j] = last[j * TK:(j + 1) * TK, :IDX_DIM].astype(BF16)
        vt3_ref[j] = vt[:, j * TK:(j + 1) * TK]


def _inproj(x2, gain, w_perm, tm, key_tiles):
    rows = x2.shape[0]
    grid = (rows // tm,)
    row_spec = lambda w: pl.BlockSpec((tm, w), lambda i: (i, 0))
    out_shape = (
        jax.ShapeDtypeStruct((rows, MAIN_W), F32),
        jax.ShapeDtypeStruct((rows, KV_W), F32),
        jax.ShapeDtypeStruct((rows, KV_W), F32),
        jax.ShapeDtypeStruct((rows, IDX_DIM), F32),
        jax.ShapeDtypeStruct((rows, LANE), F32),
    )
    out_specs = (row_spec(MAIN_W), row_spec(KV_W), row_spec(KV_W), row_spec(IDX_DIM), row_spec(LANE))
    if key_tiles:
        t3 = lambda a, b: pl.BlockSpec((tm // TK, a, b), lambda i: (i, 0, 0))
        out_shape += (jax.ShapeDtypeStruct((rows // TK, TK, KV_W), BF16),
                      jax.ShapeDtypeStruct((rows // TK, TK, IDX_DIM), BF16),
                      jax.ShapeDtypeStruct((rows // TK, VT_ALL, TK), BF16))
        out_specs += (t3(TK, KV_W), t3(TK, IDX_DIM), t3(VT_ALL, TK))
    return pl.pallas_call(
        functools.partial(_inproj_kernel, tm=tm),
        grid=grid,
        in_specs=[row_spec(D_MODEL), _const_spec((1, D_MODEL)), _const_spec(w_perm.shape)],
        out_specs=out_specs,
        out_shape=out_shape,
        compiler_params=pltpu.CompilerParams(dimension_semantics=("parallel",),
                                             vmem_limit_bytes=VMEM_LIMIT),
        name="inproj",
    )(x2, gain, w_perm)


def _ret_gammas():
    return [1.0 - 2.0 ** (-5.0 - h) for h in range(RET_HEADS)]


def _ret_tables(sb, chunk):
    lg = np.log(np.array(_ret_gammas(), np.float64))
    t = np.arange(sb)
    ci = t // chunk
    diff = t[:, None] - t[None, :]
    same = ci[:, None] == ci[None, :]
    below = ci[None, :] < ci[:, None]
    expo = np.where(same, np.abs(diff), np.where(below, diff, 0)).astype(np.float64)
    dmat = np.exp(lg[:, None, None] * expo[None]) * (same | below)[None]
    qd = np.exp(lg[None, :] * (t + 1.0)[:, None])
    kd = np.exp(lg[None, :] * (sb - 1.0 - t)[:, None])
    qd = np.repeat(qd, RET_DK, axis=1)
    kd = np.repeat(kd, RET_DK, axis=1)
    return (jnp.asarray(dmat, F32), jnp.asarray(qd, F32), jnp.asarray(kd, F32),
            [float(math.exp(v * sb)) for v in lg])


def _ret_kernel(q_ref, k_ref, v_ref, g_ref, cos_ref, sin_ref, d_ref, qd_ref, kd_ref, s0_ref,
                o_ref, sfin_ref, s_scr, *, g_block):
    n = pl.program_id(1)

    @pl.when(n == 0)
    def _():
        s_scr[...] = s0_ref[0]

    reps = RET_W // LANE
    cos = jnp.concatenate([cos_ref[...]] * reps, axis=1)
    sin = jnp.concatenate([sin_ref[...]] * reps, axis=1)
    lane = lax.broadcasted_iota(jnp.int32, cos.shape, 1)
    first_half = (lane & (RET_DK - 1)) < RET_DK // 2

    def rot(x):
        partner = jnp.where(first_half, pltpu.roll(x, RET_W - RET_DK // 2, 1),
                            pltpu.roll(x, RET_DK // 2, 1))
        return x * cos + partner * sin

    q = rot(q_ref[...])
    k = rot(k_ref[...]) * (RET_DK ** -0.5)
    v = v_ref[...].astype(BF16)
    gate = g_ref[...]
    qb = q.astype(BF16)
    kb = k.astype(BF16)
    qx = (q * qd_ref[...]).astype(BF16)
    kx = (k * kd_ref[...]).astype(BF16)
    outs = []
    for h in range(RET_HEADS):
        sl = slice(h * RET_DK, (h + 1) * RET_DK)
        s = lax.dot_general(qb[:, sl], kb[:, sl], NT_DIMS, preferred_element_type=F32)
        p = (s * d_ref[h]).astype(BF16)
        st = s_scr[h]
        o = jnp.dot(p, v[:, sl], preferred_element_type=F32)
        o = o + jnp.dot(qx[:, sl], st.astype(BF16), preferred_element_type=F32)
        s_scr[h] = g_block[h] * st + lax.dot_general(kx[:, sl], v[:, sl], TN_DIMS,
                                                     preferred_element_type=F32)
        o = o * lax.rsqrt(jnp.mean(o * o, axis=-1, keepdims=True) + EPS)
        outs.append(o)
    o_all = jnp.concatenate(outs, axis=1)
    o_ref[...] = gate * (1.0 / (1.0 + jnp.exp(-gate))) * o_all

    @pl.when(n == pl.num_programs(1) - 1)
    def _():
        sfin_ref[0] = s_scr[...]


def _retention(main, s0, cos_t, sin_t, batch, t_len, sb, chunk):
    nsb = t_len // sb
    dmat, qd, kd, g_block = _ret_tables(sb, chunk)
    col = lambda c: pl.BlockSpec((sb, RET_W), lambda b, n, c=c: (b * nsb + n, c))
    tab = pl.BlockSpec((sb, LANE), lambda b, n: (n, 0))
    st_spec = pl.BlockSpec((1, RET_HEADS, RET_DK, RET_DV), lambda b, n: (b, 0, 0, 0))
    return pl.pallas_call(
        functools.partial(_ret_kernel, g_block=g_block),
        grid=(batch, nsb),
        in_specs=[col(0), col(1), col(2), col(3), tab, tab,
                  _const_spec(dmat.shape), _const_spec(qd.shape), _const_spec(kd.shape), st_spec],
        out_specs=(pl.BlockSpec((sb, RET_W), lambda b, n: (b * nsb + n, 0)), st_spec),
        out_shape=(jax.ShapeDtypeStruct((batch * t_len, RET_W), F32),
                   jax.ShapeDtypeStruct((batch, RET_HEADS, RET_DK, RET_DV), F32)),
        scratch_shapes=[pltpu.VMEM((RET_HEADS, RET_DK, RET_DV), F32)],
        compiler_params=pltpu.CompilerParams(dimension_semantics=("parallel", "arbitrary"),
                                             vmem_limit_bytes=VMEM_LIMIT),
        name="retention",
    )(main, main, main, main, cos_t, sin_t, dmat, qd, kd, s0)


def _attn_kernel(aq_ref, iq_ref, iw_ref, ki_ref, k_ref, vt_ref, bt_ref, o_ref,
                 s_scr, sb_scr, acc_scr, m_scr, p_scr, *, tq, nkt_static, last_valid, n_valid_q, n_sel,
                 max_search):
    qblk = pl.program_id(1)
    nkt = qblk + 1 if nkt_static is None else nkt_static
    lane_f = lax.broadcasted_iota(jnp.int32, (1, tq), 1).astype(F32)
    lane_ok = lane_f < float(n_valid_q)
    klim = jnp.minimum((jnp.floor(lane_f * (1.0 / CHUNK)) + 1.0) * CHUNK, float(last_valid))
    krow = lax.broadcasted_iota(jnp.int32, (TK, tq), 0).astype(F32)
    adm_last = krow < klim
    nkt_f = nkt.astype(F32) if nkt_static is None else float(nkt)
    n_adm = (nkt_f - 1.0) * TK + klim
    k_target = jnp.minimum(float(n_sel), n_adm)

    iq = iq_ref[...]
    w_t = iw_ref[...].T[IW_OFF:IW_OFF + IDX_HEADS, :] * (IDX_HEADS ** -0.5 * IDX_DIM ** -0.5)
    iq_all = jnp.concatenate(
        [iq[:, h * IDX_DIM:(h + 1) * IDX_DIM].astype(BF16) for h in range(IDX_HEADS)], axis=0)

    def score_tile(kt, n_tiles=1):
        ki_t = ki_ref[kt] if n_tiles == 1 else jnp.concatenate(
            [ki_ref[kt + j] for j in range(n_tiles)], axis=0)
        s_all = lax.dot_general(ki_t, iq_all, NT_DIMS, preferred_element_type=F32)
        acc = jnp.maximum(s_all[:, :tq], 0.0) * w_t[0:1, :]
        for h in range(1, IDX_HEADS):
            acc = acc + jnp.maximum(s_all[:, h * tq:(h + 1) * tq], 0.0) * w_t[h:h + 1, :]
        return acc

    def fold_rows(x, op=jnp.add):
        parts = [x[i * CNT_ROWS:(i + 1) * CNT_ROWS] for i in range(x.shape[0] // CNT_ROWS)]
        while len(parts) > 1:
            parts = [op(a, b) for a, b in zip(parts[0::2], parts[1::2])]
        return parts[0]

    def p1_step(kt, n_tiles, carry):
        rmax, rmin = carry
        sc = score_tile(kt, n_tiles)
        for j in range(n_tiles):
            s_scr[kt + j] = sc[j * TK:(j + 1) * TK]
            sb_scr[kt + j] = sc[j * TK:(j + 1) * TK].astype(BF16)
        return (jnp.maximum(rmax, jnp.max(sc, axis=0, keepdims=True)),
                jnp.minimum(rmin, jnp.min(sc, axis=0, keepdims=True)))

    def run_steps(n, step, carry):
        n4 = n // 4
        carry = lax.fori_loop(0, n4, lambda i, c: step(4 * i, 4, c), carry)
        carry = lax.fori_loop(0, (n - 4 * n4) // 2, lambda i, c: step(4 * n4, 2, c), carry)
        return lax.fori_loop(0, n % 2, lambda i, c: step(n - 1, 1, c), carry)

    init = (jnp.full((1, tq), -jnp.inf, F32), jnp.full((1, tq), jnp.inf, F32))
    rmax, rmin = run_steps(nkt - 1, p1_step, init)
    sc = jnp.where(adm_last, score_tile(nkt - 1), -jnp.inf)
    s_scr[nkt - 1] = sc
    sb_scr[nkt - 1] = sc.astype(BF16)
    rmax = jnp.maximum(rmax, jnp.max(sc, axis=0, keepdims=True))
    rmin = jnp.minimum(rmin, jnp.min(jnp.where(adm_last, sc, jnp.inf), axis=0, keepdims=True))

    def count_where(pred):
        def body(kt, c):
            return c + fold_rows(jnp.where(pred(kt), 1.0, 0.0))
        c = lax.fori_loop(0, nkt, body, jnp.zeros((CNT_ROWS, tq), F32))
        return jnp.sum(c, axis=0, keepdims=True)

    def count_ge(thr):
        return count_where(lambda kt: s_scr[kt] >= thr)

    def count_ge_rounded(thr_b):
        one, zero = jnp.ones((), BF16), jnp.zeros((), BF16)

        def body(kt, c):
            ind = jnp.where(sb_scr[kt] >= thr_b, one, zero)
            parts = [ind[i * BF16_ROWS:(i + 1) * BF16_ROWS] for i in range(TK // BF16_ROWS)]
            while len(parts) > 1:
                parts = [a + b for a, b in zip(parts[0::2], parts[1::2])]
            return c + parts[0].astype(F32)
        c = lax.fori_loop(0, nkt, body, jnp.zeros((BF16_ROWS, tq), F32))
        return jnp.sum(c, axis=0, keepdims=True)

    span = jnp.maximum(jnp.maximum(rmax - rmin, jnp.abs(rmax)), 1e-30)
    hi0 = rmax + span * (2.0 ** -10)

    def max_below(bound, n_tiles):
        def body(kt, m):
            s = s_scr[kt]
            return jnp.maximum(m, fold_rows(jnp.where(s < bound, s, -jnp.inf), jnp.maximum))
        m = lax.fori_loop(0, n_tiles, body, jnp.full((CNT_ROWS, tq), -jnp.inf, F32))
        return jnp.max(m, axis=0, keepdims=True)

    def plan(it, lo, hi, c_lo, c_hi, done):
        width = hi - lo
        frac = (c_lo - k_target - 0.5) / (c_lo - c_hi)
        interp = ((it + 1) % 2).astype(F32)
        mid_i = lo + width * (0.5 + interp * (frac - 0.5))
        mid_b = lo + 0.5 * width
        mid = jnp.where((mid_i > lo) & (mid_i < hi), mid_i, mid_b)
        active = done == 0.0
        narrow = active & (jnp.logical_not((mid > lo) & (mid < hi)) | (width <= span * NARROW))
        flag = jnp.max(jnp.where(narrow, 3.0, jnp.where(active, 1.0, 0.0)))
        return mid, jnp.where(narrow, 1.0, 0.0), flag

    def search_body(st):
        it, flag, lo, hi, c_lo, c_hi, done, mid, narrow_f = st
        narrow = narrow_f > 0.0
        top = max_below(hi, jnp.where(flag > 2.0, nkt, 0))
        mid = jnp.where(narrow, top, mid)
        c = count_ge(mid)
        active = done == 0.0
        ge = c >= k_target
        up = active & ge
        dn = active & jnp.logical_not(ge)
        lo = jnp.where(up, mid, lo)
        c_lo = jnp.where(up, c, c_lo)
        hi = jnp.where(dn, mid, hi)
        c_hi = jnp.where(dn, c, c_hi)
        done = jnp.where((c_lo == k_target) | (narrow & up), 1.0, done)
        mid, narrow_f, flag = plan(it + 1, lo, hi, c_lo, c_hi, done)
        return it + 1, flag, lo, hi, c_lo, c_hi, done, mid, narrow_f

    c_zero = count_ge(0.0)
    c_pos = count_ge(F32_TINY)
    pos_side = c_pos >= k_target
    at_zero = jnp.logical_not(pos_side) & (c_zero >= k_target)
    pick = lambda p, z, n: jnp.where(pos_side, p, jnp.where(at_zero, z, n))
    lo0 = pick(F32_TINY, 0.0, rmin)
    hi0 = pick(hi0, F32_TINY, 0.0)
    c_lo0 = pick(c_pos, c_zero, n_adm)
    c_hi0 = pick(0.0, c_pos, c_zero)
    done0 = jnp.where((n_adm == k_target) | at_zero | jnp.logical_not(lane_ok), 1.0, 0.0)

    def coarse_body(it, st):
        lo, hi, c_lo, c_hi, live = st
        width = hi - lo
        frac = (c_lo - k_target - 0.5) / (c_lo - c_hi)
        interp = lax.convert_element_type((it + 1) % 2, F32)
        grid = lambda x: x.astype(BF16).astype(F32)
        mid_i = grid(lo + width * (0.5 + interp * (frac - 0.5)))
        mid_b = grid(lo + 0.5 * width)
        mid = jnp.where((mid_i > lo) & (mid_i < hi), mid_i, mid_b)
        ok = (mid > lo) & (mid < hi) & (live > 0.0)
        c = count_ge_rounded(mid.astype(BF16))
        ge = c >= k_target
        up = ok & ge
        dn = ok & jnp.logical_not(ge)
        return (jnp.where(up, mid, lo), jnp.where(dn, mid, hi), jnp.where(up, c, c_lo),
                jnp.where(dn, c, c_hi), jnp.where(ok, live, 0.0))

    lo1, hi0, c_lo0, c_hi0, _ = lax.fori_loop(
        0, N_COARSE, coarse_body, (lo0, hi0, c_lo0, c_hi0, 1.0 - done0))
    lo0 = jnp.where(lo1 != lo0, lo1 - jnp.abs(lo1) * BF16_STEP, lo0)
    c_lo0 = count_ge(lo0)
    done0 = jnp.where(c_lo0 == k_target, 1.0, done0)
    mid0, narrow0, flag0 = plan(jnp.int32(0), lo0, hi0, c_lo0, c_hi0, done0)
    st0 = (jnp.int32(0), flag0, lo0, hi0, c_lo0, c_hi0, done0, mid0, narrow0)
    _, _, lo, hi, cnt_lo, _, _, _, _ = lax.while_loop(
        lambda st: (st[0] < max_search) & (st[1] > 0.0), search_body, st0)

    excess = jnp.max(jnp.where(lane_ok, cnt_lo - k_target, 0.0))

    @pl.when(excess > 0.0)
    def _():
        need = k_target - count_ge(hi)
        r = lax.broadcasted_iota(jnp.int32, (TK, TK), 0)
        c = lax.broadcasted_iota(jnp.int32, (TK, TK), 1)
        prefix = jnp.where(r >= c, 1.0, 0.0).astype(BF16)

        def drop_step(kt, n_tiles, before):
            tiles = [s_scr[kt + j] for j in range(n_tiles)]
            ties = [(s >= lo) & (s < hi) for s in tiles]
            ranks = [jnp.dot(prefix, jnp.where(t, 1.0, 0.0).astype(BF16),
                             preferred_element_type=F32) for t in ties]
            for j in range(n_tiles):
                rank = before + ranks[j]
                s_scr[kt + j] = jnp.where(ties[j] & (rank > need), -jnp.inf, tiles[j])
                before = rank[TK - 1:TK, :]
            return before

        run_steps(nkt, drop_step, jnp.zeros((1, tq), F32))

    aq = aq_ref[...] * (ATT_HD ** -0.5)
    zeros_q = jnp.zeros((tq, ATT_HD), F32)
    q_pad = []
    for h in range(ATT_HEADS):
        qh = aq[:, h * ATT_HD:(h + 1) * ATT_HD]
        parts = [zeros_q] * ATT_KV_HEADS
        parts[h // HEADS_PER_KV] = qh
        q_pad.append(jnp.concatenate(parts, axis=1).astype(BF16))

    m_scr[...] = jnp.full(m_scr.shape, NEG, F32)
    acc_scr[...] = jnp.zeros(acc_scr.shape, F32)

    def col_max(x):
        parts = [x[i * BF16_ROWS:(i + 1) * BF16_ROWS] for i in range(x.shape[0] // BF16_ROWS)]
        while len(parts) > 1:
            parts = [jnp.maximum(a, b) for a, b in zip(parts[0::2], parts[1::2])]
        return jnp.max(parts[0].astype(F32), axis=0, keepdims=True)

    cat = lambda xs, axis: xs[0] if len(xs) == 1 else jnp.concatenate(xs, axis=axis)

    def logits(kt, n_tiles, near):
        k_t = cat([k_ref[kt + j] for j in range(n_tiles)], 0)
        s_t = cat([s_scr[kt + j] for j in range(n_tiles)], 0)
        neg_mask = jnp.where(s_t >= lo, 0.0, NEG).astype(BF16)
        out = []
        for h in range(ATT_HEADS):
            lg = lax.dot_general(k_t, q_pad[h], NT_DIMS, preferred_element_type=F32)
            if near:
                lg = lg + cat([bt_ref[h, 2 - n_tiles + j] for j in range(n_tiles)], 0)
            out.append(lg.astype(BF16) + neg_mask)
        return out

    def softmax_pv(kt, n_tiles, lgs):
        tiles = [kt + j for j in range(n_tiles)]
        rows = n_tiles * TK
        alphas = []
        for h in range(ATT_HEADS):
            g, r = divmod(h, HEADS_PER_KV)
            lg = lgs[h]
            m_old = m_scr[h:h + 1, :]
            m_new = jnp.maximum(m_old, col_max(lg))
            alphas.append(jnp.exp(m_old - m_new))
            p_scr[g, :rows, r * tq:(r + 1) * tq] = jnp.exp(lg - m_new.astype(BF16))
            m_scr[h:h + 1, :] = m_new
        for g in range(ATT_KV_HEADS):
            hs = range(g * HEADS_PER_KV, (g + 1) * HEADS_PER_KV)
            a_g = jnp.concatenate([alphas[h] for h in hs], axis=1)
            vt_g = cat([vt_ref[t, g * VT_ROWS:(g + 1) * VT_ROWS, :] for t in tiles], 1)
            acc_scr[g] = acc_scr[g] * a_g + jnp.dot(vt_g, p_scr[g, :rows, :],
                                                    preferred_element_type=F32)

    def attend(kt, n_tiles, near):
        softmax_pv(kt, n_tiles, logits(kt, n_tiles, near))

    n_far = jnp.maximum(nkt - 2, 0) if nkt_static is None else max(nkt - 2, 0)

    def far_body(i, c):
        attend(2 * i, 2, False)
        return c
    lax.fori_loop(0, n_far // 2, far_body, 0)

    def when(cond):
        return pl.when(cond) if nkt_static is None else (lambda f: f() if cond else None)

    @when(n_far % 2 == 1)
    def _():
        attend(n_far - 1, 1, False)

    @when(nkt >= 2)
    def _():
        attend(nkt - 2, 2, True)

    @when(nkt < 2)
    def _():
        attend(0, 1, True)

    outs = []
    for g in range(ATT_KV_HEADS):
        a = acc_scr[g]
        o_g = a[:ATT_HD] * (1.0 / a[ATT_HD:ATT_HD + 1])
        outs += [o_g[:, r * tq:(r + 1) * tq] for r in range(HEADS_PER_KV)]
    o_ref[...] = jnp.concatenate(outs, axis=0).T


def _attention(aq_src, iq_src, iw, ki3, k3, vt3, bt, batch, n_qblk, tq, nkt_total, nkt_static,
               last_valid, n_valid_q, n_sel, aq_col, iq_col):
    kern = functools.partial(_attn_kernel, tq=tq, nkt_static=nkt_static, last_valid=last_valid,
                             n_valid_q=n_valid_q, n_sel=n_sel, max_search=MAX_SEARCH)
    qspec = lambda c: pl.BlockSpec((tq, ATT_W), lambda b, q, c=c: (b * n_qblk + q, c))
    kspec = lambda a, c: pl.BlockSpec((nkt_total, a, c), lambda b, q: (b, 0, 0))
    return pl.pallas_call(
        kern,
        grid=(batch, n_qblk),
        in_specs=[qspec(aq_col), qspec(iq_col),
                  pl.BlockSpec((tq, LANE), lambda b, q: (b * n_qblk + q, 0)),
                  kspec(TK, IDX_DIM), kspec(TK, KV_W), kspec(VT_ALL, TK),
                  _const_spec(bt.shape)],
        out_specs=pl.BlockSpec((tq, ATT_W), lambda b, q: (b * n_qblk + q, 0)),
        out_shape=jax.ShapeDtypeStruct((batch * n_qblk * tq, ATT_W), F32),
        scratch_shapes=[pltpu.VMEM((nkt_total, TK, tq), F32),
                        pltpu.VMEM((nkt_total, TK, tq), BF16),
                        pltpu.VMEM((ATT_KV_HEADS, VT_ROWS, HEADS_PER_KV * tq), F32),
                        pltpu.VMEM((ATT_HEADS, tq), F32),
                        pltpu.VMEM((ATT_KV_HEADS, 2 * TK, HEADS_PER_KV * tq), BF16)],
        compiler_params=pltpu.CompilerParams(dimension_semantics=("parallel", "arbitrary"),
                                             vmem_limit_bytes=VMEM_LIMIT),
        name="attention",
    )(aq_src, iq_src, iw, ki3, k3, vt3, bt)


def _ffn_kernel(x_ref, oret_ref, oatt_ref, wout_ref, wup_ref, wdown_ref,
                gpost_ref, gpre_ref, gffn_ref, y_ref, *, ff_chunk):
    mix = jnp.dot(oret_ref[...].astype(BF16), wout_ref[:RET_W, :], preferred_element_type=F32)
    mix = mix + jnp.dot(oatt_ref[...].astype(BF16), wout_ref[RET_W:, :], preferred_element_type=F32)
    h = x_ref[...] + _rms(mix, gpost_ref[...])
    a = _rms(h, gpre_ref[...]).astype(BF16)
    f = jnp.zeros(h.shape, F32)
    for c in range(D_FF // ff_chunk):
        sl = slice(c * ff_chunk, (c + 1) * ff_chunk)
        u = jnp.dot(a, wup_ref[:, sl], preferred_element_type=F32)
        u = jnp.square(jnp.maximum(u, 0.0)).astype(BF16)
        f = f + jnp.dot(u, wdown_ref[sl, :], preferred_element_type=F32)
    y_ref[...] = h + _rms(f, gffn_ref[...])


def _out_ffn(x2, o_ret, o_att, w_out, w_up, w_down, g_post, g_pre, g_ffn, tm):
    rows = x2.shape[0]
    row_spec = lambda w: pl.BlockSpec((tm, w), lambda i: (i, 0))
    return pl.pallas_call(
        functools.partial(_ffn_kernel, ff_chunk=512),
        grid=(rows // tm,),
        in_specs=[row_spec(D_MODEL), row_spec(RET_W), row_spec(ATT_W),
                  _const_spec(w_out.shape), _const_spec(w_up.shape), _const_spec(w_down.shape),
                  _const_spec((1, D_MODEL)), _const_spec((1, D_MODEL)), _const_spec((1, D_MODEL))],
        out_specs=row_spec(D_MODEL),
        out_shape=jax.ShapeDtypeStruct((rows, D_MODEL), F32),
        compiler_params=pltpu.CompilerParams(dimension_semantics=("parallel",),
                                             vmem_limit_bytes=VMEM_LIMIT),
        name="out_ffn",
    )(x2, o_ret, o_att, w_out, w_up, w_down, g_post, g_pre, g_ffn)


def _t5_bucket(rel):
    half = N_BUCKETS // 2
    max_exact = half // 2
    ret = jnp.where(rel > 0, half, 0)
    n = jnp.abs(rel)
    nf = jnp.maximum(n, 1).astype(F32)
    large = max_exact + (jnp.log(nf / max_exact) / math.log(MAX_DISTANCE / max_exact)
                         * (half - max_exact)).astype(jnp.int32)
    large = jnp.minimum(large, half - 1)
    return ret + jnp.where(n < max_exact, n, large)


def _bias_tiles(rel_bias, tq):
    period = 2 * TK + tq
    rel = jnp.arange(period, dtype=jnp.int32) - (TK + tq - 1)
    table = rel_bias.astype(F32)
    far = table[_t5_bucket(jnp.int32(-MAX_DISTANCE))]
    u = (table[_t5_bucket(jnp.clip(rel, -MAX_DISTANCE, MAX_DISTANCE))] - far).T
    n = jnp.tile(u, (1, tq))[:, :tq * (period - 1)].reshape(ATT_HEADS, tq, period - 1)
    m = n[:, :, tq - 1:tq - 1 + 2 * TK]
    return jnp.transpose(m.reshape(ATT_HEADS, tq, 2, TK), (0, 2, 3, 1))


def _rope_tables(pos):
    half = RET_DK // 2
    inv = ROPE_BASE ** (-jnp.arange(half, dtype=F32) / half)
    ang = pos.astype(F32)[:, None] * inv[None, :]
    cos, sin = jnp.cos(ang), jnp.sin(ang)
    cos_h = jnp.concatenate([cos, cos], axis=1)
    sin_h = jnp.concatenate([-sin, sin], axis=1)
    reps = LANE // RET_DK
    return jnp.tile(cos_h, (1, reps)), jnp.tile(sin_h, (1, reps))


def _permute_w_in(w_in):
    offs = np.cumsum([0, 512, 512, 512, 512, 512, 128, 128, 512, 64, 8])
    seg = lambda i: w_in[:, offs[i]:offs[i + 1]]
    pad = jnp.zeros((D_MODEL, LANE - IDX_DIM - IDX_HEADS), w_in.dtype)
    return jnp.concatenate([seg(0), seg(1), seg(2), seg(3), seg(4), seg(7),
                            seg(5), seg(6), seg(8), seg(9), pad], axis=1).astype(BF16)


def _layer(x, s0, past, weights, rel_bias):
    w_perm, w_out, w_up, w_down, g_pre_mix, g_post_mix, g_pre_ffn, g_post_ffn = weights
    batch, t_len, _ = x.shape
    rows = batch * t_len
    x2 = x.reshape(rows, D_MODEL)
    tm = min(512, rows)
    main, kc, vc, ic, iw, *key_tiles = _inproj(x2, g_pre_mix, w_perm, tm, key_tiles=past is None)

    p_len = 0 if past is None else past[0].shape[1]
    pos = p_len + jnp.arange(t_len, dtype=jnp.int32)
    cos_t, sin_t = _rope_tables(pos)
    chunk = min(CHUNK, t_len)
    sb = min(256, t_len)
    o_ret, s_new = _retention(main, s0, cos_t, sin_t, batch, t_len, sb, chunk)

    l_all = p_len + t_len
    n_sel = min(TOPK_MAX, l_all // 4)
    if past is None:
        tq = TK
        n_qblk = t_len // tq
        nkt_total = t_len // TK
        bt = _bias_tiles(rel_bias, tq)
        k3, ki3, vt3 = key_tiles
        o_att = _attention(main, main, iw, ki3, k3, vt3, bt, batch, n_qblk, tq, nkt_total, None,
                           TK, tq, n_sel, aq_col=4, iq_col=5)
    else:
        tq = LANE
        pk, pv, pi = past
        assert p_len % TK == 0 and t_len <= min(tq, TK), (p_len, t_len)
        nkt_total = p_len // TK + 1
        padk = nkt_total * TK - l_all
        cat = lambda old, new: jnp.concatenate(
            [old.astype(BF16), new.reshape(batch, t_len, -1).astype(BF16),
             jnp.zeros((batch, padk, new.shape[-1]), BF16)], axis=1)
        k_all = cat(pk.reshape(batch, p_len, KV_W), kc)
        v_all = cat(pv.reshape(batch, p_len, KV_W), vc)
        i_all = cat(pi, ic)
        v_t = jnp.transpose(v_all.reshape(batch, nkt_total, TK, ATT_KV_HEADS, ATT_HD), (0, 1, 3, 4, 2))
        ones = jnp.zeros((batch, nkt_total, ATT_KV_HEADS, VT_ROWS - ATT_HD, TK), BF16).at[:, :, :, 0, :].set(1.0)
        vt_s = jnp.concatenate([v_t, ones], axis=3).reshape(batch * nkt_total, VT_ALL, TK)
        k_s = k_all.reshape(batch * nkt_total, TK, KV_W)
        i_s = i_all.reshape(batch * nkt_total, TK, IDX_DIM)
        padq = lambda a: jnp.pad(a.reshape(batch, t_len, -1), ((0, 0), (0, tq - t_len), (0, 0))
                                 ).reshape(batch * tq, -1)
        aq_s = padq(main[:, 4 * RET_W:4 * RET_W + ATT_W])
        iq_s = padq(main[:, 4 * RET_W + ATT_W:])
        iw_s = padq(iw)
        bt = _bias_tiles(rel_bias, tq)
        o_pad = _attention(aq_s, iq_s, iw_s, i_s, k_s, vt_s, bt, batch, 1, tq, nkt_total, nkt_total,
                           l_all - p_len, t_len, n_sel, aq_col=0, iq_col=0)
        o_att = o_pad.reshape(batch, tq, ATT_W)[:, :t_len].reshape(rows, ATT_W)

    y = _out_ffn(x2, o_ret, o_att, w_out, w_up, w_down, g_post_mix, g_pre_ffn, g_post_ffn, tm)
    return (y.reshape(batch, t_len, D_MODEL), s_new,
            kc.reshape(batch, t_len, ATT_KV_HEADS, ATT_HD),
            vc.reshape(batch, t_len, ATT_KV_HEADS, ATT_HD),
            ic.reshape(batch, t_len, IDX_DIM))


def kernel(x_prompt, x_sample, state_ret, cache_k, cache_v, cache_kidx, w_in, w_out, w_up, w_down,
           g_pre_mix, g_post_mix, g_pre_ffn, g_post_ffn, rel_bias):
    depth = w_in.shape[0]
    bp = x_prompt.shape[0]
    zero_state = jnp.zeros((bp, RET_HEADS, RET_DK, RET_DV), F32)
    yp, ys = x_prompt, x_sample
    outs_p, outs_s = [], []
    for l in range(depth):
        row = lambda g: g[l].reshape(1, D_MODEL).astype(F32)
        weights = (_permute_w_in(w_in[l]), w_out[l].astype(BF16), w_up[l].astype(BF16),
                   w_down[l].astype(BF16), row(g_pre_mix), row(g_post_mix), row(g_pre_ffn),
                   row(g_post_ffn))
        yp, *rest_p = _layer(yp, zero_state, None, weights, rel_bias)
        ys, *rest_s = _layer(ys, state_ret[l], (cache_k[l], cache_v[l], cache_kidx[l]), weights, rel_bias)
        outs_p.append(rest_p)
        outs_s.append(rest_s)
    stack = lambda outs, i: jnp.stack([o[i] for o in outs])
    return (yp, ys,
            stack(outs_p, 0), stack(outs_p, 1), stack(outs_p, 2), stack(outs_p, 3),
            stack(outs_s, 0), stack(outs_s, 1), stack(outs_s, 2), stack(outs_s, 3))
```

```python
import functools
import math

import jax
import jax.numpy as jnp
import numpy as np
from jax import lax
from jax.experimental import pallas as pl
from jax.experimental.pallas import tpu as pltpu

D_MODEL = 1024
CHUNK = 64
RET_HEADS = 8
RET_DK = 64
RET_DV = 64
ATT_HEADS = 8
ATT_KV_HEADS = 2
ATT_HD = 64
IDX_HEADS = 8
IDX_DIM = 64
TOPK_MAX = 256
N_BUCKETS = 32
MAX_DISTANCE = 128
D_FF = 4 * D_MODEL
ROPE_BASE = 10000.0
EPS = 1e-6

RET_W = RET_HEADS * RET_DV
ATT_W = ATT_HEADS * ATT_HD
KV_W = ATT_KV_HEADS * ATT_HD
IDX_W = IDX_HEADS * IDX_DIM
HEADS_PER_KV = ATT_HEADS // ATT_KV_HEADS

MAIN_W = 4 * RET_W + ATT_W + IDX_W
TAIL_W = 3 * 128
IW_OFF = IDX_DIM

LANE = 128
VT_ROWS = 80
VT_ALL = ATT_KV_HEADS * VT_ROWS
TK = 256
NEG = -(2.0 ** 100)
BF16_ROWS = 16
CNT_ROWS = 32
NARROW = 2.0 ** -20
MAX_SEARCH = 96
F32_TINY = float(np.finfo(np.float32).tiny)
N_COARSE = 8
SMALL_BRACKET = 2.0
BF16_STEP = 2.0 ** -7
VMEM_LIMIT = 56 * 1024 * 1024

F32 = jnp.float32
BF16 = jnp.bfloat16
NT_DIMS = (((1,), (1,)), ((), ()))
TN_DIMS = (((0,), (0,)), ((), ()))


def _const_spec(shape):
    nd = len(shape)
    return pl.BlockSpec(shape, lambda *_: (0,) * nd, pipeline_mode=pl.Buffered(1))


def _rms(x, gain):
    return x * lax.rsqrt(jnp.mean(x * x, axis=-1, keepdims=True) + EPS) * gain


def _inproj_kernel(x_ref, g_ref, w_ref, main_ref, kc_ref, vc_ref, ic_ref, iw_ref,
                   *tile_refs, tm):
    a = _rms(x_ref[...], g_ref[...]).astype(BF16)
    for c in range(MAIN_W // 512):
        main_ref[:, c * 512:(c + 1) * 512] = jnp.dot(
            a, w_ref[:, c * 512:(c + 1) * 512], preferred_element_type=F32)
    tail = jnp.dot(a, w_ref[:, MAIN_W:MAIN_W + TAIL_W], preferred_element_type=F32)
    ak = tail[:, 0:KV_W]
    av = tail[:, KV_W:2 * KV_W]
    last = tail[:, 2 * KV_W:3 * KV_W]
    kc_ref[...] = ak
    vc_ref[...] = av
    ic_ref[...] = last[:, :IDX_DIM]
    iw_ref[...] = last
    if not tile_refs:
        return
    k3_ref, ki3_ref, vt3_ref = tile_refs
    avt = av.T
    row = lax.broadcasted_iota(jnp.int32, (VT_ROWS - ATT_HD, tm), 0)
    ones_rows = jnp.where(row == 0, 1.0, 0.0).astype(F32)
    vt = jnp.concatenate([avt[:ATT_HD], ones_rows, avt[ATT_HD:], ones_rows], axis=0).astype(BF16)
    for j in range(tm // TK):
        k3_ref[j] = ak[j * TK:(j + 1) * TK].astype(BF16)
        ki3_ref[j] = last[j * TK:(j + 1) * TK, :IDX_DIM].astype(BF16)
        vt3_ref[j] = vt[:, j * TK:(j + 1) * TK]


def _inproj(x2, gain, w_perm, tm, key_tiles):
    rows = x2.shape[0]
    grid = (rows // tm,)
    row_spec = lambda w: pl.BlockSpec((tm, w), lambda i: (i, 0))
    out_shape = (
        jax.ShapeDtypeStruct((rows, MAIN_W), F32),
        jax.ShapeDtypeStruct((rows, KV_W), F32),
        jax.ShapeDtypeStruct((rows, KV_W), F32),
        jax.ShapeDtypeStruct((rows, IDX_DIM), F32),
        jax.ShapeDtypeStruct((rows, LANE), F32),
    )
    out_specs = (row_spec(MAIN_W), row_spec(KV_W), row_spec(KV_W), row_spec(IDX_DIM), row_spec(LANE))
    if key_tiles:
        t3 = lambda a, b: pl.BlockSpec((tm // TK, a, b), lambda i: (i, 0, 0))
        out_shape += (jax.ShapeDtypeStruct((rows // TK, TK, KV_W), BF16),
                      jax.ShapeDtypeStruct((rows // TK, TK, IDX_DIM), BF16),
                      jax.ShapeDtypeStruct((rows // TK, VT_ALL, TK), BF16))
        out_specs += (t3(TK, KV_W), t3(TK, IDX_DIM), t3(VT_ALL, TK))
    return pl.pallas_call(
        functools.partial(_inproj_kernel, tm=tm),
        grid=grid,
        in_specs=[row_spec(D_MODEL), _const_spec((1, D_MODEL)), _const_spec(w_perm.shape)],
        out_specs=out_specs,
        out_shape=out_shape,
        compiler_params=pltpu.CompilerParams(dimension_semantics=("parallel",),
                                             vmem_limit_bytes=VMEM_LIMIT),
        name="inproj",
    )(x2, gain, w_perm)


def _ret_gammas():
    return [1.0 - 2.0 ** (-5.0 - h) for h in range(RET_HEADS)]


def _ret_tables(sb, chunk):
    lg = np.log(np.array(_ret_gammas(), np.float64))
    t = np.arange(sb)
    ci = t // chunk
    diff = t[:, None] - t[None, :]
    same = ci[:, None] == ci[None, :]
    below = ci[None, :] < ci[:, None]
    expo = np.where(same, np.abs(diff), np.where(below, diff, 0)).astype(np.float64)
    dmat = np.exp(lg[:, None, None] * expo[None]) * (same | below)[None]
    qd = np.exp(lg[None, :] * (t + 1.0)[:, None])
    kd = np.exp(lg[None, :] * (sb - 1.0 - t)[:, None])
    qd = np.repeat(qd, RET_DK, axis=1)
    kd = np.repeat(kd, RET_DK, axis=1)
    return (jnp.asarray(dmat, F32), jnp.asarray(qd, F32), jnp.asarray(kd, F32),
            [float(math.exp(v * sb)) for v in lg])


def _ret_kernel(q_ref, k_ref, v_ref, g_ref, cos_ref, sin_ref, d_ref, qd_ref, kd_ref, s0_ref,
                o_ref, sfin_ref, s_scr, *, g_block):
    n = pl.program_id(1)

    @pl.when(n == 0)
    def _():
        s_scr[...] = s0_ref[0]

    reps = RET_W // LANE
    cos = jnp.concatenate([cos_ref[...]] * reps, axis=1)
    sin = jnp.concatenate([sin_ref[...]] * reps, axis=1)
    lane = lax.broadcasted_iota(jnp.int32, cos.shape, 1)
    first_half = (lane & (RET_DK - 1)) < RET_DK // 2

    def rot(x):
        partner = jnp.where(first_half, pltpu.roll(x, RET_W - RET_DK // 2, 1),
                            pltpu.roll(x, RET_DK // 2, 1))
        return x * cos + partner * sin

    q = rot(q_ref[...])
    k = rot(k_ref[...]) * (RET_DK ** -0.5)
    v = v_ref[...].astype(BF16)
    gate = g_ref[...]
    qb = q.astype(BF16)
    kb = k.astype(BF16)
    qx = (q * qd_ref[...]).astype(BF16)
    kx = (k * kd_ref[...]).astype(BF16)
    outs = []
    for h in range(RET_HEADS):
        sl = slice(h * RET_DK, (h + 1) * RET_DK)
        s = lax.dot_general(qb[:, sl], kb[:, sl], NT_DIMS, preferred_element_type=F32)
        p = (s * d_ref[h]).astype(BF16)
        st = s_scr[h]
        o = jnp.dot(p, v[:, sl], preferred_element_type=F32)
        o = o + jnp.dot(qx[:, sl], st.astype(BF16), preferred_element_type=F32)
        s_scr[h] = g_block[h] * st + lax.dot_general(kx[:, sl], v[:, sl], TN_DIMS,
                                                     preferred_element_type=F32)
        o = o * lax.rsqrt(jnp.mean(o * o, axis=-1, keepdims=True) + EPS)
        outs.append(o)
    o_all = jnp.concatenate(outs, axis=1)
    o_ref[...] = gate * (1.0 / (1.0 + jnp.exp(-gate))) * o_all

    @pl.when(n == pl.num_programs(1) - 1)
    def _():
        sfin_ref[0] = s_scr[...]


def _retention(main, s0, cos_t, sin_t, batch, t_len, sb, chunk):
    nsb = t_len // sb
    dmat, qd, kd, g_block = _ret_tables(sb, chunk)
    col = lambda c: pl.BlockSpec((sb, RET_W), lambda b, n, c=c: (b * nsb + n, c))
    tab = pl.BlockSpec((sb, LANE), lambda b, n: (n, 0))
    st_spec = pl.BlockSpec((1, RET_HEADS, RET_DK, RET_DV), lambda b, n: (b, 0, 0, 0))
    return pl.pallas_call(
        functools.partial(_ret_kernel, g_block=g_block),
        grid=(batch, nsb),
        in_specs=[col(0), col(1), col(2), col(3), tab, tab,
                  _const_spec(dmat.shape), _const_spec(qd.shape), _const_spec(kd.shape), st_spec],
        out_specs=(pl.BlockSpec((sb, RET_W), lambda b, n: (b * nsb + n, 0)), st_spec),
        out_shape=(jax.ShapeDtypeStruct((batch * t_len, RET_W), F32),
                   jax.ShapeDtypeStruct((batch, RET_HEADS, RET_DK, RET_DV), F32)),
        scratch_shapes=[pltpu.VMEM((RET_HEADS, RET_DK, RET_DV), F32)],
        compiler_params=pltpu.CompilerParams(dimension_semantics=("parallel", "arbitrary"),
                                             vmem_limit_bytes=VMEM_LIMIT),
        name="retention",
    )(main, main, main, main, cos_t, sin_t, dmat, qd, kd, s0)


def _attn_kernel(aq_ref, iq_ref, iw_ref, ki_ref, k_ref, vt_ref, bt_ref, o_ref,
                 s_scr, sb_scr, acc_scr, m_scr, p_scr, *, tq, nkt_static, last_valid, n_valid_q, n_sel,
                 max_search):
    qblk = pl.program_id(1)
    nkt = qblk + 1 if nkt_static is None else nkt_static
    lane_f = lax.broadcasted_iota(jnp.int32, (1, tq), 1).astype(F32)
    lane_ok = lane_f < float(n_valid_q)
    klim = jnp.minimum((jnp.floor(lane_f * (1.0 / CHUNK)) + 1.0) * CHUNK, float(last_valid))
    krow = lax.broadcasted_iota(jnp.int32, (TK, tq), 0).astype(F32)
    adm_last = krow < klim
    nkt_f = nkt.astype(F32) if nkt_static is None else float(nkt)
    n_adm = (nkt_f - 1.0) * TK + klim
    k_target = jnp.minimum(float(n_sel), n_adm)

    iq = iq_ref[...]
    w_t = iw_ref[...].T[IW_OFF:IW_OFF + IDX_HEADS, :] * (IDX_HEADS ** -0.5 * IDX_DIM ** -0.5)
    iq_all = jnp.concatenate(
        [iq[:, h * IDX_DIM:(h + 1) * IDX_DIM].astype(BF16) for h in range(IDX_HEADS)], axis=0)

    def score_tile(kt, n_tiles=1):
        ki_t = ki_ref[kt] if n_tiles == 1 else jnp.concatenate(
            [ki_ref[kt + j] for j in range(n_tiles)], axis=0)
        s_all = lax.dot_general(ki_t, iq_all, NT_DIMS, preferred_element_type=F32)
        acc = jnp.maximum(s_all[:, :tq], 0.0) * w_t[0:1, :]
        for h in range(1, IDX_HEADS):
            acc = acc + jnp.maximum(s_all[:, h * tq:(h + 1) * tq], 0.0) * w_t[h:h + 1, :]
        return acc

    def fold_rows(x, op=jnp.add):
        parts = [x[i * CNT_ROWS:(i + 1) * CNT_ROWS] for i in range(x.shape[0] // CNT_ROWS)]
        while len(parts) > 1:
            parts = [op(a, b) for a, b in zip(parts[0::2], parts[1::2])]
        return parts[0]

    def p1_step(kt, n_tiles, carry):
        rmax, rmin = carry
        sc = score_tile(kt, n_tiles)
        for j in range(n_tiles):
            s_scr[kt + j] = sc[j * TK:(j + 1) * TK]
            sb_scr[kt + j] = sc[j * TK:(j + 1) * TK].astype(BF16)
        return (jnp.maximum(rmax, jnp.max(sc, axis=0, keepdims=True)),
                jnp.minimum(rmin, jnp.min(sc, axis=0, keepdims=True)))

    def run_steps(n, step, carry):
        n4 = n // 4
        carry = lax.fori_loop(0, n4, lambda i, c: step(4 * i, 4, c), carry)
        carry = lax.fori_loop(0, (n - 4 * n4) // 2, lambda i, c: step(4 * n4, 2, c), carry)
        return lax.fori_loop(0, n % 2, lambda i, c: step(n - 1, 1, c), carry)

    init = (jnp.full((1, tq), -jnp.inf, F32), jnp.full((1, tq), jnp.inf, F32))
    rmax, rmin = run_steps(nkt - 1, p1_step, init)
    sc = jnp.where(adm_last, score_tile(nkt - 1), -jnp.inf)
    s_scr[nkt - 1] = sc
    sb_scr[nkt - 1] = sc.astype(BF16)
    rmax = jnp.maximum(rmax, jnp.max(sc, axis=0, keepdims=True))
    rmin = jnp.minimum(rmin, jnp.min(jnp.where(adm_last, sc, jnp.inf), axis=0, keepdims=True))

    def tile_loop(n, body, carry):
        carry = lax.fori_loop(0, n // 2, lambda i, c: body(2 * i + 1, body(2 * i, c)), carry)
        return lax.fori_loop(0, n % 2, lambda i, c: body(n - 1, c), carry)

    def count_where(pred):
        def body(kt, c):
            return c + fold_rows(jnp.where(pred(kt), 1.0, 0.0))
        c = tile_loop(nkt, body, jnp.zeros((CNT_ROWS, tq), F32))
        return jnp.sum(c, axis=0, keepdims=True)

    def count_ge(thr):
        return count_where(lambda kt: s_scr[kt] >= thr)

    def count_ge_rounded(thr_b):
        one, zero = jnp.ones((), BF16), jnp.zeros((), BF16)

        def body(kt, c):
            ind = jnp.where(sb_scr[kt] >= thr_b, one, zero)
            parts = [ind[i * BF16_ROWS:(i + 1) * BF16_ROWS] for i in range(TK // BF16_ROWS)]
            while len(parts) > 1:
                parts = [a + b for a, b in zip(parts[0::2], parts[1::2])]
            return c + parts[0].astype(F32)
        c = tile_loop(nkt, body, jnp.zeros((BF16_ROWS, tq), F32))
        return jnp.sum(c, axis=0, keepdims=True)

    span = jnp.maximum(jnp.maximum(rmax - rmin, jnp.abs(rmax)), 1e-30)
    hi0 = rmax + span * (2.0 ** -10)

    def max_below(bound, n_tiles):
        def body(kt, m):
            s = s_scr[kt]
            return jnp.maximum(m, fold_rows(jnp.where(s < bound, s, -jnp.inf), jnp.maximum))
        m = tile_loop(n_tiles, body, jnp.full((CNT_ROWS, tq), -jnp.inf, F32))
        return jnp.max(m, axis=0, keepdims=True)

    def plan(it, lo, hi, c_lo, c_hi, done):
        width = hi - lo
        frac = (c_lo - k_target - 0.5) / (c_lo - c_hi)
        interp = ((it + 1) % 2).astype(F32)
        mid_i = lo + width * (0.5 + interp * (frac - 0.5))
        mid_b = lo + 0.5 * width
        mid = jnp.where((mid_i > lo) & (mid_i < hi), mid_i, mid_b)
        active = done == 0.0
        narrow = jnp.logical_not((mid > lo) & (mid < hi)) | (width <= span * NARROW)
        wide = active & jnp.logical_not(narrow) & (c_lo - c_hi > SMALL_BRACKET)
        flag = jnp.max(jnp.where(wide, 5.0, jnp.where(active, 3.0, 0.0)))
        return mid, flag

    def search_body(st):
        it, flag, lo, hi, c_lo, c_hi, done, mid = st
        exact = (jnp.zeros((1, tq), F32) + flag) < 4.0
        top = max_below(hi, jnp.where(flag < 4.0, nkt, 0))
        mid = jnp.where(exact, top, mid)
        c = count_ge(mid)
        active = done == 0.0
        ge = c >= k_target
        up = active & ge
        dn = active & jnp.logical_not(ge)
        lo = jnp.where(up, mid, lo)
        c_lo = jnp.where(up, c, c_lo)
        hi = jnp.where(dn, mid, hi)
        c_hi = jnp.where(dn, c, c_hi)
        done = jnp.where((c_lo == k_target) | (exact & up), 1.0, done)
        mid, flag = plan(it + 1, lo, hi, c_lo, c_hi, done)
        return it + 1, flag, lo, hi, c_lo, c_hi, done, mid

    c_zero = count_ge(0.0)
    c_pos = count_ge(F32_TINY)
    pos_side = c_pos >= k_target
    at_zero = jnp.logical_not(pos_side) & (c_zero >= k_target)
    pick = lambda p, z, n: jnp.where(pos_side, p, jnp.where(at_zero, z, n))
    lo0 = pick(F32_TINY, 0.0, rmin)
    hi0 = pick(hi0, F32_TINY, 0.0)
    c_lo0 = pick(c_pos, c_zero, n_adm)
    c_hi0 = pick(0.0, c_pos, c_zero)
    done0 = jnp.where((n_adm == k_target) | at_zero | jnp.logical_not(lane_ok), 1.0, 0.0)

    def coarse_body(it, st):
        lo, hi, c_lo, c_hi, live = st
        width = hi - lo
        frac = (c_lo - k_target - 0.5) / (c_lo - c_hi)
        interp = lax.convert_element_type((it + 1) % 2, F32)
        grid = lambda x: x.astype(BF16).astype(F32)
        mid_i = grid(lo + width * (0.5 + interp * (frac - 0.5)))
        mid_b = grid(lo + 0.5 * width)
        mid = jnp.where((mid_i > lo) & (mid_i < hi), mid_i, mid_b)
        ok = (mid > lo) & (mid < hi) & (live > 0.0)
        c = count_ge_rounded(mid.astype(BF16))
        ge = c >= k_target
        up = ok & ge
        dn = ok & jnp.logical_not(ge)
        return (jnp.where(up, mid, lo), jnp.where(dn, mid, hi), jnp.where(up, c, c_lo),
                jnp.where(dn, c, c_hi), jnp.where(ok, live, 0.0))

    lo1, hi0, c_lo0, c_hi0, _ = lax.fori_loop(
        0, N_COARSE, coarse_body, (lo0, hi0, c_lo0, c_hi0, 1.0 - done0))
    lo0 = jnp.where(lo1 != lo0, lo1 - jnp.abs(lo1) * BF16_STEP, lo0)
    c_lo0 = count_ge(lo0)
    done0 = jnp.where(c_lo0 == k_target, 1.0, done0)
    mid0, flag0 = plan(jnp.int32(0), lo0, hi0, c_lo0, c_hi0, done0)
    st0 = (jnp.int32(0), flag0, lo0, hi0, c_lo0, c_hi0, done0, mid0)
    _, _, lo, hi, cnt_lo, _, _, _ = lax.while_loop(
        lambda st: (st[0] < max_search) & (st[1] > 0.0), search_body, st0)

    excess = jnp.max(jnp.where(lane_ok, cnt_lo - k_target, 0.0))

    @pl.when(excess > 0.0)
    def _():
        need = k_target - count_ge(hi)
        r = lax.broadcasted_iota(jnp.int32, (TK, TK), 0)
        c = lax.broadcasted_iota(jnp.int32, (TK, TK), 1)
        prefix = jnp.where(r >= c, 1.0, 0.0).astype(BF16)

        def drop_step(kt, n_tiles, before):
            tiles = [s_scr[kt + j] for j in range(n_tiles)]
            ties = [(s >= lo) & (s < hi) for s in tiles]
            ranks = [jnp.dot(prefix, jnp.where(t, 1.0, 0.0).astype(BF16),
                             preferred_element_type=F32) for t in ties]
            for j in range(n_tiles):
                rank = before + ranks[j]
                s_scr[kt + j] = jnp.where(ties[j] & (rank > need), -jnp.inf, tiles[j])
                before = rank[TK - 1:TK, :]
            return before

        run_steps(nkt, drop_step, jnp.zeros((1, tq), F32))

    aq = aq_ref[...] * (ATT_HD ** -0.5)
    zeros_q = jnp.zeros((tq, ATT_HD), F32)
    q_pad = []
    for h in range(ATT_HEADS):
        qh = aq[:, h * ATT_HD:(h + 1) * ATT_HD]
        parts = [zeros_q] * ATT_KV_HEADS
        parts[h // HEADS_PER_KV] = qh
        q_pad.append(jnp.concatenate(parts, axis=1).astype(BF16))

    m_scr[...] = jnp.full(m_scr.shape, NEG, F32)
    acc_scr[...] = jnp.zeros(acc_scr.shape, F32)

    def col_max(x):
        parts = [x[i * BF16_ROWS:(i + 1) * BF16_ROWS] for i in range(x.shape[0] // BF16_ROWS)]
        while len(parts) > 1:
            parts = [jnp.maximum(a, b) for a, b in zip(parts[0::2], parts[1::2])]
        return jnp.max(parts[0].astype(F32), axis=0, keepdims=True)

    cat = lambda xs, axis: xs[0] if len(xs) == 1 else jnp.concatenate(xs, axis=axis)

    def logits(kt, n_tiles, near):
        k_t = cat([k_ref[kt + j] for j in range(n_tiles)], 0)
        s_t = cat([s_scr[kt + j] for j in range(n_tiles)], 0)
        neg_mask = jnp.where(s_t >= lo, 0.0, NEG).astype(BF16)
        out = []
        for h in range(ATT_HEADS):
            lg = lax.dot_general(k_t, q_pad[h], NT_DIMS, preferred_element_type=F32)
            if near:
                lg = lg + cat([bt_ref[h, 2 - n_tiles + j] for j in range(n_tiles)], 0)
            out.append(lg.astype(BF16) + neg_mask)
        return out

    def softmax_pv(kt, n_tiles, lgs):
        tiles = [kt + j for j in range(n_tiles)]
        rows = n_tiles * TK
        alphas = []
        for h in range(ATT_HEADS):
            g, r = divmod(h, HEADS_PER_KV)
            lg = lgs[h]
            m_old = m_scr[h:h + 1, :]
            m_new = jnp.maximum(m_old, col_max(lg))
            alphas.append(jnp.exp(m_old - m_new))
            p_scr[g, :rows, r * tq:(r + 1) * tq] = jnp.exp(lg - m_new.astype(BF16))
            m_scr[h:h + 1, :] = m_new
        for g in range(ATT_KV_HEADS):
            hs = range(g * HEADS_PER_KV, (g + 1) * HEADS_PER_KV)
            a_g = jnp.concatenate([alphas[h] for h in hs], axis=1)
            vt_g = cat([vt_ref[t, g * VT_ROWS:(g + 1) * VT_ROWS, :] for t in tiles], 1)
            acc_scr[g] = acc_scr[g] * a_g + jnp.dot(vt_g, p_scr[g, :rows, :],
                                                    preferred_element_type=F32)

    def attend(kt, n_tiles, near):
        softmax_pv(kt, n_tiles, logits(kt, n_tiles, near))

    n_far = jnp.maximum(nkt - 2, 0) if nkt_static is None else max(nkt - 2, 0)

    def far_body(i, c):
        attend(2 * i, 2, False)
        return c
    lax.fori_loop(0, n_far // 2, far_body, 0)

    def when(cond):
        return pl.when(cond) if nkt_static is None else (lambda f: f() if cond else None)

    @when(n_far % 2 == 1)
    def _():
        attend(n_far - 1, 1, False)

    @when(nkt >= 2)
    def _():
        attend(nkt - 2, 2, True)

    @when(nkt < 2)
    def _():
        attend(0, 1, True)

    outs = []
    for g in range(ATT_KV_HEADS):
        a = acc_scr[g]
        o_g = a[:ATT_HD] * (1.0 / a[ATT_HD:ATT_HD + 1])
        outs += [o_g[:, r * tq:(r + 1) * tq] for r in range(HEADS_PER_KV)]
    o_ref[...] = jnp.concatenate(outs, axis=0).T


def _attention(aq_src, iq_src, iw, ki3, k3, vt3, bt, batch, n_qblk, tq, nkt_total, nkt_static,
               last_valid, n_valid_q, n_sel, aq_col, iq_col):
    kern = functools.partial(_attn_kernel, tq=tq, nkt_static=nkt_static, last_valid=last_valid,
                             n_valid_q=n_valid_q, n_sel=n_sel, max_search=MAX_SEARCH)
    qspec = lambda c: pl.BlockSpec((tq, ATT_W), lambda b, q, c=c: (b * n_qblk + q, c))
    kspec = lambda a, c: pl.BlockSpec((nkt_total, a, c), lambda b, q: (b, 0, 0))
    return pl.pallas_call(
        kern,
        grid=(batch, n_qblk),
        in_specs=[qspec(aq_col), qspec(iq_col),
                  pl.BlockSpec((tq, LANE), lambda b, q: (b * n_qblk + q, 0)),
                  kspec(TK, IDX_DIM), kspec(TK, KV_W), kspec(VT_ALL, TK),
                  _const_spec(bt.shape)],
        out_specs=pl.BlockSpec((tq, ATT_W), lambda b, q: (b * n_qblk + q, 0)),
        out_shape=jax.ShapeDtypeStruct((batch * n_qblk * tq, ATT_W), F32),
        scratch_shapes=[pltpu.VMEM((nkt_total, TK, tq), F32),
                        pltpu.VMEM((nkt_total, TK, tq), BF16),
                        pltpu.VMEM((ATT_KV_HEADS, VT_ROWS, HEADS_PER_KV * tq), F32),
                        pltpu.VMEM((ATT_HEADS, tq), F32),
                        pltpu.VMEM((ATT_KV_HEADS, 2 * TK, HEADS_PER_KV * tq), BF16)],
        compiler_params=pltpu.CompilerParams(dimension_semantics=("parallel", "arbitrary"),
                                             vmem_limit_bytes=VMEM_LIMIT),
        name="attention",
    )(aq_src, iq_src, iw, ki3, k3, vt3, bt)


def _ffn_kernel(x_ref, oret_ref, oatt_ref, wout_ref, wup_ref, wdown_ref,
                gpost_ref, gpre_ref, gffn_ref, y_ref, *, ff_chunk):
    mix = jnp.dot(oret_ref[...].astype(BF16), wout_ref[:RET_W, :], preferred_element_type=F32)
    mix = mix + jnp.dot(oatt_ref[...].astype(BF16), wout_ref[RET_W:, :], preferred_element_type=F32)
    h = x_ref[...] + _rms(mix, gpost_ref[...])
    a = _rms(h, gpre_ref[...]).astype(BF16)
    f = jnp.zeros(h.shape, F32)
    for c in range(D_FF // ff_chunk):
        sl = slice(c * ff_chunk, (c + 1) * ff_chunk)
        u = jnp.dot(a, wup_ref[:, sl], preferred_element_type=F32)
        u = jnp.square(jnp.maximum(u, 0.0)).astype(BF16)
        f = f + jnp.dot(u, wdown_ref[sl, :], preferred_element_type=F32)
    y_ref[...] = h + _rms(f, gffn_ref[...])


def _out_ffn(x2, o_ret, o_att, w_out, w_up, w_down, g_post, g_pre, g_ffn, tm):
    rows = x2.shape[0]
    row_spec = lambda w: pl.BlockSpec((tm, w), lambda i: (i, 0))
    return pl.pallas_call(
        functools.partial(_ffn_kernel, ff_chunk=512),
        grid=(rows // tm,),
        in_specs=[row_spec(D_MODEL), row_spec(RET_W), row_spec(ATT_W),
                  _const_spec(w_out.shape), _const_spec(w_up.shape), _const_spec(w_down.shape),
                  _const_spec((1, D_MODEL)), _const_spec((1, D_MODEL)), _const_spec((1, D_MODEL))],
        out_specs=row_spec(D_MODEL),
        out_shape=jax.ShapeDtypeStruct((rows, D_MODEL), F32),
        compiler_params=pltpu.CompilerParams(dimension_semantics=("parallel",),
                                             vmem_limit_bytes=VMEM_LIMIT),
        name="out_ffn",
    )(x2, o_ret, o_att, w_out, w_up, w_down, g_post, g_pre, g_ffn)


def _t5_bucket(rel):
    half = N_BUCKETS // 2
    max_exact = half // 2
    ret = jnp.where(rel > 0, half, 0)
    n = jnp.abs(rel)
    nf = jnp.maximum(n, 1).astype(F32)
    large = max_exact + (jnp.log(nf / max_exact) / math.log(MAX_DISTANCE / max_exact)
                         * (half - max_exact)).astype(jnp.int32)
    large = jnp.minimum(large, half - 1)
    return ret + jnp.where(n < max_exact, n, large)


def _bias_tiles(rel_bias, tq):
    period = 2 * TK + tq
    rel = jnp.arange(period, dtype=jnp.int32) - (TK + tq - 1)
    table = rel_bias.astype(F32)
    far = table[_t5_bucket(jnp.int32(-MAX_DISTANCE))]
    u = (table[_t5_bucket(jnp.clip(rel, -MAX_DISTANCE, MAX_DISTANCE))] - far).T
    n = jnp.tile(u, (1, tq))[:, :tq * (period - 1)].reshape(ATT_HEADS, tq, period - 1)
    m = n[:, :, tq - 1:tq - 1 + 2 * TK]
    return jnp.transpose(m.reshape(ATT_HEADS, tq, 2, TK), (0, 2, 3, 1))


def _rope_tables(pos):
    half = RET_DK // 2
    inv = ROPE_BASE ** (-jnp.arange(half, dtype=F32) / half)
    ang = pos.astype(F32)[:, None] * inv[None, :]
    cos, sin = jnp.cos(ang), jnp.sin(ang)
    cos_h = jnp.concatenate([cos, cos], axis=1)
    sin_h = jnp.concatenate([-sin, sin], axis=1)
    reps = LANE // RET_DK
    return jnp.tile(cos_h, (1, reps)), jnp.tile(sin_h, (1, reps))


def _permute_w_in(w_in):
    offs = np.cumsum([0, 512, 512, 512, 512, 512, 128, 128, 512, 64, 8])
    seg = lambda i: w_in[:, offs[i]:offs[i + 1]]
    pad = jnp.zeros((D_MODEL, LANE - IDX_DIM - IDX_HEADS), w_in.dtype)
    return jnp.concatenate([seg(0), seg(1), seg(2), seg(3), seg(4), seg(7),
                            seg(5), seg(6), seg(8), seg(9), pad], axis=1).astype(BF16)


def _layer(x, s0, past, weights, rel_bias):
    w_perm, w_out, w_up, w_down, g_pre_mix, g_post_mix, g_pre_ffn, g_post_ffn = weights
    batch, t_len, _ = x.shape
    rows = batch * t_len
    x2 = x.reshape(rows, D_MODEL)
    tm = min(512, rows)
    main, kc, vc, ic, iw, *key_tiles = _inproj(x2, g_pre_mix, w_perm, tm, key_tiles=past is None)

    p_len = 0 if past is None else past[0].shape[1]
    pos = p_len + jnp.arange(t_len, dtype=jnp.int32)
    cos_t, sin_t = _rope_tables(pos)
    chunk = min(CHUNK, t_len)
    sb = min(256, t_len)
    o_ret, s_new = _retention(main, s0, cos_t, sin_t, batch, t_len, sb, chunk)

    l_all = p_len + t_len
    n_sel = min(TOPK_MAX, l_all // 4)
    if past is None:
        tq = TK
        n_qblk = t_len // tq
        nkt_total = t_len // TK
        bt = _bias_tiles(rel_bias, tq)
        k3, ki3, vt3 = key_tiles
        o_att = _attention(main, main, iw, ki3, k3, vt3, bt, batch, n_qblk, tq, nkt_total, None,
                           TK, tq, n_sel, aq_col=4, iq_col=5)
    else:
        tq = LANE
        pk, pv, pi = past
        assert p_len % TK == 0 and t_len <= min(tq, TK), (p_len, t_len)
        nkt_total = p_len // TK + 1
        padk = nkt_total * TK - l_all
        cat = lambda old, new: jnp.concatenate(
            [old.astype(BF16), new.reshape(batch, t_len, -1).astype(BF16),
             jnp.zeros((batch, padk, new.shape[-1]), BF16)], axis=1)
        k_all = cat(pk.reshape(batch, p_len, KV_W), kc)
        v_all = cat(pv.reshape(batch, p_len, KV_W), vc)
        i_all = cat(pi, ic)
        v_t = jnp.transpose(v_all.reshape(batch, nkt_total, TK, ATT_KV_HEADS, ATT_HD), (0, 1, 3, 4, 2))
        ones = jnp.zeros((batch, nkt_total, ATT_KV_HEADS, VT_ROWS - ATT_HD, TK), BF16).at[:, :, :, 0, :].set(1.0)
        vt_s = jnp.concatenate([v_t, ones], axis=3).reshape(batch * nkt_total, VT_ALL, TK)
        k_s = k_all.reshape(batch * nkt_total, TK, KV_W)
        i_s = i_all.reshape(batch * nkt_total, TK, IDX_DIM)
        padq = lambda a: jnp.pad(a.reshape(batch, t_len, -1), ((0, 0), (0, tq - t_len), (0, 0))
                                 ).reshape(batch * tq, -1)
        aq_s = padq(main[:, 4 * RET_W:4 * RET_W + ATT_W])
        iq_s = padq(main[:, 4 * RET_W + ATT_W:])
        iw_s = padq(iw)
        bt = _bias_tiles(rel_bias, tq)
        o_pad = _attention(aq_s, iq_s, iw_s, i_s, k_s, vt_s, bt, batch, 1, tq, nkt_total, nkt_total,
                           l_all - p_len, t_len, n_sel, aq_col=0, iq_col=0)
        o_att = o_pad.reshape(batch, tq, ATT_W)[:, :t_len].reshape(rows, ATT_W)

    y = _out_ffn(x2, o_ret, o_att, w_out, w_up, w_down, g_post_mix, g_pre_ffn, g_post_ffn, tm)
    return (y.reshape(batch, t_len, D_MODEL), s_new,
            kc.reshape(batch, t_len, ATT_KV_HEADS, ATT_HD),
            vc.reshape(batch, t_len, ATT_KV_HEADS, ATT_HD),
            ic.reshape(batch, t_len, IDX_DIM))


def kernel(x_prompt, x_sample, state_ret, cache_k, cache_v, cache_kidx, w_in, w_out, w_up, w_down,
           g_pre_mix, g_post_mix, g_pre_ffn, g_post_ffn, rel_bias):
    depth = w_in.shape[0]
    bp = x_prompt.shape[0]
    zero_state = jnp.zeros((bp, RET_HEADS, RET_DK, RET_DV), F32)
    yp, ys = x_prompt, x_sample
    outs_p, outs_s = [], []
    for l in range(depth):
        row = lambda g: g[l].reshape(1, D_MODEL).astype(F32)
        weights = (_permute_w_in(w_in[l]), w_out[l].astype(BF16), w_up[l].astype(BF16),
                   w_down[l].astype(BF16), row(g_pre_mix), row(g_post_mix), row(g_pre_ffn),
                   row(g_post_ffn))
        yp, *rest_p = _layer(yp, zero_state, None, weights, rel_bias)
        ys, *rest_s = _layer(ys, state_ret[l], (cache_k[l], cache_v[l], cache_kidx[l]), weights, rel_bias)
        outs_p.append(rest_p)
        outs_s.append(rest_s)
    stack = lambda outs, i: jnp.stack([o[i] for o in outs])
    return (yp, ys,
            stack(outs_p, 0), stack(outs_p, 1), stack(outs_p, 2), stack(outs_p, 3),
            stack(outs_s, 0), stack(outs_s, 1), stack(outs_s, 2), stack(outs_s, 3))
```

```python
import functools
import math

import jax
import jax.numpy as jnp
import numpy as np
from jax import lax
from jax.experimental import pallas as pl
from jax.experimental.pallas import tpu as pltpu

D_MODEL = 1024
CHUNK = 64
RET_HEADS = 8
RET_DK = 64
RET_DV = 64
ATT_HEADS = 8
ATT_KV_HEADS = 2
ATT_HD = 64
IDX_HEADS = 8
IDX_DIM = 64
TOPK_MAX = 256
N_BUCKETS = 32
MAX_DISTANCE = 128
D_FF = 4 * D_MODEL
ROPE_BASE = 10000.0
EPS = 1e-6

RET_W = RET_HEADS * RET_DV
ATT_W = ATT_HEADS * ATT_HD
KV_W = ATT_KV_HEADS * ATT_HD
IDX_W = IDX_HEADS * IDX_DIM
HEADS_PER_KV = ATT_HEADS // ATT_KV_HEADS

MAIN_W = 4 * RET_W + ATT_W + IDX_W
TAIL_W = 3 * 128
IW_OFF = IDX_DIM

LANE = 128
VT_ROWS = 80
VT_ALL = ATT_KV_HEADS * VT_ROWS
TK = 256
NEG = -(2.0 ** 100)
BF16_ROWS = 16
CNT_ROWS = 32
NARROW = 2.0 ** -20
MAX_SEARCH = 96
F32_TINY = float(np.finfo(np.float32).tiny)
N_COARSE = 8
SMALL_BRACKET = 2.0
BF16_STEP = 2.0 ** -7
VMEM_LIMIT = 56 * 1024 * 1024

F32 = jnp.float32
BF16 = jnp.bfloat16
NT_DIMS = (((1,), (1,)), ((), ()))
TN_DIMS = (((0,), (0,)), ((), ()))


def _const_spec(shape):
    nd = len(shape)
    return pl.BlockSpec(shape, lambda *_: (0,) * nd, pipeline_mode=pl.Buffered(1))


def _rms(x, gain):
    return x * lax.rsqrt(jnp.mean(x * x, axis=-1, keepdims=True) + EPS) * gain


def _inproj_kernel(x_ref, g_ref, w_ref, main_ref, kc_ref, vc_ref, ic_ref, iw_ref,
                   *tile_refs, tm):
    a = _rms(x_ref[...], g_ref[...]).astype(BF16)
    for c in range(MAIN_W // 512):
        main_ref[:, c * 512:(c + 1) * 512] = jnp.dot(
            a, w_ref[:, c * 512:(c + 1) * 512], preferred_element_type=F32)
    tail = jnp.dot(a, w_ref[:, MAIN_W:MAIN_W + TAIL_W], preferred_element_type=F32)
    ak = tail[:, 0:KV_W]
    av = tail[:, KV_W:2 * KV_W]
    last = tail[:, 2 * KV_W:3 * KV_W]
    kc_ref[...] = ak
    vc_ref[...] = av
    ic_ref[...] = last[:, :IDX_DIM]
    iw_ref[...] = last
    if not tile_refs:
        return
    k3_ref, ki3_ref, vt3_ref = tile_refs
    avt = av.T
    row = lax.broadcasted_iota(jnp.int32, (VT_ROWS - ATT_HD, tm), 0)
    ones_rows = jnp.where(row == 0, 1.0, 0.0).astype(F32)
    vt = jnp.concatenate([avt[:ATT_HD], ones_rows, avt[ATT_HD:], ones_rows], axis=0).astype(BF16)
    for j in range(tm // TK):
        k3_ref[j] = ak[j * TK:(j + 1) * TK].astype(BF16)
        ki3_ref[j] = last[j * TK:(j + 1) * TK, :IDX_DIM].astype(BF16)
        vt3_ref[j] = vt[:, j * TK:(j + 1) * TK]


def _inproj(x2, gain, w_perm, tm, key_tiles):
    rows = x2.shape[0]
    grid = (rows // tm,)
    row_spec = lambda w: pl.BlockSpec((tm, w), lambda i: (i, 0))
    out_shape = (
        jax.ShapeDtypeStruct((rows, MAIN_W), F32),
        jax.ShapeDtypeStruct((rows, KV_W), F32),
        jax.ShapeDtypeStruct((rows, KV_W), F32),
        jax.ShapeDtypeStruct((rows, IDX_DIM), F32),
        jax.ShapeDtypeStruct((rows, LANE), F32),
    )
    out_specs = (row_spec(MAIN_W), row_spec(KV_W), row_spec(KV_W), row_spec(IDX_DIM), row_spec(LANE))
    if key_tiles:
        t3 = lambda a, b: pl.BlockSpec((tm // TK, a, b), lambda i: (i, 0, 0))
        out_shape += (jax.ShapeDtypeStruct((rows // TK, TK, KV_W), BF16),
                      jax.ShapeDtypeStruct((rows // TK, TK, IDX_DIM), BF16),
                      jax.ShapeDtypeStruct((rows // TK, VT_ALL, TK), BF16))
        out_specs += (t3(TK, KV_W), t3(TK, IDX_DIM), t3(VT_ALL, TK))
    return pl.pallas_call(
        functools.partial(_inproj_kernel, tm=tm),
        grid=grid,
        in_specs=[row_spec(D_MODEL), _const_spec((1, D_MODEL)), _const_spec(w_perm.shape)],
        out_specs=out_specs,
        out_shape=out_shape,
        compiler_params=pltpu.CompilerParams(dimension_semantics=("parallel",),
                                             vmem_limit_bytes=VMEM_LIMIT),
        name="inproj",
    )(x2, gain, w_perm)


def _ret_gammas():
    return [1.0 - 2.0 ** (-5.0 - h) for h in range(RET_HEADS)]


def _ret_tables(sb, chunk):
    lg = np.log(np.array(_ret_gammas(), np.float64))
    t = np.arange(sb)
    ci = t // chunk
    diff = t[:, None] - t[None, :]
    same = ci[:, None] == ci[None, :]
    below = ci[None, :] < ci[:, None]
    expo = np.where(same, np.abs(diff), np.where(below, diff, 0)).astype(np.float64)
    dmat = np.exp(lg[:, None, None] * expo[None]) * (same | below)[None]
    qd = np.exp(lg[None, :] * (t + 1.0)[:, None])
    kd = np.exp(lg[None, :] * (sb - 1.0 - t)[:, None])
    qd = np.repeat(qd, RET_DK, axis=1)
    kd = np.repeat(kd, RET_DK, axis=1)
    return (jnp.asarray(dmat, F32), jnp.asarray(qd, F32), jnp.asarray(kd, F32),
            [float(math.exp(v * sb)) for v in lg])


def _ret_kernel(q_ref, k_ref, v_ref, g_ref, cos_ref, sin_ref, d_ref, qd_ref, kd_ref, s0_ref,
                o_ref, sfin_ref, s_scr, *, g_block):
    n = pl.program_id(1)

    @pl.when(n == 0)
    def _():
        s_scr[...] = s0_ref[0]

    reps = RET_W // LANE
    cos = jnp.concatenate([cos_ref[...]] * reps, axis=1)
    sin = jnp.concatenate([sin_ref[...]] * reps, axis=1)
    lane = lax.broadcasted_iota(jnp.int32, cos.shape, 1)
    first_half = (lane & (RET_DK - 1)) < RET_DK // 2

    def rot(x):
        partner = jnp.where(first_half, pltpu.roll(x, RET_W - RET_DK // 2, 1),
                            pltpu.roll(x, RET_DK // 2, 1))
        return x * cos + partner * sin

    q = rot(q_ref[...])
    k = rot(k_ref[...]) * (RET_DK ** -0.5)
    v = v_ref[...].astype(BF16)
    gate = g_ref[...]
    qb = q.astype(BF16)
    kb = k.astype(BF16)
    qx = (q * qd_ref[...]).astype(BF16)
    kx = (k * kd_ref[...]).astype(BF16)
    outs = []
    for h in range(RET_HEADS):
        sl = slice(h * RET_DK, (h + 1) * RET_DK)
        s = lax.dot_general(qb[:, sl], kb[:, sl], NT_DIMS, preferred_element_type=F32)
        p = (s * d_ref[h]).astype(BF16)
        st = s_scr[h]
        o = jnp.dot(p, v[:, sl], preferred_element_type=F32)
        o = o + jnp.dot(qx[:, sl], st.astype(BF16), preferred_element_type=F32)
        s_scr[h] = g_block[h] * st + lax.dot_general(kx[:, sl], v[:, sl], TN_DIMS,
                                                     preferred_element_type=F32)
        o = o * lax.rsqrt(jnp.mean(o * o, axis=-1, keepdims=True) + EPS)
        outs.append(o)
    o_all = jnp.concatenate(outs, axis=1)
    o_ref[...] = gate * (1.0 / (1.0 + jnp.exp(-gate))) * o_all

    @pl.when(n == pl.num_programs(1) - 1)
    def _():
        sfin_ref[0] = s_scr[...]


def _retention(main, s0, cos_t, sin_t, batch, t_len, sb, chunk):
    nsb = t_len // sb
    dmat, qd, kd, g_block = _ret_tables(sb, chunk)
    col = lambda c: pl.BlockSpec((sb, RET_W), lambda b, n, c=c: (b * nsb + n, c))
    tab = pl.BlockSpec((sb, LANE), lambda b, n: (n, 0))
    st_spec = pl.BlockSpec((1, RET_HEADS, RET_DK, RET_DV), lambda b, n: (b, 0, 0, 0))
    return pl.pallas_call(
        functools.partial(_ret_kernel, g_block=g_block),
        grid=(batch, nsb),
        in_specs=[col(0), col(1), col(2), col(3), tab, tab,
                  _const_spec(dmat.shape), _const_spec(qd.shape), _const_spec(kd.shape), st_spec],
        out_specs=(pl.BlockSpec((sb, RET_W), lambda b, n: (b * nsb + n, 0)), st_spec),
        out_shape=(jax.ShapeDtypeStruct((batch * t_len, RET_W), F32),
                   jax.ShapeDtypeStruct((batch, RET_HEADS, RET_DK, RET_DV), F32)),
        scratch_shapes=[pltpu.VMEM((RET_HEADS, RET_DK, RET_DV), F32)],
        compiler_params=pltpu.CompilerParams(dimension_semantics=("parallel", "arbitrary"),
                                             vmem_limit_bytes=VMEM_LIMIT),
        name="retention",
    )(main, main, main, main, cos_t, sin_t, dmat, qd, kd, s0)


def _attn_kernel(aq_ref, iq_ref, iw_ref, ki_ref, k_ref, vt_ref, bt_ref, o_ref,
                 s_scr, sb_scr, acc_scr, m_scr, p_scr, *, tq, nkt_static, last_valid, n_valid_q, n_sel,
                 max_search):
    qblk = pl.program_id(1)
    nkt = qblk + 1 if nkt_static is None else nkt_static
    lane_f = lax.broadcasted_iota(jnp.int32, (1, tq), 1).astype(F32)
    lane_ok = lane_f < float(n_valid_q)
    klim = jnp.minimum((jnp.floor(lane_f * (1.0 / CHUNK)) + 1.0) * CHUNK, float(last_valid))
    krow = lax.broadcasted_iota(jnp.int32, (TK, tq), 0).astype(F32)
    adm_last = krow < klim
    nkt_f = nkt.astype(F32) if nkt_static is None else float(nkt)
    n_adm = (nkt_f - 1.0) * TK + klim
    k_target = jnp.minimum(float(n_sel), n_adm)

    iq = iq_ref[...]
    w_t = iw_ref[...].T[IW_OFF:IW_OFF + IDX_HEADS, :] * (IDX_HEADS ** -0.5 * IDX_DIM ** -0.5)
    iq_all = jnp.concatenate(
        [iq[:, h * IDX_DIM:(h + 1) * IDX_DIM].astype(BF16) for h in range(IDX_HEADS)], axis=0)

    def score_tile(kt, n_tiles=1):
        ki_t = ki_ref[kt] if n_tiles == 1 else jnp.concatenate(
            [ki_ref[kt + j] for j in range(n_tiles)], axis=0)
        s_all = lax.dot_general(ki_t, iq_all, NT_DIMS, preferred_element_type=F32)
        acc = jnp.maximum(s_all[:, :tq], 0.0) * w_t[0:1, :]
        for h in range(1, IDX_HEADS):
            acc = acc + jnp.maximum(s_all[:, h * tq:(h + 1) * tq], 0.0) * w_t[h:h + 1, :]
        return acc

    def fold_rows(x, op=jnp.add):
        parts = [x[i * CNT_ROWS:(i + 1) * CNT_ROWS] for i in range(x.shape[0] // CNT_ROWS)]
        while len(parts) > 1:
            parts = [op(a, b) for a, b in zip(parts[0::2], parts[1::2])]
        return parts[0]

    def p1_step(kt, n_tiles, carry):
        rmax, rmin = carry
        sc = score_tile(kt, n_tiles)
        for j in range(n_tiles):
            s_scr[kt + j] = sc[j * TK:(j + 1) * TK]
            sb_scr[kt + j] = sc[j * TK:(j + 1) * TK].astype(BF16)
        return (jnp.maximum(rmax, jnp.max(sc, axis=0, keepdims=True)),
                jnp.minimum(rmin, jnp.min(sc, axis=0, keepdims=True)))

    def run_steps(n, step, carry):
        n4 = n // 4
        carry = lax.fori_loop(0, n4, lambda i, c: step(4 * i, 4, c), carry)
        carry = lax.fori_loop(0, (n - 4 * n4) // 2, lambda i, c: step(4 * n4, 2, c), carry)
        return lax.fori_loop(0, n % 2, lambda i, c: step(n - 1, 1, c), carry)

    init = (jnp.full((1, tq), -jnp.inf, F32), jnp.full((1, tq), jnp.inf, F32))
    rmax, rmin = run_steps(nkt - 1, p1_step, init)
    sc = jnp.where(adm_last, score_tile(nkt - 1), -jnp.inf)
    s_scr[nkt - 1] = sc
    sb_scr[nkt - 1] = sc.astype(BF16)
    rmax = jnp.maximum(rmax, jnp.max(sc, axis=0, keepdims=True))
    rmin = jnp.minimum(rmin, jnp.min(jnp.where(adm_last, sc, jnp.inf), axis=0, keepdims=True))

    def tile_loop(n, body, carry):
        carry = lax.fori_loop(0, n // 2, lambda i, c: body(2 * i + 1, body(2 * i, c)), carry)
        return lax.fori_loop(0, n % 2, lambda i, c: body(n - 1, c), carry)

    def count_where(pred):
        def body(kt, c):
            return c + fold_rows(jnp.where(pred(kt), 1.0, 0.0))
        c = tile_loop(nkt, body, jnp.zeros((CNT_ROWS, tq), F32))
        return jnp.sum(c, axis=0, keepdims=True)

    def count_ge(thr):
        return count_where(lambda kt: s_scr[kt] >= thr)

    def count_ge_rounded(thr_b):
        one, zero = jnp.ones((), BF16), jnp.zeros((), BF16)

        def body(kt, c):
            ind = jnp.where(sb_scr[kt] >= thr_b, one, zero)
            parts = [ind[i * BF16_ROWS:(i + 1) * BF16_ROWS] for i in range(TK // BF16_ROWS)]
            while len(parts) > 1:
                parts = [a + b for a, b in zip(parts[0::2], parts[1::2])]
            return c + parts[0].astype(F32)
        c = tile_loop(nkt, body, jnp.zeros((BF16_ROWS, tq), F32))
        return jnp.sum(c, axis=0, keepdims=True)

    span = jnp.maximum(jnp.maximum(rmax - rmin, jnp.abs(rmax)), 1e-30)
    hi0 = rmax + span * (2.0 ** -10)

    def max_below(bound, n_tiles):
        def body(kt, m):
            s = s_scr[kt]
            return jnp.maximum(m, fold_rows(jnp.where(s < bound, s, -jnp.inf), jnp.maximum))
        m = tile_loop(n_tiles, body, jnp.full((CNT_ROWS, tq), -jnp.inf, F32))
        return jnp.max(m, axis=0, keepdims=True)

    def plan(it, lo, hi, c_lo, c_hi, done):
        width = hi - lo
        frac = (c_lo - k_target - 0.5) / (c_lo - c_hi)
        interp = ((it + 1) % 2).astype(F32)
        mid_i = lo + width * (0.5 + interp * (frac - 0.5))
        mid_b = lo + 0.5 * width
        mid = jnp.where((mid_i > lo) & (mid_i < hi), mid_i, mid_b)
        active = done == 0.0
        narrow = jnp.logical_not((mid > lo) & (mid < hi)) | (width <= span * NARROW)
        wide = active & jnp.logical_not(narrow) & (c_lo - c_hi > SMALL_BRACKET)
        flag = jnp.max(jnp.where(wide, 5.0, jnp.where(active, 3.0, 0.0)))
        return mid, flag

    def search_body(st):
        it, flag, lo, hi, c_lo, c_hi, done, mid = st
        exact = (jnp.zeros((1, tq), F32) + flag) < 4.0
        top = max_below(hi, jnp.where(flag < 4.0, nkt, 0))
        mid = jnp.where(exact, top, mid)
        c = count_ge(mid)
        active = done == 0.0
        ge = c >= k_target
        up = active & ge
        dn = active & jnp.logical_not(ge)
        lo = jnp.where(up, mid, lo)
        c_lo = jnp.where(up, c, c_lo)
        hi = jnp.where(dn, mid, hi)
        c_hi = jnp.where(dn, c, c_hi)
        done = jnp.where((c_lo == k_target) | (exact & up), 1.0, done)
        mid, flag = plan(it + 1, lo, hi, c_lo, c_hi, done)
        return it + 1, flag, lo, hi, c_lo, c_hi, done, mid

    c_zero = count_ge_rounded(jnp.zeros((1, tq), BF16))
    c_pos = count_ge_rounded(jnp.full((1, tq), F32_TINY, BF16))
    pos_side = c_pos >= k_target
    at_zero = jnp.logical_not(pos_side) & (c_zero >= k_target)
    pick = lambda p, z, n: jnp.where(pos_side, p, jnp.where(at_zero, z, n))
    lo0 = pick(F32_TINY, 0.0, rmin)
    hi0 = pick(hi0, F32_TINY, 0.0)
    c_lo0 = pick(c_pos, c_zero, n_adm)
    c_hi0 = pick(0.0, c_pos, c_zero)
    done0 = jnp.where((n_adm == k_target) | at_zero | jnp.logical_not(lane_ok), 1.0, 0.0)

    def coarse_body(it, st):
        lo, hi, c_lo, c_hi, live = st
        width = hi - lo
        frac = (c_lo - k_target - 0.5) / (c_lo - c_hi)
        interp = lax.convert_element_type((it + 1) % 2, F32)
        grid = lambda x: x.astype(BF16).astype(F32)
        mid_i = grid(lo + width * (0.5 + interp * (frac - 0.5)))
        mid_b = grid(lo + 0.5 * width)
        mid = jnp.where((mid_i > lo) & (mid_i < hi), mid_i, mid_b)
        ok = (mid > lo) & (mid < hi) & (live > 0.0)
        c = count_ge_rounded(mid.astype(BF16))
        ge = c >= k_target
        up = ok & ge
        dn = ok & jnp.logical_not(ge)
        return (jnp.where(up, mid, lo), jnp.where(dn, mid, hi), jnp.where(up, c, c_lo),
                jnp.where(dn, c, c_hi), jnp.where(ok, live, 0.0))

    lo1, hi0, c_lo0, c_hi0, _ = lax.fori_loop(
        0, N_COARSE, coarse_body, (lo0, hi0, c_lo0, c_hi0, 1.0 - done0))
    lo0 = jnp.where(lo1 != lo0, lo1 - jnp.abs(lo1) * BF16_STEP, lo0)
    c_lo0 = count_ge(lo0)
    done0 = jnp.where(c_lo0 == k_target, 1.0, done0)
    mid0, flag0 = plan(jnp.int32(0), lo0, hi0, c_lo0, c_hi0, done0)
    st0 = (jnp.int32(0), flag0, lo0, hi0, c_lo0, c_hi0, done0, mid0)
    _, _, lo, hi, cnt_lo, _, _, _ = lax.while_loop(
        lambda st: (st[0] < max_search) & (st[1] > 0.0), search_body, st0)

    excess = jnp.max(jnp.where(lane_ok, cnt_lo - k_target, 0.0))

    @pl.when(excess > 0.0)
    def _():
        need = k_target - count_ge(hi)
        r = lax.broadcasted_iota(jnp.int32, (TK, TK), 0)
        c = lax.broadcasted_iota(jnp.int32, (TK, TK), 1)
        prefix = jnp.where(r >= c, 1.0, 0.0).astype(BF16)

        def drop_step(kt, n_tiles, before):
            tiles = [s_scr[kt + j] for j in range(n_tiles)]
            ties = [(s >= lo) & (s < hi) for s in tiles]
            ranks = [jnp.dot(prefix, jnp.where(t, 1.0, 0.0).astype(BF16),
                             preferred_element_type=F32) for t in ties]
            for j in range(n_tiles):
                rank = before + ranks[j]
                s_scr[kt + j] = jnp.where(ties[j] & (rank > need), -jnp.inf, tiles[j])
                before = rank[TK - 1:TK, :]
            return before

        run_steps(nkt, drop_step, jnp.zeros((1, tq), F32))

    aq = aq_ref[...] * (ATT_HD ** -0.5)
    zeros_q = jnp.zeros((tq, ATT_HD), F32)
    q_pad = []
    for h in range(ATT_HEADS):
        qh = aq[:, h * ATT_HD:(h + 1) * ATT_HD]
        parts = [zeros_q] * ATT_KV_HEADS
        parts[h // HEADS_PER_KV] = qh
        q_pad.append(jnp.concatenate(parts, axis=1).astype(BF16))

    m_scr[...] = jnp.full(m_scr.shape, NEG, F32)
    acc_scr[...] = jnp.zeros(acc_scr.shape, F32)

    def col_max(x):
        parts = [x[i * BF16_ROWS:(i + 1) * BF16_ROWS] for i in range(x.shape[0] // BF16_ROWS)]
        while len(parts) > 1:
            parts = [jnp.maximum(a, b) for a, b in zip(parts[0::2], parts[1::2])]
        return jnp.max(parts[0].astype(F32), axis=0, keepdims=True)

    cat = lambda xs, axis: xs[0] if len(xs) == 1 else jnp.concatenate(xs, axis=axis)

    def logits(kt, n_tiles, near):
        k_t = cat([k_ref[kt + j] for j in range(n_tiles)], 0)
        s_t = cat([s_scr[kt + j] for j in range(n_tiles)], 0)
        neg_mask = jnp.where(s_t >= lo, 0.0, NEG).astype(BF16)
        out = []
        for h in range(ATT_HEADS):
            lg = lax.dot_general(k_t, q_pad[h], NT_DIMS, preferred_element_type=F32)
            if near:
                lg = lg + cat([bt_ref[h, 2 - n_tiles + j] for j in range(n_tiles)], 0)
            out.append(lg.astype(BF16) + neg_mask)
        return out

    def softmax_pv(kt, n_tiles, lgs):
        tiles = [kt + j for j in range(n_tiles)]
        rows = n_tiles * TK
        alphas = []
        for h in range(ATT_HEADS):
            g, r = divmod(h, HEADS_PER_KV)
            lg = lgs[h]
            m_old = m_scr[h:h + 1, :]
            m_new = jnp.maximum(m_old, col_max(lg))
            alphas.append(jnp.exp(m_old - m_new))
            p_scr[g, :rows, r * tq:(r + 1) * tq] = jnp.exp(lg - m_new.astype(BF16))
            m_scr[h:h + 1, :] = m_new
        for g in range(ATT_KV_HEADS):
            hs = range(g * HEADS_PER_KV, (g + 1) * HEADS_PER_KV)
            a_g = jnp.concatenate([alphas[h] for h in hs], axis=1)
            vt_g = cat([vt_ref[t, g * VT_ROWS:(g + 1) * VT_ROWS, :] for t in tiles], 1)
            acc_scr[g] = acc_scr[g] * a_g + jnp.dot(vt_g, p_scr[g, :rows, :],
                                                    preferred_element_type=F32)

    def attend(kt, n_tiles, near):
        softmax_pv(kt, n_tiles, logits(kt, n_tiles, near))

    n_far = jnp.maximum(nkt - 2, 0) if nkt_static is None else max(nkt - 2, 0)

    def far_body(i, c):
        attend(2 * i, 2, False)
        return c
    lax.fori_loop(0, n_far // 2, far_body, 0)

    def when(cond):
        return pl.when(cond) if nkt_static is None else (lambda f: f() if cond else None)

    @when(n_far % 2 == 1)
    def _():
        attend(n_far - 1, 1, False)

    @when(nkt >= 2)
    def _():
        attend(nkt - 2, 2, True)

    @when(nkt < 2)
    def _():
        attend(0, 1, True)

    outs = []
    for g in range(ATT_KV_HEADS):
        a = acc_scr[g]
        o_g = a[:ATT_HD] * (1.0 / a[ATT_HD:ATT_HD + 1])
        outs += [o_g[:, r * tq:(r + 1) * tq] for r in range(HEADS_PER_KV)]
    o_ref[...] = jnp.concatenate(outs, axis=0).T


def _attention(aq_src, iq_src, iw, ki3, k3, vt3, bt, batch, n_qblk, tq, nkt_total, nkt_static,
               last_valid, n_valid_q, n_sel, aq_col, iq_col):
    kern = functools.partial(_attn_kernel, tq=tq, nkt_static=nkt_static, last_valid=last_valid,
                             n_valid_q=n_valid_q, n_sel=n_sel, max_search=MAX_SEARCH)
    qspec = lambda c: pl.BlockSpec((tq, ATT_W), lambda b, q, c=c: (b * n_qblk + q, c))
    kspec = lambda a, c: pl.BlockSpec((nkt_total, a, c), lambda b, q: (b, 0, 0))
    return pl.pallas_call(
        kern,
        grid=(batch, n_qblk),
        in_specs=[qspec(aq_col), qspec(iq_col),
                  pl.BlockSpec((tq, LANE), lambda b, q: (b * n_qblk + q, 0)),
                  kspec(TK, IDX_DIM), kspec(TK, KV_W), kspec(VT_ALL, TK),
                  _const_spec(bt.shape)],
        out_specs=pl.BlockSpec((tq, ATT_W), lambda b, q: (b * n_qblk + q, 0)),
        out_shape=jax.ShapeDtypeStruct((batch * n_qblk * tq, ATT_W), F32),
        scratch_shapes=[pltpu.VMEM((nkt_total, TK, tq), F32),
                        pltpu.VMEM((nkt_total, TK, tq), BF16),
                        pltpu.VMEM((ATT_KV_HEADS, VT_ROWS, HEADS_PER_KV * tq), F32),
                        pltpu.VMEM((ATT_HEADS, tq), F32),
                        pltpu.VMEM((ATT_KV_HEADS, 2 * TK, HEADS_PER_KV * tq), BF16)],
        compiler_params=pltpu.CompilerParams(dimension_semantics=("parallel", "arbitrary"),
                                             vmem_limit_bytes=VMEM_LIMIT),
        name="attention",
    )(aq_src, iq_src, iw, ki3, k3, vt3, bt)


def _ffn_kernel(x_ref, oret_ref, oatt_ref, wout_ref, wup_ref, wdown_ref,
                gpost_ref, gpre_ref, gffn_ref, y_ref, *, ff_chunk):
    mix = jnp.dot(oret_ref[...].astype(BF16), wout_ref[:RET_W, :], preferred_element_type=F32)
    mix = mix + jnp.dot(oatt_ref[...].astype(BF16), wout_ref[RET_W:, :], preferred_element_type=F32)
    h = x_ref[...] + _rms(mix, gpost_ref[...])
    a = _rms(h, gpre_ref[...]).astype(BF16)
    f = jnp.zeros(h.shape, F32)
    for c in range(D_FF // ff_chunk):
        sl = slice(c * ff_chunk, (c + 1) * ff_chunk)
        u = jnp.dot(a, wup_ref[:, sl], preferred_element_type=F32)
        u = jnp.square(jnp.maximum(u, 0.0)).astype(BF16)
        f = f + jnp.dot(u, wdown_ref[sl, :], preferred_element_type=F32)
    y_ref[...] = h + _rms(f, gffn_ref[...])


def _out_ffn(x2, o_ret, o_att, w_out, w_up, w_down, g_post, g_pre, g_ffn, tm):
    rows = x2.shape[0]
    row_spec = lambda w: pl.BlockSpec((tm, w), lambda i: (i, 0))
    return pl.pallas_call(
        functools.partial(_ffn_kernel, ff_chunk=512),
        grid=(rows // tm,),
        in_specs=[row_spec(D_MODEL), row_spec(RET_W), row_spec(ATT_W),
                  _const_spec(w_out.shape), _const_spec(w_up.shape), _const_spec(w_down.shape),
                  _const_spec((1, D_MODEL)), _const_spec((1, D_MODEL)), _const_spec((1, D_MODEL))],
        out_specs=row_spec(D_MODEL),
        out_shape=jax.ShapeDtypeStruct((rows, D_MODEL), F32),
        compiler_params=pltpu.CompilerParams(dimension_semantics=("parallel",),
                                             vmem_limit_bytes=VMEM_LIMIT),
        name="out_ffn",
    )(x2, o_ret, o_att, w_out, w_up, w_down, g_post, g_pre, g_ffn)


def _t5_bucket(rel):
    half = N_BUCKETS // 2
    max_exact = half // 2
    ret = jnp.where(rel > 0, half, 0)
    n = jnp.abs(rel)
    nf = jnp.maximum(n, 1).astype(F32)
    large = max_exact + (jnp.log(nf / max_exact) / math.log(MAX_DISTANCE / max_exact)
                         * (half - max_exact)).astype(jnp.int32)
    large = jnp.minimum(large, half - 1)
    return ret + jnp.where(n < max_exact, n, large)


def _bias_tiles(rel_bias, tq):
    period = 2 * TK + tq
    rel = jnp.arange(period, dtype=jnp.int32) - (TK + tq - 1)
    table = rel_bias.astype(F32)
    far = table[_t5_bucket(jnp.int32(-MAX_DISTANCE))]
    u = (table[_t5_bucket(jnp.clip(rel, -MAX_DISTANCE, MAX_DISTANCE))] - far).T
    n = jnp.tile(u, (1, tq))[:, :tq * (period - 1)].reshape(ATT_HEADS, tq, period - 1)
    m = n[:, :, tq - 1:tq - 1 + 2 * TK]
    return jnp.transpose(m.reshape(ATT_HEADS, tq, 2, TK), (0, 2, 3, 1))


def _rope_tables(pos):
    half = RET_DK // 2
    inv = ROPE_BASE ** (-jnp.arange(half, dtype=F32) / half)
    ang = pos.astype(F32)[:, None] * inv[None, :]
    cos, sin = jnp.cos(ang), jnp.sin(ang)
    cos_h = jnp.concatenate([cos, cos], axis=1)
    sin_h = jnp.concatenate([-sin, sin], axis=1)
    reps = LANE // RET_DK
    return jnp.tile(cos_h, (1, reps)), jnp.tile(sin_h, (1, reps))


def _permute_w_in(w_in):
    offs = np.cumsum([0, 512, 512, 512, 512, 512, 128, 128, 512, 64, 8])
    seg = lambda i: w_in[:, offs[i]:offs[i + 1]]
    pad = jnp.zeros((D_MODEL, LANE - IDX_DIM - IDX_HEADS), w_in.dtype)
    return jnp.concatenate([seg(0), seg(1), seg(2), seg(3), seg(4), seg(7),
                            seg(5), seg(6), seg(8), seg(9), pad], axis=1).astype(BF16)


def _layer(x, s0, past, weights, rel_bias):
    w_perm, w_out, w_up, w_down, g_pre_mix, g_post_mix, g_pre_ffn, g_post_ffn = weights
    batch, t_len, _ = x.shape
    rows = batch * t_len
    x2 = x.reshape(rows, D_MODEL)
    tm = min(512, rows)
    main, kc, vc, ic, iw, *key_tiles = _inproj(x2, g_pre_mix, w_perm, tm, key_tiles=past is None)

    p_len = 0 if past is None else past[0].shape[1]
    pos = p_len + jnp.arange(t_len, dtype=jnp.int32)
    cos_t, sin_t = _rope_tables(pos)
    chunk = min(CHUNK, t_len)
    sb = min(256, t_len)
    o_ret, s_new = _retention(main, s0, cos_t, sin_t, batch, t_len, sb, chunk)

    l_all = p_len + t_len
    n_sel = min(TOPK_MAX, l_all // 4)
    if past is None:
        tq = TK
        n_qblk = t_len // tq
        nkt_total = t_len // TK
        bt = _bias_tiles(rel_bias, tq)
        k3, ki3, vt3 = key_tiles
        o_att = _attention(main, main, iw, ki3, k3, vt3, bt, batch, n_qblk, tq, nkt_total, None,
                           TK, tq, n_sel, aq_col=4, iq_col=5)
    else:
        tq = LANE
        pk, pv, pi = past
        assert p_len % TK == 0 and t_len <= min(tq, TK), (p_len, t_len)
        nkt_total = p_len // TK + 1
        padk = nkt_total * TK - l_all
        cat = lambda old, new: jnp.concatenate(
            [old.astype(BF16), new.reshape(batch, t_len, -1).astype(BF16),
             jnp.zeros((batch, padk, new.shape[-1]), BF16)], axis=1)
        k_all = cat(pk.reshape(batch, p_len, KV_W), kc)
        v_all = cat(pv.reshape(batch, p_len, KV_W), vc)
        i_all = cat(pi, ic)
        v_t = jnp.transpose(v_all.reshape(batch, nkt_total, TK, ATT_KV_HEADS, ATT_HD), (0, 1, 3, 4, 2))
        ones = jnp.zeros((batch, nkt_total, ATT_KV_HEADS, VT_ROWS - ATT_HD, TK), BF16).at[:, :, :, 0, :].set(1.0)
        vt_s = jnp.concatenate([v_t, ones], axis=3).reshape(batch * nkt_total, VT_ALL, TK)
        k_s = k_all.reshape(batch * nkt_total, TK, KV_W)
        i_s = i_all.reshape(batch * nkt_total, TK, IDX_DIM)
        padq = lambda a: jnp.pad(a.reshape(batch, t_len, -1), ((0, 0), (0, tq - t_len), (0, 0))
                                 ).reshape(batch * tq, -1)
        aq_s = padq(main[:, 4 * RET_W:4 * RET_W + ATT_W])
        iq_s = padq(main[:, 4 * RET_W + ATT_W:])
        iw_s = padq(iw)
        bt = _bias_tiles(rel_bias, tq)
        o_pad = _attention(aq_s, iq_s, iw_s, i_s, k_s, vt_s, bt, batch, 1, tq, nkt_total, nkt_total,
                           l_all - p_len, t_len, n_sel, aq_col=0, iq_col=0)
        o_att = o_pad.reshape(batch, tq, ATT_W)[:, :t_len].reshape(rows, ATT_W)

    y = _out_ffn(x2, o_ret, o_att, w_out, w_up, w_down, g_post_mix, g_pre_ffn, g_post_ffn, tm)
    return (y.reshape(batch, t_len, D_MODEL), s_new,
            kc.reshape(batch, t_len, ATT_KV_HEADS, ATT_HD),
            vc.reshape(batch, t_len, ATT_KV_HEADS, ATT_HD),
            ic.reshape(batch, t_len, IDX_DIM))


def kernel(x_prompt, x_sample, state_ret, cache_k, cache_v, cache_kidx, w_in, w_out, w_up, w_down,
           g_pre_mix, g_post_mix, g_pre_ffn, g_post_ffn, rel_bias):
    depth = w_in.shape[0]
    bp = x_prompt.shape[0]
    zero_state = jnp.zeros((bp, RET_HEADS, RET_DK, RET_DV), F32)
    yp, ys = x_prompt, x_sample
    outs_p, outs_s = [], []
    for l in range(depth):
        row = lambda g: g[l].reshape(1, D_MODEL).astype(F32)
        weights = (_permute_w_in(w_in[l]), w_out[l].astype(BF16), w_up[l].astype(BF16),
                   w_down[l].astype(BF16), row(g_pre_mix), row(g_post_mix), row(g_pre_ffn),
                   row(g_post_ffn))
        yp, *rest_p = _layer(yp, zero_state, None, weights, rel_bias)
        ys, *rest_s = _layer(ys, state_ret[l], (cache_k[l], cache_v[l], cache_kidx[l]), weights, rel_bias)
        outs_p.append(rest_p)
        outs_s.append(rest_s)
    stack = lambda outs, i: jnp.stack([o[i] for o in outs])
    return (yp, ys,
            stack(outs_p, 0), stack(outs_p, 1), stack(outs_p, 2), stack(outs_p, 3),
            stack(outs_s, 0), stack(outs_s, 1), stack(outs_s, 2), stack(outs_s, 3))
```

```python
import functools
import math

import jax
import jax.numpy as jnp
import numpy as np
from jax import lax
from jax.experimental import pallas as pl
from jax.experimental.pallas import tpu as pltpu

D_MODEL = 1024
CHUNK = 64
RET_HEADS = 8
RET_DK = 64
RET_DV = 64
ATT_HEADS = 8
ATT_KV_HEADS = 2
ATT_HD = 64
IDX_HEADS = 8
IDX_DIM = 64
TOPK_MAX = 256
N_BUCKETS = 32
MAX_DISTANCE = 128
D_FF = 4 * D_MODEL
ROPE_BASE = 10000.0
EPS = 1e-6

RET_W = RET_HEADS * RET_DV
ATT_W = ATT_HEADS * ATT_HD
KV_W = ATT_KV_HEADS * ATT_HD
IDX_W = IDX_HEADS * IDX_DIM
HEADS_PER_KV = ATT_HEADS // ATT_KV_HEADS

MAIN_W = 4 * RET_W + ATT_W + IDX_W
TAIL_W = 3 * 128
IW_OFF = IDX_DIM

LANE = 128
VT_ROWS = 80
VT_ALL = ATT_KV_HEADS * VT_ROWS
TK = 256
ROW_TILE = 512
COL_CHUNK = 512
RET_BLOCK = 256
NEG = -(2.0 ** 100)
BF16_ROWS = 16
CNT_ROWS = 32
NARROW = 2.0 ** -20
MAX_SEARCH = 96
F32_TINY = float(np.finfo(np.float32).tiny)
N_COARSE = 8
SMALL_BRACKET = 2.0
BF16_STEP = 2.0 ** -7
VMEM_LIMIT = 56 * 1024 * 1024

F32 = jnp.float32
BF16 = jnp.bfloat16
NT_DIMS = (((1,), (1,)), ((), ()))
TN_DIMS = (((0,), (0,)), ((), ()))


def _const_spec(shape):
    nd = len(shape)
    return pl.BlockSpec(shape, lambda *_: (0,) * nd, pipeline_mode=pl.Buffered(1))


def _rms(x, gain):
    return x * lax.rsqrt(jnp.mean(x * x, axis=-1, keepdims=True) + EPS) * gain


def _inproj_kernel(x_ref, g_ref, w_ref, main_ref, kc_ref, vc_ref, ic_ref, iw_ref,
                   *tile_refs, tm):
    a = _rms(x_ref[...], g_ref[...]).astype(BF16)
    for c in range(MAIN_W // COL_CHUNK):
        cols = slice(c * COL_CHUNK, (c + 1) * COL_CHUNK)
        main_ref[:, cols] = jnp.dot(a, w_ref[:, cols], preferred_element_type=F32)
    tail = jnp.dot(a, w_ref[:, MAIN_W:MAIN_W + TAIL_W], preferred_element_type=F32)
    ak = tail[:, 0:KV_W]
    av = tail[:, KV_W:2 * KV_W]
    last = tail[:, 2 * KV_W:3 * KV_W]
    kc_ref[...] = ak
    vc_ref[...] = av
    ic_ref[...] = last[:, :IDX_DIM]
    iw_ref[...] = last
    if not tile_refs:
        return
    k3_ref, ki3_ref, vt3_ref = tile_refs
    avt = av.T
    row = lax.broadcasted_iota(jnp.int32, (VT_ROWS - ATT_HD, tm), 0)
    ones_rows = jnp.where(row == 0, 1.0, 0.0).astype(F32)
    vt = jnp.concatenate([avt[:ATT_HD], ones_rows, avt[ATT_HD:], ones_rows], axis=0).astype(BF16)
    for j in range(tm // TK):
        k3_ref[j] = ak[j * TK:(j + 1) * TK].astype(BF16)
        ki3_ref[j] = last[j * TK:(j + 1) * TK, :IDX_DIM].astype(BF16)
        vt3_ref[j] = vt[:, j * TK:(j + 1) * TK]


def _inproj(x2, gain, w_perm, tm, key_tiles):
    rows = x2.shape[0]
    grid = (rows // tm,)
    row_spec = lambda w: pl.BlockSpec((tm, w), lambda i: (i, 0))
    out_shape = (
        jax.ShapeDtypeStruct((rows, MAIN_W), F32),
        jax.ShapeDtypeStruct((rows, KV_W), F32),
        jax.ShapeDtypeStruct((rows, KV_W), F32),
        jax.ShapeDtypeStruct((rows, IDX_DIM), F32),
        jax.ShapeDtypeStruct((rows, LANE), F32),
    )
    out_specs = (row_spec(MAIN_W), row_spec(KV_W), row_spec(KV_W), row_spec(IDX_DIM), row_spec(LANE))
    if key_tiles:
        t3 = lambda a, b: pl.BlockSpec((tm // TK, a, b), lambda i: (i, 0, 0))
        out_shape += (jax.ShapeDtypeStruct((rows // TK, TK, KV_W), BF16),
                      jax.ShapeDtypeStruct((rows // TK, TK, IDX_DIM), BF16),
                      jax.ShapeDtypeStruct((rows // TK, VT_ALL, TK), BF16))
        out_specs += (t3(TK, KV_W), t3(TK, IDX_DIM), t3(VT_ALL, TK))
    return pl.pallas_call(
        functools.partial(_inproj_kernel, tm=tm),
        grid=grid,
        in_specs=[row_spec(D_MODEL), _const_spec((1, D_MODEL)), _const_spec(w_perm.shape)],
        out_specs=out_specs,
        out_shape=out_shape,
        compiler_params=pltpu.CompilerParams(dimension_semantics=("parallel",),
                                             vmem_limit_bytes=VMEM_LIMIT),
        name="inproj",
    )(x2, gain, w_perm)


def _ret_gammas():
    return [1.0 - 2.0 ** (-5.0 - h) for h in range(RET_HEADS)]


def _ret_tables(sb, chunk):
    lg = np.log(np.array(_ret_gammas(), np.float64))
    t = np.arange(sb)
    ci = t // chunk
    diff = t[:, None] - t[None, :]
    same = ci[:, None] == ci[None, :]
    below = ci[None, :] < ci[:, None]
    expo = np.where(same, np.abs(diff), np.where(below, diff, 0)).astype(np.float64)
    dmat = np.exp(lg[:, None, None] * expo[None]) * (same | below)[None]
    qd = np.exp(lg[None, :] * (t + 1.0)[:, None])
    kd = np.exp(lg[None, :] * (sb - 1.0 - t)[:, None])
    qd = np.repeat(qd, RET_DK, axis=1)
    kd = np.repeat(kd, RET_DK, axis=1)
    return (jnp.asarray(dmat, F32), jnp.asarray(qd, F32), jnp.asarray(kd, F32),
            [float(math.exp(v * sb)) for v in lg])


def _ret_kernel(q_ref, k_ref, v_ref, g_ref, cos_ref, sin_ref, d_ref, qd_ref, kd_ref, s0_ref,
                o_ref, sfin_ref, s_scr, *, g_block):
    n = pl.program_id(1)

    @pl.when(n == 0)
    def _():
        s_scr[...] = s0_ref[0]

    reps = RET_W // LANE
    cos = jnp.concatenate([cos_ref[...]] * reps, axis=1)
    sin = jnp.concatenate([sin_ref[...]] * reps, axis=1)
    lane = lax.broadcasted_iota(jnp.int32, cos.shape, 1)
    first_half = (lane & (RET_DK - 1)) < RET_DK // 2

    def rot(x):
        partner = jnp.where(first_half, pltpu.roll(x, RET_W - RET_DK // 2, 1),
                            pltpu.roll(x, RET_DK // 2, 1))
        return x * cos + partner * sin

    q = rot(q_ref[...])
    k = rot(k_ref[...]) * (RET_DK ** -0.5)
    v = v_ref[...].astype(BF16)
    gate = g_ref[...]
    qb = q.astype(BF16)
    kb = k.astype(BF16)
    qx = (q * qd_ref[...]).astype(BF16)
    kx = (k * kd_ref[...]).astype(BF16)
    outs = []
    for h in range(RET_HEADS):
        sl = slice(h * RET_DK, (h + 1) * RET_DK)
        s = lax.dot_general(qb[:, sl], kb[:, sl], NT_DIMS, preferred_element_type=F32)
        p = (s * d_ref[h]).astype(BF16)
        st = s_scr[h]
        o = jnp.dot(p, v[:, sl], preferred_element_type=F32)
        o = o + jnp.dot(qx[:, sl], st.astype(BF16), preferred_element_type=F32)
        s_scr[h] = g_block[h] * st + lax.dot_general(kx[:, sl], v[:, sl], TN_DIMS,
                                                     preferred_element_type=F32)
        o = o * lax.rsqrt(jnp.mean(o * o, axis=-1, keepdims=True) + EPS)
        outs.append(o)
    o_all = jnp.concatenate(outs, axis=1)
    o_ref[...] = gate * (1.0 / (1.0 + jnp.exp(-gate))) * o_all

    @pl.when(n == pl.num_programs(1) - 1)
    def _():
        sfin_ref[0] = s_scr[...]


def _retention(main, s0, cos_t, sin_t, batch, t_len, sb, chunk):
    nsb = t_len // sb
    dmat, qd, kd, g_block = _ret_tables(sb, chunk)
    col = lambda c: pl.BlockSpec((sb, RET_W), lambda b, n, c=c: (b * nsb + n, c))
    tab = pl.BlockSpec((sb, LANE), lambda b, n: (n, 0))
    st_spec = pl.BlockSpec((1, RET_HEADS, RET_DK, RET_DV), lambda b, n: (b, 0, 0, 0))
    return pl.pallas_call(
        functools.partial(_ret_kernel, g_block=g_block),
        grid=(batch, nsb),
        in_specs=[col(0), col(1), col(2), col(3), tab, tab,
                  _const_spec(dmat.shape), _const_spec(qd.shape), _const_spec(kd.shape), st_spec],
        out_specs=(pl.BlockSpec((sb, RET_W), lambda b, n: (b * nsb + n, 0)), st_spec),
        out_shape=(jax.ShapeDtypeStruct((batch * t_len, RET_W), F32),
                   jax.ShapeDtypeStruct((batch, RET_HEADS, RET_DK, RET_DV), F32)),
        scratch_shapes=[pltpu.VMEM((RET_HEADS, RET_DK, RET_DV), F32)],
        compiler_params=pltpu.CompilerParams(dimension_semantics=("parallel", "arbitrary"),
                                             vmem_limit_bytes=VMEM_LIMIT),
        name="retention",
    )(main, main, main, main, cos_t, sin_t, dmat, qd, kd, s0)


def _attn_kernel(aq_ref, iq_ref, iw_ref, ki_ref, k_ref, vt_ref, bt_ref, o_ref,
                 s_scr, sb_scr, acc_scr, m_scr, p_scr, *, tq, nkt_static, last_valid, n_valid_q, n_sel,
                 max_search):
    qblk = pl.program_id(1)
    nkt = qblk + 1 if nkt_static is None else nkt_static
    lane_f = lax.broadcasted_iota(jnp.int32, (1, tq), 1).astype(F32)
    lane_ok = lane_f < float(n_valid_q)
    klim = jnp.minimum((jnp.floor(lane_f * (1.0 / CHUNK)) + 1.0) * CHUNK, float(last_valid))
    krow = lax.broadcasted_iota(jnp.int32, (TK, tq), 0).astype(F32)
    adm_last = krow < klim
    nkt_f = nkt.astype(F32) if nkt_static is None else float(nkt)
    n_adm = (nkt_f - 1.0) * TK + klim
    k_target = jnp.minimum(float(n_sel), n_adm)

    iq = iq_ref[...]
    w_t = iw_ref[...].T[IW_OFF:IW_OFF + IDX_HEADS, :] * (IDX_HEADS ** -0.5 * IDX_DIM ** -0.5)
    iq_all = jnp.concatenate(
        [iq[:, h * IDX_DIM:(h + 1) * IDX_DIM].astype(BF16) for h in range(IDX_HEADS)], axis=0)

    def score_tile(kt, n_tiles=1):
        ki_t = ki_ref[kt] if n_tiles == 1 else jnp.concatenate(
            [ki_ref[kt + j] for j in range(n_tiles)], axis=0)
        s_all = lax.dot_general(ki_t, iq_all, NT_DIMS, preferred_element_type=F32)
        acc = jnp.maximum(s_all[:, :tq], 0.0) * w_t[0:1, :]
        for h in range(1, IDX_HEADS):
            acc = acc + jnp.maximum(s_all[:, h * tq:(h + 1) * tq], 0.0) * w_t[h:h + 1, :]
        return acc

    def fold_rows(x, op=jnp.add):
        parts = [x[i * CNT_ROWS:(i + 1) * CNT_ROWS] for i in range(x.shape[0] // CNT_ROWS)]
        while len(parts) > 1:
            parts = [op(a, b) for a, b in zip(parts[0::2], parts[1::2])]
        return parts[0]

    def p1_step(kt, n_tiles, carry):
        rmax, rmin = carry
        sc = score_tile(kt, n_tiles)
        for j in range(n_tiles):
            s_scr[kt + j] = sc[j * TK:(j + 1) * TK]
            sb_scr[kt + j] = sc[j * TK:(j + 1) * TK].astype(BF16)
        return (jnp.maximum(rmax, jnp.max(sc, axis=0, keepdims=True)),
                jnp.minimum(rmin, jnp.min(sc, axis=0, keepdims=True)))

    def run_steps(n, step, carry):
        n4 = n // 4
        carry = lax.fori_loop(0, n4, lambda i, c: step(4 * i, 4, c), carry)
        carry = lax.fori_loop(0, (n - 4 * n4) // 2, lambda i, c: step(4 * n4, 2, c), carry)
        return lax.fori_loop(0, n % 2, lambda i, c: step(n - 1, 1, c), carry)

    init = (jnp.full((1, tq), -jnp.inf, F32), jnp.full((1, tq), jnp.inf, F32))
    rmax, rmin = run_steps(nkt - 1, p1_step, init)
    sc = jnp.where(adm_last, score_tile(nkt - 1), -jnp.inf)
    s_scr[nkt - 1] = sc
    sb_scr[nkt - 1] = sc.astype(BF16)
    rmax = jnp.maximum(rmax, jnp.max(sc, axis=0, keepdims=True))
    rmin = jnp.minimum(rmin, jnp.min(jnp.where(adm_last, sc, jnp.inf), axis=0, keepdims=True))

    def tile_loop(n, body, carry):
        carry = lax.fori_loop(0, n // 2, lambda i, c: body(2 * i + 1, body(2 * i, c)), carry)
        return lax.fori_loop(0, n % 2, lambda i, c: body(n - 1, c), carry)

    def count_where(pred):
        def body(kt, c):
            return c + fold_rows(jnp.where(pred(kt), 1.0, 0.0))
        c = tile_loop(nkt, body, jnp.zeros((CNT_ROWS, tq), F32))
        return jnp.sum(c, axis=0, keepdims=True)

    def count_ge(thr):
        return count_where(lambda kt: s_scr[kt] >= thr)

    def count_ge_rounded(thr_b):
        one, zero = jnp.ones((), BF16), jnp.zeros((), BF16)

        def body(kt, c):
            ind = jnp.where(sb_scr[kt] >= thr_b, one, zero)
            parts = [ind[i * BF16_ROWS:(i + 1) * BF16_ROWS] for i in range(TK // BF16_ROWS)]
            while len(parts) > 1:
                parts = [a + b for a, b in zip(parts[0::2], parts[1::2])]
            return c + parts[0].astype(F32)
        c = tile_loop(nkt, body, jnp.zeros((BF16_ROWS, tq), F32))
        return jnp.sum(c, axis=0, keepdims=True)

    span = jnp.maximum(jnp.maximum(rmax - rmin, jnp.abs(rmax)), 1e-30)
    hi0 = rmax + span * (2.0 ** -10)

    def max_below(bound, n_tiles):
        def body(kt, m):
            s = s_scr[kt]
            return jnp.maximum(m, fold_rows(jnp.where(s < bound, s, -jnp.inf), jnp.maximum))
        m = tile_loop(n_tiles, body, jnp.full((CNT_ROWS, tq), -jnp.inf, F32))
        return jnp.max(m, axis=0, keepdims=True)

    def plan(it, lo, hi, c_lo, c_hi, done):
        width = hi - lo
        frac = (c_lo - k_target - 0.5) / (c_lo - c_hi)
        interp = ((it + 1) % 2).astype(F32)
        mid_i = lo + width * (0.5 + interp * (frac - 0.5))
        mid_b = lo + 0.5 * width
        mid = jnp.where((mid_i > lo) & (mid_i < hi), mid_i, mid_b)
        active = done == 0.0
        narrow = jnp.logical_not((mid > lo) & (mid < hi)) | (width <= span * NARROW)
        wide = active & jnp.logical_not(narrow) & (c_lo - c_hi > SMALL_BRACKET)
        flag = jnp.max(jnp.where(wide, 5.0, jnp.where(active, 3.0, 0.0)))
        return mid, flag

    def search_body(st):
        it, flag, lo, hi, c_lo, c_hi, done, mid = st
        exact = (jnp.zeros((1, tq), F32) + flag) < 4.0
        top = max_below(hi, jnp.where(flag < 4.0, nkt, 0))
        mid = jnp.where(exact, top, mid)
        c = count_ge(mid)
        active = done == 0.0
        ge = c >= k_target
        up = active & ge
        dn = active & jnp.logical_not(ge)
        lo = jnp.where(up, mid, lo)
        c_lo = jnp.where(up, c, c_lo)
        hi = jnp.where(dn, mid, hi)
        c_hi = jnp.where(dn, c, c_hi)
        done = jnp.where((c_lo == k_target) | (exact & up), 1.0, done)
        mid, flag = plan(it + 1, lo, hi, c_lo, c_hi, done)
        return it + 1, flag, lo, hi, c_lo, c_hi, done, mid

    c_zero = count_ge_rounded(jnp.zeros((1, tq), BF16))
    c_pos = count_ge_rounded(jnp.full((1, tq), F32_TINY, BF16))
    pos_side = c_pos >= k_target
    at_zero = jnp.logical_not(pos_side) & (c_zero >= k_target)
    pick = lambda p, z, n: jnp.where(pos_side, p, jnp.where(at_zero, z, n))
    lo0 = pick(F32_TINY, 0.0, rmin)
    hi0 = pick(hi0, F32_TINY, 0.0)
    c_lo0 = pick(c_pos, c_zero, n_adm)
    c_hi0 = pick(0.0, c_pos, c_zero)
    done0 = jnp.where((n_adm == k_target) | at_zero | jnp.logical_not(lane_ok), 1.0, 0.0)

    def coarse_body(it, st):
        lo, hi, c_lo, c_hi, live = st
        width = hi - lo
        frac = (c_lo - k_target - 0.5) / (c_lo - c_hi)
        interp = lax.convert_element_type((it + 1) % 2, F32)
        grid = lambda x: x.astype(BF16).astype(F32)
        mid_i = grid(lo + width * (0.5 + interp * (frac - 0.5)))
        mid_b = grid(lo + 0.5 * width)
        mid = jnp.where((mid_i > lo) & (mid_i < hi), mid_i, mid_b)
        ok = (mid > lo) & (mid < hi) & (live > 0.0)
        c = count_ge_rounded(mid.astype(BF16))
        ge = c >= k_target
        up = ok & ge
        dn = ok & jnp.logical_not(ge)
        return (jnp.where(up, mid, lo), jnp.where(dn, mid, hi), jnp.where(up, c, c_lo),
                jnp.where(dn, c, c_hi), jnp.where(ok, live, 0.0))

    lo1, hi0, c_lo0, c_hi0, _ = lax.fori_loop(
        0, N_COARSE, coarse_body, (lo0, hi0, c_lo0, c_hi0, 1.0 - done0))
    lo0 = jnp.where(lo1 != lo0, lo1 - jnp.abs(lo1) * BF16_STEP, lo0)
    c_lo0 = count_ge(lo0)
    done0 = jnp.where(c_lo0 == k_target, 1.0, done0)
    mid0, flag0 = plan(jnp.int32(0), lo0, hi0, c_lo0, c_hi0, done0)
    st0 = (jnp.int32(0), flag0, lo0, hi0, c_lo0, c_hi0, done0, mid0)
    _, _, lo, hi, cnt_lo, _, _, _ = lax.while_loop(
        lambda st: (st[0] < max_search) & (st[1] > 0.0), search_body, st0)

    excess = jnp.max(jnp.where(lane_ok, cnt_lo - k_target, 0.0))

    @pl.when(excess > 0.0)
    def _():
        need = k_target - count_ge(hi)
        r = lax.broadcasted_iota(jnp.int32, (TK, TK), 0)
        c = lax.broadcasted_iota(jnp.int32, (TK, TK), 1)
        prefix = jnp.where(r >= c, 1.0, 0.0).astype(BF16)

        def drop_step(kt, n_tiles, before):
            tiles = [s_scr[kt + j] for j in range(n_tiles)]
            ties = [(s >= lo) & (s < hi) for s in tiles]
            ranks = [jnp.dot(prefix, jnp.where(t, 1.0, 0.0).astype(BF16),
                             preferred_element_type=F32) for t in ties]
            for j in range(n_tiles):
                rank = before + ranks[j]
                s_scr[kt + j] = jnp.where(ties[j] & (rank > need), -jnp.inf, tiles[j])
                before = rank[TK - 1:TK, :]
            return before

        run_steps(nkt, drop_step, jnp.zeros((1, tq), F32))

    aq = aq_ref[...] * (ATT_HD ** -0.5)
    zeros_q = jnp.zeros((tq, ATT_HD), F32)
    q_pad = []
    for h in range(ATT_HEADS):
        qh = aq[:, h * ATT_HD:(h + 1) * ATT_HD]
        parts = [zeros_q] * ATT_KV_HEADS
        parts[h // HEADS_PER_KV] = qh
        q_pad.append(jnp.concatenate(parts, axis=1).astype(BF16))

    m_scr[...] = jnp.full(m_scr.shape, NEG, F32)
    acc_scr[...] = jnp.zeros(acc_scr.shape, F32)

    def col_max(x):
        parts = [x[i * BF16_ROWS:(i + 1) * BF16_ROWS] for i in range(x.shape[0] // BF16_ROWS)]
        while len(parts) > 1:
            parts = [jnp.maximum(a, b) for a, b in zip(parts[0::2], parts[1::2])]
        return jnp.max(parts[0].astype(F32), axis=0, keepdims=True)

    cat = lambda xs, axis: xs[0] if len(xs) == 1 else jnp.concatenate(xs, axis=axis)

    def logits(kt, n_tiles, near):
        k_t = cat([k_ref[kt + j] for j in range(n_tiles)], 0)
        s_t = cat([s_scr[kt + j] for j in range(n_tiles)], 0)
        neg_mask = jnp.where(s_t >= lo, 0.0, NEG).astype(BF16)
        out = []
        for h in range(ATT_HEADS):
            lg = lax.dot_general(k_t, q_pad[h], NT_DIMS, preferred_element_type=F32)
            if near:
                lg = lg + cat([bt_ref[h, 2 - n_tiles + j] for j in range(n_tiles)], 0)
            out.append(lg.astype(BF16) + neg_mask)
        return out

    def softmax_pv(kt, n_tiles, lgs):
        tiles = [kt + j for j in range(n_tiles)]
        rows = n_tiles * TK
        alphas = []
        for h in range(ATT_HEADS):
            g, r = divmod(h, HEADS_PER_KV)
            lg = lgs[h]
            m_old = m_scr[h:h + 1, :]
            m_new = jnp.maximum(m_old, col_max(lg))
            alphas.append(jnp.exp(m_old - m_new))
            p_scr[g, :rows, r * tq:(r + 1) * tq] = jnp.exp(lg - m_new.astype(BF16))
            m_scr[h:h + 1, :] = m_new
        for g in range(ATT_KV_HEADS):
            hs = range(g * HEADS_PER_KV, (g + 1) * HEADS_PER_KV)
            a_g = jnp.concatenate([alphas[h] for h in hs], axis=1)
            vt_g = cat([vt_ref[t, g * VT_ROWS:(g + 1) * VT_ROWS, :] for t in tiles], 1)
            acc_scr[g] = acc_scr[g] * a_g + jnp.dot(vt_g, p_scr[g, :rows, :],
                                                    preferred_element_type=F32)

    def attend(kt, n_tiles, near):
        softmax_pv(kt, n_tiles, logits(kt, n_tiles, near))

    n_far = jnp.maximum(nkt - 2, 0) if nkt_static is None else max(nkt - 2, 0)

    def far_body(i, c):
        attend(2 * i, 2, False)
        return c
    lax.fori_loop(0, n_far // 2, far_body, 0)

    def when(cond):
        return pl.when(cond) if nkt_static is None else (lambda f: f() if cond else None)

    @when(n_far % 2 == 1)
    def _():
        attend(n_far - 1, 1, False)

    @when(nkt >= 2)
    def _():
        attend(nkt - 2, 2, True)

    @when(nkt < 2)
    def _():
        attend(0, 1, True)

    outs = []
    for g in range(ATT_KV_HEADS):
        a = acc_scr[g]
        o_g = a[:ATT_HD] * (1.0 / a[ATT_HD:ATT_HD + 1])
        outs += [o_g[:, r * tq:(r + 1) * tq] for r in range(HEADS_PER_KV)]
    o_ref[...] = jnp.concatenate(outs, axis=0).T


def _attention(aq_src, iq_src, iw, ki3, k3, vt3, bt, batch, n_qblk, tq, nkt_total, nkt_static,
               last_valid, n_valid_q, n_sel, aq_col, iq_col):
    kern = functools.partial(_attn_kernel, tq=tq, nkt_static=nkt_static, last_valid=last_valid,
                             n_valid_q=n_valid_q, n_sel=n_sel, max_search=MAX_SEARCH)
    qspec = lambda c: pl.BlockSpec((tq, ATT_W), lambda b, q, c=c: (b * n_qblk + q, c))
    kspec = lambda a, c: pl.BlockSpec((nkt_total, a, c), lambda b, q: (b, 0, 0))
    return pl.pallas_call(
        kern,
        grid=(batch, n_qblk),
        in_specs=[qspec(aq_col), qspec(iq_col),
                  pl.BlockSpec((tq, LANE), lambda b, q: (b * n_qblk + q, 0)),
                  kspec(TK, IDX_DIM), kspec(TK, KV_W), kspec(VT_ALL, TK),
                  _const_spec(bt.shape)],
        out_specs=pl.BlockSpec((tq, ATT_W), lambda b, q: (b * n_qblk + q, 0)),
        out_shape=jax.ShapeDtypeStruct((batch * n_qblk * tq, ATT_W), F32),
        scratch_shapes=[pltpu.VMEM((nkt_total, TK, tq), F32),
                        pltpu.VMEM((nkt_total, TK, tq), BF16),
                        pltpu.VMEM((ATT_KV_HEADS, VT_ROWS, HEADS_PER_KV * tq), F32),
                        pltpu.VMEM((ATT_HEADS, tq), F32),
                        pltpu.VMEM((ATT_KV_HEADS, 2 * TK, HEADS_PER_KV * tq), BF16)],
        compiler_params=pltpu.CompilerParams(dimension_semantics=("parallel", "arbitrary"),
                                             vmem_limit_bytes=VMEM_LIMIT),
        name="attention",
    )(aq_src, iq_src, iw, ki3, k3, vt3, bt)


def _ffn_kernel(x_ref, oret_ref, oatt_ref, wout_ref, wup_ref, wdown_ref,
                gpost_ref, gpre_ref, gffn_ref, y_ref, *, ff_chunk):
    mix = jnp.dot(oret_ref[...].astype(BF16), wout_ref[:RET_W, :], preferred_element_type=F32)
    mix = mix + jnp.dot(oatt_ref[...].astype(BF16), wout_ref[RET_W:, :], preferred_element_type=F32)
    h = x_ref[...] + _rms(mix, gpost_ref[...])
    a = _rms(h, gpre_ref[...]).astype(BF16)
    f = jnp.zeros(h.shape, F32)
    for c in range(D_FF // ff_chunk):
        sl = slice(c * ff_chunk, (c + 1) * ff_chunk)
        u = jnp.dot(a, wup_ref[:, sl], preferred_element_type=F32)
        u = jnp.square(jnp.maximum(u, 0.0)).astype(BF16)
        f = f + jnp.dot(u, wdown_ref[sl, :], preferred_element_type=F32)
    y_ref[...] = h + _rms(f, gffn_ref[...])


def _out_ffn(x2, o_ret, o_att, w_out, w_up, w_down, g_post, g_pre, g_ffn, tm):
    rows = x2.shape[0]
    row_spec = lambda w: pl.BlockSpec((tm, w), lambda i: (i, 0))
    return pl.pallas_call(
        functools.partial(_ffn_kernel, ff_chunk=COL_CHUNK),
        grid=(rows // tm,),
        in_specs=[row_spec(D_MODEL), row_spec(RET_W), row_spec(ATT_W),
                  _const_spec(w_out.shape), _const_spec(w_up.shape), _const_spec(w_down.shape),
                  _const_spec((1, D_MODEL)), _const_spec((1, D_MODEL)), _const_spec((1, D_MODEL))],
        out_specs=row_spec(D_MODEL),
        out_shape=jax.ShapeDtypeStruct((rows, D_MODEL), F32),
        compiler_params=pltpu.CompilerParams(dimension_semantics=("parallel",),
                                             vmem_limit_bytes=VMEM_LIMIT),
        name="out_ffn",
    )(x2, o_ret, o_att, w_out, w_up, w_down, g_post, g_pre, g_ffn)


def _t5_bucket(rel):
    half = N_BUCKETS // 2
    max_exact = half // 2
    ret = jnp.where(rel > 0, half, 0)
    n = jnp.abs(rel)
    nf = jnp.maximum(n, 1).astype(F32)
    large = max_exact + (jnp.log(nf / max_exact) / math.log(MAX_DISTANCE / max_exact)
                         * (half - max_exact)).astype(jnp.int32)
    large = jnp.minimum(large, half - 1)
    return ret + jnp.where(n < max_exact, n, large)


def _bias_tiles(rel_bias, tq):
    period = 2 * TK + tq
    rel = jnp.arange(period, dtype=jnp.int32) - (TK + tq - 1)
    table = rel_bias.astype(F32)
    far = table[_t5_bucket(jnp.int32(-MAX_DISTANCE))]
    u = (table[_t5_bucket(jnp.clip(rel, -MAX_DISTANCE, MAX_DISTANCE))] - far).T
    n = jnp.tile(u, (1, tq))[:, :tq * (period - 1)].reshape(ATT_HEADS, tq, period - 1)
    m = n[:, :, tq - 1:tq - 1 + 2 * TK]
    return jnp.transpose(m.reshape(ATT_HEADS, tq, 2, TK), (0, 2, 3, 1))


def _rope_tables(pos):
    half = RET_DK // 2
    inv = ROPE_BASE ** (-jnp.arange(half, dtype=F32) / half)
    ang = pos.astype(F32)[:, None] * inv[None, :]
    cos, sin = jnp.cos(ang), jnp.sin(ang)
    cos_h = jnp.concatenate([cos, cos], axis=1)
    sin_h = jnp.concatenate([-sin, sin], axis=1)
    reps = LANE // RET_DK
    return jnp.tile(cos_h, (1, reps)), jnp.tile(sin_h, (1, reps))


def _permute_w_in(w_in):
    offs = np.cumsum([0, RET_W, RET_W, RET_W, RET_W, ATT_W, KV_W, KV_W, IDX_W, IDX_DIM, IDX_HEADS])
    seg = lambda i: w_in[:, offs[i]:offs[i + 1]]
    pad = jnp.zeros((D_MODEL, LANE - IDX_DIM - IDX_HEADS), w_in.dtype)
    return jnp.concatenate([seg(0), seg(1), seg(2), seg(3), seg(4), seg(7),
                            seg(5), seg(6), seg(8), seg(9), pad], axis=1).astype(BF16)


def _layer(x, s0, past, weights, rel_bias):
    w_perm, w_out, w_up, w_down, g_pre_mix, g_post_mix, g_pre_ffn, g_post_ffn = weights
    batch, t_len, _ = x.shape
    rows = batch * t_len
    x2 = x.reshape(rows, D_MODEL)
    tm = min(ROW_TILE, rows)
    main, kc, vc, ic, iw, *key_tiles = _inproj(x2, g_pre_mix, w_perm, tm, key_tiles=past is None)

    p_len = 0 if past is None else past[0].shape[1]
    pos = p_len + jnp.arange(t_len, dtype=jnp.int32)
    cos_t, sin_t = _rope_tables(pos)
    chunk = min(CHUNK, t_len)
    sb = min(RET_BLOCK, t_len)
    o_ret, s_new = _retention(main, s0, cos_t, sin_t, batch, t_len, sb, chunk)

    l_all = p_len + t_len
    n_sel = min(TOPK_MAX, l_all // 4)
    if past is None:
        tq = TK
        n_qblk = t_len // tq
        nkt_total = t_len // TK
        bt = _bias_tiles(rel_bias, tq)
        k3, ki3, vt3 = key_tiles
        o_att = _attention(main, main, iw, ki3, k3, vt3, bt, batch, n_qblk, tq, nkt_total, None,
                           TK, tq, n_sel, aq_col=4, iq_col=5)
    else:
        tq = LANE
        pk, pv, pi = past
        assert p_len % TK == 0 and t_len <= min(tq, TK), (p_len, t_len)
        nkt_total = p_len // TK + 1
        padk = nkt_total * TK - l_all
        cat = lambda old, new: jnp.concatenate(
            [old.astype(BF16), new.reshape(batch, t_len, -1).astype(BF16),
             jnp.zeros((batch, padk, new.shape[-1]), BF16)], axis=1)
        k_all = cat(pk.reshape(batch, p_len, KV_W), kc)
        v_all = cat(pv.reshape(batch, p_len, KV_W), vc)
        i_all = cat(pi, ic)
        v_t = jnp.transpose(v_all.reshape(batch, nkt_total, TK, ATT_KV_HEADS, ATT_HD), (0, 1, 3, 4, 2))
        ones = jnp.zeros((batch, nkt_total, ATT_KV_HEADS, VT_ROWS - ATT_HD, TK), BF16).at[:, :, :, 0, :].set(1.0)
        vt_s = jnp.concatenate([v_t, ones], axis=3).reshape(batch * nkt_total, VT_ALL, TK)
        k_s = k_all.reshape(batch * nkt_total, TK, KV_W)
        i_s = i_all.reshape(batch * nkt_total, TK, IDX_DIM)
        padq = lambda a: jnp.pad(a.reshape(batch, t_len, -1), ((0, 0), (0, tq - t_len), (0, 0))
                                 ).reshape(batch * tq, -1)
        aq_s = padq(main[:, 4 * RET_W:4 * RET_W + ATT_W])
        iq_s = padq(main[:, 4 * RET_W + ATT_W:])
        iw_s = padq(iw)
        bt = _bias_tiles(rel_bias, tq)
        o_pad = _attention(aq_s, iq_s, iw_s, i_s, k_s, vt_s, bt, batch, 1, tq, nkt_total, nkt_total,
                           l_all - p_len, t_len, n_sel, aq_col=0, iq_col=0)
        o_att = o_pad.reshape(batch, tq, ATT_W)[:, :t_len].reshape(rows, ATT_W)

    y = _out_ffn(x2, o_ret, o_att, w_out, w_up, w_down, g_post_mix, g_pre_ffn, g_post_ffn, tm)
    return (y.reshape(batch, t_len, D_MODEL), s_new,
            kc.reshape(batch, t_len, ATT_KV_HEADS, ATT_HD),
            vc.reshape(batch, t_len, ATT_KV_HEADS, ATT_HD),
            ic.reshape(batch, t_len, IDX_DIM))


def kernel(x_prompt, x_sample, state_ret, cache_k, cache_v, cache_kidx, w_in, w_out, w_up, w_down,
           g_pre_mix, g_post_mix, g_pre_ffn, g_post_ffn, rel_bias):
    depth = w_in.shape[0]
    bp = x_prompt.shape[0]
    zero_state = jnp.zeros((bp, RET_HEADS, RET_DK, RET_DV), F32)
    yp, ys = x_prompt, x_sample
    outs_p, outs_s = [], []
    for l in range(depth):
        row = lambda g: g[l].reshape(1, D_MODEL).astype(F32)
        weights = (_permute_w_in(w_in[l]), w_out[l].astype(BF16), w_up[l].astype(BF16),
                   w_down[l].astype(BF16), row(g_pre_mix), row(g_post_mix), row(g_pre_ffn),
                   row(g_post_ffn))
        yp, *rest_p = _layer(yp, zero_state, None, weights, rel_bias)
        ys, *rest_s = _layer(ys, state_ret[l], (cache_k[l], cache_v[l], cache_kidx[l]), weights, rel_bias)
        outs_p.append(rest_p)
        outs_s.append(rest_s)
    stack = lambda outs, i: jnp.stack([o[i] for o in outs])
    return (yp, ys,
            stack(outs_p, 0), stack(outs_p, 1), stack(outs_p, 2), stack(outs_p, 3),
            stack(outs_s, 0), stack(outs_s, 1), stack(outs_s, 2), stack(outs_s, 3))
```

```python
import functools
import math

import jax
import jax.numpy as jnp
import numpy as np
from jax import lax
from jax.experimental import pallas as pl
from jax.experimental.pallas import tpu as pltpu

D_MODEL = 1024
CHUNK = 64
RET_HEADS = 8
RET_DK = 64
RET_DV = 64
ATT_HEADS = 8
ATT_KV_HEADS = 2
ATT_HD = 64
IDX_HEADS = 8
IDX_DIM = 64
TOPK_MAX = 256
N_BUCKETS = 32
MAX_DISTANCE = 128
D_FF = 4 * D_MODEL
ROPE_BASE = 10000.0
EPS = 1e-6

RET_W = RET_HEADS * RET_DV
ATT_W = ATT_HEADS * ATT_HD
KV_W = ATT_KV_HEADS * ATT_HD
IDX_W = IDX_HEADS * IDX_DIM
HEADS_PER_KV = ATT_HEADS // ATT_KV_HEADS

MAIN_W = 4 * RET_W + ATT_W + IDX_W
TAIL_W = 3 * 128
IW_OFF = IDX_DIM

LANE = 128
VT_ROWS = 80
VT_ALL = ATT_KV_HEADS * VT_ROWS
TK = 256
ROW_TILE = 512
COL_CHUNK = 512
RET_BLOCK = 512
NEG = -(2.0 ** 100)
BF16_ROWS = 16
CNT_ROWS = 32
NARROW = 2.0 ** -20
MAX_SEARCH = 96
F32_TINY = float(np.finfo(np.float32).tiny)
N_COARSE = 8
SMALL_BRACKET = 2.0
BF16_STEP = 2.0 ** -7
VMEM_LIMIT = 56 * 1024 * 1024

F32 = jnp.float32
BF16 = jnp.bfloat16
NT_DIMS = (((1,), (1,)), ((), ()))
TN_DIMS = (((0,), (0,)), ((), ()))


def _const_spec(shape):
    nd = len(shape)
    return pl.BlockSpec(shape, lambda *_: (0,) * nd, pipeline_mode=pl.Buffered(1))


def _rms(x, gain):
    return x * lax.rsqrt(jnp.mean(x * x, axis=-1, keepdims=True) + EPS) * gain


def _inproj_kernel(x_ref, g_ref, w_ref, cos_ref, sin_ref, main_ref, kc_ref, vc_ref, ic_ref, iw_ref,
                   *tile_refs, tm):
    a = _rms(x_ref[...], g_ref[...]).astype(BF16)
    reps = RET_W // LANE
    cos = jnp.concatenate([cos_ref[...]] * reps, axis=1)
    sin = jnp.concatenate([sin_ref[...]] * reps, axis=1)
    lane = lax.broadcasted_iota(jnp.int32, cos.shape, 1)
    first_half = (lane & (RET_DK - 1)) < RET_DK // 2

    def rot(x):
        partner = jnp.where(first_half, pltpu.roll(x, RET_W - RET_DK // 2, 1),
                            pltpu.roll(x, RET_DK // 2, 1))
        return x * cos + partner * sin

    assert COL_CHUNK == RET_W
    for c in range(MAIN_W // COL_CHUNK):
        cols = slice(c * COL_CHUNK, (c + 1) * COL_CHUNK)
        y = jnp.dot(a, w_ref[:, cols], preferred_element_type=F32)
        if c == 0:
            y = rot(y)
        elif c == 1:
            y = rot(y) * (RET_DK ** -0.5)
        main_ref[:, cols] = y
    tail = jnp.dot(a, w_ref[:, MAIN_W:MAIN_W + TAIL_W], preferred_element_type=F32)
    ak = tail[:, 0:KV_W]
    av = tail[:, KV_W:2 * KV_W]
    last = tail[:, 2 * KV_W:3 * KV_W]
    kc_ref[...] = ak
    vc_ref[...] = av
    ic_ref[...] = last[:, :IDX_DIM]
    iw_ref[...] = last
    if not tile_refs:
        return
    k3_ref, ki3_ref, vt3_ref = tile_refs
    avt = av.T
    row = lax.broadcasted_iota(jnp.int32, (VT_ROWS - ATT_HD, tm), 0)
    ones_rows = jnp.where(row == 0, 1.0, 0.0).astype(F32)
    vt = jnp.concatenate([avt[:ATT_HD], ones_rows, avt[ATT_HD:], ones_rows], axis=0).astype(BF16)
    for j in range(tm // TK):
        k3_ref[j] = ak[j * TK:(j + 1) * TK].astype(BF16)
        ki3_ref[j] = last[j * TK:(j + 1) * TK, :IDX_DIM].astype(BF16)
        vt3_ref[j] = vt[:, j * TK:(j + 1) * TK]


def _inproj(x2, gain, w_perm, cos_t, sin_t, tm, key_tiles):
    rows = x2.shape[0]
    grid = (rows // tm,)
    t_len = cos_t.shape[0]
    if t_len < tm:
        assert tm % t_len == 0
        cos_t, sin_t = jnp.tile(cos_t, (tm // t_len, 1)), jnp.tile(sin_t, (tm // t_len, 1))
        t_len = tm
    assert t_len % tm == 0
    tab_spec = pl.BlockSpec((tm, LANE), lambda i: (i % (t_len // tm), 0))
    row_spec = lambda w: pl.BlockSpec((tm, w), lambda i: (i, 0))
    out_shape = (
        jax.ShapeDtypeStruct((rows, MAIN_W), F32),
        jax.ShapeDtypeStruct((rows, KV_W), F32),
        jax.ShapeDtypeStruct((rows, KV_W), F32),
        jax.ShapeDtypeStruct((rows, IDX_DIM), F32),
        jax.ShapeDtypeStruct((rows, LANE), F32),
    )
    out_specs = (row_spec(MAIN_W), row_spec(KV_W), row_spec(KV_W), row_spec(IDX_DIM), row_spec(LANE))
    if key_tiles:
        t3 = lambda a, b: pl.BlockSpec((tm // TK, a, b), lambda i: (i, 0, 0))
        out_shape += (jax.ShapeDtypeStruct((rows // TK, TK, KV_W), BF16),
                      jax.ShapeDtypeStruct((rows // TK, TK, IDX_DIM), BF16),
                      jax.ShapeDtypeStruct((rows // TK, VT_ALL, TK), BF16))
        out_specs += (t3(TK, KV_W), t3(TK, IDX_DIM), t3(VT_ALL, TK))
    return pl.pallas_call(
        functools.partial(_inproj_kernel, tm=tm),
        grid=grid,
        in_specs=[row_spec(D_MODEL), _const_spec((1, D_MODEL)), _const_spec(w_perm.shape),
                  tab_spec, tab_spec],
        out_specs=out_specs,
        out_shape=out_shape,
        compiler_params=pltpu.CompilerParams(dimension_semantics=("parallel",),
                                             vmem_limit_bytes=VMEM_LIMIT),
        name="inproj",
    )(x2, gain, w_perm, cos_t, sin_t)


def _ret_gammas():
    return [1.0 - 2.0 ** (-5.0 - h) for h in range(RET_HEADS)]


def _ret_tables(sb, chunk):
    lg = np.log(np.array(_ret_gammas(), np.float64))
    t = np.arange(sb)
    ci = t // chunk
    diff = t[:, None] - t[None, :]
    same = ci[:, None] == ci[None, :]
    below = ci[None, :] < ci[:, None]
    expo = np.where(same, np.abs(diff), np.where(below, diff, 0)).astype(np.float64)
    dmat = np.exp(lg[:, None, None] * expo[None]) * (same | below)[None]
    qd = np.exp(lg[None, :] * (t + 1.0)[:, None])
    kd = np.exp(lg[None, :] * (sb - 1.0 - t)[:, None])
    qd = np.repeat(qd, RET_DK, axis=1)
    kd = np.repeat(kd, RET_DK, axis=1)
    return (jnp.asarray(dmat, F32), jnp.asarray(qd, F32), jnp.asarray(kd, F32),
            [float(math.exp(v * sb)) for v in lg])


def _ret_kernel(q_ref, k_ref, v_ref, g_ref, d_ref, qd_ref, kd_ref, s0_ref,
                o_ref, sfin_ref, s_scr, *, g_block):
    n = pl.program_id(1)

    @pl.when(n == 0)
    def _():
        s_scr[...] = s0_ref[0]

    q = q_ref[...]
    k = k_ref[...]
    v = v_ref[...].astype(BF16)
    gate = g_ref[...]
    qb = q.astype(BF16)
    kb = k.astype(BF16)
    qx = (q * qd_ref[...]).astype(BF16)
    kx = (k * kd_ref[...]).astype(BF16)
    outs = []
    for h in range(RET_HEADS):
        sl = slice(h * RET_DK, (h + 1) * RET_DK)
        s = lax.dot_general(qb[:, sl], kb[:, sl], NT_DIMS, preferred_element_type=F32)
        p = (s * d_ref[h]).astype(BF16)
        st = s_scr[h]
        o = jnp.dot(p, v[:, sl], preferred_element_type=F32)
        o = o + jnp.dot(qx[:, sl], st.astype(BF16), preferred_element_type=F32)
        s_scr[h] = g_block[h] * st + lax.dot_general(kx[:, sl], v[:, sl], TN_DIMS,
                                                     preferred_element_type=F32)
        o = o * lax.rsqrt(jnp.mean(o * o, axis=-1, keepdims=True) + EPS)
        outs.append(o)
    o_all = jnp.concatenate(outs, axis=1)
    o_ref[...] = gate * (1.0 / (1.0 + jnp.exp(-gate))) * o_all

    @pl.when(n == pl.num_programs(1) - 1)
    def _():
        sfin_ref[0] = s_scr[...]


def _retention(main, s0, batch, t_len, sb, chunk):
    nsb = t_len // sb
    dmat, qd, kd, g_block = _ret_tables(sb, chunk)
    col = lambda c: pl.BlockSpec((sb, RET_W), lambda b, n, c=c: (b * nsb + n, c))
    st_spec = pl.BlockSpec((1, RET_HEADS, RET_DK, RET_DV), lambda b, n: (b, 0, 0, 0))
    return pl.pallas_call(
        functools.partial(_ret_kernel, g_block=g_block),
        grid=(batch, nsb),
        in_specs=[col(0), col(1), col(2), col(3),
                  _const_spec(dmat.shape), _const_spec(qd.shape), _const_spec(kd.shape), st_spec],
        out_specs=(pl.BlockSpec((sb, RET_W), lambda b, n: (b * nsb + n, 0)), st_spec),
        out_shape=(jax.ShapeDtypeStruct((batch * t_len, RET_W), F32),
                   jax.ShapeDtypeStruct((batch, RET_HEADS, RET_DK, RET_DV), F32)),
        scratch_shapes=[pltpu.VMEM((RET_HEADS, RET_DK, RET_DV), F32)],
        compiler_params=pltpu.CompilerParams(dimension_semantics=("parallel", "arbitrary"),
                                             vmem_limit_bytes=VMEM_LIMIT),
        name="retention",
    )(main, main, main, main, dmat, qd, kd, s0)


def _attn_kernel(aq_ref, iq_ref, iw_ref, ki_ref, k_ref, vt_ref, bt_ref, o_ref,
                 s_scr, sb_scr, acc_scr, m_scr, p_scr, *, tq, nkt_static, last_valid, n_valid_q, n_sel,
                 max_search):
    qblk = pl.program_id(1)
    nkt = qblk + 1 if nkt_static is None else nkt_static
    lane_f = lax.broadcasted_iota(jnp.int32, (1, tq), 1).astype(F32)
    lane_ok = lane_f < float(n_valid_q)
    klim = jnp.minimum((jnp.floor(lane_f * (1.0 / CHUNK)) + 1.0) * CHUNK, float(last_valid))
    krow = lax.broadcasted_iota(jnp.int32, (TK, tq), 0).astype(F32)
    adm_last = krow < klim
    nkt_f = nkt.astype(F32) if nkt_static is None else float(nkt)
    n_adm = (nkt_f - 1.0) * TK + klim
    k_target = jnp.minimum(float(n_sel), n_adm)

    iq = iq_ref[...]
    w_t = iw_ref[...].T[IW_OFF:IW_OFF + IDX_HEADS, :] * (IDX_HEADS ** -0.5 * IDX_DIM ** -0.5)
    iq_all = jnp.concatenate(
        [iq[:, h * IDX_DIM:(h + 1) * IDX_DIM].astype(BF16) for h in range(IDX_HEADS)], axis=0)

    def score_tile(kt, n_tiles=1):
        ki_t = ki_ref[kt] if n_tiles == 1 else jnp.concatenate(
            [ki_ref[kt + j] for j in range(n_tiles)], axis=0)
        s_all = lax.dot_general(ki_t, iq_all, NT_DIMS, preferred_element_type=F32)
        acc = jnp.maximum(s_all[:, :tq], 0.0) * w_t[0:1, :]
        for h in range(1, IDX_HEADS):
            acc = acc + jnp.maximum(s_all[:, h * tq:(h + 1) * tq], 0.0) * w_t[h:h + 1, :]
        return acc

    def fold_rows(x, op=jnp.add):
        parts = [x[i * CNT_ROWS:(i + 1) * CNT_ROWS] for i in range(x.shape[0] // CNT_ROWS)]
        while len(parts) > 1:
            parts = [op(a, b) for a, b in zip(parts[0::2], parts[1::2])]
        return parts[0]

    def p1_step(kt, n_tiles, carry):
        rmax, rmin = carry
        sc = score_tile(kt, n_tiles)
        for j in range(n_tiles):
            s_scr[kt + j] = sc[j * TK:(j + 1) * TK]
            sb_scr[kt + j] = sc[j * TK:(j + 1) * TK].astype(BF16)
        return (jnp.maximum(rmax, jnp.max(sc, axis=0, keepdims=True)),
                jnp.minimum(rmin, jnp.min(sc, axis=0, keepdims=True)))

    def run_steps(n, step, carry):
        n4 = n // 4
        carry = lax.fori_loop(0, n4, lambda i, c: step(4 * i, 4, c), carry)
        carry = lax.fori_loop(0, (n - 4 * n4) // 2, lambda i, c: step(4 * n4, 2, c), carry)
        return lax.fori_loop(0, n % 2, lambda i, c: step(n - 1, 1, c), carry)

    init = (jnp.full((1, tq), -jnp.inf, F32), jnp.full((1, tq), jnp.inf, F32))
    rmax, rmin = run_steps(nkt - 1, p1_step, init)
    sc = jnp.where(adm_last, score_tile(nkt - 1), -jnp.inf)
    s_scr[nkt - 1] = sc
    sb_scr[nkt - 1] = sc.astype(BF16)
    rmax = jnp.maximum(rmax, jnp.max(sc, axis=0, keepdims=True))
    rmin = jnp.minimum(rmin, jnp.min(jnp.where(adm_last, sc, jnp.inf), axis=0, keepdims=True))

    def tile_loop(n, body, carry):
        carry = lax.fori_loop(0, n // 2, lambda i, c: body(2 * i + 1, body(2 * i, c)), carry)
        return lax.fori_loop(0, n % 2, lambda i, c: body(n - 1, c), carry)

    def count_where(pred):
        def body(kt, c):
            return c + fold_rows(jnp.where(pred(kt), 1.0, 0.0))
        c = tile_loop(nkt, body, jnp.zeros((CNT_ROWS, tq), F32))
        return jnp.sum(c, axis=0, keepdims=True)

    def count_ge(thr):
        return count_where(lambda kt: s_scr[kt] >= thr)

    def count_ge_rounded(thr_b):
        one, zero = jnp.ones((), BF16), jnp.zeros((), BF16)

        def body(kt, c):
            ind = jnp.where(sb_scr[kt] >= thr_b, one, zero)
            parts = [ind[i * BF16_ROWS:(i + 1) * BF16_ROWS] for i in range(TK // BF16_ROWS)]
            while len(parts) > 1:
                parts = [a + b for a, b in zip(parts[0::2], parts[1::2])]
            return c + parts[0].astype(F32)
        c = tile_loop(nkt, body, jnp.zeros((BF16_ROWS, tq), F32))
        return jnp.sum(c, axis=0, keepdims=True)

    span = jnp.maximum(jnp.maximum(rmax - rmin, jnp.abs(rmax)), 1e-30)
    hi0 = rmax + span * (2.0 ** -10)

    def max_below(bound, n_tiles):
        def body(kt, m):
            s = s_scr[kt]
            return jnp.maximum(m, fold_rows(jnp.where(s < bound, s, -jnp.inf), jnp.maximum))
        m = tile_loop(n_tiles, body, jnp.full((CNT_ROWS, tq), -jnp.inf, F32))
        return jnp.max(m, axis=0, keepdims=True)

    def plan(it, lo, hi, c_lo, c_hi, done):
        width = hi - lo
        frac = (c_lo - k_target - 0.5) / (c_lo - c_hi)
        interp = ((it + 1) % 2).astype(F32)
        mid_i = lo + width * (0.5 + interp * (frac - 0.5))
        mid_b = lo + 0.5 * width
        mid = jnp.where((mid_i > lo) & (mid_i < hi), mid_i, mid_b)
        active = done == 0.0
        narrow = jnp.logical_not((mid > lo) & (mid < hi)) | (width <= span * NARROW)
        wide = active & jnp.logical_not(narrow) & (c_lo - c_hi > SMALL_BRACKET)
        flag = jnp.max(jnp.where(wide, 5.0, jnp.where(active, 3.0, 0.0)))
        return mid, flag

    def search_body(st):
        it, flag, lo, hi, c_lo, c_hi, done, mid = st
        exact = (jnp.zeros((1, tq), F32) + flag) < 4.0
        top = max_below(hi, jnp.where(flag < 4.0, nkt, 0))
        mid = jnp.where(exact, top, mid)
        c = count_ge(mid)
        active = done == 0.0
        ge = c >= k_target
        up = active & ge
        dn = active & jnp.logical_not(ge)
        lo = jnp.where(up, mid, lo)
        c_lo = jnp.where(up, c, c_lo)
        hi = jnp.where(dn, mid, hi)
        c_hi = jnp.where(dn, c, c_hi)
        done = jnp.where((c_lo == k_target) | (exact & up), 1.0, done)
        mid, flag = plan(it + 1, lo, hi, c_lo, c_hi, done)
        return it + 1, flag, lo, hi, c_lo, c_hi, done, mid

    c_zero = count_ge_rounded(jnp.zeros((1, tq), BF16))
    c_pos = count_ge_rounded(jnp.full((1, tq), F32_TINY, BF16))
    pos_side = c_pos >= k_target
    at_zero = jnp.logical_not(pos_side) & (c_zero >= k_target)
    pick = lambda p, z, n: jnp.where(pos_side, p, jnp.where(at_zero, z, n))
    lo0 = pick(F32_TINY, 0.0, rmin)
    hi0 = pick(hi0, F32_TINY, 0.0)
    c_lo0 = pick(c_pos, c_zero, n_adm)
    c_hi0 = pick(0.0, c_pos, c_zero)
    done0 = jnp.where((n_adm == k_target) | at_zero | jnp.logical_not(lane_ok), 1.0, 0.0)

    def coarse_body(it, st):
        lo, hi, c_lo, c_hi, live = st
        width = hi - lo
        frac = (c_lo - k_target - 0.5) / (c_lo - c_hi)
        interp = lax.convert_element_type((it + 1) % 2, F32)
        grid = lambda x: x.astype(BF16).astype(F32)
        mid_i = grid(lo + width * (0.5 + interp * (frac - 0.5)))
        mid_b = grid(lo + 0.5 * width)
        mid = jnp.where((mid_i > lo) & (mid_i < hi), mid_i, mid_b)
        ok = (mid > lo) & (mid < hi) & (live > 0.0)
        c = count_ge_rounded(mid.astype(BF16))
        ge = c >= k_target
        up = ok & ge
        dn = ok & jnp.logical_not(ge)
        return (jnp.where(up, mid, lo), jnp.where(dn, mid, hi), jnp.where(up, c, c_lo),
                jnp.where(dn, c, c_hi), jnp.where(ok, live, 0.0))

    lo1, hi0, c_lo0, c_hi0, _ = lax.fori_loop(
        0, N_COARSE, coarse_body, (lo0, hi0, c_lo0, c_hi0, 1.0 - done0))
    lo0 = jnp.where(lo1 != lo0, lo1 - jnp.abs(lo1) * BF16_STEP, lo0)
    c_lo0 = count_ge(lo0)
    done0 = jnp.where(c_lo0 == k_target, 1.0, done0)
    mid0, flag0 = plan(jnp.int32(0), lo0, hi0, c_lo0, c_hi0, done0)
    st0 = (jnp.int32(0), flag0, lo0, hi0, c_lo0, c_hi0, done0, mid0)
    _, _, lo, hi, cnt_lo, _, _, _ = lax.while_loop(
        lambda st: (st[0] < max_search) & (st[1] > 0.0), search_body, st0)

    excess = jnp.max(jnp.where(lane_ok, cnt_lo - k_target, 0.0))

    @pl.when(excess > 0.0)
    def _():
        need = k_target - count_ge(hi)
        r = lax.broadcasted_iota(jnp.int32, (TK, TK), 0)
        c = lax.broadcasted_iota(jnp.int32, (TK, TK), 1)
        prefix = jnp.where(r >= c, 1.0, 0.0).astype(BF16)

        def drop_step(kt, n_tiles, before):
            tiles = [s_scr[kt + j] for j in range(n_tiles)]
            ties = [(s >= lo) & (s < hi) for s in tiles]
            ranks = [jnp.dot(prefix, jnp.where(t, 1.0, 0.0).astype(BF16),
                             preferred_element_type=F32) for t in ties]
            for j in range(n_tiles):
                rank = before + ranks[j]
                s_scr[kt + j] = jnp.where(ties[j] & (rank > need), -jnp.inf, tiles[j])
                before = rank[TK - 1:TK, :]
            return before

        run_steps(nkt, drop_step, jnp.zeros((1, tq), F32))

    aq = aq_ref[...] * (ATT_HD ** -0.5)
    zeros_q = jnp.zeros((tq, ATT_HD), F32)
    q_pad = []
    for h in range(ATT_HEADS):
        qh = aq[:, h * ATT_HD:(h + 1) * ATT_HD]
        parts = [zeros_q] * ATT_KV_HEADS
        parts[h // HEADS_PER_KV] = qh
        q_pad.append(jnp.concatenate(parts, axis=1).astype(BF16))

    m_scr[...] = jnp.full(m_scr.shape, NEG, F32)
    acc_scr[...] = jnp.zeros(acc_scr.shape, F32)

    def col_max(x):
        parts = [x[i * BF16_ROWS:(i + 1) * BF16_ROWS] for i in range(x.shape[0] // BF16_ROWS)]
        while len(parts) > 1:
            parts = [jnp.maximum(a, b) for a, b in zip(parts[0::2], parts[1::2])]
        return jnp.max(parts[0].astype(F32), axis=0, keepdims=True)

    cat = lambda xs, axis: xs[0] if len(xs) == 1 else jnp.concatenate(xs, axis=axis)

    def logits(kt, n_tiles, near):
        k_t = cat([k_ref[kt + j] for j in range(n_tiles)], 0)
        s_t = cat([s_scr[kt + j] for j in range(n_tiles)], 0)
        neg_mask = jnp.where(s_t >= lo, 0.0, NEG).astype(BF16)
        out = []
        for h in range(ATT_HEADS):
            lg = lax.dot_general(k_t, q_pad[h], NT_DIMS, preferred_element_type=F32)
            if near:
                lg = lg + cat([bt_ref[h, 2 - n_tiles + j] for j in range(n_tiles)], 0)
            out.append(lg.astype(BF16) + neg_mask)
        return out

    def softmax_pv(kt, n_tiles, lgs):
        tiles = [kt + j for j in range(n_tiles)]
        rows = n_tiles * TK
        alphas = []
        for h in range(ATT_HEADS):
            g, r = divmod(h, HEADS_PER_KV)
            lg = lgs[h]
            m_old = m_scr[h:h + 1, :]
            m_new = jnp.maximum(m_old, col_max(lg))
            alphas.append(jnp.exp(m_old - m_new))
            p_scr[g, :rows, r * tq:(r + 1) * tq] = jnp.exp(lg - m_new.astype(BF16))
            m_scr[h:h + 1, :] = m_new
        for g in range(ATT_KV_HEADS):
            hs = range(g * HEADS_PER_KV, (g + 1) * HEADS_PER_KV)
            a_g = jnp.concatenate([alphas[h] for h in hs], axis=1)
            vt_g = cat([vt_ref[t, g * VT_ROWS:(g + 1) * VT_ROWS, :] for t in tiles], 1)
            acc_scr[g] = acc_scr[g] * a_g + jnp.dot(vt_g, p_scr[g, :rows, :],
                                                    preferred_element_type=F32)

    def attend(kt, n_tiles, near):
        softmax_pv(kt, n_tiles, logits(kt, n_tiles, near))

    n_far = jnp.maximum(nkt - 2, 0) if nkt_static is None else max(nkt - 2, 0)

    def far_body(i, c):
        attend(2 * i, 2, False)
        return c
    lax.fori_loop(0, n_far // 2, far_body, 0)

    def when(cond):
        return pl.when(cond) if nkt_static is None else (lambda f: f() if cond else None)

    @when(n_far % 2 == 1)
    def _():
        attend(n_far - 1, 1, False)

    @when(nkt >= 2)
    def _():
        attend(nkt - 2, 2, True)

    @when(nkt < 2)
    def _():
        attend(0, 1, True)

    outs = []
    for g in range(ATT_KV_HEADS):
        a = acc_scr[g]
        o_g = a[:ATT_HD] * (1.0 / a[ATT_HD:ATT_HD + 1])
        outs += [o_g[:, r * tq:(r + 1) * tq] for r in range(HEADS_PER_KV)]
    o_ref[...] = jnp.concatenate(outs, axis=0).T


def _attention(aq_src, iq_src, iw, ki3, k3, vt3, bt, batch, n_qblk, tq, nkt_total, nkt_static,
               last_valid, n_valid_q, n_sel, aq_col, iq_col):
    kern = functools.partial(_attn_kernel, tq=tq, nkt_static=nkt_static, last_valid=last_valid,
                             n_valid_q=n_valid_q, n_sel=n_sel, max_search=MAX_SEARCH)
    qspec = lambda c: pl.BlockSpec((tq, ATT_W), lambda b, q, c=c: (b * n_qblk + q, c))
    kspec = lambda a, c: pl.BlockSpec((nkt_total, a, c), lambda b, q: (b, 0, 0))
    return pl.pallas_call(
        kern,
        grid=(batch, n_qblk),
        in_specs=[qspec(aq_col), qspec(iq_col),
                  pl.BlockSpec((tq, LANE), lambda b, q: (b * n_qblk + q, 0)),
                  kspec(TK, IDX_DIM), kspec(TK, KV_W), kspec(VT_ALL, TK),
                  _const_spec(bt.shape)],
        out_specs=pl.BlockSpec((tq, ATT_W), lambda b, q: (b * n_qblk + q, 0)),
        out_shape=jax.ShapeDtypeStruct((batch * n_qblk * tq, ATT_W), F32),
        scratch_shapes=[pltpu.VMEM((nkt_total, TK, tq), F32),
                        pltpu.VMEM((nkt_total, TK, tq), BF16),
                        pltpu.VMEM((ATT_KV_HEADS, VT_ROWS, HEADS_PER_KV * tq), F32),
                        pltpu.VMEM((ATT_HEADS, tq), F32),
                        pltpu.VMEM((ATT_KV_HEADS, 2 * TK, HEADS_PER_KV * tq), BF16)],
        compiler_params=pltpu.CompilerParams(dimension_semantics=("parallel", "arbitrary"),
                                             vmem_limit_bytes=VMEM_LIMIT),
        name="attention",
    )(aq_src, iq_src, iw, ki3, k3, vt3, bt)


def _ffn_kernel(x_ref, oret_ref, oatt_ref, wout_ref, wup_ref, wdown_ref,
                gpost_ref, gpre_ref, gffn_ref, y_ref, *, ff_chunk):
    mix = jnp.dot(oret_ref[...].astype(BF16), wout_ref[:RET_W, :], preferred_element_type=F32)
    mix = mix + jnp.dot(oatt_ref[...].astype(BF16), wout_ref[RET_W:, :], preferred_element_type=F32)
    h = x_ref[...] + _rms(mix, gpost_ref[...])
    a = _rms(h, gpre_ref[...]).astype(BF16)
    f = jnp.zeros(h.shape, F32)
    for c in range(D_FF // ff_chunk):
        sl = slice(c * ff_chunk, (c + 1) * ff_chunk)
        u = jnp.dot(a, wup_ref[:, sl], preferred_element_type=F32)
        u = jnp.square(jnp.maximum(u, 0.0)).astype(BF16)
        f = f + jnp.dot(u, wdown_ref[sl, :], preferred_element_type=F32)
    y_ref[...] = h + _rms(f, gffn_ref[...])


def _out_ffn(x2, o_ret, o_att, w_out, w_up, w_down, g_post, g_pre, g_ffn, tm):
    rows = x2.shape[0]
    row_spec = lambda w: pl.BlockSpec((tm, w), lambda i: (i, 0))
    return pl.pallas_call(
        functools.partial(_ffn_kernel, ff_chunk=COL_CHUNK),
        grid=(rows // tm,),
        in_specs=[row_spec(D_MODEL), row_spec(RET_W), row_spec(ATT_W),
                  _const_spec(w_out.shape), _const_spec(w_up.shape), _const_spec(w_down.shape),
                  _const_spec((1, D_MODEL)), _const_spec((1, D_MODEL)), _const_spec((1, D_MODEL))],
        out_specs=row_spec(D_MODEL),
        out_shape=jax.ShapeDtypeStruct((rows, D_MODEL), F32),
        compiler_params=pltpu.CompilerParams(dimension_semantics=("parallel",),
                                             vmem_limit_bytes=VMEM_LIMIT),
        name="out_ffn",
    )(x2, o_ret, o_att, w_out, w_up, w_down, g_post, g_pre, g_ffn)


def _t5_bucket(rel):
    half = N_BUCKETS // 2
    max_exact = half // 2
    ret = jnp.where(rel > 0, half, 0)
    n = jnp.abs(rel)
    nf = jnp.maximum(n, 1).astype(F32)
    large = max_exact + (jnp.log(nf / max_exact) / math.log(MAX_DISTANCE / max_exact)
                         * (half - max_exact)).astype(jnp.int32)
    large = jnp.minimum(large, half - 1)
    return ret + jnp.where(n < max_exact, n, large)


def _bias_tiles(rel_bias, tq):
    period = 2 * TK + tq
    rel = jnp.arange(period, dtype=jnp.int32) - (TK + tq - 1)
    table = rel_bias.astype(F32)
    far = table[_t5_bucket(jnp.int32(-MAX_DISTANCE))]
    u = (table[_t5_bucket(jnp.clip(rel, -MAX_DISTANCE, MAX_DISTANCE))] - far).T
    n = jnp.tile(u, (1, tq))[:, :tq * (period - 1)].reshape(ATT_HEADS, tq, period - 1)
    m = n[:, :, tq - 1:tq - 1 + 2 * TK]
    return jnp.transpose(m.reshape(ATT_HEADS, tq, 2, TK), (0, 2, 3, 1))


def _rope_tables(pos):
    half = RET_DK // 2
    inv = ROPE_BASE ** (-jnp.arange(half, dtype=F32) / half)
    ang = pos.astype(F32)[:, None] * inv[None, :]
    cos, sin = jnp.cos(ang), jnp.sin(ang)
    cos_h = jnp.concatenate([cos, cos], axis=1)
    sin_h = jnp.concatenate([-sin, sin], axis=1)
    reps = LANE // RET_DK
    return jnp.tile(cos_h, (1, reps)), jnp.tile(sin_h, (1, reps))


def _permute_w_in(w_in):
    offs = np.cumsum([0, RET_W, RET_W, RET_W, RET_W, ATT_W, KV_W, KV_W, IDX_W, IDX_DIM, IDX_HEADS])
    seg = lambda i: w_in[:, offs[i]:offs[i + 1]]
    pad = jnp.zeros((D_MODEL, LANE - IDX_DIM - IDX_HEADS), w_in.dtype)
    return jnp.concatenate([seg(0), seg(1), seg(2), seg(3), seg(4), seg(7),
                            seg(5), seg(6), seg(8), seg(9), pad], axis=1).astype(BF16)


def _layer(x, s0, past, weights, rel_bias):
    w_perm, w_out, w_up, w_down, g_pre_mix, g_post_mix, g_pre_ffn, g_post_ffn = weights
    batch, t_len, _ = x.shape
    rows = batch * t_len
    x2 = x.reshape(rows, D_MODEL)
    tm = min(ROW_TILE, rows)
    p_len = 0 if past is None else past[0].shape[1]
    pos = p_len + jnp.arange(t_len, dtype=jnp.int32)
    cos_t, sin_t = _rope_tables(pos)
    main, kc, vc, ic, iw, *key_tiles = _inproj(x2, g_pre_mix, w_perm, cos_t, sin_t, tm,
                                               key_tiles=past is None)

    chunk = min(CHUNK, t_len)
    sb = min(RET_BLOCK, t_len)
    o_ret, s_new = _retention(main, s0, batch, t_len, sb, chunk)

    l_all = p_len + t_len
    n_sel = min(TOPK_MAX, l_all // 4)
    if past is None:
        tq = TK
        n_qblk = t_len // tq
        nkt_total = t_len // TK
        bt = _bias_tiles(rel_bias, tq)
        k3, ki3, vt3 = key_tiles
        o_att = _attention(main, main, iw, ki3, k3, vt3, bt, batch, n_qblk, tq, nkt_total, None,
                           TK, tq, n_sel, aq_col=4, iq_col=5)
    else:
        tq = LANE
        pk, pv, pi = past
        assert p_len % TK == 0 and t_len <= min(tq, TK), (p_len, t_len)
        nkt_total = p_len // TK + 1
        padk = nkt_total * TK - l_all
        cat = lambda old, new: jnp.concatenate(
            [old.astype(BF16), new.reshape(batch, t_len, -1).astype(BF16),
             jnp.zeros((batch, padk, new.shape[-1]), BF16)], axis=1)
        k_all = cat(pk.reshape(batch, p_len, KV_W), kc)
        v_all = cat(pv.reshape(batch, p_len, KV_W), vc)
        i_all = cat(pi, ic)
        v_t = jnp.transpose(v_all.reshape(batch, nkt_total, TK, ATT_KV_HEADS, ATT_HD), (0, 1, 3, 4, 2))
        ones = jnp.zeros((batch, nkt_total, ATT_KV_HEADS, VT_ROWS - ATT_HD, TK), BF16).at[:, :, :, 0, :].set(1.0)
        vt_s = jnp.concatenate([v_t, ones], axis=3).reshape(batch * nkt_total, VT_ALL, TK)
        k_s = k_all.reshape(batch * nkt_total, TK, KV_W)
        i_s = i_all.reshape(batch * nkt_total, TK, IDX_DIM)
        padq = lambda a: jnp.pad(a.reshape(batch, t_len, -1), ((0, 0), (0, tq - t_len), (0, 0))
                                 ).reshape(batch * tq, -1)
        aq_s = padq(main[:, 4 * RET_W:4 * RET_W + ATT_W])
        iq_s = padq(main[:, 4 * RET_W + ATT_W:])
        iw_s = padq(iw)
        bt = _bias_tiles(rel_bias, tq)
        o_pad = _attention(aq_s, iq_s, iw_s, i_s, k_s, vt_s, bt, batch, 1, tq, nkt_total, nkt_total,
                           l_all - p_len, t_len, n_sel, aq_col=0, iq_col=0)
        o_att = o_pad.reshape(batch, tq, ATT_W)[:, :t_len].reshape(rows, ATT_W)

    y = _out_ffn(x2, o_ret, o_att, w_out, w_up, w_down, g_post_mix, g_pre_ffn, g_post_ffn, tm)
    return (y.reshape(batch, t_len, D_MODEL), s_new,
            kc.reshape(batch, t_len, ATT_KV_HEADS, ATT_HD),
            vc.reshape(batch, t_len, ATT_KV_HEADS, ATT_HD),
            ic.reshape(batch, t_len, IDX_DIM))


def kernel(x_prompt, x_sample, state_ret, cache_k, cache_v, cache_kidx, w_in, w_out, w_up, w_down,
           g_pre_mix, g_post_mix, g_pre_ffn, g_post_ffn, rel_bias):
    depth = w_in.shape[0]
    bp = x_prompt.shape[0]
    zero_state = jnp.zeros((bp, RET_HEADS, RET_DK, RET_DV), F32)
    yp, ys = x_prompt, x_sample
    outs_p, outs_s = [], []
    for l in range(depth):
        row = lambda g: g[l].reshape(1, D_MODEL).astype(F32)
        weights = (_permute_w_in(w_in[l]), w_out[l].astype(BF16), w_up[l].astype(BF16),
                   w_down[l].astype(BF16), row(g_pre_mix), row(g_post_mix), row(g_pre_ffn),
                   row(g_post_ffn))
        yp, *rest_p = _layer(yp, zero_state, None, weights, rel_bias)
        ys, *rest_s = _layer(ys, state_ret[l], (cache_k[l], cache_v[l], cache_kidx[l]), weights, rel_bias)
        outs_p.append(rest_p)
        outs_s.append(rest_s)
    stack = lambda outs, i: jnp.stack([o[i] for o in outs])
    return (yp, ys,
            stack(outs_p, 0), stack(outs_p, 1), stack(outs_p, 2), stack(outs_p, 3),
            stack(outs_s, 0), stack(outs_s, 1), stack(outs_s, 2), stack(outs_s, 3))
```

```python
import functools
import math

import jax
import jax.numpy as jnp
import numpy as np
from jax import lax
from jax.experimental import pallas as pl
from jax.experimental.pallas import tpu as pltpu

D_MODEL = 1024
CHUNK = 64
RET_HEADS = 8
RET_DK = 64
RET_DV = 64
ATT_HEADS = 8
ATT_KV_HEADS = 2
ATT_HD = 64
IDX_HEADS = 8
IDX_DIM = 64
TOPK_MAX = 256
N_BUCKETS = 32
MAX_DISTANCE = 128
D_FF = 4 * D_MODEL
ROPE_BASE = 10000.0
EPS = 1e-6

RET_W = RET_HEADS * RET_DV
ATT_W = ATT_HEADS * ATT_HD
KV_W = ATT_KV_HEADS * ATT_HD
IDX_W = IDX_HEADS * IDX_DIM
HEADS_PER_KV = ATT_HEADS // ATT_KV_HEADS

MAIN_W = 4 * RET_W + ATT_W + IDX_W
TAIL_W = 3 * 128
IW_OFF = IDX_DIM

LANE = 128
VT_ROWS = 80
VT_ALL = ATT_KV_HEADS * VT_ROWS
TK = 256
ROW_TILE = 512
COL_CHUNK = 512
RET_BLOCK = 512
NEG = -(2.0 ** 100)
BF16_ROWS = 16
CNT_ROWS = 32
NARROW = 2.0 ** -20
MAX_SEARCH = 96
F32_TINY = float(np.finfo(np.float32).tiny)
N_COARSE = 8
SMALL_BRACKET = 2.0
N_BLIND = 4
BF16_STEP = 2.0 ** -7
VMEM_LIMIT = 56 * 1024 * 1024

F32 = jnp.float32
BF16 = jnp.bfloat16
NT_DIMS = (((1,), (1,)), ((), ()))
TN_DIMS = (((0,), (0,)), ((), ()))


def _const_spec(shape):
    nd = len(shape)
    return pl.BlockSpec(shape, lambda *_: (0,) * nd, pipeline_mode=pl.Buffered(1))


def _rms(x, gain):
    return x * lax.rsqrt(jnp.mean(x * x, axis=-1, keepdims=True) + EPS) * gain


def _inproj_kernel(x_ref, g_ref, w_ref, cos_ref, sin_ref, main_ref, kc_ref, vc_ref, ic_ref, iw_ref,
                   *tile_refs, tm):
    a = _rms(x_ref[...], g_ref[...]).astype(BF16)
    reps = RET_W // LANE
    cos = jnp.concatenate([cos_ref[...]] * reps, axis=1)
    sin = jnp.concatenate([sin_ref[...]] * reps, axis=1)
    lane = lax.broadcasted_iota(jnp.int32, cos.shape, 1)
    first_half = (lane & (RET_DK - 1)) < RET_DK // 2

    def rot(x):
        partner = jnp.where(first_half, pltpu.roll(x, RET_W - RET_DK // 2, 1),
                            pltpu.roll(x, RET_DK // 2, 1))
        return x * cos + partner * sin

    assert COL_CHUNK == RET_W
    for c in range(MAIN_W // COL_CHUNK):
        cols = slice(c * COL_CHUNK, (c + 1) * COL_CHUNK)
        y = jnp.dot(a, w_ref[:, cols], preferred_element_type=F32)
        if c == 0:
            y = rot(y)
        elif c == 1:
            y = rot(y) * (RET_DK ** -0.5)
        main_ref[:, cols] = y
    tail = jnp.dot(a, w_ref[:, MAIN_W:MAIN_W + TAIL_W], preferred_element_type=F32)
    ak = tail[:, 0:KV_W]
    av = tail[:, KV_W:2 * KV_W]
    last = tail[:, 2 * KV_W:3 * KV_W]
    kc_ref[...] = ak
    vc_ref[...] = av
    ic_ref[...] = last[:, :IDX_DIM]
    iw_ref[...] = last
    if not tile_refs:
        return
    k3_ref, ki3_ref, vt3_ref = tile_refs
    avt = av.T
    row = lax.broadcasted_iota(jnp.int32, (VT_ROWS - ATT_HD, tm), 0)
    ones_rows = jnp.where(row == 0, 1.0, 0.0).astype(F32)
    vt = jnp.concatenate([avt[:ATT_HD], ones_rows, avt[ATT_HD:], ones_rows], axis=0).astype(BF16)
    for j in range(tm // TK):
        k3_ref[j] = ak[j * TK:(j + 1) * TK].astype(BF16)
        ki3_ref[j] = last[j * TK:(j + 1) * TK, :IDX_DIM].astype(BF16)
        vt3_ref[j] = vt[:, j * TK:(j + 1) * TK]


def _inproj(x2, gain, w_perm, cos_t, sin_t, tm, key_tiles):
    rows = x2.shape[0]
    grid = (rows // tm,)
    t_len = cos_t.shape[0]
    if t_len < tm:
        assert tm % t_len == 0
        cos_t, sin_t = jnp.tile(cos_t, (tm // t_len, 1)), jnp.tile(sin_t, (tm // t_len, 1))
        t_len = tm
    assert t_len % tm == 0
    tab_spec = pl.BlockSpec((tm, LANE), lambda i: (i % (t_len // tm), 0))
    row_spec = lambda w: pl.BlockSpec((tm, w), lambda i: (i, 0))
    out_shape = (
        jax.ShapeDtypeStruct((rows, MAIN_W), F32),
        jax.ShapeDtypeStruct((rows, KV_W), F32),
        jax.ShapeDtypeStruct((rows, KV_W), F32),
        jax.ShapeDtypeStruct((rows, IDX_DIM), F32),
        jax.ShapeDtypeStruct((rows, LANE), F32),
    )
    out_specs = (row_spec(MAIN_W), row_spec(KV_W), row_spec(KV_W), row_spec(IDX_DIM), row_spec(LANE))
    if key_tiles:
        t3 = lambda a, b: pl.BlockSpec((tm // TK, a, b), lambda i: (i, 0, 0))
        out_shape += (jax.ShapeDtypeStruct((rows // TK, TK, KV_W), BF16),
                      jax.ShapeDtypeStruct((rows // TK, TK, IDX_DIM), BF16),
                      jax.ShapeDtypeStruct((rows // TK, VT_ALL, TK), BF16))
        out_specs += (t3(TK, KV_W), t3(TK, IDX_DIM), t3(VT_ALL, TK))
    return pl.pallas_call(
        functools.partial(_inproj_kernel, tm=tm),
        grid=grid,
        in_specs=[row_spec(D_MODEL), _const_spec((1, D_MODEL)), _const_spec(w_perm.shape),
                  tab_spec, tab_spec],
        out_specs=out_specs,
        out_shape=out_shape,
        compiler_params=pltpu.CompilerParams(dimension_semantics=("parallel",),
                                             vmem_limit_bytes=VMEM_LIMIT),
        name="inproj",
    )(x2, gain, w_perm, cos_t, sin_t)


def _ret_gammas():
    return [1.0 - 2.0 ** (-5.0 - h) for h in range(RET_HEADS)]


def _ret_tables(sb, chunk):
    lg = np.log(np.array(_ret_gammas(), np.float64))
    t = np.arange(sb)
    ci = t // chunk
    diff = t[:, None] - t[None, :]
    same = ci[:, None] == ci[None, :]
    below = ci[None, :] < ci[:, None]
    expo = np.where(same, np.abs(diff), np.where(below, diff, 0)).astype(np.float64)
    dmat = np.exp(lg[:, None, None] * expo[None]) * (same | below)[None]
    qd = np.exp(lg[None, :] * (t + 1.0)[:, None])
    kd = np.exp(lg[None, :] * (sb - 1.0 - t)[:, None])
    qd = np.repeat(qd, RET_DK, axis=1)
    kd = np.repeat(kd, RET_DK, axis=1)
    return (jnp.asarray(dmat, F32), jnp.asarray(qd, F32), jnp.asarray(kd, F32),
            [float(math.exp(v * sb)) for v in lg])


def _ret_kernel(q_ref, k_ref, v_ref, g_ref, d_ref, qd_ref, kd_ref, s0_ref,
                o_ref, sfin_ref, s_scr, *, g_block):
    n = pl.program_id(1)

    @pl.when(n == 0)
    def _():
        s_scr[...] = s0_ref[0]

    q = q_ref[...]
    k = k_ref[...]
    v = v_ref[...].astype(BF16)
    gate = g_ref[...]
    qb = q.astype(BF16)
    kb = k.astype(BF16)
    qx = (q * qd_ref[...]).astype(BF16)
    kx = (k * kd_ref[...]).astype(BF16)
    outs = []
    for h in range(RET_HEADS):
        sl = slice(h * RET_DK, (h + 1) * RET_DK)
        s = lax.dot_general(qb[:, sl], kb[:, sl], NT_DIMS, preferred_element_type=F32)
        p = (s * d_ref[h]).astype(BF16)
        st = s_scr[h]
        o = jnp.dot(p, v[:, sl], preferred_element_type=F32)
        o = o + jnp.dot(qx[:, sl], st.astype(BF16), preferred_element_type=F32)
        s_scr[h] = g_block[h] * st + lax.dot_general(kx[:, sl], v[:, sl], TN_DIMS,
                                                     preferred_element_type=F32)
        o = o * lax.rsqrt(jnp.mean(o * o, axis=-1, keepdims=True) + EPS)
        outs.append(o)
    o_all = jnp.concatenate(outs, axis=1)
    o_ref[...] = gate * (1.0 / (1.0 + jnp.exp(-gate))) * o_all

    @pl.when(n == pl.num_programs(1) - 1)
    def _():
        sfin_ref[0] = s_scr[...]


def _retention(main, s0, batch, t_len, sb, chunk):
    nsb = t_len // sb
    dmat, qd, kd, g_block = _ret_tables(sb, chunk)
    col = lambda c: pl.BlockSpec((sb, RET_W), lambda b, n, c=c: (b * nsb + n, c))
    st_spec = pl.BlockSpec((1, RET_HEADS, RET_DK, RET_DV), lambda b, n: (b, 0, 0, 0))
    return pl.pallas_call(
        functools.partial(_ret_kernel, g_block=g_block),
        grid=(batch, nsb),
        in_specs=[col(0), col(1), col(2), col(3),
                  _const_spec(dmat.shape), _const_spec(qd.shape), _const_spec(kd.shape), st_spec],
        out_specs=(pl.BlockSpec((sb, RET_W), lambda b, n: (b * nsb + n, 0)), st_spec),
        out_shape=(jax.ShapeDtypeStruct((batch * t_len, RET_W), F32),
                   jax.ShapeDtypeStruct((batch, RET_HEADS, RET_DK, RET_DV), F32)),
        scratch_shapes=[pltpu.VMEM((RET_HEADS, RET_DK, RET_DV), F32)],
        compiler_params=pltpu.CompilerParams(dimension_semantics=("parallel", "arbitrary"),
                                             vmem_limit_bytes=VMEM_LIMIT),
        name="retention",
    )(main, main, main, main, dmat, qd, kd, s0)


def _attn_kernel(aq_ref, iq_ref, iw_ref, ki_ref, k_ref, vt_ref, bt_ref, o_ref,
                 s_scr, sb_scr, acc_scr, m_scr, p_scr, *, tq, nkt_static, last_valid, n_valid_q, n_sel,
                 max_search):
    qblk = pl.program_id(1)
    nkt = qblk + 1 if nkt_static is None else nkt_static
    lane_f = lax.broadcasted_iota(jnp.int32, (1, tq), 1).astype(F32)
    lane_ok = lane_f < float(n_valid_q)
    klim = jnp.minimum((jnp.floor(lane_f * (1.0 / CHUNK)) + 1.0) * CHUNK, float(last_valid))
    krow = lax.broadcasted_iota(jnp.int32, (TK, tq), 0).astype(F32)
    adm_last = krow < klim
    nkt_f = nkt.astype(F32) if nkt_static is None else float(nkt)
    n_adm = (nkt_f - 1.0) * TK + klim
    k_target = jnp.minimum(float(n_sel), n_adm)

    iq = iq_ref[...]
    w_t = iw_ref[...].T[IW_OFF:IW_OFF + IDX_HEADS, :] * (IDX_HEADS ** -0.5 * IDX_DIM ** -0.5)
    iq_all = jnp.concatenate(
        [iq[:, h * IDX_DIM:(h + 1) * IDX_DIM].astype(BF16) for h in range(IDX_HEADS)], axis=0)

    def score_tile(kt, n_tiles=1):
        ki_t = ki_ref[kt] if n_tiles == 1 else jnp.concatenate(
            [ki_ref[kt + j] for j in range(n_tiles)], axis=0)
        s_all = lax.dot_general(ki_t, iq_all, NT_DIMS, preferred_element_type=F32)
        acc = jnp.maximum(s_all[:, :tq], 0.0) * w_t[0:1, :]
        for h in range(1, IDX_HEADS):
            acc = acc + jnp.maximum(s_all[:, h * tq:(h + 1) * tq], 0.0) * w_t[h:h + 1, :]
        return acc

    def fold_rows(x, op=jnp.add):
        parts = [x[i * CNT_ROWS:(i + 1) * CNT_ROWS] for i in range(x.shape[0] // CNT_ROWS)]
        while len(parts) > 1:
            parts = [op(a, b) for a, b in zip(parts[0::2], parts[1::2])]
        return parts[0]

    def p1_step(kt, n_tiles, carry):
        rmax, rmin = carry
        sc = score_tile(kt, n_tiles)
        for j in range(n_tiles):
            s_scr[kt + j] = sc[j * TK:(j + 1) * TK]
            sb_scr[kt + j] = sc[j * TK:(j + 1) * TK].astype(BF16)
        return (jnp.maximum(rmax, jnp.max(sc, axis=0, keepdims=True)),
                jnp.minimum(rmin, jnp.min(sc, axis=0, keepdims=True)))

    def run_steps(n, step, carry):
        n4 = n // 4
        carry = lax.fori_loop(0, n4, lambda i, c: step(4 * i, 4, c), carry)
        carry = lax.fori_loop(0, (n - 4 * n4) // 2, lambda i, c: step(4 * n4, 2, c), carry)
        return lax.fori_loop(0, n % 2, lambda i, c: step(n - 1, 1, c), carry)

    init = (jnp.full((1, tq), -jnp.inf, F32), jnp.full((1, tq), jnp.inf, F32))
    rmax, rmin = run_steps(nkt - 1, p1_step, init)
    sc = jnp.where(adm_last, score_tile(nkt - 1), -jnp.inf)
    s_scr[nkt - 1] = sc
    sb_scr[nkt - 1] = sc.astype(BF16)
    rmax = jnp.maximum(rmax, jnp.max(sc, axis=0, keepdims=True))
    rmin = jnp.minimum(rmin, jnp.min(jnp.where(adm_last, sc, jnp.inf), axis=0, keepdims=True))

    def tile_loop(n, body, carry):
        carry = lax.fori_loop(0, n // 2, lambda i, c: body(2 * i + 1, body(2 * i, c)), carry)
        return lax.fori_loop(0, n % 2, lambda i, c: body(n - 1, c), carry)

    def count_where(pred):
        def body(kt, c):
            return c + fold_rows(jnp.where(pred(kt), 1.0, 0.0))
        c = tile_loop(nkt, body, jnp.zeros((CNT_ROWS, tq), F32))
        return jnp.sum(c, axis=0, keepdims=True)

    def count_ge(thr):
        return count_where(lambda kt: s_scr[kt] >= thr)

    def count_ge_rounded(thr_b):
        one, zero = jnp.ones((), BF16), jnp.zeros((), BF16)

        def body(kt, c):
            ind = jnp.where(sb_scr[kt] >= thr_b, one, zero)
            parts = [ind[i * BF16_ROWS:(i + 1) * BF16_ROWS] for i in range(TK // BF16_ROWS)]
            while len(parts) > 1:
                parts = [a + b for a, b in zip(parts[0::2], parts[1::2])]
            return c + parts[0].astype(F32)
        c = tile_loop(nkt, body, jnp.zeros((BF16_ROWS, tq), F32))
        return jnp.sum(c, axis=0, keepdims=True)

    span = jnp.maximum(jnp.maximum(rmax - rmin, jnp.abs(rmax)), 1e-30)
    hi0 = rmax + span * (2.0 ** -10)

    def max_below(bound, n_tiles):
        def body(kt, m):
            s = s_scr[kt]
            return jnp.maximum(m, fold_rows(jnp.where(s < bound, s, -jnp.inf), jnp.maximum))
        m = tile_loop(n_tiles, body, jnp.full((CNT_ROWS, tq), -jnp.inf, F32))
        return jnp.max(m, axis=0, keepdims=True)

    def plan(it, lo, hi, c_lo, c_hi, done, want_flag=True):
        width = hi - lo
        frac = (c_lo - k_target - 0.5) / (c_lo - c_hi)
        interp = lax.convert_element_type((it + 1) % 2, F32)
        mid_i = lo + width * (0.5 + interp * (frac - 0.5))
        mid_b = lo + 0.5 * width
        mid = jnp.where((mid_i > lo) & (mid_i < hi), mid_i, mid_b)
        if not want_flag:
            return mid, None
        active = done == 0.0
        narrow = jnp.logical_not((mid > lo) & (mid < hi)) | (width <= span * NARROW)
        wide = active & jnp.logical_not(narrow) & (c_lo - c_hi > SMALL_BRACKET)
        flag = jnp.max(jnp.where(wide, 5.0, jnp.where(active, 3.0, 0.0)))
        return mid, flag

    def accept(mid, c, exact, lo, hi, c_lo, c_hi, done):
        active = done == 0.0
        ge = c >= k_target
        up = active & ge
        dn = active & jnp.logical_not(ge)
        lo = jnp.where(up, mid, lo)
        c_lo = jnp.where(up, c, c_lo)
        hi = jnp.where(dn, mid, hi)
        c_hi = jnp.where(dn, c, c_hi)
        finished = (c_lo == k_target) if exact is None else (c_lo == k_target) | (exact & up)
        return lo, hi, c_lo, c_hi, jnp.where(finished, 1.0, done)

    def blind_body(it, st):
        lo, hi, c_lo, c_hi, done = st
        mid, _ = plan(it, lo, hi, c_lo, c_hi, done, want_flag=False)
        return accept(mid, count_ge(mid), None, lo, hi, c_lo, c_hi, done)

    def search_body(st):
        it, flag, lo, hi, c_lo, c_hi, done, mid = st
        exact = (jnp.zeros((1, tq), F32) + flag) < 4.0
        top = max_below(hi, jnp.where(flag < 4.0, nkt, 0))
        mid = jnp.where(exact, top, mid)
        lo, hi, c_lo, c_hi, done = accept(mid, count_ge(mid), exact, lo, hi, c_lo, c_hi, done)
        mid, flag = plan(it + 1, lo, hi, c_lo, c_hi, done)
        return it + 1, flag, lo, hi, c_lo, c_hi, done, mid

    c_zero = count_ge_rounded(jnp.zeros((1, tq), BF16))
    c_pos = count_ge_rounded(jnp.full((1, tq), F32_TINY, BF16))
    pos_side = c_pos >= k_target
    at_zero = jnp.logical_not(pos_side) & (c_zero >= k_target)
    pick = lambda p, z, n: jnp.where(pos_side, p, jnp.where(at_zero, z, n))
    lo0 = pick(F32_TINY, 0.0, rmin)
    hi0 = pick(hi0, F32_TINY, 0.0)
    c_lo0 = pick(c_pos, c_zero, n_adm)
    c_hi0 = pick(0.0, c_pos, c_zero)
    done0 = jnp.where((n_adm == k_target) | at_zero | jnp.logical_not(lane_ok), 1.0, 0.0)

    def coarse_body(it, st):
        lo, hi, c_lo, c_hi, live = st
        width = hi - lo
        frac = (c_lo - k_target - 0.5) / (c_lo - c_hi)
        interp = lax.convert_element_type((it + 1) % 2, F32)
        grid = lambda x: x.astype(BF16).astype(F32)
        mid_i = grid(lo + width * (0.5 + interp * (frac - 0.5)))
        mid_b = grid(lo + 0.5 * width)
        mid = jnp.where((mid_i > lo) & (mid_i < hi), mid_i, mid_b)
        ok = (mid > lo) & (mid < hi) & (live > 0.0)
        c = count_ge_rounded(mid.astype(BF16))
        ge = c >= k_target
        up = ok & ge
        dn = ok & jnp.logical_not(ge)
        return (jnp.where(up, mid, lo), jnp.where(dn, mid, hi), jnp.where(up, c, c_lo),
                jnp.where(dn, c, c_hi), jnp.where(ok, live, 0.0))

    lo1, hi0, c_lo0, c_hi0, _ = lax.fori_loop(
        0, N_COARSE, coarse_body, (lo0, hi0, c_lo0, c_hi0, 1.0 - done0))
    lo0 = jnp.where(lo1 != lo0, lo1 - jnp.abs(lo1) * BF16_STEP, lo0)
    c_lo0 = count_ge(lo0)
    done0 = jnp.where(c_lo0 == k_target, 1.0, done0)
    lo0, hi0, c_lo0, c_hi0, done0 = lax.fori_loop(
        0, N_BLIND, blind_body, (lo0, hi0, c_lo0, c_hi0, done0))
    mid0, flag0 = plan(jnp.int32(N_BLIND), lo0, hi0, c_lo0, c_hi0, done0)
    st0 = (jnp.int32(N_BLIND), flag0, lo0, hi0, c_lo0, c_hi0, done0, mid0)
    _, _, lo, hi, cnt_lo, _, _, _ = lax.while_loop(
        lambda st: (st[0] < max_search) & (st[1] > 0.0), search_body, st0)

    excess = jnp.max(jnp.where(lane_ok, cnt_lo - k_target, 0.0))

    @pl.when(excess > 0.0)
    def _():
        need = k_target - count_ge(hi)
        r = lax.broadcasted_iota(jnp.int32, (TK, TK), 0)
        c = lax.broadcasted_iota(jnp.int32, (TK, TK), 1)
        prefix = jnp.where(r >= c, 1.0, 0.0).astype(BF16)

        def drop_step(kt, n_tiles, before):
            tiles = [s_scr[kt + j] for j in range(n_tiles)]
            ties = [(s >= lo) & (s < hi) for s in tiles]
            ranks = [jnp.dot(prefix, jnp.where(t, 1.0, 0.0).astype(BF16),
                             preferred_element_type=F32) for t in ties]
            for j in range(n_tiles):
                rank = before + ranks[j]
                s_scr[kt + j] = jnp.where(ties[j] & (rank > need), -jnp.inf, tiles[j])
                before = rank[TK - 1:TK, :]
            return before

        run_steps(nkt, drop_step, jnp.zeros((1, tq), F32))

    aq = aq_ref[...] * (ATT_HD ** -0.5)
    zeros_q = jnp.zeros((tq, ATT_HD), F32)
    q_pad = []
    for h in range(ATT_HEADS):
        qh = aq[:, h * ATT_HD:(h + 1) * ATT_HD]
        parts = [zeros_q] * ATT_KV_HEADS
        parts[h // HEADS_PER_KV] = qh
        q_pad.append(jnp.concatenate(parts, axis=1).astype(BF16))

    m_scr[...] = jnp.full(m_scr.shape, NEG, F32)
    acc_scr[...] = jnp.zeros(acc_scr.shape, F32)

    def col_max(x):
        parts = [x[i * BF16_ROWS:(i + 1) * BF16_ROWS] for i in range(x.shape[0] // BF16_ROWS)]
        while len(parts) > 1:
            parts = [jnp.maximum(a, b) for a, b in zip(parts[0::2], parts[1::2])]
        return jnp.max(parts[0].astype(F32), axis=0, keepdims=True)

    cat = lambda xs, axis: xs[0] if len(xs) == 1 else jnp.concatenate(xs, axis=axis)

    def logits(kt, n_tiles, near):
        k_t = cat([k_ref[kt + j] for j in range(n_tiles)], 0)
        s_t = cat([s_scr[kt + j] for j in range(n_tiles)], 0)
        neg_mask = jnp.where(s_t >= lo, 0.0, NEG).astype(BF16)
        out = []
        for h in range(ATT_HEADS):
            lg = lax.dot_general(k_t, q_pad[h], NT_DIMS, preferred_element_type=F32)
            if near:
                lg = lg + cat([bt_ref[h, 2 - n_tiles + j] for j in range(n_tiles)], 0)
            out.append(lg.astype(BF16) + neg_mask)
        return out

    def softmax_pv(kt, n_tiles, lgs):
        tiles = [kt + j for j in range(n_tiles)]
        rows = n_tiles * TK
        alphas = []
        for h in range(ATT_HEADS):
            g, r = divmod(h, HEADS_PER_KV)
            lg = lgs[h]
            m_old = m_scr[h:h + 1, :]
            m_new = jnp.maximum(m_old, col_max(lg))
            alphas.append(jnp.exp(m_old - m_new))
            p_scr[g, :rows, r * tq:(r + 1) * tq] = jnp.exp(lg - m_new.astype(BF16))
            m_scr[h:h + 1, :] = m_new
        for g in range(ATT_KV_HEADS):
            hs = range(g * HEADS_PER_KV, (g + 1) * HEADS_PER_KV)
            a_g = jnp.concatenate([alphas[h] for h in hs], axis=1)
            vt_g = cat([vt_ref[t, g * VT_ROWS:(g + 1) * VT_ROWS, :] for t in tiles], 1)
            acc_scr[g] = acc_scr[g] * a_g + jnp.dot(vt_g, p_scr[g, :rows, :],
                                                    preferred_element_type=F32)

    def attend(kt, n_tiles, near):
        softmax_pv(kt, n_tiles, logits(kt, n_tiles, near))

    n_far = jnp.maximum(nkt - 2, 0) if nkt_static is None else max(nkt - 2, 0)

    def far_body(i, c):
        attend(2 * i, 2, False)
        return c
    lax.fori_loop(0, n_far // 2, far_body, 0)

    def when(cond):
        return pl.when(cond) if nkt_static is None else (lambda f: f() if cond else None)

    @when(n_far % 2 == 1)
    def _():
        attend(n_far - 1, 1, False)

    @when(nkt >= 2)
    def _():
        attend(nkt - 2, 2, True)

    @when(nkt < 2)
    def _():
        attend(0, 1, True)

    outs = []
    for g in range(ATT_KV_HEADS):
        a = acc_scr[g]
        o_g = a[:ATT_HD] * (1.0 / a[ATT_HD:ATT_HD + 1])
        outs += [o_g[:, r * tq:(r + 1) * tq] for r in range(HEADS_PER_KV)]
    o_ref[...] = jnp.concatenate(outs, axis=0).T


def _attention(aq_src, iq_src, iw, ki3, k3, vt3, bt, batch, n_qblk, tq, nkt_total, nkt_static,
               last_valid, n_valid_q, n_sel, aq_col, iq_col):
    kern = functools.partial(_attn_kernel, tq=tq, nkt_static=nkt_static, last_valid=last_valid,
                             n_valid_q=n_valid_q, n_sel=n_sel, max_search=MAX_SEARCH)
    qspec = lambda c: pl.BlockSpec((tq, ATT_W), lambda b, q, c=c: (b * n_qblk + q, c))
    kspec = lambda a, c: pl.BlockSpec((nkt_total, a, c), lambda b, q: (b, 0, 0))
    return pl.pallas_call(
        kern,
        grid=(batch, n_qblk),
        in_specs=[qspec(aq_col), qspec(iq_col),
                  pl.BlockSpec((tq, LANE), lambda b, q: (b * n_qblk + q, 0)),
                  kspec(TK, IDX_DIM), kspec(TK, KV_W), kspec(VT_ALL, TK),
                  _const_spec(bt.shape)],
        out_specs=pl.BlockSpec((tq, ATT_W), lambda b, q: (b * n_qblk + q, 0)),
        out_shape=jax.ShapeDtypeStruct((batch * n_qblk * tq, ATT_W), F32),
        scratch_shapes=[pltpu.VMEM((nkt_total, TK, tq), F32),
                        pltpu.VMEM((nkt_total, TK, tq), BF16),
                        pltpu.VMEM((ATT_KV_HEADS, VT_ROWS, HEADS_PER_KV * tq), F32),
                        pltpu.VMEM((ATT_HEADS, tq), F32),
                        pltpu.VMEM((ATT_KV_HEADS, 2 * TK, HEADS_PER_KV * tq), BF16)],
        compiler_params=pltpu.CompilerParams(dimension_semantics=("parallel", "arbitrary"),
                                             vmem_limit_bytes=VMEM_LIMIT),
        name="attention",
    )(aq_src, iq_src, iw, ki3, k3, vt3, bt)


def _ffn_kernel(x_ref, oret_ref, oatt_ref, wout_ref, wup_ref, wdown_ref,
                gpost_ref, gpre_ref, gffn_ref, y_ref, *, ff_chunk):
    mix = jnp.dot(oret_ref[...].astype(BF16), wout_ref[:RET_W, :], preferred_element_type=F32)
    mix = mix + jnp.dot(oatt_ref[...].astype(BF16), wout_ref[RET_W:, :], preferred_element_type=F32)
    h = x_ref[...] + _rms(mix, gpost_ref[...])
    a = _rms(h, gpre_ref[...]).astype(BF16)
    f = jnp.zeros(h.shape, F32)
    for c in range(D_FF // ff_chunk):
        sl = slice(c * ff_chunk, (c + 1) * ff_chunk)
        u = jnp.dot(a, wup_ref[:, sl], preferred_element_type=F32)
        u = jnp.square(jnp.maximum(u, 0.0)).astype(BF16)
        f = f + jnp.dot(u, wdown_ref[sl, :], preferred_element_type=F32)
    y_ref[...] = h + _rms(f, gffn_ref[...])


def _out_ffn(x2, o_ret, o_att, w_out, w_up, w_down, g_post, g_pre, g_ffn, tm):
    rows = x2.shape[0]
    row_spec = lambda w: pl.BlockSpec((tm, w), lambda i: (i, 0))
    return pl.pallas_call(
        functools.partial(_ffn_kernel, ff_chunk=COL_CHUNK),
        grid=(rows // tm,),
        in_specs=[row_spec(D_MODEL), row_spec(RET_W), row_spec(ATT_W),
                  _const_spec(w_out.shape), _const_spec(w_up.shape), _const_spec(w_down.shape),
                  _const_spec((1, D_MODEL)), _const_spec((1, D_MODEL)), _const_spec((1, D_MODEL))],
        out_specs=row_spec(D_MODEL),
        out_shape=jax.ShapeDtypeStruct((rows, D_MODEL), F32),
        compiler_params=pltpu.CompilerParams(dimension_semantics=("parallel",),
                                             vmem_limit_bytes=VMEM_LIMIT),
        name="out_ffn",
    )(x2, o_ret, o_att, w_out, w_up, w_down, g_post, g_pre, g_ffn)


def _t5_bucket(rel):
    half = N_BUCKETS // 2
    max_exact = half // 2
    ret = jnp.where(rel > 0, half, 0)
    n = jnp.abs(rel)
    nf = jnp.maximum(n, 1).astype(F32)
    large = max_exact + (jnp.log(nf / max_exact) / math.log(MAX_DISTANCE / max_exact)
                         * (half - max_exact)).astype(jnp.int32)
    large = jnp.minimum(large, half - 1)
    return ret + jnp.where(n < max_exact, n, large)


def _bias_tiles(rel_bias, tq):
    period = 2 * TK + tq
    rel = jnp.arange(period, dtype=jnp.int32) - (TK + tq - 1)
    table = rel_bias.astype(F32)
    far = table[_t5_bucket(jnp.int32(-MAX_DISTANCE))]
    u = (table[_t5_bucket(jnp.clip(rel, -MAX_DISTANCE, MAX_DISTANCE))] - far).T
    n = jnp.tile(u, (1, tq))[:, :tq * (period - 1)].reshape(ATT_HEADS, tq, period - 1)
    m = n[:, :, tq - 1:tq - 1 + 2 * TK]
    return jnp.transpose(m.reshape(ATT_HEADS, tq, 2, TK), (0, 2, 3, 1))


def _rope_tables(pos):
    half = RET_DK // 2
    inv = ROPE_BASE ** (-jnp.arange(half, dtype=F32) / half)
    ang = pos.astype(F32)[:, None] * inv[None, :]
    cos, sin = jnp.cos(ang), jnp.sin(ang)
    cos_h = jnp.concatenate([cos, cos], axis=1)
    sin_h = jnp.concatenate([-sin, sin], axis=1)
    reps = LANE // RET_DK
    return jnp.tile(cos_h, (1, reps)), jnp.tile(sin_h, (1, reps))


def _permute_w_in(w_in):
    offs = np.cumsum([0, RET_W, RET_W, RET_W, RET_W, ATT_W, KV_W, KV_W, IDX_W, IDX_DIM, IDX_HEADS])
    seg = lambda i: w_in[:, offs[i]:offs[i + 1]]
    pad = jnp.zeros((D_MODEL, LANE - IDX_DIM - IDX_HEADS), w_in.dtype)
    return jnp.concatenate([seg(0), seg(1), seg(2), seg(3), seg(4), seg(7),
                            seg(5), seg(6), seg(8), seg(9), pad], axis=1).astype(BF16)


def _layer(x, s0, past, weights, rel_bias):
    w_perm, w_out, w_up, w_down, g_pre_mix, g_post_mix, g_pre_ffn, g_post_ffn = weights
    batch, t_len, _ = x.shape
    rows = batch * t_len
    x2 = x.reshape(rows, D_MODEL)
    tm = min(ROW_TILE, rows)
    p_len = 0 if past is None else past[0].shape[1]
    pos = p_len + jnp.arange(t_len, dtype=jnp.int32)
    cos_t, sin_t = _rope_tables(pos)
    main, kc, vc, ic, iw, *key_tiles = _inproj(x2, g_pre_mix, w_perm, cos_t, sin_t, tm,
                                               key_tiles=past is None)

    chunk = min(CHUNK, t_len)
    sb = min(RET_BLOCK, t_len)
    o_ret, s_new = _retention(main, s0, batch, t_len, sb, chunk)

    l_all = p_len + t_len
    n_sel = min(TOPK_MAX, l_all // 4)
    if past is None:
        tq = TK
        n_qblk = t_len // tq
        nkt_total = t_len // TK
        bt = _bias_tiles(rel_bias, tq)
        k3, ki3, vt3 = key_tiles
        o_att = _attention(main, main, iw, ki3, k3, vt3, bt, batch, n_qblk, tq, nkt_total, None,
                           TK, tq, n_sel, aq_col=4, iq_col=5)
    else:
        tq = LANE
        pk, pv, pi = past
        assert p_len % TK == 0 and t_len <= min(tq, TK), (p_len, t_len)
        nkt_total = p_len // TK + 1
        padk = nkt_total * TK - l_all
        cat = lambda old, new: jnp.concatenate(
            [old.astype(BF16), new.reshape(batch, t_len, -1).astype(BF16),
             jnp.zeros((batch, padk, new.shape[-1]), BF16)], axis=1)
        k_all = cat(pk.reshape(batch, p_len, KV_W), kc)
        v_all = cat(pv.reshape(batch, p_len, KV_W), vc)
        i_all = cat(pi, ic)
        v_t = jnp.transpose(v_all.reshape(batch, nkt_total, TK, ATT_KV_HEADS, ATT_HD), (0, 1, 3, 4, 2))
        ones = jnp.zeros((batch, nkt_total, ATT_KV_HEADS, VT_ROWS - ATT_HD, TK), BF16).at[:, :, :, 0, :].set(1.0)
        vt_s = jnp.concatenate([v_t, ones], axis=3).reshape(batch * nkt_total, VT_ALL, TK)
        k_s = k_all.reshape(batch * nkt_total, TK, KV_W)
        i_s = i_all.reshape(batch * nkt_total, TK, IDX_DIM)
        padq = lambda a: jnp.pad(a.reshape(batch, t_len, -1), ((0, 0), (0, tq - t_len), (0, 0))
                                 ).reshape(batch * tq, -1)
        aq_s = padq(main[:, 4 * RET_W:4 * RET_W + ATT_W])
        iq_s = padq(main[:, 4 * RET_W + ATT_W:])
        iw_s = padq(iw)
        bt = _bias_tiles(rel_bias, tq)
        o_pad = _attention(aq_s, iq_s, iw_s, i_s, k_s, vt_s, bt, batch, 1, tq, nkt_total, nkt_total,
                           l_all - p_len, t_len, n_sel, aq_col=0, iq_col=0)
        o_att = o_pad.reshape(batch, tq, ATT_W)[:, :t_len].reshape(rows, ATT_W)

    y = _out_ffn(x2, o_ret, o_att, w_out, w_up, w_down, g_post_mix, g_pre_ffn, g_post_ffn, tm)
    return (y.reshape(batch, t_len, D_MODEL), s_new,
            kc.reshape(batch, t_len, ATT_KV_HEADS, ATT_HD),
            vc.reshape(batch, t_len, ATT_KV_HEADS, ATT_HD),
            ic.reshape(batch, t_len, IDX_DIM))


def kernel(x_prompt, x_sample, state_ret, cache_k, cache_v, cache_kidx, w_in, w_out, w_up, w_down,
           g_pre_mix, g_post_mix, g_pre_ffn, g_post_ffn, rel_bias):
    depth = w_in.shape[0]
    bp = x_prompt.shape[0]
    zero_state = jnp.zeros((bp, RET_HEADS, RET_DK, RET_DV), F32)
    yp, ys = x_prompt, x_sample
    outs_p, outs_s = [], []
    for l in range(depth):
        row = lambda g: g[l].reshape(1, D_MODEL).astype(F32)
        weights = (_permute_w_in(w_in[l]), w_out[l].astype(BF16), w_up[l].astype(BF16),
                   w_down[l].astype(BF16), row(g_pre_mix), row(g_post_mix), row(g_pre_ffn),
                   row(g_post_ffn))
        yp, *rest_p = _layer(yp, zero_state, None, weights, rel_bias)
        ys, *rest_s = _layer(ys, state_ret[l], (cache_k[l], cache_v[l], cache_kidx[l]), weights, rel_bias)
        outs_p.append(rest_p)
        outs_s.append(rest_s)
    stack = lambda outs, i: jnp.stack([o[i] for o in outs])
    return (yp, ys,
            stack(outs_p, 0), stack(outs_p, 1), stack(outs_p, 2), stack(outs_p, 3),
            stack(outs_s, 0), stack(outs_s, 1), stack(outs_s, 2), stack(outs_s, 3))
```

```python
import functools
import math

import jax
import jax.numpy as jnp
import numpy as np
from jax import lax
from jax.experimental import pallas as pl
from jax.experimental.pallas import tpu as pltpu

D_MODEL = 1024
CHUNK = 64
RET_HEADS = 8
RET_DK = 64
RET_DV = 64
ATT_HEADS = 8
ATT_KV_HEADS = 2
ATT_HD = 64
IDX_HEADS = 8
IDX_DIM = 64
TOPK_MAX = 256
N_BUCKETS = 32
MAX_DISTANCE = 128
D_FF = 4 * D_MODEL
ROPE_BASE = 10000.0
EPS = 1e-6

RET_W = RET_HEADS * RET_DV
ATT_W = ATT_HEADS * ATT_HD
KV_W = ATT_KV_HEADS * ATT_HD
IDX_W = IDX_HEADS * IDX_DIM
HEADS_PER_KV = ATT_HEADS // ATT_KV_HEADS

MAIN_W = 4 * RET_W + ATT_W + IDX_W
TAIL_W = 3 * 128
IW_OFF = IDX_DIM

LANE = 128
VT_ROWS = 80
VT_ALL = ATT_KV_HEADS * VT_ROWS
TK = 256
ROW_TILE = 512
COL_CHUNK = 512
RET_BLOCK = 512
NEG = -(2.0 ** 100)
BF16_ROWS = 16
CNT_ROWS = 32
NARROW = 2.0 ** -20
MAX_SEARCH = 96
F32_TINY = float(np.finfo(np.float32).tiny)
N_COARSE = 8
SMALL_BRACKET = 2.0
N_BLIND = 4
BLIND_SLACK = 6
BF16_STEP = 2.0 ** -7
VMEM_LIMIT = 56 * 1024 * 1024

F32 = jnp.float32
BF16 = jnp.bfloat16
NT_DIMS = (((1,), (1,)), ((), ()))
TN_DIMS = (((0,), (0,)), ((), ()))


def _const_spec(shape):
    nd = len(shape)
    return pl.BlockSpec(shape, lambda *_: (0,) * nd, pipeline_mode=pl.Buffered(1))


def _rms(x, gain):
    return x * lax.rsqrt(jnp.mean(x * x, axis=-1, keepdims=True) + EPS) * gain


def _inproj_kernel(x_ref, g_ref, w_ref, cos_ref, sin_ref, main_ref, kc_ref, vc_ref, ic_ref, iw_ref,
                   *tile_refs, tm):
    a = _rms(x_ref[...], g_ref[...]).astype(BF16)
    reps = RET_W // LANE
    cos = jnp.concatenate([cos_ref[...]] * reps, axis=1)
    sin = jnp.concatenate([sin_ref[...]] * reps, axis=1)
    lane = lax.broadcasted_iota(jnp.int32, cos.shape, 1)
    first_half = (lane & (RET_DK - 1)) < RET_DK // 2

    def rot(x):
        partner = jnp.where(first_half, pltpu.roll(x, RET_W - RET_DK // 2, 1),
                            pltpu.roll(x, RET_DK // 2, 1))
        return x * cos + partner * sin

    assert COL_CHUNK == RET_W
    for c in range(MAIN_W // COL_CHUNK):
        cols = slice(c * COL_CHUNK, (c + 1) * COL_CHUNK)
        y = jnp.dot(a, w_ref[:, cols], preferred_element_type=F32)
        if c == 0:
            y = rot(y)
        elif c == 1:
            y = rot(y) * (RET_DK ** -0.5)
        main_ref[:, cols] = y
    tail = jnp.dot(a, w_ref[:, MAIN_W:MAIN_W + TAIL_W], preferred_element_type=F32)
    ak = tail[:, 0:KV_W]
    av = tail[:, KV_W:2 * KV_W]
    last = tail[:, 2 * KV_W:3 * KV_W]
    kc_ref[...] = ak
    vc_ref[...] = av
    ic_ref[...] = last[:, :IDX_DIM]
    iw_ref[...] = last
    if not tile_refs:
        return
    k3_ref, ki3_ref, vt3_ref = tile_refs
    avt = av.T
    row = lax.broadcasted_iota(jnp.int32, (VT_ROWS - ATT_HD, tm), 0)
    ones_rows = jnp.where(row == 0, 1.0, 0.0).astype(F32)
    vt = jnp.concatenate([avt[:ATT_HD], ones_rows, avt[ATT_HD:], ones_rows], axis=0).astype(BF16)
    for j in range(tm // TK):
        k3_ref[j] = ak[j * TK:(j + 1) * TK].astype(BF16)
        ki3_ref[j] = last[j * TK:(j + 1) * TK, :IDX_DIM].astype(BF16)
        vt3_ref[j] = vt[:, j * TK:(j + 1) * TK]


def _inproj(x2, gain, w_perm, cos_t, sin_t, tm, key_tiles):
    rows = x2.shape[0]
    grid = (rows // tm,)
    t_len = cos_t.shape[0]
    if t_len < tm:
        assert tm % t_len == 0
        cos_t, sin_t = jnp.tile(cos_t, (tm // t_len, 1)), jnp.tile(sin_t, (tm // t_len, 1))
        t_len = tm
    assert t_len % tm == 0
    tab_spec = pl.BlockSpec((tm, LANE), lambda i: (i % (t_len // tm), 0))
    row_spec = lambda w: pl.BlockSpec((tm, w), lambda i: (i, 0))
    out_shape = (
        jax.ShapeDtypeStruct((rows, MAIN_W), F32),
        jax.ShapeDtypeStruct((rows, KV_W), F32),
        jax.ShapeDtypeStruct((rows, KV_W), F32),
        jax.ShapeDtypeStruct((rows, IDX_DIM), F32),
        jax.ShapeDtypeStruct((rows, LANE), F32),
    )
    out_specs = (row_spec(MAIN_W), row_spec(KV_W), row_spec(KV_W), row_spec(IDX_DIM), row_spec(LANE))
    if key_tiles:
        t3 = lambda a, b: pl.BlockSpec((tm // TK, a, b), lambda i: (i, 0, 0))
        out_shape += (jax.ShapeDtypeStruct((rows // TK, TK, KV_W), BF16),
                      jax.ShapeDtypeStruct((rows // TK, TK, IDX_DIM), BF16),
                      jax.ShapeDtypeStruct((rows // TK, VT_ALL, TK), BF16))
        out_specs += (t3(TK, KV_W), t3(TK, IDX_DIM), t3(VT_ALL, TK))
    return pl.pallas_call(
        functools.partial(_inproj_kernel, tm=tm),
        grid=grid,
        in_specs=[row_spec(D_MODEL), _const_spec((1, D_MODEL)), _const_spec(w_perm.shape),
                  tab_spec, tab_spec],
        out_specs=out_specs,
        out_shape=out_shape,
        compiler_params=pltpu.CompilerParams(dimension_semantics=("parallel",),
                                             vmem_limit_bytes=VMEM_LIMIT),
        name="inproj",
    )(x2, gain, w_perm, cos_t, sin_t)


def _ret_gammas():
    return [1.0 - 2.0 ** (-5.0 - h) for h in range(RET_HEADS)]


def _ret_tables(sb, chunk):
    lg = np.log(np.array(_ret_gammas(), np.float64))
    t = np.arange(sb)
    ci = t // chunk
    diff = t[:, None] - t[None, :]
    same = ci[:, None] == ci[None, :]
    below = ci[None, :] < ci[:, None]
    expo = np.where(same, np.abs(diff), np.where(below, diff, 0)).astype(np.float64)
    dmat = np.exp(lg[:, None, None] * expo[None]) * (same | below)[None]
    qd = np.exp(lg[None, :] * (t + 1.0)[:, None])
    kd = np.exp(lg[None, :] * (sb - 1.0 - t)[:, None])
    qd = np.repeat(qd, RET_DK, axis=1)
    kd = np.repeat(kd, RET_DK, axis=1)
    return (jnp.asarray(dmat, F32), jnp.asarray(qd, F32), jnp.asarray(kd, F32),
            [float(math.exp(v * sb)) for v in lg])


def _ret_kernel(q_ref, k_ref, v_ref, g_ref, d_ref, qd_ref, kd_ref, s0_ref,
                o_ref, sfin_ref, s_scr, *, g_block):
    n = pl.program_id(1)

    @pl.when(n == 0)
    def _():
        s_scr[...] = s0_ref[0]

    q = q_ref[...]
    k = k_ref[...]
    v = v_ref[...].astype(BF16)
    gate = g_ref[...]
    qb = q.astype(BF16)
    kb = k.astype(BF16)
    qx = (q * qd_ref[...]).astype(BF16)
    kx = (k * kd_ref[...]).astype(BF16)
    outs = []
    for h in range(RET_HEADS):
        sl = slice(h * RET_DK, (h + 1) * RET_DK)
        s = lax.dot_general(qb[:, sl], kb[:, sl], NT_DIMS, preferred_element_type=F32)
        p = (s * d_ref[h]).astype(BF16)
        st = s_scr[h]
        o = jnp.dot(p, v[:, sl], preferred_element_type=F32)
        o = o + jnp.dot(qx[:, sl], st.astype(BF16), preferred_element_type=F32)
        s_scr[h] = g_block[h] * st + lax.dot_general(kx[:, sl], v[:, sl], TN_DIMS,
                                                     preferred_element_type=F32)
        o = o * lax.rsqrt(jnp.mean(o * o, axis=-1, keepdims=True) + EPS)
        outs.append(o)
    o_all = jnp.concatenate(outs, axis=1)
    o_ref[...] = gate * (1.0 / (1.0 + jnp.exp(-gate))) * o_all

    @pl.when(n == pl.num_programs(1) - 1)
    def _():
        sfin_ref[0] = s_scr[...]


def _retention(main, s0, batch, t_len, sb, chunk):
    nsb = t_len // sb
    dmat, qd, kd, g_block = _ret_tables(sb, chunk)
    col = lambda c: pl.BlockSpec((sb, RET_W), lambda b, n, c=c: (b * nsb + n, c))
    st_spec = pl.BlockSpec((1, RET_HEADS, RET_DK, RET_DV), lambda b, n: (b, 0, 0, 0))
    return pl.pallas_call(
        functools.partial(_ret_kernel, g_block=g_block),
        grid=(batch, nsb),
        in_specs=[col(0), col(1), col(2), col(3),
                  _const_spec(dmat.shape), _const_spec(qd.shape), _const_spec(kd.shape), st_spec],
        out_specs=(pl.BlockSpec((sb, RET_W), lambda b, n: (b * nsb + n, 0)), st_spec),
        out_shape=(jax.ShapeDtypeStruct((batch * t_len, RET_W), F32),
                   jax.ShapeDtypeStruct((batch, RET_HEADS, RET_DK, RET_DV), F32)),
        scratch_shapes=[pltpu.VMEM((RET_HEADS, RET_DK, RET_DV), F32)],
        compiler_params=pltpu.CompilerParams(dimension_semantics=("parallel", "arbitrary"),
                                             vmem_limit_bytes=VMEM_LIMIT),
        name="retention",
    )(main, main, main, main, dmat, qd, kd, s0)


def _attn_kernel(aq_ref, iq_ref, iw_ref, ki_ref, k_ref, vt_ref, bt_ref, o_ref,
                 s_scr, sb_scr, acc_scr, m_scr, p_scr, *, tq, nkt_static, last_valid, n_valid_q, n_sel,
                 max_search, n_blind):
    qblk = pl.program_id(1)
    nkt = qblk + 1 if nkt_static is None else nkt_static
    lane_f = lax.broadcasted_iota(jnp.int32, (1, tq), 1).astype(F32)
    lane_ok = lane_f < float(n_valid_q)
    klim = jnp.minimum((jnp.floor(lane_f * (1.0 / CHUNK)) + 1.0) * CHUNK, float(last_valid))
    krow = lax.broadcasted_iota(jnp.int32, (TK, tq), 0).astype(F32)
    adm_last = krow < klim
    nkt_f = nkt.astype(F32) if nkt_static is None else float(nkt)
    n_adm = (nkt_f - 1.0) * TK + klim
    k_target = jnp.minimum(float(n_sel), n_adm)

    iq = iq_ref[...]
    w_t = iw_ref[...].T[IW_OFF:IW_OFF + IDX_HEADS, :] * (IDX_HEADS ** -0.5 * IDX_DIM ** -0.5)
    iq_all = jnp.concatenate(
        [iq[:, h * IDX_DIM:(h + 1) * IDX_DIM].astype(BF16) for h in range(IDX_HEADS)], axis=0)

    def score_tile(kt, n_tiles=1):
        ki_t = ki_ref[kt] if n_tiles == 1 else jnp.concatenate(
            [ki_ref[kt + j] for j in range(n_tiles)], axis=0)
        s_all = lax.dot_general(ki_t, iq_all, NT_DIMS, preferred_element_type=F32)
        acc = jnp.maximum(s_all[:, :tq], 0.0) * w_t[0:1, :]
        for h in range(1, IDX_HEADS):
            acc = acc + jnp.maximum(s_all[:, h * tq:(h + 1) * tq], 0.0) * w_t[h:h + 1, :]
        return acc

    def fold_rows(x, op=jnp.add):
        parts = [x[i * CNT_ROWS:(i + 1) * CNT_ROWS] for i in range(x.shape[0] // CNT_ROWS)]
        while len(parts) > 1:
            parts = [op(a, b) for a, b in zip(parts[0::2], parts[1::2])]
        return parts[0]

    def p1_step(kt, n_tiles, carry):
        rmax, rmin = carry
        sc = score_tile(kt, n_tiles)
        for j in range(n_tiles):
            s_scr[kt + j] = sc[j * TK:(j + 1) * TK]
            sb_scr[kt + j] = sc[j * TK:(j + 1) * TK].astype(BF16)
        return (jnp.maximum(rmax, jnp.max(sc, axis=0, keepdims=True)),
                jnp.minimum(rmin, jnp.min(sc, axis=0, keepdims=True)))

    def run_steps(n, step, carry):
        n4 = n // 4
        carry = lax.fori_loop(0, n4, lambda i, c: step(4 * i, 4, c), carry)
        carry = lax.fori_loop(0, (n - 4 * n4) // 2, lambda i, c: step(4 * n4, 2, c), carry)
        return lax.fori_loop(0, n % 2, lambda i, c: step(n - 1, 1, c), carry)

    init = (jnp.full((1, tq), -jnp.inf, F32), jnp.full((1, tq), jnp.inf, F32))
    rmax, rmin = run_steps(nkt - 1, p1_step, init)
    sc = jnp.where(adm_last, score_tile(nkt - 1), -jnp.inf)
    s_scr[nkt - 1] = sc
    sb_scr[nkt - 1] = sc.astype(BF16)
    rmax = jnp.maximum(rmax, jnp.max(sc, axis=0, keepdims=True))
    rmin = jnp.minimum(rmin, jnp.min(jnp.where(adm_last, sc, jnp.inf), axis=0, keepdims=True))

    def tile_loop(n, body, carry):
        carry = lax.fori_loop(0, n // 2, lambda i, c: body(2 * i + 1, body(2 * i, c)), carry)
        return lax.fori_loop(0, n % 2, lambda i, c: body(n - 1, c), carry)

    def count_where(pred):
        def body(kt, c):
            return c + fold_rows(jnp.where(pred(kt), 1.0, 0.0))
        c = tile_loop(nkt, body, jnp.zeros((CNT_ROWS, tq), F32))
        return jnp.sum(c, axis=0, keepdims=True)

    def count_ge(thr):
        return count_where(lambda kt: s_scr[kt] >= thr)

    def count_ge_rounded(thr_b):
        one, zero = jnp.ones((), BF16), jnp.zeros((), BF16)

        def body(kt, c):
            ind = jnp.where(sb_scr[kt] >= thr_b, one, zero)
            parts = [ind[i * BF16_ROWS:(i + 1) * BF16_ROWS] for i in range(TK // BF16_ROWS)]
            while len(parts) > 1:
                parts = [a + b for a, b in zip(parts[0::2], parts[1::2])]
            return c + parts[0].astype(F32)
        c = tile_loop(nkt, body, jnp.zeros((BF16_ROWS, tq), F32))
        return jnp.sum(c, axis=0, keepdims=True)

    span = jnp.maximum(jnp.maximum(rmax - rmin, jnp.abs(rmax)), 1e-30)
    hi0 = rmax + span * (2.0 ** -10)

    def max_below(bound, n_tiles):
        def body(kt, m):
            s = s_scr[kt]
            return jnp.maximum(m, fold_rows(jnp.where(s < bound, s, -jnp.inf), jnp.maximum))
        m = tile_loop(n_tiles, body, jnp.full((CNT_ROWS, tq), -jnp.inf, F32))
        return jnp.max(m, axis=0, keepdims=True)

    def plan(it, lo, hi, c_lo, c_hi, done, want_flag=True):
        width = hi - lo
        frac = (c_lo - k_target - 0.5) / (c_lo - c_hi)
        interp = lax.convert_element_type((it + 1) % 2, F32)
        mid_i = lo + width * (0.5 + interp * (frac - 0.5))
        mid_b = lo + 0.5 * width
        mid = jnp.where((mid_i > lo) & (mid_i < hi), mid_i, mid_b)
        if not want_flag:
            return mid, None
        active = done == 0.0
        narrow = jnp.logical_not((mid > lo) & (mid < hi)) | (width <= span * NARROW)
        wide = active & jnp.logical_not(narrow) & (c_lo - c_hi > SMALL_BRACKET)
        flag = jnp.max(jnp.where(wide, 5.0, jnp.where(active, 3.0, 0.0)))
        return mid, flag

    def accept(mid, c, exact, lo, hi, c_lo, c_hi, done):
        active = done == 0.0
        ge = c >= k_target
        up = active & ge
        dn = active & jnp.logical_not(ge)
        lo = jnp.where(up, mid, lo)
        c_lo = jnp.where(up, c, c_lo)
        hi = jnp.where(dn, mid, hi)
        c_hi = jnp.where(dn, c, c_hi)
        finished = (c_lo == k_target) if exact is None else (c_lo == k_target) | (exact & up)
        return lo, hi, c_lo, c_hi, jnp.where(finished, 1.0, done)

    def blind_body(it, st):
        lo, hi, c_lo, c_hi, done = st
        mid, _ = plan(it, lo, hi, c_lo, c_hi, done, want_flag=False)
        return accept(mid, count_ge(mid), None, lo, hi, c_lo, c_hi, done)

    def search_body(st):
        it, flag, lo, hi, c_lo, c_hi, done, mid = st
        exact = (jnp.zeros((1, tq), F32) + flag) < 4.0
        top = max_below(hi, jnp.where(flag < 4.0, nkt, 0))
        mid = jnp.where(exact, top, mid)
        lo, hi, c_lo, c_hi, done = accept(mid, count_ge(mid), exact, lo, hi, c_lo, c_hi, done)
        mid, flag = plan(it + 1, lo, hi, c_lo, c_hi, done)
        return it + 1, flag, lo, hi, c_lo, c_hi, done, mid

    c_zero = count_ge_rounded(jnp.zeros((1, tq), BF16))
    c_pos = count_ge_rounded(jnp.full((1, tq), F32_TINY, BF16))
    pos_side = c_pos >= k_target
    at_zero = jnp.logical_not(pos_side) & (c_zero >= k_target)
    pick = lambda p, z, n: jnp.where(pos_side, p, jnp.where(at_zero, z, n))
    lo0 = pick(F32_TINY, 0.0, rmin)
    hi0 = pick(hi0, F32_TINY, 0.0)
    c_lo0 = pick(c_pos, c_zero, n_adm)
    c_hi0 = pick(0.0, c_pos, c_zero)
    done0 = jnp.where((n_adm == k_target) | at_zero | jnp.logical_not(lane_ok), 1.0, 0.0)

    def coarse_body(it, st):
        lo, hi, c_lo, c_hi, live = st
        width = hi - lo
        frac = (c_lo - k_target - 0.5) / (c_lo - c_hi)
        interp = lax.convert_element_type((it + 1) % 2, F32)
        grid = lambda x: x.astype(BF16).astype(F32)
        mid_i = grid(lo + width * (0.5 + interp * (frac - 0.5)))
        mid_b = grid(lo + 0.5 * width)
        mid = jnp.where((mid_i > lo) & (mid_i < hi), mid_i, mid_b)
        ok = (mid > lo) & (mid < hi) & (live > 0.0)
        c = count_ge_rounded(mid.astype(BF16))
        ge = c >= k_target
        up = ok & ge
        dn = ok & jnp.logical_not(ge)
        return (jnp.where(up, mid, lo), jnp.where(dn, mid, hi), jnp.where(up, c, c_lo),
                jnp.where(dn, c, c_hi), jnp.where(ok, live, 0.0))

    lo1, hi0, c_lo0, c_hi0, _ = lax.fori_loop(
        0, N_COARSE, coarse_body, (lo0, hi0, c_lo0, c_hi0, 1.0 - done0))
    lo0 = jnp.where(lo1 != lo0, lo1 - jnp.abs(lo1) * BF16_STEP, lo0)
    c_lo0 = count_ge(lo0)
    done0 = jnp.where(c_lo0 == k_target, 1.0, done0)
    lo0, hi0, c_lo0, c_hi0, done0 = lax.fori_loop(
        0, n_blind, blind_body, (lo0, hi0, c_lo0, c_hi0, done0))
    mid0, flag0 = plan(jnp.int32(n_blind), lo0, hi0, c_lo0, c_hi0, done0)
    st0 = (jnp.int32(n_blind), flag0, lo0, hi0, c_lo0, c_hi0, done0, mid0)
    _, _, lo, hi, cnt_lo, _, _, _ = lax.while_loop(
        lambda st: (st[0] < max_search) & (st[1] > 0.0), search_body, st0)

    excess = jnp.max(jnp.where(lane_ok, cnt_lo - k_target, 0.0))

    @pl.when(excess > 0.0)
    def _():
        need = k_target - count_ge(hi)
        r = lax.broadcasted_iota(jnp.int32, (TK, TK), 0)
        c = lax.broadcasted_iota(jnp.int32, (TK, TK), 1)
        prefix = jnp.where(r >= c, 1.0, 0.0).astype(BF16)

        def drop_step(kt, n_tiles, before):
            tiles = [s_scr[kt + j] for j in range(n_tiles)]
            ties = [(s >= lo) & (s < hi) for s in tiles]
            ranks = [jnp.dot(prefix, jnp.where(t, 1.0, 0.0).astype(BF16),
                             preferred_element_type=F32) for t in ties]
            for j in range(n_tiles):
                rank = before + ranks[j]
                s_scr[kt + j] = jnp.where(ties[j] & (rank > need), -jnp.inf, tiles[j])
                before = rank[TK - 1:TK, :]
            return before

        run_steps(nkt, drop_step, jnp.zeros((1, tq), F32))

    aq = aq_ref[...] * (ATT_HD ** -0.5)
    zeros_q = jnp.zeros((tq, ATT_HD), F32)
    q_pad = []
    for h in range(ATT_HEADS):
        qh = aq[:, h * ATT_HD:(h + 1) * ATT_HD]
        parts = [zeros_q] * ATT_KV_HEADS
        parts[h // HEADS_PER_KV] = qh
        q_pad.append(jnp.concatenate(parts, axis=1).astype(BF16))

    m_scr[...] = jnp.full(m_scr.shape, NEG, F32)
    acc_scr[...] = jnp.zeros(acc_scr.shape, F32)

    def col_max(x):
        parts = [x[i * BF16_ROWS:(i + 1) * BF16_ROWS] for i in range(x.shape[0] // BF16_ROWS)]
        while len(parts) > 1:
            parts = [jnp.maximum(a, b) for a, b in zip(parts[0::2], parts[1::2])]
        return jnp.max(parts[0].astype(F32), axis=0, keepdims=True)

    cat = lambda xs, axis: xs[0] if len(xs) == 1 else jnp.concatenate(xs, axis=axis)

    def logits(kt, n_tiles, near):
        k_t = cat([k_ref[kt + j] for j in range(n_tiles)], 0)
        s_t = cat([s_scr[kt + j] for j in range(n_tiles)], 0)
        neg_mask = jnp.where(s_t >= lo, 0.0, NEG).astype(BF16)
        out = []
        for h in range(ATT_HEADS):
            lg = lax.dot_general(k_t, q_pad[h], NT_DIMS, preferred_element_type=F32)
            if near:
                lg = lg + cat([bt_ref[h, 2 - n_tiles + j] for j in range(n_tiles)], 0)
            out.append(lg.astype(BF16) + neg_mask)
        return out

    def softmax_pv(kt, n_tiles, lgs):
        tiles = [kt + j for j in range(n_tiles)]
        rows = n_tiles * TK
        alphas = []
        for h in range(ATT_HEADS):
            g, r = divmod(h, HEADS_PER_KV)
            lg = lgs[h]
            m_old = m_scr[h:h + 1, :]
            m_new = jnp.maximum(m_old, col_max(lg))
            alphas.append(jnp.exp(m_old - m_new))
            p_scr[g, :rows, r * tq:(r + 1) * tq] = jnp.exp(lg - m_new.astype(BF16))
            m_scr[h:h + 1, :] = m_new
        for g in range(ATT_KV_HEADS):
            hs = range(g * HEADS_PER_KV, (g + 1) * HEADS_PER_KV)
            a_g = jnp.concatenate([alphas[h] for h in hs], axis=1)
            vt_g = cat([vt_ref[t, g * VT_ROWS:(g + 1) * VT_ROWS, :] for t in tiles], 1)
            acc_scr[g] = acc_scr[g] * a_g + jnp.dot(vt_g, p_scr[g, :rows, :],
                                                    preferred_element_type=F32)

    def attend(kt, n_tiles, near):
        softmax_pv(kt, n_tiles, logits(kt, n_tiles, near))

    n_far = jnp.maximum(nkt - 2, 0) if nkt_static is None else max(nkt - 2, 0)

    def far_body(i, c):
        attend(2 * i, 2, False)
        return c
    lax.fori_loop(0, n_far // 2, far_body, 0)

    def when(cond):
        return pl.when(cond) if nkt_static is None else (lambda f: f() if cond else None)

    @when(n_far % 2 == 1)
    def _():
        attend(n_far - 1, 1, False)

    @when(nkt >= 2)
    def _():
        attend(nkt - 2, 2, True)

    @when(nkt < 2)
    def _():
        attend(0, 1, True)

    outs = []
    for g in range(ATT_KV_HEADS):
        a = acc_scr[g]
        o_g = a[:ATT_HD] * (1.0 / a[ATT_HD:ATT_HD + 1])
        outs += [o_g[:, r * tq:(r + 1) * tq] for r in range(HEADS_PER_KV)]
    o_ref[...] = jnp.concatenate(outs, axis=0).T


def _attention(aq_src, iq_src, iw, ki3, k3, vt3, bt, batch, n_qblk, tq, nkt_total, nkt_static,
               last_valid, n_valid_q, n_sel, aq_col, iq_col):
    kern = functools.partial(_attn_kernel, tq=tq, nkt_static=nkt_static, last_valid=last_valid,
                             n_valid_q=n_valid_q, n_sel=n_sel, max_search=MAX_SEARCH,
                             n_blind=max(N_BLIND, int(math.log2(nkt_total * TK)) - BLIND_SLACK))
    qspec = lambda c: pl.BlockSpec((tq, ATT_W), lambda b, q, c=c: (b * n_qblk + q, c))
    kspec = lambda a, c: pl.BlockSpec((nkt_total, a, c), lambda b, q: (b, 0, 0))
    return pl.pallas_call(
        kern,
        grid=(batch, n_qblk),
        in_specs=[qspec(aq_col), qspec(iq_col),
                  pl.BlockSpec((tq, LANE), lambda b, q: (b * n_qblk + q, 0)),
                  kspec(TK, IDX_DIM), kspec(TK, KV_W), kspec(VT_ALL, TK),
                  _const_spec(bt.shape)],
        out_specs=pl.BlockSpec((tq, ATT_W), lambda b, q: (b * n_qblk + q, 0)),
        out_shape=jax.ShapeDtypeStruct((batch * n_qblk * tq, ATT_W), F32),
        scratch_shapes=[pltpu.VMEM((nkt_total, TK, tq), F32),
                        pltpu.VMEM((nkt_total, TK, tq), BF16),
                        pltpu.VMEM((ATT_KV_HEADS, VT_ROWS, HEADS_PER_KV * tq), F32),
                        pltpu.VMEM((ATT_HEADS, tq), F32),
                        pltpu.VMEM((ATT_KV_HEADS, 2 * TK, HEADS_PER_KV * tq), BF16)],
        compiler_params=pltpu.CompilerParams(dimension_semantics=("parallel", "arbitrary"),
                                             vmem_limit_bytes=VMEM_LIMIT),
        name="attention",
    )(aq_src, iq_src, iw, ki3, k3, vt3, bt)


def _ffn_kernel(x_ref, oret_ref, oatt_ref, wout_ref, wup_ref, wdown_ref,
                gpost_ref, gpre_ref, gffn_ref, y_ref, *, ff_chunk):
    mix = jnp.dot(oret_ref[...].astype(BF16), wout_ref[:RET_W, :], preferred_element_type=F32)
    mix = mix + jnp.dot(oatt_ref[...].astype(BF16), wout_ref[RET_W:, :], preferred_element_type=F32)
    h = x_ref[...] + _rms(mix, gpost_ref[...])
    a = _rms(h, gpre_ref[...]).astype(BF16)
    f = jnp.zeros(h.shape, F32)
    for c in range(D_FF // ff_chunk):
        sl = slice(c * ff_chunk, (c + 1) * ff_chunk)
        u = jnp.dot(a, wup_ref[:, sl], preferred_element_type=F32)
        u = jnp.square(jnp.maximum(u, 0.0)).astype(BF16)
        f = f + jnp.dot(u, wdown_ref[sl, :], preferred_element_type=F32)
    y_ref[...] = h + _rms(f, gffn_ref[...])


def _out_ffn(x2, o_ret, o_att, w_out, w_up, w_down, g_post, g_pre, g_ffn, tm):
    rows = x2.shape[0]
    row_spec = lambda w: pl.BlockSpec((tm, w), lambda i: (i, 0))
    return pl.pallas_call(
        functools.partial(_ffn_kernel, ff_chunk=COL_CHUNK),
        grid=(rows // tm,),
        in_specs=[row_spec(D_MODEL), row_spec(RET_W), row_spec(ATT_W),
                  _const_spec(w_out.shape), _const_spec(w_up.shape), _const_spec(w_down.shape),
                  _const_spec((1, D_MODEL)), _const_spec((1, D_MODEL)), _const_spec((1, D_MODEL))],
        out_specs=row_spec(D_MODEL),
        out_shape=jax.ShapeDtypeStruct((rows, D_MODEL), F32),
        compiler_params=pltpu.CompilerParams(dimension_semantics=("parallel",),
                                             vmem_limit_bytes=VMEM_LIMIT),
        name="out_ffn",
    )(x2, o_ret, o_att, w_out, w_up, w_down, g_post, g_pre, g_ffn)


def _t5_bucket(rel):
    half = N_BUCKETS // 2
    max_exact = half // 2
    ret = jnp.where(rel > 0, half, 0)
    n = jnp.abs(rel)
    nf = jnp.maximum(n, 1).astype(F32)
    large = max_exact + (jnp.log(nf / max_exact) / math.log(MAX_DISTANCE / max_exact)
                         * (half - max_exact)).astype(jnp.int32)
    large = jnp.minimum(large, half - 1)
    return ret + jnp.where(n < max_exact, n, large)


def _bias_tiles(rel_bias, tq):
    period = 2 * TK + tq
    rel = jnp.arange(period, dtype=jnp.int32) - (TK + tq - 1)
    table = rel_bias.astype(F32)
    far = table[_t5_bucket(jnp.int32(-MAX_DISTANCE))]
    u = (table[_t5_bucket(jnp.clip(rel, -MAX_DISTANCE, MAX_DISTANCE))] - far).T
    n = jnp.tile(u, (1, tq))[:, :tq * (period - 1)].reshape(ATT_HEADS, tq, period - 1)
    m = n[:, :, tq - 1:tq - 1 + 2 * TK]
    return jnp.transpose(m.reshape(ATT_HEADS, tq, 2, TK), (0, 2, 3, 1))


def _rope_tables(pos):
    half = RET_DK // 2
    inv = ROPE_BASE ** (-jnp.arange(half, dtype=F32) / half)
    ang = pos.astype(F32)[:, None] * inv[None, :]
    cos, sin = jnp.cos(ang), jnp.sin(ang)
    cos_h = jnp.concatenate([cos, cos], axis=1)
    sin_h = jnp.concatenate([-sin, sin], axis=1)
    reps = LANE // RET_DK
    return jnp.tile(cos_h, (1, reps)), jnp.tile(sin_h, (1, reps))


def _permute_w_in(w_in):
    offs = np.cumsum([0, RET_W, RET_W, RET_W, RET_W, ATT_W, KV_W, KV_W, IDX_W, IDX_DIM, IDX_HEADS])
    seg = lambda i: w_in[:, offs[i]:offs[i + 1]]
    pad = jnp.zeros((D_MODEL, LANE - IDX_DIM - IDX_HEADS), w_in.dtype)
    return jnp.concatenate([seg(0), seg(1), seg(2), seg(3), seg(4), seg(7),
                            seg(5), seg(6), seg(8), seg(9), pad], axis=1).astype(BF16)


def _layer(x, s0, past, weights, rel_bias):
    w_perm, w_out, w_up, w_down, g_pre_mix, g_post_mix, g_pre_ffn, g_post_ffn = weights
    batch, t_len, _ = x.shape
    rows = batch * t_len
    x2 = x.reshape(rows, D_MODEL)
    tm = min(ROW_TILE, rows)
    p_len = 0 if past is None else past[0].shape[1]
    pos = p_len + jnp.arange(t_len, dtype=jnp.int32)
    cos_t, sin_t = _rope_tables(pos)
    main, kc, vc, ic, iw, *key_tiles = _inproj(x2, g_pre_mix, w_perm, cos_t, sin_t, tm,
                                               key_tiles=past is None)

    chunk = min(CHUNK, t_len)
    sb = min(RET_BLOCK, t_len)
    o_ret, s_new = _retention(main, s0, batch, t_len, sb, chunk)

    l_all = p_len + t_len
    n_sel = min(TOPK_MAX, l_all // 4)
    if past is None:
        tq = TK
        n_qblk = t_len // tq
        nkt_total = t_len // TK
        bt = _bias_tiles(rel_bias, tq)
        k3, ki3, vt3 = key_tiles
        o_att = _attention(main, main, iw, ki3, k3, vt3, bt, batch, n_qblk, tq, nkt_total, None,
                           TK, tq, n_sel, aq_col=4, iq_col=5)
    else:
        tq = LANE
        pk, pv, pi = past
        assert p_len % TK == 0 and t_len <= min(tq, TK), (p_len, t_len)
        nkt_total = p_len // TK + 1
        padk = nkt_total * TK - l_all
        cat = lambda old, new: jnp.concatenate(
            [old.astype(BF16), new.reshape(batch, t_len, -1).astype(BF16),
             jnp.zeros((batch, padk, new.shape[-1]), BF16)], axis=1)
        k_all = cat(pk.reshape(batch, p_len, KV_W), kc)
        v_all = cat(pv.reshape(batch, p_len, KV_W), vc)
        i_all = cat(pi, ic)
        v_t = jnp.transpose(v_all.reshape(batch, nkt_total, TK, ATT_KV_HEADS, ATT_HD), (0, 1, 3, 4, 2))
        ones = jnp.zeros((batch, nkt_total, ATT_KV_HEADS, VT_ROWS - ATT_HD, TK), BF16).at[:, :, :, 0, :].set(1.0)
        vt_s = jnp.concatenate([v_t, ones], axis=3).reshape(batch * nkt_total, VT_ALL, TK)
        k_s = k_all.reshape(batch * nkt_total, TK, KV_W)
        i_s = i_all.reshape(batch * nkt_total, TK, IDX_DIM)
        padq = lambda a: jnp.pad(a.reshape(batch, t_len, -1), ((0, 0), (0, tq - t_len), (0, 0))
                                 ).reshape(batch * tq, -1)
        aq_s = padq(main[:, 4 * RET_W:4 * RET_W + ATT_W])
        iq_s = padq(main[:, 4 * RET_W + ATT_W:])
        iw_s = padq(iw)
        bt = _bias_tiles(rel_bias, tq)
        o_pad = _attention(aq_s, iq_s, iw_s, i_s, k_s, vt_s, bt, batch, 1, tq, nkt_total, nkt_total,
                           l_all - p_len, t_len, n_sel, aq_col=0, iq_col=0)
        o_att = o_pad.reshape(batch, tq, ATT_W)[:, :t_len].reshape(rows, ATT_W)

    y = _out_ffn(x2, o_ret, o_att, w_out, w_up, w_down, g_post_mix, g_pre_ffn, g_post_ffn, tm)
    return (y.reshape(batch, t_len, D_MODEL), s_new,
            kc.reshape(batch, t_len, ATT_KV_HEADS, ATT_HD),
            vc.reshape(batch, t_len, ATT_KV_HEADS, ATT_HD),
            ic.reshape(batch, t_len, IDX_DIM))


def kernel(x_prompt, x_sample, state_ret, cache_k, cache_v, cache_kidx, w_in, w_out, w_up, w_down,
           g_pre_mix, g_post_mix, g_pre_ffn, g_post_ffn, rel_bias):
    depth = w_in.shape[0]
    bp = x_prompt.shape[0]
    zero_state = jnp.zeros((bp, RET_HEADS, RET_DK, RET_DV), F32)
    yp, ys = x_prompt, x_sample
    outs_p, outs_s = [], []
    for l in range(depth):
        row = lambda g: g[l].reshape(1, D_MODEL).astype(F32)
        weights = (_permute_w_in(w_in[l]), w_out[l].astype(BF16), w_up[l].astype(BF16),
                   w_down[l].astype(BF16), row(g_pre_mix), row(g_post_mix), row(g_pre_ffn),
                   row(g_post_ffn))
        yp, *rest_p = _layer(yp, zero_state, None, weights, rel_bias)
        ys, *rest_s = _layer(ys, state_ret[l], (cache_k[l], cache_v[l], cache_kidx[l]), weights, rel_bias)
        outs_p.append(rest_p)
        outs_s.append(rest_s)
    stack = lambda outs, i: jnp.stack([o[i] for o in outs])
    return (yp, ys,
            stack(outs_p, 0), stack(outs_p, 1), stack(outs_p, 2), stack(outs_p, 3),
            stack(outs_s, 0), stack(outs_s, 1), stack(outs_s, 2), stack(outs_s, 3))
```

```python
import functools
import math

import jax
import jax.numpy as jnp
import numpy as np
from jax import lax
from jax.experimental import pallas as pl
from jax.experimental.pallas import tpu as pltpu

D_MODEL = 1024
CHUNK = 64
RET_HEADS = 8
RET_DK = 64
RET_DV = 64
ATT_HEADS = 8
ATT_KV_HEADS = 2
ATT_HD = 64
IDX_HEADS = 8
IDX_DIM = 64
TOPK_MAX = 256
N_BUCKETS = 32
MAX_DISTANCE = 128
D_FF = 4 * D_MODEL
ROPE_BASE = 10000.0
EPS = 1e-6

RET_W = RET_HEADS * RET_DV
ATT_W = ATT_HEADS * ATT_HD
KV_W = ATT_KV_HEADS * ATT_HD
IDX_W = IDX_HEADS * IDX_DIM
HEADS_PER_KV = ATT_HEADS // ATT_KV_HEADS

MAIN_W = 4 * RET_W + ATT_W + IDX_W
TAIL_W = 3 * 128
IW_OFF = IDX_DIM

LANE = 128
VT_ROWS = 80
VT_ALL = ATT_KV_HEADS * VT_ROWS
TK = 256
ROW_TILE = 512
COL_CHUNK = 512
RET_BLOCK = 512
NEG = -(2.0 ** 100)
BF16_ROWS = 16
CNT_ROWS = 32
NARROW = 2.0 ** -20
MAX_SEARCH = 96
F32_TINY = float(np.finfo(np.float32).tiny)
N_COARSE = 8
SMALL_BRACKET = 2.0
N_BLIND = 4
BLIND_SLACK = 6
BF16_STEP = 2.0 ** -7
VMEM_LIMIT = 56 * 1024 * 1024

F32 = jnp.float32
BF16 = jnp.bfloat16
NT_DIMS = (((1,), (1,)), ((), ()))
TN_DIMS = (((0,), (0,)), ((), ()))


def _const_spec(shape):
    nd = len(shape)
    return pl.BlockSpec(shape, lambda *_: (0,) * nd, pipeline_mode=pl.Buffered(1))


def _rms(x, gain):
    return x * lax.rsqrt(jnp.mean(x * x, axis=-1, keepdims=True) + EPS) * gain


def _inproj_kernel(x_ref, g_ref, w_ref, cos_ref, sin_ref, main_ref, kc_ref, vc_ref, ic_ref, iw_ref,
                   *tile_refs, tm):
    a = _rms(x_ref[...], g_ref[...]).astype(BF16)
    reps = RET_W // LANE
    cos = jnp.concatenate([cos_ref[...]] * reps, axis=1)
    sin = jnp.concatenate([sin_ref[...]] * reps, axis=1)
    lane = lax.broadcasted_iota(jnp.int32, cos.shape, 1)
    first_half = (lane & (RET_DK - 1)) < RET_DK // 2

    def rot(x):
        partner = jnp.where(first_half, pltpu.roll(x, RET_W - RET_DK // 2, 1),
                            pltpu.roll(x, RET_DK // 2, 1))
        return x * cos + partner * sin

    tail = jnp.dot(a, w_ref[:, MAIN_W:MAIN_W + TAIL_W], preferred_element_type=F32)
    ak = tail[:, 0:KV_W]
    av = tail[:, KV_W:2 * KV_W]
    last = tail[:, 2 * KV_W:3 * KV_W]
    for g in range(ATT_KV_HEADS):
        kc_ref[:, g, :] = ak[:, g * ATT_HD:(g + 1) * ATT_HD]
        vc_ref[:, g, :] = av[:, g * ATT_HD:(g + 1) * ATT_HD]
    ic_ref[...] = last[:, :IDX_DIM]
    iw_ref[...] = last
    if tile_refs:
        k3_ref, ki3_ref, vt3_ref = tile_refs
        avt = av.T
        row = lax.broadcasted_iota(jnp.int32, (VT_ROWS - ATT_HD, tm), 0)
        ones_rows = jnp.where(row == 0, 1.0, 0.0).astype(F32)
        vt = jnp.concatenate([avt[:ATT_HD], ones_rows, avt[ATT_HD:], ones_rows], axis=0).astype(BF16)
        for j in range(tm // TK):
            k3_ref[j] = ak[j * TK:(j + 1) * TK].astype(BF16)
            ki3_ref[j] = last[j * TK:(j + 1) * TK, :IDX_DIM].astype(BF16)
            vt3_ref[j] = vt[:, j * TK:(j + 1) * TK]

    assert COL_CHUNK == RET_W
    for c in range(MAIN_W // COL_CHUNK):
        cols = slice(c * COL_CHUNK, (c + 1) * COL_CHUNK)
        y = jnp.dot(a, w_ref[:, cols], preferred_element_type=F32)
        if c == 0:
            y = rot(y)
        elif c == 1:
            y = rot(y) * (RET_DK ** -0.5)
        main_ref[:, cols] = y


def _inproj(x2, gain, w_perm, cos_t, sin_t, tm, key_tiles):
    rows = x2.shape[0]
    grid = (rows // tm,)
    t_len = cos_t.shape[0]
    if t_len < tm:
        assert tm % t_len == 0
        cos_t, sin_t = jnp.tile(cos_t, (tm // t_len, 1)), jnp.tile(sin_t, (tm // t_len, 1))
        t_len = tm
    assert t_len % tm == 0
    tab_spec = pl.BlockSpec((tm, LANE), lambda i: (i % (t_len // tm), 0))
    row_spec = lambda w: pl.BlockSpec((tm, w), lambda i: (i, 0))
    out_shape = (
        jax.ShapeDtypeStruct((rows, MAIN_W), F32),
        jax.ShapeDtypeStruct((rows, ATT_KV_HEADS, ATT_HD), F32),
        jax.ShapeDtypeStruct((rows, ATT_KV_HEADS, ATT_HD), F32),
        jax.ShapeDtypeStruct((rows, IDX_DIM), F32),
        jax.ShapeDtypeStruct((rows, LANE), F32),
    )
    kv_spec = pl.BlockSpec((tm, ATT_KV_HEADS, ATT_HD), lambda i: (i, 0, 0))
    out_specs = (row_spec(MAIN_W), kv_spec, kv_spec, row_spec(IDX_DIM), row_spec(LANE))
    if key_tiles:
        t3 = lambda a, b: pl.BlockSpec((tm // TK, a, b), lambda i: (i, 0, 0))
        out_shape += (jax.ShapeDtypeStruct((rows // TK, TK, KV_W), BF16),
                      jax.ShapeDtypeStruct((rows // TK, TK, IDX_DIM), BF16),
                      jax.ShapeDtypeStruct((rows // TK, VT_ALL, TK), BF16))
        out_specs += (t3(TK, KV_W), t3(TK, IDX_DIM), t3(VT_ALL, TK))
    return pl.pallas_call(
        functools.partial(_inproj_kernel, tm=tm),
        grid=grid,
        in_specs=[row_spec(D_MODEL), _const_spec((1, D_MODEL)), _const_spec(w_perm.shape),
                  tab_spec, tab_spec],
        out_specs=out_specs,
        out_shape=out_shape,
        compiler_params=pltpu.CompilerParams(dimension_semantics=("parallel",),
                                             vmem_limit_bytes=VMEM_LIMIT),
        name="inproj",
    )(x2, gain, w_perm, cos_t, sin_t)


def _ret_gammas():
    return [1.0 - 2.0 ** (-5.0 - h) for h in range(RET_HEADS)]


def _ret_tables(sb, chunk):
    lg = np.log(np.array(_ret_gammas(), np.float64))
    t = np.arange(sb)
    ci = t // chunk
    diff = t[:, None] - t[None, :]
    same = ci[:, None] == ci[None, :]
    below = ci[None, :] < ci[:, None]
    expo = np.where(same, np.abs(diff), np.where(below, diff, 0)).astype(np.float64)
    dmat = np.exp(lg[:, None, None] * expo[None]) * (same | below)[None]
    qd = np.exp(lg[None, :] * (t + 1.0)[:, None])
    kd = np.exp(lg[None, :] * (sb - 1.0 - t)[:, None])
    qd = np.repeat(qd, RET_DK, axis=1)
    kd = np.repeat(kd, RET_DK, axis=1)
    return (jnp.asarray(dmat, F32), jnp.asarray(qd, F32), jnp.asarray(kd, F32),
            [float(math.exp(v * sb)) for v in lg])


def _ret_kernel(q_ref, k_ref, v_ref, g_ref, d_ref, qd_ref, kd_ref, s0_ref,
                o_ref, sfin_ref, s_scr, *, g_block):
    n = pl.program_id(1)

    @pl.when(n == 0)
    def _():
        s_scr[...] = s0_ref[0]

    q = q_ref[...]
    k = k_ref[...]
    v = v_ref[...].astype(BF16)
    gate = g_ref[...]
    qb = q.astype(BF16)
    kb = k.astype(BF16)
    qx = (q * qd_ref[...]).astype(BF16)
    kx = (k * kd_ref[...]).astype(BF16)
    outs = []
    for h in range(RET_HEADS):
        sl = slice(h * RET_DK, (h + 1) * RET_DK)
        s = lax.dot_general(qb[:, sl], kb[:, sl], NT_DIMS, preferred_element_type=F32)
        p = (s * d_ref[h]).astype(BF16)
        st = s_scr[h]
        o = jnp.dot(p, v[:, sl], preferred_element_type=F32)
        o = o + jnp.dot(qx[:, sl], st.astype(BF16), preferred_element_type=F32)
        s_scr[h] = g_block[h] * st + lax.dot_general(kx[:, sl], v[:, sl], TN_DIMS,
                                                     preferred_element_type=F32)
        o = o * lax.rsqrt(jnp.mean(o * o, axis=-1, keepdims=True) + EPS)
        outs.append(o)
    o_all = jnp.concatenate(outs, axis=1)
    o_ref[...] = gate * (1.0 / (1.0 + jnp.exp(-gate))) * o_all

    @pl.when(n == pl.num_programs(1) - 1)
    def _():
        sfin_ref[0] = s_scr[...]


def _retention(main, s0, batch, t_len, sb, chunk):
    nsb = t_len // sb
    dmat, qd, kd, g_block = _ret_tables(sb, chunk)
    col = lambda c: pl.BlockSpec((sb, RET_W), lambda b, n, c=c: (b * nsb + n, c))
    st_spec = pl.BlockSpec((1, RET_HEADS, RET_DK, RET_DV), lambda b, n: (b, 0, 0, 0))
    return pl.pallas_call(
        functools.partial(_ret_kernel, g_block=g_block),
        grid=(batch, nsb),
        in_specs=[col(0), col(1), col(2), col(3),
                  _const_spec(dmat.shape), _const_spec(qd.shape), _const_spec(kd.shape), st_spec],
        out_specs=(pl.BlockSpec((sb, RET_W), lambda b, n: (b * nsb + n, 0)), st_spec),
        out_shape=(jax.ShapeDtypeStruct((batch * t_len, RET_W), F32),
                   jax.ShapeDtypeStruct((batch, RET_HEADS, RET_DK, RET_DV), F32)),
        scratch_shapes=[pltpu.VMEM((RET_HEADS, RET_DK, RET_DV), F32)],
        compiler_params=pltpu.CompilerParams(dimension_semantics=("parallel", "arbitrary"),
                                             vmem_limit_bytes=VMEM_LIMIT),
        name="retention",
    )(main, main, main, main, dmat, qd, kd, s0)


def _attn_kernel(aq_ref, iq_ref, iw_ref, ki_ref, k_ref, vt_ref, bt_ref, o_ref,
                 s_scr, sb_scr, acc_scr, m_scr, p_scr, *, tq, nkt_static, last_valid, n_valid_q, n_sel,
                 max_search, n_blind):
    qblk = pl.program_id(1)
    nkt = qblk + 1 if nkt_static is None else nkt_static
    lane_f = lax.broadcasted_iota(jnp.int32, (1, tq), 1).astype(F32)
    lane_ok = lane_f < float(n_valid_q)
    klim = jnp.minimum((jnp.floor(lane_f * (1.0 / CHUNK)) + 1.0) * CHUNK, float(last_valid))
    krow = lax.broadcasted_iota(jnp.int32, (TK, tq), 0).astype(F32)
    adm_last = krow < klim
    nkt_f = nkt.astype(F32) if nkt_static is None else float(nkt)
    n_adm = (nkt_f - 1.0) * TK + klim
    k_target = jnp.minimum(float(n_sel), n_adm)

    iq = iq_ref[...]
    w_t = iw_ref[...].T[IW_OFF:IW_OFF + IDX_HEADS, :] * (IDX_HEADS ** -0.5 * IDX_DIM ** -0.5)
    iq_all = jnp.concatenate(
        [iq[:, h * IDX_DIM:(h + 1) * IDX_DIM].astype(BF16) for h in range(IDX_HEADS)], axis=0)

    def score_tile(kt, n_tiles=1):
        ki_t = ki_ref[kt] if n_tiles == 1 else jnp.concatenate(
            [ki_ref[kt + j] for j in range(n_tiles)], axis=0)
        s_all = lax.dot_general(ki_t, iq_all, NT_DIMS, preferred_element_type=F32)
        acc = jnp.maximum(s_all[:, :tq], 0.0) * w_t[0:1, :]
        for h in range(1, IDX_HEADS):
            acc = acc + jnp.maximum(s_all[:, h * tq:(h + 1) * tq], 0.0) * w_t[h:h + 1, :]
        return acc

    def fold_rows(x, op=jnp.add):
        parts = [x[i * CNT_ROWS:(i + 1) * CNT_ROWS] for i in range(x.shape[0] // CNT_ROWS)]
        while len(parts) > 1:
            parts = [op(a, b) for a, b in zip(parts[0::2], parts[1::2])]
        return parts[0]

    def p1_step(kt, n_tiles, carry):
        rmax, rmin = carry
        sc = score_tile(kt, n_tiles)
        for j in range(n_tiles):
            s_scr[kt + j] = sc[j * TK:(j + 1) * TK]
            sb_scr[kt + j] = sc[j * TK:(j + 1) * TK].astype(BF16)
        return (jnp.maximum(rmax, jnp.max(sc, axis=0, keepdims=True)),
                jnp.minimum(rmin, jnp.min(sc, axis=0, keepdims=True)))

    def run_steps(n, step, carry):
        n4 = n // 4
        carry = lax.fori_loop(0, n4, lambda i, c: step(4 * i, 4, c), carry)
        carry = lax.fori_loop(0, (n - 4 * n4) // 2, lambda i, c: step(4 * n4, 2, c), carry)
        return lax.fori_loop(0, n % 2, lambda i, c: step(n - 1, 1, c), carry)

    init = (jnp.full((1, tq), -jnp.inf, F32), jnp.full((1, tq), jnp.inf, F32))
    rmax, rmin = run_steps(nkt - 1, p1_step, init)
    sc = jnp.where(adm_last, score_tile(nkt - 1), -jnp.inf)
    s_scr[nkt - 1] = sc
    sb_scr[nkt - 1] = sc.astype(BF16)
    rmax = jnp.maximum(rmax, jnp.max(sc, axis=0, keepdims=True))
    rmin = jnp.minimum(rmin, jnp.min(jnp.where(adm_last, sc, jnp.inf), axis=0, keepdims=True))

    def tile_loop(n, body, carry):
        carry = lax.fori_loop(0, n // 2, lambda i, c: body(2 * i + 1, body(2 * i, c)), carry)
        return lax.fori_loop(0, n % 2, lambda i, c: body(n - 1, c), carry)

    def count_where(pred):
        def body(kt, c):
            return c + fold_rows(jnp.where(pred(kt), 1.0, 0.0))
        c = tile_loop(nkt, body, jnp.zeros((CNT_ROWS, tq), F32))
        return jnp.sum(c, axis=0, keepdims=True)

    def count_ge(thr):
        return count_where(lambda kt: s_scr[kt] >= thr)

    def count_ge_rounded(thr_b):
        one, zero = jnp.ones((), BF16), jnp.zeros((), BF16)

        def body(kt, c):
            ind = jnp.where(sb_scr[kt] >= thr_b, one, zero)
            parts = [ind[i * BF16_ROWS:(i + 1) * BF16_ROWS] for i in range(TK // BF16_ROWS)]
            while len(parts) > 1:
                parts = [a + b for a, b in zip(parts[0::2], parts[1::2])]
            return c + parts[0].astype(F32)
        c = tile_loop(nkt, body, jnp.zeros((BF16_ROWS, tq), F32))
        return jnp.sum(c, axis=0, keepdims=True)

    span = jnp.maximum(jnp.maximum(rmax - rmin, jnp.abs(rmax)), 1e-30)
    hi0 = rmax + span * (2.0 ** -10)

    def max_below(bound, n_tiles):
        def body(kt, m):
            s = s_scr[kt]
            return jnp.maximum(m, fold_rows(jnp.where(s < bound, s, -jnp.inf), jnp.maximum))
        m = tile_loop(n_tiles, body, jnp.full((CNT_ROWS, tq), -jnp.inf, F32))
        return jnp.max(m, axis=0, keepdims=True)

    def plan(it, lo, hi, c_lo, c_hi, done, want_flag=True):
        width = hi - lo
        frac = (c_lo - k_target - 0.5) / (c_lo - c_hi)
        interp = lax.convert_element_type((it + 1) % 2, F32)
        mid_i = lo + width * (0.5 + interp * (frac - 0.5))
        mid_b = lo + 0.5 * width
        mid = jnp.where((mid_i > lo) & (mid_i < hi), mid_i, mid_b)
        if not want_flag:
            return mid, None
        active = done == 0.0
        narrow = jnp.logical_not((mid > lo) & (mid < hi)) | (width <= span * NARROW)
        wide = active & jnp.logical_not(narrow) & (c_lo - c_hi > SMALL_BRACKET)
        flag = jnp.max(jnp.where(wide, 5.0, jnp.where(active, 3.0, 0.0)))
        return mid, flag

    def accept(mid, c, exact, lo, hi, c_lo, c_hi, done):
        active = done == 0.0
        ge = c >= k_target
        up = active & ge
        dn = active & jnp.logical_not(ge)
        lo = jnp.where(up, mid, lo)
        c_lo = jnp.where(up, c, c_lo)
        hi = jnp.where(dn, mid, hi)
        c_hi = jnp.where(dn, c, c_hi)
        finished = (c_lo == k_target) if exact is None else (c_lo == k_target) | (exact & up)
        return lo, hi, c_lo, c_hi, jnp.where(finished, 1.0, done)

    def blind_body(it, st):
        lo, hi, c_lo, c_hi, done = st
        mid, _ = plan(it, lo, hi, c_lo, c_hi, done, want_flag=False)
        return accept(mid, count_ge(mid), None, lo, hi, c_lo, c_hi, done)

    def search_body(st):
        it, flag, lo, hi, c_lo, c_hi, done, mid = st
        exact = (jnp.zeros((1, tq), F32) + flag) < 4.0
        top = max_below(hi, jnp.where(flag < 4.0, nkt, 0))
        mid = jnp.where(exact, top, mid)
        lo, hi, c_lo, c_hi, done = accept(mid, count_ge(mid), exact, lo, hi, c_lo, c_hi, done)
        mid, flag = plan(it + 1, lo, hi, c_lo, c_hi, done)
        return it + 1, flag, lo, hi, c_lo, c_hi, done, mid

    c_zero = count_ge_rounded(jnp.zeros((1, tq), BF16))
    c_pos = count_ge_rounded(jnp.full((1, tq), F32_TINY, BF16))
    pos_side = c_pos >= k_target
    at_zero = jnp.logical_not(pos_side) & (c_zero >= k_target)
    pick = lambda p, z, n: jnp.where(pos_side, p, jnp.where(at_zero, z, n))
    lo0 = pick(F32_TINY, 0.0, rmin)
    hi0 = pick(hi0, F32_TINY, 0.0)
    c_lo0 = pick(c_pos, c_zero, n_adm)
    c_hi0 = pick(0.0, c_pos, c_zero)
    done0 = jnp.where((n_adm == k_target) | at_zero | jnp.logical_not(lane_ok), 1.0, 0.0)

    def coarse_body(it, st):
        lo, hi, c_lo, c_hi, live = st
        width = hi - lo
        frac = (c_lo - k_target - 0.5) / (c_lo - c_hi)
        interp = lax.convert_element_type((it + 1) % 2, F32)
        grid = lambda x: x.astype(BF16).astype(F32)
        mid_i = grid(lo + width * (0.5 + interp * (frac - 0.5)))
        mid_b = grid(lo + 0.5 * width)
        mid = jnp.where((mid_i > lo) & (mid_i < hi), mid_i, mid_b)
        ok = (mid > lo) & (mid < hi) & (live > 0.0)
        c = count_ge_rounded(mid.astype(BF16))
        ge = c >= k_target
        up = ok & ge
        dn = ok & jnp.logical_not(ge)
        return (jnp.where(up, mid, lo), jnp.where(dn, mid, hi), jnp.where(up, c, c_lo),
                jnp.where(dn, c, c_hi), jnp.where(ok, live, 0.0))

    lo1, hi0, c_lo0, c_hi0, _ = lax.fori_loop(
        0, N_COARSE, coarse_body, (lo0, hi0, c_lo0, c_hi0, 1.0 - done0))
    lo0 = jnp.where(lo1 != lo0, lo1 - jnp.abs(lo1) * BF16_STEP, lo0)
    c_lo0 = count_ge(lo0)
    done0 = jnp.where(c_lo0 == k_target, 1.0, done0)
    lo0, hi0, c_lo0, c_hi0, done0 = lax.fori_loop(
        0, n_blind, blind_body, (lo0, hi0, c_lo0, c_hi0, done0))
    mid0, flag0 = plan(jnp.int32(n_blind), lo0, hi0, c_lo0, c_hi0, done0)
    st0 = (jnp.int32(n_blind), flag0, lo0, hi0, c_lo0, c_hi0, done0, mid0)
    _, _, lo, hi, cnt_lo, _, _, _ = lax.while_loop(
        lambda st: (st[0] < max_search) & (st[1] > 0.0), search_body, st0)

    excess = jnp.max(jnp.where(lane_ok, cnt_lo - k_target, 0.0))

    @pl.when(excess > 0.0)
    def _():
        need = k_target - count_ge(hi)
        r = lax.broadcasted_iota(jnp.int32, (TK, TK), 0)
        c = lax.broadcasted_iota(jnp.int32, (TK, TK), 1)
        prefix = jnp.where(r >= c, 1.0, 0.0).astype(BF16)

        def drop_step(kt, n_tiles, before):
            tiles = [s_scr[kt + j] for j in range(n_tiles)]
            ties = [(s >= lo) & (s < hi) for s in tiles]
            ranks = [jnp.dot(prefix, jnp.where(t, 1.0, 0.0).astype(BF16),
                             preferred_element_type=F32) for t in ties]
            for j in range(n_tiles):
                rank = before + ranks[j]
                s_scr[kt + j] = jnp.where(ties[j] & (rank > need), -jnp.inf, tiles[j])
                before = rank[TK - 1:TK, :]
            return before

        run_steps(nkt, drop_step, jnp.zeros((1, tq), F32))

    aq = aq_ref[...] * (ATT_HD ** -0.5)
    zeros_q = jnp.zeros((tq, ATT_HD), F32)
    q_pad = []
    for h in range(ATT_HEADS):
        qh = aq[:, h * ATT_HD:(h + 1) * ATT_HD]
        parts = [zeros_q] * ATT_KV_HEADS
        parts[h // HEADS_PER_KV] = qh
        q_pad.append(jnp.concatenate(parts, axis=1).astype(BF16))

    m_scr[...] = jnp.full(m_scr.shape, NEG, F32)
    acc_scr[...] = jnp.zeros(acc_scr.shape, F32)

    def col_max(x):
        parts = [x[i * BF16_ROWS:(i + 1) * BF16_ROWS] for i in range(x.shape[0] // BF16_ROWS)]
        while len(parts) > 1:
            parts = [jnp.maximum(a, b) for a, b in zip(parts[0::2], parts[1::2])]
        return jnp.max(parts[0].astype(F32), axis=0, keepdims=True)

    cat = lambda xs, axis: xs[0] if len(xs) == 1 else jnp.concatenate(xs, axis=axis)

    def logits(kt, n_tiles, near):
        k_t = cat([k_ref[kt + j] for j in range(n_tiles)], 0)
        s_t = cat([s_scr[kt + j] for j in range(n_tiles)], 0)
        neg_mask = jnp.where(s_t >= lo, 0.0, NEG).astype(BF16)
        out = []
        for h in range(ATT_HEADS):
            lg = lax.dot_general(k_t, q_pad[h], NT_DIMS, preferred_element_type=F32)
            if near:
                lg = lg + cat([bt_ref[h, 2 - n_tiles + j] for j in range(n_tiles)], 0)
            out.append(lg.astype(BF16) + neg_mask)
        return out

    def softmax_pv(kt, n_tiles, lgs):
        tiles = [kt + j for j in range(n_tiles)]
        rows = n_tiles * TK
        alphas = []
        for h in range(ATT_HEADS):
            g, r = divmod(h, HEADS_PER_KV)
            lg = lgs[h]
            m_old = m_scr[h:h + 1, :]
            m_new = jnp.maximum(m_old, col_max(lg))
            alphas.append(jnp.exp(m_old - m_new))
            p_scr[g, :rows, r * tq:(r + 1) * tq] = jnp.exp(lg - m_new.astype(BF16))
            m_scr[h:h + 1, :] = m_new
        for g in range(ATT_KV_HEADS):
            hs = range(g * HEADS_PER_KV, (g + 1) * HEADS_PER_KV)
            a_g = jnp.concatenate([alphas[h] for h in hs], axis=1)
            vt_g = cat([vt_ref[t, g * VT_ROWS:(g + 1) * VT_ROWS, :] for t in tiles], 1)
            acc_scr[g] = acc_scr[g] * a_g + jnp.dot(vt_g, p_scr[g, :rows, :],
                                                    preferred_element_type=F32)

    def attend(kt, n_tiles, near):
        softmax_pv(kt, n_tiles, logits(kt, n_tiles, near))

    n_far = jnp.maximum(nkt - 2, 0) if nkt_static is None else max(nkt - 2, 0)

    def far_body(i, c):
        attend(2 * i, 2, False)
        return c
    lax.fori_loop(0, n_far // 2, far_body, 0)

    def when(cond):
        return pl.when(cond) if nkt_static is None else (lambda f: f() if cond else None)

    @when(n_far % 2 == 1)
    def _():
        attend(n_far - 1, 1, False)

    @when(nkt >= 2)
    def _():
        attend(nkt - 2, 2, True)

    @when(nkt < 2)
    def _():
        attend(0, 1, True)

    outs = []
    for g in range(ATT_KV_HEADS):
        a = acc_scr[g]
        o_g = a[:ATT_HD] * (1.0 / a[ATT_HD:ATT_HD + 1])
        outs += [o_g[:, r * tq:(r + 1) * tq] for r in range(HEADS_PER_KV)]
    o_ref[...] = jnp.concatenate(outs, axis=0).T


def _attention(aq_src, iq_src, iw, ki3, k3, vt3, bt, batch, n_qblk, tq, nkt_total, nkt_static,
               last_valid, n_valid_q, n_sel, aq_col, iq_col):
    kern = functools.partial(_attn_kernel, tq=tq, nkt_static=nkt_static, last_valid=last_valid,
                             n_valid_q=n_valid_q, n_sel=n_sel, max_search=MAX_SEARCH,
                             n_blind=max(N_BLIND, int(math.log2(nkt_total * TK)) - BLIND_SLACK))
    qspec = lambda c: pl.BlockSpec((tq, ATT_W), lambda b, q, c=c: (b * n_qblk + q, c))
    kspec = lambda a, c: pl.BlockSpec((nkt_total, a, c), lambda b, q: (b, 0, 0))
    return pl.pallas_call(
        kern,
        grid=(batch, n_qblk),
        in_specs=[qspec(aq_col), qspec(iq_col),
                  pl.BlockSpec((tq, LANE), lambda b, q: (b * n_qblk + q, 0)),
                  kspec(TK, IDX_DIM), kspec(TK, KV_W), kspec(VT_ALL, TK),
                  _const_spec(bt.shape)],
        out_specs=pl.BlockSpec((tq, ATT_W), lambda b, q: (b * n_qblk + q, 0)),
        out_shape=jax.ShapeDtypeStruct((batch * n_qblk * tq, ATT_W), F32),
        scratch_shapes=[pltpu.VMEM((nkt_total, TK, tq), F32),
                        pltpu.VMEM((nkt_total, TK, tq), BF16),
                        pltpu.VMEM((ATT_KV_HEADS, VT_ROWS, HEADS_PER_KV * tq), F32),
                        pltpu.VMEM((ATT_HEADS, tq), F32),
                        pltpu.VMEM((ATT_KV_HEADS, 2 * TK, HEADS_PER_KV * tq), BF16)],
        compiler_params=pltpu.CompilerParams(dimension_semantics=("parallel", "arbitrary"),
                                             vmem_limit_bytes=VMEM_LIMIT),
        name="attention",
    )(aq_src, iq_src, iw, ki3, k3, vt3, bt)


def _ffn_kernel(x_ref, oret_ref, oatt_ref, wout_ref, wup_ref, wdown_ref,
                gpost_ref, gpre_ref, gffn_ref, y_ref, *, ff_chunk):
    mix = jnp.dot(oret_ref[...].astype(BF16), wout_ref[:RET_W, :], preferred_element_type=F32)
    mix = mix + jnp.dot(oatt_ref[...].astype(BF16), wout_ref[RET_W:, :], preferred_element_type=F32)
    h = x_ref[...] + _rms(mix, gpost_ref[...])
    a = _rms(h, gpre_ref[...]).astype(BF16)
    f = jnp.zeros(h.shape, F32)
    for c in range(D_FF // ff_chunk):
        sl = slice(c * ff_chunk, (c + 1) * ff_chunk)
        u = jnp.dot(a, wup_ref[:, sl], preferred_element_type=F32)
        u = jnp.square(jnp.maximum(u, 0.0)).astype(BF16)
        f = f + jnp.dot(u, wdown_ref[sl, :], preferred_element_type=F32)
    y_ref[...] = h + _rms(f, gffn_ref[...])


def _out_ffn(x2, o_ret, o_att, w_out, w_up, w_down, g_post, g_pre, g_ffn, tm):
    rows = x2.shape[0]
    row_spec = lambda w: pl.BlockSpec((tm, w), lambda i: (i, 0))
    return pl.pallas_call(
        functools.partial(_ffn_kernel, ff_chunk=COL_CHUNK),
        grid=(rows // tm,),
        in_specs=[row_spec(D_MODEL), row_spec(RET_W), row_spec(ATT_W),
                  _const_spec(w_out.shape), _const_spec(w_up.shape), _const_spec(w_down.shape),
                  _const_spec((1, D_MODEL)), _const_spec((1, D_MODEL)), _const_spec((1, D_MODEL))],
        out_specs=row_spec(D_MODEL),
        out_shape=jax.ShapeDtypeStruct((rows, D_MODEL), F32),
        compiler_params=pltpu.CompilerParams(dimension_semantics=("parallel",),
                                             vmem_limit_bytes=VMEM_LIMIT),
        name="out_ffn",
    )(x2, o_ret, o_att, w_out, w_up, w_down, g_post, g_pre, g_ffn)


def _t5_bucket(rel):
    half = N_BUCKETS // 2
    max_exact = half // 2
    ret = jnp.where(rel > 0, half, 0)
    n = jnp.abs(rel)
    nf = jnp.maximum(n, 1).astype(F32)
    large = max_exact + (jnp.log(nf / max_exact) / math.log(MAX_DISTANCE / max_exact)
                         * (half - max_exact)).astype(jnp.int32)
    large = jnp.minimum(large, half - 1)
    return ret + jnp.where(n < max_exact, n, large)


def _bias_tiles(rel_bias, tq):
    period = 2 * TK + tq
    rel = jnp.arange(period, dtype=jnp.int32) - (TK + tq - 1)
    table = rel_bias.astype(F32)
    far = table[_t5_bucket(jnp.int32(-MAX_DISTANCE))]
    u = (table[_t5_bucket(jnp.clip(rel, -MAX_DISTANCE, MAX_DISTANCE))] - far).T
    n = jnp.tile(u, (1, tq))[:, :tq * (period - 1)].reshape(ATT_HEADS, tq, period - 1)
    m = n[:, :, tq - 1:tq - 1 + 2 * TK]
    return jnp.transpose(m.reshape(ATT_HEADS, tq, 2, TK), (0, 2, 3, 1))


def _rope_tables(pos):
    half = RET_DK // 2
    inv = ROPE_BASE ** (-jnp.arange(half, dtype=F32) / half)
    ang = pos.astype(F32)[:, None] * inv[None, :]
    cos, sin = jnp.cos(ang), jnp.sin(ang)
    cos_h = jnp.concatenate([cos, cos], axis=1)
    sin_h = jnp.concatenate([-sin, sin], axis=1)
    reps = LANE // RET_DK
    return jnp.tile(cos_h, (1, reps)), jnp.tile(sin_h, (1, reps))


def _permute_w_in(w_in):
    offs = np.cumsum([0, RET_W, RET_W, RET_W, RET_W, ATT_W, KV_W, KV_W, IDX_W, IDX_DIM, IDX_HEADS])
    seg = lambda i: w_in[:, offs[i]:offs[i + 1]]
    pad = jnp.zeros((D_MODEL, LANE - IDX_DIM - IDX_HEADS), w_in.dtype)
    return jnp.concatenate([seg(0), seg(1), seg(2), seg(3), seg(4), seg(7),
                            seg(5), seg(6), seg(8), seg(9), pad], axis=1).astype(BF16)


def _layer(x, s0, past, weights, rel_bias):
    w_perm, w_out, w_up, w_down, g_pre_mix, g_post_mix, g_pre_ffn, g_post_ffn = weights
    batch, t_len, _ = x.shape
    rows = batch * t_len
    x2 = x.reshape(rows, D_MODEL)
    tm = min(ROW_TILE, rows)
    p_len = 0 if past is None else past[0].shape[1]
    pos = p_len + jnp.arange(t_len, dtype=jnp.int32)
    cos_t, sin_t = _rope_tables(pos)
    main, kc, vc, ic, iw, *key_tiles = _inproj(x2, g_pre_mix, w_perm, cos_t, sin_t, tm,
                                               key_tiles=past is None)

    chunk = min(CHUNK, t_len)
    sb = min(RET_BLOCK, t_len)
    o_ret, s_new = _retention(main, s0, batch, t_len, sb, chunk)

    l_all = p_len + t_len
    n_sel = min(TOPK_MAX, l_all // 4)
    if past is None:
        tq = TK
        n_qblk = t_len // tq
        nkt_total = t_len // TK
        bt = _bias_tiles(rel_bias, tq)
        k3, ki3, vt3 = key_tiles
        o_att = _attention(main, main, iw, ki3, k3, vt3, bt, batch, n_qblk, tq, nkt_total, None,
                           TK, tq, n_sel, aq_col=4, iq_col=5)
    else:
        tq = LANE
        pk, pv, pi = past
        assert p_len % TK == 0 and t_len <= min(tq, TK), (p_len, t_len)
        nkt_total = p_len // TK + 1
        padk = nkt_total * TK - l_all
        cat = lambda old, new: jnp.concatenate(
            [old.astype(BF16), new.reshape(batch, t_len, -1).astype(BF16),
             jnp.zeros((batch, padk, old.shape[-1]), BF16)], axis=1)
        k_all = cat(pk.reshape(batch, p_len, KV_W), kc)
        v_all = cat(pv.reshape(batch, p_len, KV_W), vc)
        i_all = cat(pi, ic)
        v_t = jnp.transpose(v_all.reshape(batch, nkt_total, TK, ATT_KV_HEADS, ATT_HD), (0, 1, 3, 4, 2))
        ones = jnp.zeros((batch, nkt_total, ATT_KV_HEADS, VT_ROWS - ATT_HD, TK), BF16).at[:, :, :, 0, :].set(1.0)
        vt_s = jnp.concatenate([v_t, ones], axis=3).reshape(batch * nkt_total, VT_ALL, TK)
        k_s = k_all.reshape(batch * nkt_total, TK, KV_W)
        i_s = i_all.reshape(batch * nkt_total, TK, IDX_DIM)
        padq = lambda a: jnp.pad(a.reshape(batch, t_len, -1), ((0, 0), (0, tq - t_len), (0, 0))
                                 ).reshape(batch * tq, -1)
        aq_s = padq(main[:, 4 * RET_W:4 * RET_W + ATT_W])
        iq_s = padq(main[:, 4 * RET_W + ATT_W:])
        iw_s = padq(iw)
        bt = _bias_tiles(rel_bias, tq)
        o_pad = _attention(aq_s, iq_s, iw_s, i_s, k_s, vt_s, bt, batch, 1, tq, nkt_total, nkt_total,
                           l_all - p_len, t_len, n_sel, aq_col=0, iq_col=0)
        o_att = o_pad.reshape(batch, tq, ATT_W)[:, :t_len].reshape(rows, ATT_W)

    y = _out_ffn(x2, o_ret, o_att, w_out, w_up, w_down, g_post_mix, g_pre_ffn, g_post_ffn, tm)
    return (y.reshape(batch, t_len, D_MODEL), s_new,
            kc.reshape(batch, t_len, ATT_KV_HEADS, ATT_HD),
            vc.reshape(batch, t_len, ATT_KV_HEADS, ATT_HD),
            ic.reshape(batch, t_len, IDX_DIM))


def kernel(x_prompt, x_sample, state_ret, cache_k, cache_v, cache_kidx, w_in, w_out, w_up, w_down,
           g_pre_mix, g_post_mix, g_pre_ffn, g_post_ffn, rel_bias):
    depth = w_in.shape[0]
    bp = x_prompt.shape[0]
    zero_state = jnp.zeros((bp, RET_HEADS, RET_DK, RET_DV), F32)
    yp, ys = x_prompt, x_sample
    outs_p, outs_s = [], []
    for l in range(depth):
        row = lambda g: g[l].reshape(1, D_MODEL).astype(F32)
        weights = (_permute_w_in(w_in[l]), w_out[l].astype(BF16), w_up[l].astype(BF16),
                   w_down[l].astype(BF16), row(g_pre_mix), row(g_post_mix), row(g_pre_ffn),
                   row(g_post_ffn))
        yp, *rest_p = _layer(yp, zero_state, None, weights, rel_bias)
        ys, *rest_s = _layer(ys, state_ret[l], (cache_k[l], cache_v[l], cache_kidx[l]), weights, rel_bias)
        outs_p.append(rest_p)
        outs_s.append(rest_s)
    stack = lambda outs, i: jnp.stack([o[i] for o in outs])
    return (yp, ys,
            stack(outs_p, 0), stack(outs_p, 1), stack(outs_p, 2), stack(outs_p, 3),
            stack(outs_s, 0), stack(outs_s, 1), stack(outs_s, 2), stack(outs_s, 3))
```

```python
import functools
import math

import jax
import jax.numpy as jnp
import numpy as np
from jax import lax
from jax.experimental import pallas as pl
from jax.experimental.pallas import tpu as pltpu

D_MODEL = 1024
CHUNK = 64
RET_HEADS = 8
RET_DK = 64
RET_DV = 64
ATT_HEADS = 8
ATT_KV_HEADS = 2
ATT_HD = 64
IDX_HEADS = 8
IDX_DIM = 64
TOPK_MAX = 256
N_BUCKETS = 32
MAX_DISTANCE = 128
D_FF = 4 * D_MODEL
ROPE_BASE = 10000.0
EPS = 1e-6

RET_W = RET_HEADS * RET_DV
ATT_W = ATT_HEADS * ATT_HD
KV_W = ATT_KV_HEADS * ATT_HD
IDX_W = IDX_HEADS * IDX_DIM
HEADS_PER_KV = ATT_HEADS // ATT_KV_HEADS

MAIN_W = 4 * RET_W + ATT_W + IDX_W
TAIL_W = 3 * 128
IW_OFF = IDX_DIM

LANE = 128
VT_ROWS = 80
VT_ALL = ATT_KV_HEADS * VT_ROWS
TK = 256
ROW_TILE = 512
COL_CHUNK = 512
RET_BLOCK = 512
NEG = -(2.0 ** 100)
BF16_ROWS = 16
CNT_ROWS = 32
NARROW = 2.0 ** -20
MAX_SEARCH = 96
F32_TINY = float(np.finfo(np.float32).tiny)
N_COARSE = 8
SMALL_BRACKET = 2.0
N_BLIND = 4
BLIND_SLACK = 6
BF16_STEP = 2.0 ** -7
VMEM_LIMIT = 56 * 1024 * 1024

F32 = jnp.float32
BF16 = jnp.bfloat16
NT_DIMS = (((1,), (1,)), ((), ()))
TN_DIMS = (((0,), (0,)), ((), ()))


def _const_spec(shape):
    nd = len(shape)
    return pl.BlockSpec(shape, lambda *_: (0,) * nd, pipeline_mode=pl.Buffered(1))


def _rms(x, gain):
    return x * lax.rsqrt(jnp.mean(x * x, axis=-1, keepdims=True) + EPS) * gain


def _inproj_kernel(x_ref, g_ref, w_ref, cos_ref, sin_ref, main_ref, kc_ref, vc_ref, ic_ref, iw_ref,
                   *tile_refs, tm):
    a = _rms(x_ref[...], g_ref[...]).astype(BF16)
    reps = RET_W // LANE
    cos = jnp.concatenate([cos_ref[...]] * reps, axis=1)
    sin = jnp.concatenate([sin_ref[...]] * reps, axis=1)
    lane = lax.broadcasted_iota(jnp.int32, cos.shape, 1)
    first_half = (lane & (RET_DK - 1)) < RET_DK // 2

    def rot(x):
        partner = jnp.where(first_half, pltpu.roll(x, RET_W - RET_DK // 2, 1),
                            pltpu.roll(x, RET_DK // 2, 1))
        return x * cos + partner * sin

    tail = jnp.dot(a, w_ref[:, MAIN_W:MAIN_W + TAIL_W], preferred_element_type=F32)
    ak = tail[:, 0:KV_W]
    av = tail[:, KV_W:2 * KV_W]
    last = tail[:, 2 * KV_W:3 * KV_W]
    for g in range(ATT_KV_HEADS):
        kc_ref[:, g, :] = ak[:, g * ATT_HD:(g + 1) * ATT_HD]
        vc_ref[:, g, :] = av[:, g * ATT_HD:(g + 1) * ATT_HD]
    ic_ref[...] = last[:, :IDX_DIM]
    iw_ref[...] = last
    if tile_refs:
        k3_ref, ki3_ref, vt3_ref = tile_refs
        avt = av.T
        row = lax.broadcasted_iota(jnp.int32, (VT_ROWS - ATT_HD, tm), 0)
        ones_rows = jnp.where(row == 0, 1.0, 0.0).astype(F32)
        vt = jnp.concatenate([avt[:ATT_HD], ones_rows, avt[ATT_HD:], ones_rows], axis=0).astype(BF16)
        for j in range(tm // TK):
            k3_ref[j] = ak[j * TK:(j + 1) * TK].astype(BF16)
            ki3_ref[j] = last[j * TK:(j + 1) * TK, :IDX_DIM].astype(BF16)
            vt3_ref[j] = vt[:, j * TK:(j + 1) * TK]

    assert COL_CHUNK == RET_W
    for c in range(MAIN_W // COL_CHUNK):
        cols = slice(c * COL_CHUNK, (c + 1) * COL_CHUNK)
        y = jnp.dot(a, w_ref[:, cols], preferred_element_type=F32)
        if c == 0:
            y = rot(y)
        elif c == 1:
            y = rot(y) * (RET_DK ** -0.5)
        main_ref[:, cols] = y


def _inproj(x2, gain, w_perm, cos_t, sin_t, tm, key_tiles):
    rows = x2.shape[0]
    grid = (rows // tm,)
    t_len = cos_t.shape[0]
    if t_len < tm:
        assert tm % t_len == 0
        cos_t, sin_t = jnp.tile(cos_t, (tm // t_len, 1)), jnp.tile(sin_t, (tm // t_len, 1))
        t_len = tm
    assert t_len % tm == 0
    tab_spec = pl.BlockSpec((tm, LANE), lambda i: (i % (t_len // tm), 0))
    row_spec = lambda w: pl.BlockSpec((tm, w), lambda i: (i, 0))
    out_shape = (
        jax.ShapeDtypeStruct((rows, MAIN_W), F32),
        jax.ShapeDtypeStruct((rows, ATT_KV_HEADS, ATT_HD), F32),
        jax.ShapeDtypeStruct((rows, ATT_KV_HEADS, ATT_HD), F32),
        jax.ShapeDtypeStruct((rows, IDX_DIM), F32),
        jax.ShapeDtypeStruct((rows, LANE), F32),
    )
    kv_spec = pl.BlockSpec((tm, ATT_KV_HEADS, ATT_HD), lambda i: (i, 0, 0))
    out_specs = (row_spec(MAIN_W), kv_spec, kv_spec, row_spec(IDX_DIM), row_spec(LANE))
    if key_tiles:
        t3 = lambda a, b: pl.BlockSpec((tm // TK, a, b), lambda i: (i, 0, 0))
        out_shape += (jax.ShapeDtypeStruct((rows // TK, TK, KV_W), BF16),
                      jax.ShapeDtypeStruct((rows // TK, TK, IDX_DIM), BF16),
                      jax.ShapeDtypeStruct((rows // TK, VT_ALL, TK), BF16))
        out_specs += (t3(TK, KV_W), t3(TK, IDX_DIM), t3(VT_ALL, TK))
    return pl.pallas_call(
        functools.partial(_inproj_kernel, tm=tm),
        grid=grid,
        in_specs=[row_spec(D_MODEL), _const_spec((1, D_MODEL)), _const_spec(w_perm.shape),
                  tab_spec, tab_spec],
        out_specs=out_specs,
        out_shape=out_shape,
        compiler_params=pltpu.CompilerParams(dimension_semantics=("parallel",),
                                             vmem_limit_bytes=VMEM_LIMIT),
        name="inproj",
    )(x2, gain, w_perm, cos_t, sin_t)


def _ret_gammas():
    return [1.0 - 2.0 ** (-5.0 - h) for h in range(RET_HEADS)]


def _ret_tables(sb, chunk):
    lg = np.log(np.array(_ret_gammas(), np.float64))
    t = np.arange(sb)
    ci = t // chunk
    diff = t[:, None] - t[None, :]
    same = ci[:, None] == ci[None, :]
    below = ci[None, :] < ci[:, None]
    expo = np.where(same, np.abs(diff), np.where(below, diff, 0)).astype(np.float64)
    dmat = np.exp(lg[:, None, None] * expo[None]) * (same | below)[None]
    qd = np.exp(lg[None, :] * (t + 1.0)[:, None])
    kd = np.exp(lg[None, :] * (sb - 1.0 - t)[:, None])
    qd = np.repeat(qd, RET_DK, axis=1)
    kd = np.repeat(kd, RET_DK, axis=1)
    return (jnp.asarray(dmat, F32), jnp.asarray(qd, F32), jnp.asarray(kd, F32),
            [float(math.exp(v * sb)) for v in lg])


def _ret_kernel(q_ref, k_ref, v_ref, g_ref, d_ref, qd_ref, kd_ref, s0_ref,
                o_ref, sfin_ref, s_scr, *, g_block):
    n = pl.program_id(1)

    @pl.when(n == 0)
    def _():
        s_scr[...] = s0_ref[0]

    q = q_ref[...]
    k = k_ref[...]
    v = v_ref[...].astype(BF16)
    gate = g_ref[...]
    qb = q.astype(BF16)
    kb = k.astype(BF16)
    qx = (q * qd_ref[...]).astype(BF16)
    kx = (k * kd_ref[...]).astype(BF16)
    outs = []
    for h in range(RET_HEADS):
        sl = slice(h * RET_DK, (h + 1) * RET_DK)
        s = lax.dot_general(qb[:, sl], kb[:, sl], NT_DIMS, preferred_element_type=F32)
        p = (s * d_ref[h]).astype(BF16)
        st = s_scr[h]
        o = jnp.dot(p, v[:, sl], preferred_element_type=F32)
        o = o + jnp.dot(qx[:, sl], st.astype(BF16), preferred_element_type=F32)
        s_scr[h] = g_block[h] * st + lax.dot_general(kx[:, sl], v[:, sl], TN_DIMS,
                                                     preferred_element_type=F32)
        o = o * lax.rsqrt(jnp.mean(o * o, axis=-1, keepdims=True) + EPS)
        outs.append(o)
    o_all = jnp.concatenate(outs, axis=1)
    o_ref[...] = gate * (1.0 / (1.0 + jnp.exp(-gate))) * o_all

    @pl.when(n == pl.num_programs(1) - 1)
    def _():
        sfin_ref[0] = s_scr[...]


def _retention(main, s0, batch, t_len, sb, chunk):
    nsb = t_len // sb
    dmat, qd, kd, g_block = _ret_tables(sb, chunk)
    col = lambda c: pl.BlockSpec((sb, RET_W), lambda b, n, c=c: (b * nsb + n, c))
    st_spec = pl.BlockSpec((1, RET_HEADS, RET_DK, RET_DV), lambda b, n: (b, 0, 0, 0))
    return pl.pallas_call(
        functools.partial(_ret_kernel, g_block=g_block),
        grid=(batch, nsb),
        in_specs=[col(0), col(1), col(2), col(3),
                  _const_spec(dmat.shape), _const_spec(qd.shape), _const_spec(kd.shape), st_spec],
        out_specs=(pl.BlockSpec((sb, RET_W), lambda b, n: (b * nsb + n, 0)), st_spec),
        out_shape=(jax.ShapeDtypeStruct((batch * t_len, RET_W), F32),
                   jax.ShapeDtypeStruct((batch, RET_HEADS, RET_DK, RET_DV), F32)),
        scratch_shapes=[pltpu.VMEM((RET_HEADS, RET_DK, RET_DV), F32)],
        compiler_params=pltpu.CompilerParams(dimension_semantics=("parallel", "arbitrary"),
                                             vmem_limit_bytes=VMEM_LIMIT),
        name="retention",
    )(main, main, main, main, dmat, qd, kd, s0)


def _attn_kernel(aq_ref, iq_ref, iw_ref, ki_ref, k_ref, vt_ref, bt_ref, o_ref,
                 s_scr, sb_scr, acc_scr, m_scr, p_scr, *, tq, nkt_static, last_valid, n_valid_q, n_sel,
                 max_search, n_blind):
    qblk = pl.program_id(1)
    nkt = qblk + 1 if nkt_static is None else nkt_static
    lane_f = lax.broadcasted_iota(jnp.int32, (1, tq), 1).astype(F32)
    lane_ok = lane_f < float(n_valid_q)
    klim = jnp.minimum((jnp.floor(lane_f * (1.0 / CHUNK)) + 1.0) * CHUNK, float(last_valid))
    krow = lax.broadcasted_iota(jnp.int32, (TK, tq), 0).astype(F32)
    adm_last = krow < klim
    nkt_f = nkt.astype(F32) if nkt_static is None else float(nkt)
    n_adm = (nkt_f - 1.0) * TK + klim
    k_target = jnp.minimum(float(n_sel), n_adm)

    def query_rows(ref):
        x = ref[...]
        if n_valid_q == tq:
            return x
        return jnp.concatenate([x, jnp.zeros((tq - n_valid_q, x.shape[1]), x.dtype)], axis=0)

    iq = query_rows(iq_ref)
    w_t = query_rows(iw_ref).T[IW_OFF:IW_OFF + IDX_HEADS, :] * (IDX_HEADS ** -0.5 * IDX_DIM ** -0.5)
    iq_all = jnp.concatenate(
        [iq[:, h * IDX_DIM:(h + 1) * IDX_DIM].astype(BF16) for h in range(IDX_HEADS)], axis=0)

    def score_tile(kt, n_tiles=1):
        ki_t = ki_ref[kt] if n_tiles == 1 else jnp.concatenate(
            [ki_ref[kt + j] for j in range(n_tiles)], axis=0)
        s_all = lax.dot_general(ki_t, iq_all, NT_DIMS, preferred_element_type=F32)
        acc = jnp.maximum(s_all[:, :tq], 0.0) * w_t[0:1, :]
        for h in range(1, IDX_HEADS):
            acc = acc + jnp.maximum(s_all[:, h * tq:(h + 1) * tq], 0.0) * w_t[h:h + 1, :]
        return acc

    def fold_rows(x, op=jnp.add):
        parts = [x[i * CNT_ROWS:(i + 1) * CNT_ROWS] for i in range(x.shape[0] // CNT_ROWS)]
        while len(parts) > 1:
            parts = [op(a, b) for a, b in zip(parts[0::2], parts[1::2])]
        return parts[0]

    def p1_step(kt, n_tiles, carry):
        rmax, rmin = carry
        sc = score_tile(kt, n_tiles)
        for j in range(n_tiles):
            s_scr[kt + j] = sc[j * TK:(j + 1) * TK]
            sb_scr[kt + j] = sc[j * TK:(j + 1) * TK].astype(BF16)
        return (jnp.maximum(rmax, jnp.max(sc, axis=0, keepdims=True)),
                jnp.minimum(rmin, jnp.min(sc, axis=0, keepdims=True)))

    def run_steps(n, step, carry):
        n4 = n // 4
        carry = lax.fori_loop(0, n4, lambda i, c: step(4 * i, 4, c), carry)
        carry = lax.fori_loop(0, (n - 4 * n4) // 2, lambda i, c: step(4 * n4, 2, c), carry)
        return lax.fori_loop(0, n % 2, lambda i, c: step(n - 1, 1, c), carry)

    init = (jnp.full((1, tq), -jnp.inf, F32), jnp.full((1, tq), jnp.inf, F32))
    rmax, rmin = run_steps(nkt - 1, p1_step, init)
    sc = jnp.where(adm_last, score_tile(nkt - 1), -jnp.inf)
    s_scr[nkt - 1] = sc
    sb_scr[nkt - 1] = sc.astype(BF16)
    rmax = jnp.maximum(rmax, jnp.max(sc, axis=0, keepdims=True))
    rmin = jnp.minimum(rmin, jnp.min(jnp.where(adm_last, sc, jnp.inf), axis=0, keepdims=True))

    def tile_loop(n, body, carry):
        carry = lax.fori_loop(0, n // 2, lambda i, c: body(2 * i + 1, body(2 * i, c)), carry)
        return lax.fori_loop(0, n % 2, lambda i, c: body(n - 1, c), carry)

    def count_where(pred):
        def body(kt, c):
            return c + fold_rows(jnp.where(pred(kt), 1.0, 0.0))
        c = tile_loop(nkt, body, jnp.zeros((CNT_ROWS, tq), F32))
        return jnp.sum(c, axis=0, keepdims=True)

    def count_ge(thr):
        return count_where(lambda kt: s_scr[kt] >= thr)

    def count_ge_rounded(thr_b):
        one, zero = jnp.ones((), BF16), jnp.zeros((), BF16)

        def body(kt, c):
            ind = jnp.where(sb_scr[kt] >= thr_b, one, zero)
            parts = [ind[i * BF16_ROWS:(i + 1) * BF16_ROWS] for i in range(TK // BF16_ROWS)]
            while len(parts) > 1:
                parts = [a + b for a, b in zip(parts[0::2], parts[1::2])]
            return c + parts[0].astype(F32)
        c = tile_loop(nkt, body, jnp.zeros((BF16_ROWS, tq), F32))
        return jnp.sum(c, axis=0, keepdims=True)

    span = jnp.maximum(jnp.maximum(rmax - rmin, jnp.abs(rmax)), 1e-30)
    hi0 = rmax + span * (2.0 ** -10)

    def max_below(bound, n_tiles):
        def body(kt, m):
            s = s_scr[kt]
            return jnp.maximum(m, fold_rows(jnp.where(s < bound, s, -jnp.inf), jnp.maximum))
        m = tile_loop(n_tiles, body, jnp.full((CNT_ROWS, tq), -jnp.inf, F32))
        return jnp.max(m, axis=0, keepdims=True)

    def plan(it, lo, hi, c_lo, c_hi, done, want_flag=True):
        width = hi - lo
        frac = (c_lo - k_target - 0.5) / (c_lo - c_hi)
        interp = lax.convert_element_type((it + 1) % 2, F32)
        mid_i = lo + width * (0.5 + interp * (frac - 0.5))
        mid_b = lo + 0.5 * width
        mid = jnp.where((mid_i > lo) & (mid_i < hi), mid_i, mid_b)
        if not want_flag:
            return mid, None
        active = done == 0.0
        narrow = jnp.logical_not((mid > lo) & (mid < hi)) | (width <= span * NARROW)
        wide = active & jnp.logical_not(narrow) & (c_lo - c_hi > SMALL_BRACKET)
        flag = jnp.max(jnp.where(wide, 5.0, jnp.where(active, 3.0, 0.0)))
        return mid, flag

    def accept(mid, c, exact, lo, hi, c_lo, c_hi, done):
        active = done == 0.0
        ge = c >= k_target
        up = active & ge
        dn = active & jnp.logical_not(ge)
        lo = jnp.where(up, mid, lo)
        c_lo = jnp.where(up, c, c_lo)
        hi = jnp.where(dn, mid, hi)
        c_hi = jnp.where(dn, c, c_hi)
        finished = (c_lo == k_target) if exact is None else (c_lo == k_target) | (exact & up)
        return lo, hi, c_lo, c_hi, jnp.where(finished, 1.0, done)

    def blind_body(it, st):
        lo, hi, c_lo, c_hi, done = st
        mid, _ = plan(it, lo, hi, c_lo, c_hi, done, want_flag=False)
        return accept(mid, count_ge(mid), None, lo, hi, c_lo, c_hi, done)

    def search_body(st):
        it, flag, lo, hi, c_lo, c_hi, done, mid = st
        exact = (jnp.zeros((1, tq), F32) + flag) < 4.0
        top = max_below(hi, jnp.where(flag < 4.0, nkt, 0))
        mid = jnp.where(exact, top, mid)
        lo, hi, c_lo, c_hi, done = accept(mid, count_ge(mid), exact, lo, hi, c_lo, c_hi, done)
        mid, flag = plan(it + 1, lo, hi, c_lo, c_hi, done)
        return it + 1, flag, lo, hi, c_lo, c_hi, done, mid

    c_zero = count_ge_rounded(jnp.zeros((1, tq), BF16))
    c_pos = count_ge_rounded(jnp.full((1, tq), F32_TINY, BF16))
    pos_side = c_pos >= k_target
    at_zero = jnp.logical_not(pos_side) & (c_zero >= k_target)
    pick = lambda p, z, n: jnp.where(pos_side, p, jnp.where(at_zero, z, n))
    lo0 = pick(F32_TINY, 0.0, rmin)
    hi0 = pick(hi0, F32_TINY, 0.0)
    c_lo0 = pick(c_pos, c_zero, n_adm)
    c_hi0 = pick(0.0, c_pos, c_zero)
    done0 = jnp.where((n_adm == k_target) | at_zero | jnp.logical_not(lane_ok), 1.0, 0.0)

    def coarse_body(it, st):
        lo, hi, c_lo, c_hi, live = st
        width = hi - lo
        frac = (c_lo - k_target - 0.5) / (c_lo - c_hi)
        interp = lax.convert_element_type((it + 1) % 2, F32)
        grid = lambda x: x.astype(BF16).astype(F32)
        mid_i = grid(lo + width * (0.5 + interp * (frac - 0.5)))
        mid_b = grid(lo + 0.5 * width)
        mid = jnp.where((mid_i > lo) & (mid_i < hi), mid_i, mid_b)
        ok = (mid > lo) & (mid < hi) & (live > 0.0)
        c = count_ge_rounded(mid.astype(BF16))
        ge = c >= k_target
        up = ok & ge
        dn = ok & jnp.logical_not(ge)
        return (jnp.where(up, mid, lo), jnp.where(dn, mid, hi), jnp.where(up, c, c_lo),
                jnp.where(dn, c, c_hi), jnp.where(ok, live, 0.0))

    lo1, hi0, c_lo0, c_hi0, _ = lax.fori_loop(
        0, N_COARSE, coarse_body, (lo0, hi0, c_lo0, c_hi0, 1.0 - done0))
    lo0 = jnp.where(lo1 != lo0, lo1 - jnp.abs(lo1) * BF16_STEP, lo0)
    c_lo0 = count_ge(lo0)
    done0 = jnp.where(c_lo0 == k_target, 1.0, done0)
    lo0, hi0, c_lo0, c_hi0, done0 = lax.fori_loop(
        0, n_blind, blind_body, (lo0, hi0, c_lo0, c_hi0, done0))
    mid0, flag0 = plan(jnp.int32(n_blind), lo0, hi0, c_lo0, c_hi0, done0)
    st0 = (jnp.int32(n_blind), flag0, lo0, hi0, c_lo0, c_hi0, done0, mid0)
    _, _, lo, hi, cnt_lo, _, _, _ = lax.while_loop(
        lambda st: (st[0] < max_search) & (st[1] > 0.0), search_body, st0)

    excess = jnp.max(jnp.where(lane_ok, cnt_lo - k_target, 0.0))

    @pl.when(excess > 0.0)
    def _():
        need = k_target - count_ge(hi)
        r = lax.broadcasted_iota(jnp.int32, (TK, TK), 0)
        c = lax.broadcasted_iota(jnp.int32, (TK, TK), 1)
        prefix = jnp.where(r >= c, 1.0, 0.0).astype(BF16)

        def drop_step(kt, n_tiles, before):
            tiles = [s_scr[kt + j] for j in range(n_tiles)]
            ties = [(s >= lo) & (s < hi) for s in tiles]
            ranks = [jnp.dot(prefix, jnp.where(t, 1.0, 0.0).astype(BF16),
                             preferred_element_type=F32) for t in ties]
            for j in range(n_tiles):
                rank = before + ranks[j]
                s_scr[kt + j] = jnp.where(ties[j] & (rank > need), -jnp.inf, tiles[j])
                before = rank[TK - 1:TK, :]
            return before

        run_steps(nkt, drop_step, jnp.zeros((1, tq), F32))

    aq = query_rows(aq_ref) * (ATT_HD ** -0.5)
    zeros_q = jnp.zeros((tq, ATT_HD), F32)
    q_pad = []
    for h in range(ATT_HEADS):
        qh = aq[:, h * ATT_HD:(h + 1) * ATT_HD]
        parts = [zeros_q] * ATT_KV_HEADS
        parts[h // HEADS_PER_KV] = qh
        q_pad.append(jnp.concatenate(parts, axis=1).astype(BF16))

    m_scr[...] = jnp.full(m_scr.shape, NEG, F32)
    acc_scr[...] = jnp.zeros(acc_scr.shape, F32)

    def col_max(x):
        parts = [x[i * BF16_ROWS:(i + 1) * BF16_ROWS] for i in range(x.shape[0] // BF16_ROWS)]
        while len(parts) > 1:
            parts = [jnp.maximum(a, b) for a, b in zip(parts[0::2], parts[1::2])]
        return jnp.max(parts[0].astype(F32), axis=0, keepdims=True)

    cat = lambda xs, axis: xs[0] if len(xs) == 1 else jnp.concatenate(xs, axis=axis)

    def logits(kt, n_tiles, near):
        k_t = cat([k_ref[kt + j] for j in range(n_tiles)], 0)
        s_t = cat([s_scr[kt + j] for j in range(n_tiles)], 0)
        neg_mask = jnp.where(s_t >= lo, 0.0, NEG).astype(BF16)
        out = []
        for h in range(ATT_HEADS):
            lg = lax.dot_general(k_t, q_pad[h], NT_DIMS, preferred_element_type=F32)
            if near:
                lg = lg + cat([bt_ref[h, 2 - n_tiles + j] for j in range(n_tiles)], 0)
            out.append(lg.astype(BF16) + neg_mask)
        return out

    def softmax_pv(kt, n_tiles, lgs):
        tiles = [kt + j for j in range(n_tiles)]
        rows = n_tiles * TK
        alphas = []
        for h in range(ATT_HEADS):
            g, r = divmod(h, HEADS_PER_KV)
            lg = lgs[h]
            m_old = m_scr[h:h + 1, :]
            m_new = jnp.maximum(m_old, col_max(lg))
            alphas.append(jnp.exp(m_old - m_new))
            p_scr[g, :rows, r * tq:(r + 1) * tq] = jnp.exp(lg - m_new.astype(BF16))
            m_scr[h:h + 1, :] = m_new
        for g in range(ATT_KV_HEADS):
            hs = range(g * HEADS_PER_KV, (g + 1) * HEADS_PER_KV)
            a_g = jnp.concatenate([alphas[h] for h in hs], axis=1)
            vt_g = cat([vt_ref[t, g * VT_ROWS:(g + 1) * VT_ROWS, :] for t in tiles], 1)
            acc_scr[g] = acc_scr[g] * a_g + jnp.dot(vt_g, p_scr[g, :rows, :],
                                                    preferred_element_type=F32)

    def attend(kt, n_tiles, near):
        softmax_pv(kt, n_tiles, logits(kt, n_tiles, near))

    n_far = jnp.maximum(nkt - 2, 0) if nkt_static is None else max(nkt - 2, 0)

    def far_body(i, c):
        attend(2 * i, 2, False)
        return c
    lax.fori_loop(0, n_far // 2, far_body, 0)

    def when(cond):
        return pl.when(cond) if nkt_static is None else (lambda f: f() if cond else None)

    @when(n_far % 2 == 1)
    def _():
        attend(n_far - 1, 1, False)

    @when(nkt >= 2)
    def _():
        attend(nkt - 2, 2, True)

    @when(nkt < 2)
    def _():
        attend(0, 1, True)

    outs = []
    for g in range(ATT_KV_HEADS):
        a = acc_scr[g]
        o_g = a[:ATT_HD] * (1.0 / a[ATT_HD:ATT_HD + 1])
        outs += [o_g[:, r * tq:(r + 1) * tq] for r in range(HEADS_PER_KV)]
    o_ref[...] = jnp.concatenate(outs, axis=0).T[:n_valid_q]


def _attention(aq_src, iq_src, iw, ki3, k3, vt3, bt, batch, n_qblk, tq, nkt_total, nkt_static,
               last_valid, n_valid_q, n_sel, aq_col, iq_col):
    kern = functools.partial(_attn_kernel, tq=tq, nkt_static=nkt_static, last_valid=last_valid,
                             n_valid_q=n_valid_q, n_sel=n_sel, max_search=MAX_SEARCH,
                             n_blind=max(N_BLIND, int(math.log2(nkt_total * TK)) - BLIND_SLACK))
    qspec = lambda c: pl.BlockSpec((n_valid_q, ATT_W), lambda b, q, c=c: (b * n_qblk + q, c))
    kspec = lambda a, c: pl.BlockSpec((nkt_total, a, c), lambda b, q: (b, 0, 0))
    return pl.pallas_call(
        kern,
        grid=(batch, n_qblk),
        in_specs=[qspec(aq_col), qspec(iq_col),
                  pl.BlockSpec((n_valid_q, LANE), lambda b, q: (b * n_qblk + q, 0)),
                  kspec(TK, IDX_DIM), kspec(TK, KV_W), kspec(VT_ALL, TK),
                  _const_spec(bt.shape)],
        out_specs=pl.BlockSpec((n_valid_q, ATT_W), lambda b, q: (b * n_qblk + q, 0)),
        out_shape=jax.ShapeDtypeStruct((batch * n_qblk * n_valid_q, ATT_W), F32),
        scratch_shapes=[pltpu.VMEM((nkt_total, TK, tq), F32),
                        pltpu.VMEM((nkt_total, TK, tq), BF16),
                        pltpu.VMEM((ATT_KV_HEADS, VT_ROWS, HEADS_PER_KV * tq), F32),
                        pltpu.VMEM((ATT_HEADS, tq), F32),
                        pltpu.VMEM((ATT_KV_HEADS, 2 * TK, HEADS_PER_KV * tq), BF16)],
        compiler_params=pltpu.CompilerParams(dimension_semantics=("parallel", "arbitrary"),
                                             vmem_limit_bytes=VMEM_LIMIT),
        name="attention",
    )(aq_src, iq_src, iw, ki3, k3, vt3, bt)


def _ffn_kernel(x_ref, oret_ref, oatt_ref, wout_ref, wup_ref, wdown_ref,
                gpost_ref, gpre_ref, gffn_ref, y_ref, *, ff_chunk):
    mix = jnp.dot(oret_ref[...].astype(BF16), wout_ref[:RET_W, :], preferred_element_type=F32)
    mix = mix + jnp.dot(oatt_ref[...].astype(BF16), wout_ref[RET_W:, :], preferred_element_type=F32)
    h = x_ref[...] + _rms(mix, gpost_ref[...])
    a = _rms(h, gpre_ref[...]).astype(BF16)
    f = jnp.zeros(h.shape, F32)
    for c in range(D_FF // ff_chunk):
        sl = slice(c * ff_chunk, (c + 1) * ff_chunk)
        u = jnp.dot(a, wup_ref[:, sl], preferred_element_type=F32)
        u = jnp.square(jnp.maximum(u, 0.0)).astype(BF16)
        f = f + jnp.dot(u, wdown_ref[sl, :], preferred_element_type=F32)
    y_ref[...] = h + _rms(f, gffn_ref[...])


def _out_ffn(x2, o_ret, o_att, w_out, w_up, w_down, g_post, g_pre, g_ffn, tm):
    rows = x2.shape[0]
    row_spec = lambda w: pl.BlockSpec((tm, w), lambda i: (i, 0))
    return pl.pallas_call(
        functools.partial(_ffn_kernel, ff_chunk=COL_CHUNK),
        grid=(rows // tm,),
        in_specs=[row_spec(D_MODEL), row_spec(RET_W), row_spec(ATT_W),
                  _const_spec(w_out.shape), _const_spec(w_up.shape), _const_spec(w_down.shape),
                  _const_spec((1, D_MODEL)), _const_spec((1, D_MODEL)), _const_spec((1, D_MODEL))],
        out_specs=row_spec(D_MODEL),
        out_shape=jax.ShapeDtypeStruct((rows, D_MODEL), F32),
        compiler_params=pltpu.CompilerParams(dimension_semantics=("parallel",),
                                             vmem_limit_bytes=VMEM_LIMIT),
        name="out_ffn",
    )(x2, o_ret, o_att, w_out, w_up, w_down, g_post, g_pre, g_ffn)


def _t5_bucket(rel):
    half = N_BUCKETS // 2
    max_exact = half // 2
    ret = jnp.where(rel > 0, half, 0)
    n = jnp.abs(rel)
    nf = jnp.maximum(n, 1).astype(F32)
    large = max_exact + (jnp.log(nf / max_exact) / math.log(MAX_DISTANCE / max_exact)
                         * (half - max_exact)).astype(jnp.int32)
    large = jnp.minimum(large, half - 1)
    return ret + jnp.where(n < max_exact, n, large)


def _bias_tiles(rel_bias, tq):
    period = 2 * TK + tq
    rel = jnp.arange(period, dtype=jnp.int32) - (TK + tq - 1)
    table = rel_bias.astype(F32)
    far = table[_t5_bucket(jnp.int32(-MAX_DISTANCE))]
    u = (table[_t5_bucket(jnp.clip(rel, -MAX_DISTANCE, MAX_DISTANCE))] - far).T
    n = jnp.tile(u, (1, tq))[:, :tq * (period - 1)].reshape(ATT_HEADS, tq, period - 1)
    m = n[:, :, tq - 1:tq - 1 + 2 * TK]
    return jnp.transpose(m.reshape(ATT_HEADS, tq, 2, TK), (0, 2, 3, 1))


def _rope_tables(pos):
    half = RET_DK // 2
    inv = ROPE_BASE ** (-jnp.arange(half, dtype=F32) / half)
    ang = pos.astype(F32)[:, None] * inv[None, :]
    cos, sin = jnp.cos(ang), jnp.sin(ang)
    cos_h = jnp.concatenate([cos, cos], axis=1)
    sin_h = jnp.concatenate([-sin, sin], axis=1)
    reps = LANE // RET_DK
    return jnp.tile(cos_h, (1, reps)), jnp.tile(sin_h, (1, reps))


def _permute_w_in(w_in):
    offs = np.cumsum([0, RET_W, RET_W, RET_W, RET_W, ATT_W, KV_W, KV_W, IDX_W, IDX_DIM, IDX_HEADS])
    seg = lambda i: w_in[:, offs[i]:offs[i + 1]]
    pad = jnp.zeros((D_MODEL, LANE - IDX_DIM - IDX_HEADS), w_in.dtype)
    return jnp.concatenate([seg(0), seg(1), seg(2), seg(3), seg(4), seg(7),
                            seg(5), seg(6), seg(8), seg(9), pad], axis=1).astype(BF16)


def _layer(x, s0, past, weights, rel_bias):
    w_perm, w_out, w_up, w_down, g_pre_mix, g_post_mix, g_pre_ffn, g_post_ffn = weights
    batch, t_len, _ = x.shape
    rows = batch * t_len
    x2 = x.reshape(rows, D_MODEL)
    tm = min(ROW_TILE, rows)
    p_len = 0 if past is None else past[0].shape[1]
    pos = p_len + jnp.arange(t_len, dtype=jnp.int32)
    cos_t, sin_t = _rope_tables(pos)
    main, kc, vc, ic, iw, *key_tiles = _inproj(x2, g_pre_mix, w_perm, cos_t, sin_t, tm,
                                               key_tiles=past is None)

    chunk = min(CHUNK, t_len)
    sb = min(RET_BLOCK, t_len)
    o_ret, s_new = _retention(main, s0, batch, t_len, sb, chunk)

    l_all = p_len + t_len
    n_sel = min(TOPK_MAX, l_all // 4)
    if past is None:
        tq = TK
        n_qblk = t_len // tq
        nkt_total = t_len // TK
        bt = _bias_tiles(rel_bias, tq)
        k3, ki3, vt3 = key_tiles
        o_att = _attention(main, main, iw, ki3, k3, vt3, bt, batch, n_qblk, tq, nkt_total, None,
                           TK, tq, n_sel, aq_col=4, iq_col=5)
    else:
        tq = LANE
        pk, pv, pi = past
        assert p_len % TK == 0 and t_len <= min(tq, TK), (p_len, t_len)
        nkt_total = p_len // TK + 1
        padk = nkt_total * TK - l_all
        cat = lambda old, new: jnp.concatenate(
            [old.astype(BF16), new.reshape(batch, t_len, -1).astype(BF16),
             jnp.zeros((batch, padk, old.shape[-1]), BF16)], axis=1)
        k_all = cat(pk.reshape(batch, p_len, KV_W), kc)
        v_all = cat(pv.reshape(batch, p_len, KV_W), vc)
        i_all = cat(pi, ic)
        v_t = jnp.transpose(v_all.reshape(batch, nkt_total, TK, ATT_KV_HEADS, ATT_HD), (0, 1, 3, 4, 2))
        ones = jnp.zeros((batch, nkt_total, ATT_KV_HEADS, VT_ROWS - ATT_HD, TK), BF16).at[:, :, :, 0, :].set(1.0)
        vt_s = jnp.concatenate([v_t, ones], axis=3).reshape(batch * nkt_total, VT_ALL, TK)
        k_s = k_all.reshape(batch * nkt_total, TK, KV_W)
        i_s = i_all.reshape(batch * nkt_total, TK, IDX_DIM)
        bt = _bias_tiles(rel_bias, tq)
        o_att = _attention(main, main, iw, i_s, k_s, vt_s, bt, batch, 1, tq, nkt_total, nkt_total,
                           l_all - p_len, t_len, n_sel, aq_col=4, iq_col=5)

    y = _out_ffn(x2, o_ret, o_att, w_out, w_up, w_down, g_post_mix, g_pre_ffn, g_post_ffn, tm)
    return (y.reshape(batch, t_len, D_MODEL), s_new,
            kc.reshape(batch, t_len, ATT_KV_HEADS, ATT_HD),
            vc.reshape(batch, t_len, ATT_KV_HEADS, ATT_HD),
            ic.reshape(batch, t_len, IDX_DIM))


def kernel(x_prompt, x_sample, state_ret, cache_k, cache_v, cache_kidx, w_in, w_out, w_up, w_down,
           g_pre_mix, g_post_mix, g_pre_ffn, g_post_ffn, rel_bias):
    depth = w_in.shape[0]
    bp = x_prompt.shape[0]
    zero_state = jnp.zeros((bp, RET_HEADS, RET_DK, RET_DV), F32)
    yp, ys = x_prompt, x_sample
    outs_p, outs_s = [], []
    for l in range(depth):
        row = lambda g: g[l].reshape(1, D_MODEL).astype(F32)
        weights = (_permute_w_in(w_in[l]), w_out[l].astype(BF16), w_up[l].astype(BF16),
                   w_down[l].astype(BF16), row(g_pre_mix), row(g_post_mix), row(g_pre_ffn),
                   row(g_post_ffn))
        yp, *rest_p = _layer(yp, zero_state, None, weights, rel_bias)
        ys, *rest_s = _layer(ys, state_ret[l], (cache_k[l], cache_v[l], cache_kidx[l]), weights, rel_bias)
        outs_p.append(rest_p)
        outs_s.append(rest_s)
    stack = lambda outs, i: jnp.stack([o[i] for o in outs])
    return (yp, ys,
            stack(outs_p, 0), stack(outs_p, 1), stack(outs_p, 2), stack(outs_p, 3),
            stack(outs_s, 0), stack(outs_s, 1), stack(outs_s, 2), stack(outs_s, 3))
```

```python
import functools
import math

import jax
import jax.numpy as jnp
import numpy as np
from jax import lax
from jax.experimental import pallas as pl
from jax.experimental.pallas import tpu as pltpu

D_MODEL = 1024
CHUNK = 64
RET_HEADS = 8
RET_DK = 64
RET_DV = 64
ATT_HEADS = 8
ATT_KV_HEADS = 2
ATT_HD = 64
IDX_HEADS = 8
IDX_DIM = 64
TOPK_MAX = 256
N_BUCKETS = 32
MAX_DISTANCE = 128
D_FF = 4 * D_MODEL
ROPE_BASE = 10000.0
EPS = 1e-6

RET_W = RET_HEADS * RET_DV
ATT_W = ATT_HEADS * ATT_HD
KV_W = ATT_KV_HEADS * ATT_HD
IDX_W = IDX_HEADS * IDX_DIM
HEADS_PER_KV = ATT_HEADS // ATT_KV_HEADS

MAIN_W = 4 * RET_W + ATT_W + IDX_W
TAIL_W = 3 * 128
IW_OFF = IDX_DIM

LANE = 128
VT_ROWS = 80
VT_ALL = ATT_KV_HEADS * VT_ROWS
TK = 256
ROW_TILE = 512
COL_CHUNK = 512
RET_BLOCK = 512
NEG = -(2.0 ** 100)
BF16_ROWS = 16
CNT_ROWS = 32
NARROW = 2.0 ** -20
MAX_SEARCH = 96
F32_TINY = float(np.finfo(np.float32).tiny)
N_COARSE = 8
SMALL_BRACKET = 2.0
N_BLIND = 4
BLIND_SLACK = 6
BF16_STEP = 2.0 ** -7
VMEM_LIMIT = 56 * 1024 * 1024

F32 = jnp.float32
BF16 = jnp.bfloat16
NT_DIMS = (((1,), (1,)), ((), ()))
TN_DIMS = (((0,), (0,)), ((), ()))


def _const_spec(shape):
    nd = len(shape)
    return pl.BlockSpec(shape, lambda *_: (0,) * nd, pipeline_mode=pl.Buffered(1))


def _rms(x, gain):
    return x * lax.rsqrt(jnp.mean(x * x, axis=-1, keepdims=True) + EPS) * gain


def _inproj_kernel(x_ref, g_ref, w_ref, cos_ref, sin_ref, main_ref, kc_ref, vc_ref, ic_ref, iw_ref,
                   *tile_refs, tm):
    a = _rms(x_ref[...], g_ref[...]).astype(BF16)
    reps = RET_W // LANE
    cos = jnp.concatenate([cos_ref[...]] * reps, axis=1)
    sin = jnp.concatenate([sin_ref[...]] * reps, axis=1)
    lane = lax.broadcasted_iota(jnp.int32, cos.shape, 1)
    first_half = (lane & (RET_DK - 1)) < RET_DK // 2

    def rot(x):
        partner = jnp.where(first_half, pltpu.roll(x, RET_W - RET_DK // 2, 1),
                            pltpu.roll(x, RET_DK // 2, 1))
        return x * cos + partner * sin

    tail = jnp.dot(a, w_ref[:, MAIN_W:MAIN_W + TAIL_W], preferred_element_type=F32)
    ak = tail[:, 0:KV_W]
    av = tail[:, KV_W:2 * KV_W]
    last = tail[:, 2 * KV_W:3 * KV_W]
    for g in range(ATT_KV_HEADS):
        kc_ref[:, g, :] = ak[:, g * ATT_HD:(g + 1) * ATT_HD]
        vc_ref[:, g, :] = av[:, g * ATT_HD:(g + 1) * ATT_HD]
    ic_ref[...] = last[:, :IDX_DIM]
    iw_ref[...] = last
    if tile_refs:
        k3_ref, ki3_ref, vt3_ref = tile_refs
        avt = av.T
        row = lax.broadcasted_iota(jnp.int32, (VT_ROWS - ATT_HD, tm), 0)
        ones_rows = jnp.where(row == 0, 1.0, 0.0).astype(F32)
        vt = jnp.concatenate([avt[:ATT_HD], ones_rows, avt[ATT_HD:], ones_rows], axis=0).astype(BF16)
        for j in range(tm // TK):
            k3_ref[j] = ak[j * TK:(j + 1) * TK].astype(BF16)
            ki3_ref[j] = last[j * TK:(j + 1) * TK, :IDX_DIM].astype(BF16)
            vt3_ref[j] = vt[:, j * TK:(j + 1) * TK]

    assert COL_CHUNK == RET_W
    for c in range(MAIN_W // COL_CHUNK):
        cols = slice(c * COL_CHUNK, (c + 1) * COL_CHUNK)
        y = jnp.dot(a, w_ref[:, cols], preferred_element_type=F32)
        if c == 0:
            y = rot(y)
        elif c == 1:
            y = rot(y) * (RET_DK ** -0.5)
        main_ref[:, cols] = y


def _inproj(x2, gain, w_perm, cos_t, sin_t, tm, key_tiles):
    rows = x2.shape[0]
    grid = (rows // tm,)
    t_len = cos_t.shape[0]
    if t_len < tm:
        assert tm % t_len == 0
        cos_t, sin_t = jnp.tile(cos_t, (tm // t_len, 1)), jnp.tile(sin_t, (tm // t_len, 1))
        t_len = tm
    assert t_len % tm == 0
    tab_spec = pl.BlockSpec((tm, LANE), lambda i: (i % (t_len // tm), 0))
    row_spec = lambda w: pl.BlockSpec((tm, w), lambda i: (i, 0))
    out_shape = (
        jax.ShapeDtypeStruct((rows, MAIN_W), F32),
        jax.ShapeDtypeStruct((rows, ATT_KV_HEADS, ATT_HD), F32),
        jax.ShapeDtypeStruct((rows, ATT_KV_HEADS, ATT_HD), F32),
        jax.ShapeDtypeStruct((rows, IDX_DIM), F32),
        jax.ShapeDtypeStruct((rows, LANE), F32),
    )
    kv_spec = pl.BlockSpec((tm, ATT_KV_HEADS, ATT_HD), lambda i: (i, 0, 0))
    out_specs = (row_spec(MAIN_W), kv_spec, kv_spec, row_spec(IDX_DIM), row_spec(LANE))
    if key_tiles:
        t3 = lambda a, b: pl.BlockSpec((tm // TK, a, b), lambda i: (i, 0, 0))
        out_shape += (jax.ShapeDtypeStruct((rows // TK, TK, KV_W), BF16),
                      jax.ShapeDtypeStruct((rows // TK, TK, IDX_DIM), BF16),
                      jax.ShapeDtypeStruct((rows // TK, VT_ALL, TK), BF16))
        out_specs += (t3(TK, KV_W), t3(TK, IDX_DIM), t3(VT_ALL, TK))
    return pl.pallas_call(
        functools.partial(_inproj_kernel, tm=tm),
        grid=grid,
        in_specs=[row_spec(D_MODEL), _const_spec((1, D_MODEL)), _const_spec(w_perm.shape),
                  tab_spec, tab_spec],
        out_specs=out_specs,
        out_shape=out_shape,
        compiler_params=pltpu.CompilerParams(dimension_semantics=("parallel",),
                                             vmem_limit_bytes=VMEM_LIMIT),
        name="inproj",
    )(x2, gain, w_perm, cos_t, sin_t)


def _ret_gammas():
    return [1.0 - 2.0 ** (-5.0 - h) for h in range(RET_HEADS)]


def _ret_tables(sb, chunk):
    lg = np.log(np.array(_ret_gammas(), np.float64))
    t = np.arange(sb)
    ci = t // chunk
    diff = t[:, None] - t[None, :]
    same = ci[:, None] == ci[None, :]
    below = ci[None, :] < ci[:, None]
    expo = np.where(same, np.abs(diff), np.where(below, diff, 0)).astype(np.float64)
    dmat = np.exp(lg[:, None, None] * expo[None]) * (same | below)[None]
    qd = np.exp(lg[None, :] * (t + 1.0)[:, None])
    kd = np.exp(lg[None, :] * (sb - 1.0 - t)[:, None])
    qd = np.repeat(qd, RET_DK, axis=1)
    kd = np.repeat(kd, RET_DK, axis=1)
    return (jnp.asarray(dmat, F32), jnp.asarray(qd, F32), jnp.asarray(kd, F32),
            [float(math.exp(v * sb)) for v in lg])


def _ret_kernel(q_ref, k_ref, v_ref, g_ref, d_ref, qd_ref, kd_ref, s0_ref,
                o_ref, sfin_ref, s_scr, *, g_block):
    n = pl.program_id(1)

    @pl.when(n == 0)
    def _():
        s_scr[...] = s0_ref[0]

    q = q_ref[...]
    k = k_ref[...]
    v = v_ref[...].astype(BF16)
    gate = g_ref[...]
    qb = q.astype(BF16)
    kb = k.astype(BF16)
    qx = (q * qd_ref[...]).astype(BF16)
    kx = (k * kd_ref[...]).astype(BF16)
    outs = []
    for h in range(RET_HEADS):
        sl = slice(h * RET_DK, (h + 1) * RET_DK)
        s = lax.dot_general(qb[:, sl], kb[:, sl], NT_DIMS, preferred_element_type=F32)
        p = (s * d_ref[h]).astype(BF16)
        st = s_scr[h]
        o = jnp.dot(p, v[:, sl], preferred_element_type=F32)
        o = o + jnp.dot(qx[:, sl], st.astype(BF16), preferred_element_type=F32)
        s_scr[h] = g_block[h] * st + lax.dot_general(kx[:, sl], v[:, sl], TN_DIMS,
                                                     preferred_element_type=F32)
        o = o * lax.rsqrt(jnp.mean(o * o, axis=-1, keepdims=True) + EPS)
        outs.append(o)
    o_all = jnp.concatenate(outs, axis=1)
    o_ref[...] = gate * (1.0 / (1.0 + jnp.exp(-gate))) * o_all

    @pl.when(n == pl.num_programs(1) - 1)
    def _():
        sfin_ref[0] = s_scr[...]


def _retention(main, s0, batch, t_len, sb, chunk):
    nsb = t_len // sb
    dmat, qd, kd, g_block = _ret_tables(sb, chunk)
    col = lambda c: pl.BlockSpec((sb, RET_W), lambda b, n, c=c: (b * nsb + n, c))
    st_spec = pl.BlockSpec((1, RET_HEADS, RET_DK, RET_DV), lambda b, n: (b, 0, 0, 0))
    return pl.pallas_call(
        functools.partial(_ret_kernel, g_block=g_block),
        grid=(batch, nsb),
        in_specs=[col(0), col(1), col(2), col(3),
                  _const_spec(dmat.shape), _const_spec(qd.shape), _const_spec(kd.shape), st_spec],
        out_specs=(pl.BlockSpec((sb, RET_W), lambda b, n: (b * nsb + n, 0)), st_spec),
        out_shape=(jax.ShapeDtypeStruct((batch * t_len, RET_W), F32),
                   jax.ShapeDtypeStruct((batch, RET_HEADS, RET_DK, RET_DV), F32)),
        scratch_shapes=[pltpu.VMEM((RET_HEADS, RET_DK, RET_DV), F32)],
        compiler_params=pltpu.CompilerParams(dimension_semantics=("parallel", "arbitrary"),
                                             vmem_limit_bytes=VMEM_LIMIT),
        name="retention",
    )(main, main, main, main, dmat, qd, kd, s0)


def _attn_kernel(aq_ref, iq_ref, iw_ref, ki_ref, k_ref, vt_ref, bt_ref, o_ref,
                 s_scr, sb_scr, acc_scr, m_scr, p_scr, *, tq, nkt_static, last_valid, n_valid_q, n_sel,
                 max_search, n_blind):
    qblk = pl.program_id(1)
    nkt = qblk + 1 if nkt_static is None else nkt_static
    lane_f = lax.broadcasted_iota(jnp.int32, (1, tq), 1).astype(F32)
    lane_ok = lane_f < float(n_valid_q)
    klim = jnp.minimum((jnp.floor(lane_f * (1.0 / CHUNK)) + 1.0) * CHUNK, float(last_valid))
    krow = lax.broadcasted_iota(jnp.int32, (TK, tq), 0).astype(F32)
    adm_last = krow < klim
    nkt_f = nkt.astype(F32) if nkt_static is None else float(nkt)
    n_adm = (nkt_f - 1.0) * TK + klim
    k_target = jnp.minimum(float(n_sel), n_adm)

    def query_rows(ref):
        x = ref[...]
        if n_valid_q == tq:
            return x
        return jnp.concatenate([x, jnp.zeros((tq - n_valid_q, x.shape[1]), x.dtype)], axis=0)

    iq = query_rows(iq_ref)
    w_t = query_rows(iw_ref).T[IW_OFF:IW_OFF + IDX_HEADS, :] * (IDX_HEADS ** -0.5 * IDX_DIM ** -0.5)
    iq_all = jnp.concatenate(
        [iq[:, h * IDX_DIM:(h + 1) * IDX_DIM].astype(BF16) for h in range(IDX_HEADS)], axis=0)

    def score_tile(kt, n_tiles=1):
        ki_t = ki_ref[kt] if n_tiles == 1 else jnp.concatenate(
            [ki_ref[kt + j] for j in range(n_tiles)], axis=0)
        s_all = lax.dot_general(ki_t, iq_all, NT_DIMS, preferred_element_type=F32)
        acc = jnp.maximum(s_all[:, :tq], 0.0) * w_t[0:1, :]
        for h in range(1, IDX_HEADS):
            acc = acc + jnp.maximum(s_all[:, h * tq:(h + 1) * tq], 0.0) * w_t[h:h + 1, :]
        return acc

    def fold_rows(x, op=jnp.add):
        parts = [x[i * CNT_ROWS:(i + 1) * CNT_ROWS] for i in range(x.shape[0] // CNT_ROWS)]
        while len(parts) > 1:
            parts = [op(a, b) for a, b in zip(parts[0::2], parts[1::2])]
        return parts[0]

    def p1_step(kt, n_tiles, carry):
        rmax, rmin = carry
        sc = score_tile(kt, n_tiles)
        for j in range(n_tiles):
            s_scr[kt + j] = sc[j * TK:(j + 1) * TK]
            sb_scr[kt + j] = sc[j * TK:(j + 1) * TK].astype(BF16)
        return (jnp.maximum(rmax, jnp.max(sc, axis=0, keepdims=True)),
                jnp.minimum(rmin, jnp.min(sc, axis=0, keepdims=True)))

    def run_steps(n, step, carry):
        n4 = n // 4
        carry = lax.fori_loop(0, n4, lambda i, c: step(4 * i, 4, c), carry)
        carry = lax.fori_loop(0, (n - 4 * n4) // 2, lambda i, c: step(4 * n4, 2, c), carry)
        return lax.fori_loop(0, n % 2, lambda i, c: step(n - 1, 1, c), carry)

    init = (jnp.full((1, tq), -jnp.inf, F32), jnp.full((1, tq), jnp.inf, F32))
    rmax, rmin = run_steps(nkt - 1, p1_step, init)
    sc = jnp.where(adm_last, score_tile(nkt - 1), -jnp.inf)
    s_scr[nkt - 1] = sc
    sb_scr[nkt - 1] = sc.astype(BF16)
    rmax = jnp.maximum(rmax, jnp.max(sc, axis=0, keepdims=True))
    rmin = jnp.minimum(rmin, jnp.min(jnp.where(adm_last, sc, jnp.inf), axis=0, keepdims=True))

    def tile_loop(n, body, carry):
        carry = lax.fori_loop(0, n // 2, lambda i, c: body(2 * i + 1, body(2 * i, c)), carry)
        return lax.fori_loop(0, n % 2, lambda i, c: body(n - 1, c), carry)

    def count_where(pred):
        def body(kt, c):
            return c + fold_rows(jnp.where(pred(kt), 1.0, 0.0))
        c = tile_loop(nkt, body, jnp.zeros((CNT_ROWS, tq), F32))
        return jnp.sum(c, axis=0, keepdims=True)

    def count_ge(thr):
        return count_where(lambda kt: s_scr[kt] >= thr)

    def count_ge_rounded(thr_b):
        one, zero = jnp.ones((), BF16), jnp.zeros((), BF16)

        def body(kt, c):
            ind = jnp.where(sb_scr[kt] >= thr_b, one, zero)
            parts = [ind[i * BF16_ROWS:(i + 1) * BF16_ROWS] for i in range(TK // BF16_ROWS)]
            while len(parts) > 1:
                parts = [a + b for a, b in zip(parts[0::2], parts[1::2])]
            return c + parts[0].astype(F32)
        c = tile_loop(nkt, body, jnp.zeros((BF16_ROWS, tq), F32))
        return jnp.sum(c, axis=0, keepdims=True)

    span = jnp.maximum(jnp.maximum(rmax - rmin, jnp.abs(rmax)), 1e-30)
    hi0 = rmax + span * (2.0 ** -10)

    def max_below(bound, n_tiles):
        def body(kt, m):
            s = s_scr[kt]
            return jnp.maximum(m, fold_rows(jnp.where(s < bound, s, -jnp.inf), jnp.maximum))
        m = tile_loop(n_tiles, body, jnp.full((CNT_ROWS, tq), -jnp.inf, F32))
        return jnp.max(m, axis=0, keepdims=True)

    def plan(it, lo, hi, c_lo, c_hi, done, want_flag=True):
        width = hi - lo
        frac = (c_lo - k_target - 0.5) / (c_lo - c_hi)
        interp = lax.convert_element_type((it + 1) % 2, F32)
        mid_i = lo + width * (0.5 + interp * (frac - 0.5))
        mid_b = lo + 0.5 * width
        mid = jnp.where((mid_i > lo) & (mid_i < hi), mid_i, mid_b)
        if not want_flag:
            return mid, None
        active = done == 0.0
        narrow = jnp.logical_not((mid > lo) & (mid < hi)) | (width <= span * NARROW)
        wide = active & jnp.logical_not(narrow) & (c_lo - c_hi > SMALL_BRACKET)
        flag = jnp.max(jnp.where(wide, 5.0, jnp.where(active, 3.0, 0.0)))
        return mid, flag

    def accept(mid, c, exact, lo, hi, c_lo, c_hi, done):
        active = done == 0.0
        ge = c >= k_target
        up = active & ge
        dn = active & jnp.logical_not(ge)
        lo = jnp.where(up, mid, lo)
        c_lo = jnp.where(up, c, c_lo)
        hi = jnp.where(dn, mid, hi)
        c_hi = jnp.where(dn, c, c_hi)
        finished = (c_lo == k_target) if exact is None else (c_lo == k_target) | (exact & up)
        return lo, hi, c_lo, c_hi, jnp.where(finished, 1.0, done)

    def blind_body(it, st):
        lo, hi, c_lo, c_hi, done = st
        mid, _ = plan(it, lo, hi, c_lo, c_hi, done, want_flag=False)
        return accept(mid, count_ge(mid), None, lo, hi, c_lo, c_hi, done)

    def search_body(st):
        it, flag, lo, hi, c_lo, c_hi, done, mid = st
        exact = (jnp.zeros((1, tq), F32) + flag) < 4.0
        top = max_below(hi, jnp.where(flag < 4.0, nkt, 0))
        mid = jnp.where(exact, top, mid)
        lo, hi, c_lo, c_hi, done = accept(mid, count_ge(mid), exact, lo, hi, c_lo, c_hi, done)
        mid, flag = plan(it + 1, lo, hi, c_lo, c_hi, done)
        return it + 1, flag, lo, hi, c_lo, c_hi, done, mid

    c_zero = count_ge_rounded(jnp.zeros((1, tq), BF16))
    c_pos = count_ge_rounded(jnp.full((1, tq), F32_TINY, BF16))
    pos_side = c_pos >= k_target
    at_zero = jnp.logical_not(pos_side) & (c_zero >= k_target)
    pick = lambda p, z, n: jnp.where(pos_side, p, jnp.where(at_zero, z, n))
    lo0 = pick(F32_TINY, 0.0, rmin)
    hi0 = pick(hi0, F32_TINY, 0.0)
    c_lo0 = pick(c_pos, c_zero, n_adm)
    c_hi0 = pick(0.0, c_pos, c_zero)
    done0 = jnp.where((n_adm == k_target) | at_zero | jnp.logical_not(lane_ok), 1.0, 0.0)

    def coarse_body(it, st):
        lo, hi, c_lo, c_hi, live = st
        width = hi - lo
        frac = (c_lo - k_target - 0.5) / (c_lo - c_hi)
        interp = lax.convert_element_type((it + 1) % 2, F32)
        grid = lambda x: x.astype(BF16).astype(F32)
        mid_i = grid(lo + width * (0.5 + interp * (frac - 0.5)))
        mid_b = grid(lo + 0.5 * width)
        mid = jnp.where((mid_i > lo) & (mid_i < hi), mid_i, mid_b)
        ok = (mid > lo) & (mid < hi) & (live > 0.0)
        c = count_ge_rounded(mid.astype(BF16))
        ge = c >= k_target
        up = ok & ge
        dn = ok & jnp.logical_not(ge)
        return (jnp.where(up, mid, lo), jnp.where(dn, mid, hi), jnp.where(up, c, c_lo),
                jnp.where(dn, c, c_hi), jnp.where(ok, live, 0.0))

    lo1, hi0, c_lo0, c_hi0, _ = lax.fori_loop(
        0, N_COARSE, coarse_body, (lo0, hi0, c_lo0, c_hi0, 1.0 - done0))
    lo0 = jnp.where(lo1 != lo0, lo1 - jnp.abs(lo1) * BF16_STEP, lo0)
    c_lo0 = count_ge(lo0)
    done0 = jnp.where(c_lo0 == k_target, 1.0, done0)
    lo0, hi0, c_lo0, c_hi0, done0 = lax.fori_loop(
        0, n_blind, blind_body, (lo0, hi0, c_lo0, c_hi0, done0))
    mid0, flag0 = plan(jnp.int32(n_blind), lo0, hi0, c_lo0, c_hi0, done0)
    st0 = (jnp.int32(n_blind), flag0, lo0, hi0, c_lo0, c_hi0, done0, mid0)
    _, _, lo, hi, cnt_lo, _, _, _ = lax.while_loop(
        lambda st: (st[0] < max_search) & (st[1] > 0.0), search_body, st0)

    excess = jnp.max(jnp.where(lane_ok, cnt_lo - k_target, 0.0))

    @pl.when(excess > 0.0)
    def _():
        need = k_target - count_ge(hi)
        r = lax.broadcasted_iota(jnp.int32, (TK, TK), 0)
        c = lax.broadcasted_iota(jnp.int32, (TK, TK), 1)
        prefix = jnp.where(r >= c, 1.0, 0.0).astype(BF16)

        def drop_step(kt, n_tiles, before):
            tiles = [s_scr[kt + j] for j in range(n_tiles)]
            ties = [(s >= lo) & (s < hi) for s in tiles]
            ranks = [jnp.dot(prefix, jnp.where(t, 1.0, 0.0).astype(BF16),
                             preferred_element_type=F32) for t in ties]
            for j in range(n_tiles):
                rank = before + ranks[j]
                s_scr[kt + j] = jnp.where(ties[j] & (rank > need), -jnp.inf, tiles[j])
                before = rank[TK - 1:TK, :]
            return before

        run_steps(nkt, drop_step, jnp.zeros((1, tq), F32))

    aq = query_rows(aq_ref) * (ATT_HD ** -0.5)
    zeros_q = jnp.zeros((tq, ATT_HD), F32)
    q_pad = []
    for h in range(ATT_HEADS):
        qh = aq[:, h * ATT_HD:(h + 1) * ATT_HD]
        parts = [zeros_q] * ATT_KV_HEADS
        parts[h // HEADS_PER_KV] = qh
        q_pad.append(jnp.concatenate(parts, axis=1).astype(BF16))

    m_scr[...] = jnp.full(m_scr.shape, NEG, F32)
    acc_scr[...] = jnp.zeros(acc_scr.shape, F32)

    def col_max(x):
        parts = [x[i * BF16_ROWS:(i + 1) * BF16_ROWS] for i in range(x.shape[0] // BF16_ROWS)]
        while len(parts) > 1:
            parts = [jnp.maximum(a, b) for a, b in zip(parts[0::2], parts[1::2])]
        return jnp.max(parts[0].astype(F32), axis=0, keepdims=True)

    cat = lambda xs, axis: xs[0] if len(xs) == 1 else jnp.concatenate(xs, axis=axis)

    def logits(kt, n_tiles, near):
        k_t = cat([k_ref[kt + j] for j in range(n_tiles)], 0)
        s_t = cat([s_scr[kt + j] for j in range(n_tiles)], 0)
        neg_mask = jnp.where(s_t >= lo, 0.0, NEG).astype(BF16)
        out = []
        for h in range(ATT_HEADS):
            lg = lax.dot_general(k_t, q_pad[h], NT_DIMS, preferred_element_type=F32)
            if near:
                lg = lg + cat([bt_ref[h, 2 - n_tiles + j] for j in range(n_tiles)], 0)
            out.append(lg.astype(BF16) + neg_mask)
        return out

    def softmax_pv(kt, n_tiles, lgs):
        tiles = [kt + j for j in range(n_tiles)]
        rows = n_tiles * TK
        alphas = []
        for h in range(ATT_HEADS):
            g, r = divmod(h, HEADS_PER_KV)
            lg = lgs[h]
            m_old = m_scr[h:h + 1, :]
            m_new = jnp.maximum(m_old, col_max(lg))
            alphas.append(jnp.exp(m_old - m_new))
            p_scr[g, :rows, r * tq:(r + 1) * tq] = jnp.exp(lg - m_new.astype(BF16))
            m_scr[h:h + 1, :] = m_new
        for g in range(ATT_KV_HEADS):
            hs = range(g * HEADS_PER_KV, (g + 1) * HEADS_PER_KV)
            a_g = jnp.concatenate([alphas[h] for h in hs], axis=1)
            vt_g = cat([vt_ref[t, g * VT_ROWS:(g + 1) * VT_ROWS, :] for t in tiles], 1)
            acc_scr[g] = acc_scr[g] * a_g + jnp.dot(vt_g, p_scr[g, :rows, :],
                                                    preferred_element_type=F32)

    def attend(kt, n_tiles, near):
        softmax_pv(kt, n_tiles, logits(kt, n_tiles, near))

    n_far = jnp.maximum(nkt - 2, 0) if nkt_static is None else max(nkt - 2, 0)

    def far_body(i, c):
        attend(2 * i, 2, False)
        return c
    lax.fori_loop(0, n_far // 2, far_body, 0)

    def when(cond):
        return pl.when(cond) if nkt_static is None else (lambda f: f() if cond else None)

    @when(n_far % 2 == 1)
    def _():
        attend(n_far - 1, 1, False)

    @when(nkt >= 2)
    def _():
        attend(nkt - 2, 2, True)

    @when(nkt < 2)
    def _():
        attend(0, 1, True)

    outs = []
    for g in range(ATT_KV_HEADS):
        a = acc_scr[g]
        o_g = a[:ATT_HD] * (1.0 / a[ATT_HD:ATT_HD + 1])
        outs += [o_g[:, r * tq:(r + 1) * tq] for r in range(HEADS_PER_KV)]
    o_ref[...] = jnp.concatenate(outs, axis=0).T[:n_valid_q]


def _attention(aq_src, iq_src, iw, ki3, k3, vt3, bt, batch, n_qblk, tq, nkt_total, nkt_static,
               last_valid, n_valid_q, n_sel, aq_col, iq_col):
    kern = functools.partial(_attn_kernel, tq=tq, nkt_static=nkt_static, last_valid=last_valid,
                             n_valid_q=n_valid_q, n_sel=n_sel, max_search=MAX_SEARCH,
                             n_blind=max(N_BLIND, int(math.log2(nkt_total * TK)) - BLIND_SLACK))
    qspec = lambda c: pl.BlockSpec((n_valid_q, ATT_W), lambda b, q, c=c: (b * n_qblk + q, c))
    kspec = lambda a, c: pl.BlockSpec((nkt_total, a, c), lambda b, q: (b, 0, 0))
    return pl.pallas_call(
        kern,
        grid=(batch, n_qblk),
        in_specs=[qspec(aq_col), qspec(iq_col),
                  pl.BlockSpec((n_valid_q, LANE), lambda b, q: (b * n_qblk + q, 0)),
                  kspec(TK, IDX_DIM), kspec(TK, KV_W), kspec(VT_ALL, TK),
                  _const_spec(bt.shape)],
        out_specs=pl.BlockSpec((n_valid_q, ATT_W), lambda b, q: (b * n_qblk + q, 0)),
        out_shape=jax.ShapeDtypeStruct((batch * n_qblk * n_valid_q, ATT_W), F32),
        scratch_shapes=[pltpu.VMEM((nkt_total, TK, tq), F32),
                        pltpu.VMEM((nkt_total, TK, tq), BF16),
                        pltpu.VMEM((ATT_KV_HEADS, VT_ROWS, HEADS_PER_KV * tq), F32),
                        pltpu.VMEM((ATT_HEADS, tq), F32),
                        pltpu.VMEM((ATT_KV_HEADS, 2 * TK, HEADS_PER_KV * tq), BF16)],
        compiler_params=pltpu.CompilerParams(dimension_semantics=("parallel", "arbitrary"),
                                             vmem_limit_bytes=VMEM_LIMIT),
        name="attention",
    )(aq_src, iq_src, iw, ki3, k3, vt3, bt)


def _ffn_kernel(x_ref, oret_ref, oatt_ref, wout_ref, wup_ref, wdown_ref,
                gpost_ref, gpre_ref, gffn_ref, y_ref, *, ff_chunk):
    mix = jnp.dot(oret_ref[...].astype(BF16), wout_ref[:RET_W, :], preferred_element_type=F32)
    mix = mix + jnp.dot(oatt_ref[...].astype(BF16), wout_ref[RET_W:, :], preferred_element_type=F32)
    h = x_ref[...] + _rms(mix, gpost_ref[...])
    a = _rms(h, gpre_ref[...]).astype(BF16)
    f = jnp.zeros(h.shape, F32)
    for c in range(D_FF // ff_chunk):
        sl = slice(c * ff_chunk, (c + 1) * ff_chunk)
        u = jnp.dot(a, wup_ref[:, sl], preferred_element_type=F32)
        u = jnp.square(jnp.maximum(u, 0.0)).astype(BF16)
        f = f + jnp.dot(u, wdown_ref[sl, :], preferred_element_type=F32)
    y_ref[...] = h + _rms(f, gffn_ref[...])


def _out_ffn(x2, o_ret, o_att, w_out, w_up, w_down, g_post, g_pre, g_ffn, tm):
    rows = x2.shape[0]
    row_spec = lambda w: pl.BlockSpec((tm, w), lambda i: (i, 0))
    return pl.pallas_call(
        functools.partial(_ffn_kernel, ff_chunk=COL_CHUNK),
        grid=(rows // tm,),
        in_specs=[row_spec(D_MODEL), row_spec(RET_W), row_spec(ATT_W),
                  _const_spec(w_out.shape), _const_spec(w_up.shape), _const_spec(w_down.shape),
                  _const_spec((1, D_MODEL)), _const_spec((1, D_MODEL)), _const_spec((1, D_MODEL))],
        out_specs=row_spec(D_MODEL),
        out_shape=jax.ShapeDtypeStruct((rows, D_MODEL), F32),
        compiler_params=pltpu.CompilerParams(dimension_semantics=("parallel",),
                                             vmem_limit_bytes=VMEM_LIMIT),
        name="out_ffn",
    )(x2, o_ret, o_att, w_out, w_up, w_down, g_post, g_pre, g_ffn)


def _t5_bucket(rel):
    half = N_BUCKETS // 2
    max_exact = half // 2
    ret = jnp.where(rel > 0, half, 0)
    n = jnp.abs(rel)
    nf = jnp.maximum(n, 1).astype(F32)
    large = max_exact + (jnp.log(nf / max_exact) / math.log(MAX_DISTANCE / max_exact)
                         * (half - max_exact)).astype(jnp.int32)
    large = jnp.minimum(large, half - 1)
    return ret + jnp.where(n < max_exact, n, large)


def _bias_tiles(rel_bias, tq):
    period = 2 * TK + tq
    rel = jnp.arange(period, dtype=jnp.int32) - (TK + tq - 1)
    table = rel_bias.astype(F32)
    far = table[_t5_bucket(jnp.int32(-MAX_DISTANCE))]
    u = (table[_t5_bucket(jnp.clip(rel, -MAX_DISTANCE, MAX_DISTANCE))] - far).T
    n = jnp.tile(u, (1, tq))[:, :tq * (period - 1)].reshape(ATT_HEADS, tq, period - 1)
    m = n[:, :, tq - 1:tq - 1 + 2 * TK]
    return jnp.transpose(m.reshape(ATT_HEADS, tq, 2, TK), (0, 2, 3, 1))


def _rope_tables(pos):
    half = RET_DK // 2
    inv = ROPE_BASE ** (-jnp.arange(half, dtype=F32) / half)
    ang = pos.astype(F32)[:, None] * inv[None, :]
    cos, sin = jnp.cos(ang), jnp.sin(ang)
    cos_h = jnp.concatenate([cos, cos], axis=1)
    sin_h = jnp.concatenate([-sin, sin], axis=1)
    reps = LANE // RET_DK
    return jnp.tile(cos_h, (1, reps)), jnp.tile(sin_h, (1, reps))


def _permute_w_in(w_in):
    offs = np.cumsum([0, RET_W, RET_W, RET_W, RET_W, ATT_W, KV_W, KV_W, IDX_W, IDX_DIM, IDX_HEADS])
    seg = lambda i: w_in[:, offs[i]:offs[i + 1]]
    pad = jnp.zeros((D_MODEL, LANE - IDX_DIM - IDX_HEADS), w_in.dtype)
    return jnp.concatenate([seg(0), seg(1), seg(2), seg(3), seg(4), seg(7),
                            seg(5), seg(6), seg(8), seg(9), pad], axis=1).astype(BF16)


def _layer(x, s0, past, weights, bias_tiles):
    w_perm, w_out, w_up, w_down, g_pre_mix, g_post_mix, g_pre_ffn, g_post_ffn = weights
    batch, t_len, _ = x.shape
    rows = batch * t_len
    x2 = x.reshape(rows, D_MODEL)
    tm = min(ROW_TILE, rows)
    p_len = 0 if past is None else past[0].shape[1]
    pos = p_len + jnp.arange(t_len, dtype=jnp.int32)
    cos_t, sin_t = _rope_tables(pos)
    main, kc, vc, ic, iw, *key_tiles = _inproj(x2, g_pre_mix, w_perm, cos_t, sin_t, tm,
                                               key_tiles=past is None)

    chunk = min(CHUNK, t_len)
    sb = min(RET_BLOCK, t_len)
    o_ret, s_new = _retention(main, s0, batch, t_len, sb, chunk)

    l_all = p_len + t_len
    n_sel = min(TOPK_MAX, l_all // 4)
    if past is None:
        tq = TK
        n_qblk = t_len // tq
        nkt_total = t_len // TK
        bt = bias_tiles
        k3, ki3, vt3 = key_tiles
        o_att = _attention(main, main, iw, ki3, k3, vt3, bt, batch, n_qblk, tq, nkt_total, None,
                           TK, tq, n_sel, aq_col=4, iq_col=5)
    else:
        tq = LANE
        pk, pv, pi = past
        assert p_len % TK == 0 and t_len <= min(tq, TK), (p_len, t_len)
        nkt_total = p_len // TK + 1
        padk = nkt_total * TK - l_all
        cat = lambda old, new: jnp.concatenate(
            [old.astype(BF16), new.reshape(batch, t_len, -1).astype(BF16),
             jnp.zeros((batch, padk, old.shape[-1]), BF16)], axis=1)
        k_all = cat(pk.reshape(batch, p_len, KV_W), kc)
        v_all = cat(pv.reshape(batch, p_len, KV_W), vc)
        i_all = cat(pi, ic)
        v_t = jnp.transpose(v_all.reshape(batch, nkt_total, TK, ATT_KV_HEADS, ATT_HD), (0, 1, 3, 4, 2))
        ones = jnp.zeros((batch, nkt_total, ATT_KV_HEADS, VT_ROWS - ATT_HD, TK), BF16).at[:, :, :, 0, :].set(1.0)
        vt_s = jnp.concatenate([v_t, ones], axis=3).reshape(batch * nkt_total, VT_ALL, TK)
        k_s = k_all.reshape(batch * nkt_total, TK, KV_W)
        i_s = i_all.reshape(batch * nkt_total, TK, IDX_DIM)
        bt = bias_tiles[..., :tq]
        o_att = _attention(main, main, iw, i_s, k_s, vt_s, bt, batch, 1, tq, nkt_total, nkt_total,
                           l_all - p_len, t_len, n_sel, aq_col=4, iq_col=5)

    y = _out_ffn(x2, o_ret, o_att, w_out, w_up, w_down, g_post_mix, g_pre_ffn, g_post_ffn, tm)
    return (y.reshape(batch, t_len, D_MODEL), s_new,
            kc.reshape(batch, t_len, ATT_KV_HEADS, ATT_HD),
            vc.reshape(batch, t_len, ATT_KV_HEADS, ATT_HD),
            ic.reshape(batch, t_len, IDX_DIM))


def kernel(x_prompt, x_sample, state_ret, cache_k, cache_v, cache_kidx, w_in, w_out, w_up, w_down,
           g_pre_mix, g_post_mix, g_pre_ffn, g_post_ffn, rel_bias):
    depth = w_in.shape[0]
    bp = x_prompt.shape[0]
    zero_state = jnp.zeros((bp, RET_HEADS, RET_DK, RET_DV), F32)
    bias_tiles = _bias_tiles(rel_bias, TK)
    yp, ys = x_prompt, x_sample
    outs_p, outs_s = [], []
    for l in range(depth):
        row = lambda g: g[l].reshape(1, D_MODEL).astype(F32)
        weights = (_permute_w_in(w_in[l]), w_out[l].astype(BF16), w_up[l].astype(BF16),
                   w_down[l].astype(BF16), row(g_pre_mix), row(g_post_mix), row(g_pre_ffn),
                   row(g_post_ffn))
        yp, *rest_p = _layer(yp, zero_state, None, weights, bias_tiles)
        ys, *rest_s = _layer(ys, state_ret[l], (cache_k[l], cache_v[l], cache_kidx[l]), weights,
                             bias_tiles)
        outs_p.append(rest_p)
        outs_s.append(rest_s)
    stack = lambda outs, i: jnp.stack([o[i] for o in outs])
    return (yp, ys,
            stack(outs_p, 0), stack(outs_p, 1), stack(outs_p, 2), stack(outs_p, 3),
            stack(outs_s, 0), stack(outs_s, 1), stack(outs_s, 2), stack(outs_s, 3))
```

```python
import functools
import math

import jax
import jax.numpy as jnp
import numpy as np
from jax import lax
from jax.experimental import pallas as pl
from jax.experimental.pallas import tpu as pltpu

D_MODEL = 1024
CHUNK = 64
RET_HEADS = 8
RET_DK = 64
RET_DV = 64
ATT_HEADS = 8
ATT_KV_HEADS = 2
ATT_HD = 64
IDX_HEADS = 8
IDX_DIM = 64
TOPK_MAX = 256
N_BUCKETS = 32
MAX_DISTANCE = 128
D_FF = 4 * D_MODEL
ROPE_BASE = 10000.0
EPS = 1e-6

RET_W = RET_HEADS * RET_DV
ATT_W = ATT_HEADS * ATT_HD
KV_W = ATT_KV_HEADS * ATT_HD
IDX_W = IDX_HEADS * IDX_DIM
HEADS_PER_KV = ATT_HEADS // ATT_KV_HEADS

MAIN_W = 4 * RET_W + ATT_W + IDX_W
TAIL_W = 3 * 128
IW_OFF = IDX_DIM

LANE = 128
VT_ROWS = 80
VT_ALL = ATT_KV_HEADS * VT_ROWS
TK = 256
ROW_TILE = 512
COL_CHUNK = 512
RET_BLOCK = 512
NEG = -(2.0 ** 100)
BF16_ROWS = 16
CNT_ROWS = 32
NARROW = 2.0 ** -20
MAX_SEARCH = 96
F32_TINY = float(np.finfo(np.float32).tiny)
N_COARSE = 8
SMALL_BRACKET = 2.0
N_BLIND = 4
BLIND_SLACK = 6
BF16_STEP = 2.0 ** -7
VMEM_LIMIT = 56 * 1024 * 1024

F32 = jnp.float32
BF16 = jnp.bfloat16
NT_DIMS = (((1,), (1,)), ((), ()))
TN_DIMS = (((0,), (0,)), ((), ()))


def _const_spec(shape):
    nd = len(shape)
    return pl.BlockSpec(shape, lambda *_: (0,) * nd, pipeline_mode=pl.Buffered(1))


def _rms(x, gain):
    return x * lax.rsqrt(jnp.mean(x * x, axis=-1, keepdims=True) + EPS) * gain


def _inproj_kernel(x_ref, g_ref, w_ref, cos_ref, sin_ref, main_ref, kc_ref, vc_ref, ic_ref, iw_ref,
                   *tile_refs, tm):
    a = _rms(x_ref[...], g_ref[...]).astype(BF16)
    reps = RET_W // LANE
    cos = jnp.concatenate([cos_ref[...]] * reps, axis=1)
    sin = jnp.concatenate([sin_ref[...]] * reps, axis=1)
    lane = lax.broadcasted_iota(jnp.int32, cos.shape, 1)
    first_half = (lane & (RET_DK - 1)) < RET_DK // 2

    def rot(x):
        partner = jnp.where(first_half, pltpu.roll(x, RET_W - RET_DK // 2, 1),
                            pltpu.roll(x, RET_DK // 2, 1))
        return x * cos + partner * sin

    tail = jnp.dot(a, w_ref[:, MAIN_W:MAIN_W + TAIL_W], preferred_element_type=F32)
    ak = tail[:, 0:KV_W]
    av = tail[:, KV_W:2 * KV_W]
    last = tail[:, 2 * KV_W:3 * KV_W]
    for g in range(ATT_KV_HEADS):
        kc_ref[:, g, :] = ak[:, g * ATT_HD:(g + 1) * ATT_HD]
        vc_ref[:, g, :] = av[:, g * ATT_HD:(g + 1) * ATT_HD]
    ic_ref[...] = last[:, :IDX_DIM]
    iw_ref[...] = last
    if tile_refs:
        k3_ref, ki3_ref, vt3_ref = tile_refs
        avt = av.T
        row = lax.broadcasted_iota(jnp.int32, (VT_ROWS - ATT_HD, tm), 0)
        ones_rows = jnp.where(row == 0, 1.0, 0.0).astype(F32)
        vt = jnp.concatenate([avt[:ATT_HD], ones_rows, avt[ATT_HD:], ones_rows], axis=0).astype(BF16)
        for j in range(tm // TK):
            k3_ref[j] = ak[j * TK:(j + 1) * TK].astype(BF16)
            ki3_ref[j] = last[j * TK:(j + 1) * TK, :IDX_DIM].astype(BF16)
            vt3_ref[j] = vt[:, j * TK:(j + 1) * TK]

    assert COL_CHUNK == RET_W
    for c in range(MAIN_W // COL_CHUNK):
        cols = slice(c * COL_CHUNK, (c + 1) * COL_CHUNK)
        y = jnp.dot(a, w_ref[:, cols], preferred_element_type=F32)
        if c == 0:
            y = rot(y)
        elif c == 1:
            y = rot(y) * (RET_DK ** -0.5)
        main_ref[:, cols] = y


def _inproj(x2, gain, w_perm, cos_t, sin_t, tm, key_tiles):
    rows = x2.shape[0]
    grid = (rows // tm,)
    t_len = cos_t.shape[0]
    if t_len < tm:
        assert tm % t_len == 0
        cos_t, sin_t = jnp.tile(cos_t, (tm // t_len, 1)), jnp.tile(sin_t, (tm // t_len, 1))
        t_len = tm
    assert t_len % tm == 0
    tab_spec = pl.BlockSpec((tm, LANE), lambda i: (i % (t_len // tm), 0))
    row_spec = lambda w: pl.BlockSpec((tm, w), lambda i: (i, 0))
    out_shape = (
        jax.ShapeDtypeStruct((rows, MAIN_W), F32),
        jax.ShapeDtypeStruct((rows, ATT_KV_HEADS, ATT_HD), F32),
        jax.ShapeDtypeStruct((rows, ATT_KV_HEADS, ATT_HD), F32),
        jax.ShapeDtypeStruct((rows, IDX_DIM), F32),
        jax.ShapeDtypeStruct((rows, LANE), F32),
    )
    kv_spec = pl.BlockSpec((tm, ATT_KV_HEADS, ATT_HD), lambda i: (i, 0, 0))
    out_specs = (row_spec(MAIN_W), kv_spec, kv_spec, row_spec(IDX_DIM), row_spec(LANE))
    if key_tiles:
        t3 = lambda a, b: pl.BlockSpec((tm // TK, a, b), lambda i: (i, 0, 0))
        out_shape += (jax.ShapeDtypeStruct((rows // TK, TK, KV_W), BF16),
                      jax.ShapeDtypeStruct((rows // TK, TK, IDX_DIM), BF16),
                      jax.ShapeDtypeStruct((rows // TK, VT_ALL, TK), BF16))
        out_specs += (t3(TK, KV_W), t3(TK, IDX_DIM), t3(VT_ALL, TK))
    return pl.pallas_call(
        functools.partial(_inproj_kernel, tm=tm),
        grid=grid,
        in_specs=[row_spec(D_MODEL), _const_spec((1, D_MODEL)), _const_spec(w_perm.shape),
                  tab_spec, tab_spec],
        out_specs=out_specs,
        out_shape=out_shape,
        compiler_params=pltpu.CompilerParams(dimension_semantics=("parallel",),
                                             vmem_limit_bytes=VMEM_LIMIT),
        name="inproj",
    )(x2, gain, w_perm, cos_t, sin_t)


def _ret_gammas():
    return [1.0 - 2.0 ** (-5.0 - h) for h in range(RET_HEADS)]


def _ret_tables(sb, chunk):
    lg = np.log(np.array(_ret_gammas(), np.float64))
    t = np.arange(sb)
    ci = t // chunk
    diff = t[:, None] - t[None, :]
    same = ci[:, None] == ci[None, :]
    below = ci[None, :] < ci[:, None]
    expo = np.where(same, np.abs(diff), np.where(below, diff, 0)).astype(np.float64)
    dmat = np.exp(lg[:, None, None] * expo[None]) * (same | below)[None]
    qd = np.exp(lg[None, :] * (t + 1.0)[:, None])
    kd = np.exp(lg[None, :] * (sb - 1.0 - t)[:, None])
    qd = np.repeat(qd, RET_DK, axis=1)
    kd = np.repeat(kd, RET_DK, axis=1)
    return (jnp.asarray(dmat, F32), jnp.asarray(qd, F32), jnp.asarray(kd, F32),
            [float(math.exp(v * sb)) for v in lg])


def _ret_kernel(q_ref, k_ref, v_ref, g_ref, d_ref, qd_ref, kd_ref, s0_ref,
                o_ref, sfin_ref, s_scr, *, g_block):
    n = pl.program_id(1)

    @pl.when(n == 0)
    def _():
        s_scr[...] = s0_ref[0]

    q = q_ref[...]
    k = k_ref[...]
    v = v_ref[...].astype(BF16)
    gate = g_ref[...]
    qb = q.astype(BF16)
    kb = k.astype(BF16)
    qx = (q * qd_ref[...]).astype(BF16)
    kx = (k * kd_ref[...]).astype(BF16)
    outs = []
    for h in range(RET_HEADS):
        sl = slice(h * RET_DK, (h + 1) * RET_DK)
        s = lax.dot_general(qb[:, sl], kb[:, sl], NT_DIMS, preferred_element_type=F32)
        p = (s * d_ref[h]).astype(BF16)
        st = s_scr[h]
        o = jnp.dot(p, v[:, sl], preferred_element_type=F32)
        o = o + jnp.dot(qx[:, sl], st.astype(BF16), preferred_element_type=F32)
        s_scr[h] = g_block[h] * st + lax.dot_general(kx[:, sl], v[:, sl], TN_DIMS,
                                                     preferred_element_type=F32)
        o = o * lax.rsqrt(jnp.mean(o * o, axis=-1, keepdims=True) + EPS)
        outs.append(o)
    o_all = jnp.concatenate(outs, axis=1)
    o_ref[...] = gate * (1.0 / (1.0 + jnp.exp(-gate))) * o_all

    @pl.when(n == pl.num_programs(1) - 1)
    def _():
        sfin_ref[0] = s_scr[...]


def _retention(main, s0, batch, t_len, sb, chunk):
    nsb = t_len // sb
    dmat, qd, kd, g_block = _ret_tables(sb, chunk)
    col = lambda c: pl.BlockSpec((sb, RET_W), lambda b, n, c=c: (b * nsb + n, c))
    st_spec = pl.BlockSpec((1, RET_HEADS, RET_DK, RET_DV), lambda b, n: (b, 0, 0, 0))
    return pl.pallas_call(
        functools.partial(_ret_kernel, g_block=g_block),
        grid=(batch, nsb),
        in_specs=[col(0), col(1), col(2), col(3),
                  _const_spec(dmat.shape), _const_spec(qd.shape), _const_spec(kd.shape), st_spec],
        out_specs=(pl.BlockSpec((sb, RET_W), lambda b, n: (b * nsb + n, 0)), st_spec),
        out_shape=(jax.ShapeDtypeStruct((batch * t_len, RET_W), F32),
                   jax.ShapeDtypeStruct((batch, RET_HEADS, RET_DK, RET_DV), F32)),
        scratch_shapes=[pltpu.VMEM((RET_HEADS, RET_DK, RET_DV), F32)],
        compiler_params=pltpu.CompilerParams(dimension_semantics=("parallel", "arbitrary"),
                                             vmem_limit_bytes=VMEM_LIMIT),
        name="retention",
    )(main, main, main, main, dmat, qd, kd, s0)


def _attn_kernel(aq_ref, iq_ref, iw_ref, ki_ref, k_ref, vt_ref, bt_ref, o_ref,
                 s_scr, sb_scr, acc_scr, m_scr, p_scr, *, tq, nkt_static, last_valid, n_valid_q, n_sel,
                 max_search, n_blind):
    qblk = pl.program_id(1)
    nkt = qblk + 1 if nkt_static is None else nkt_static
    lane_f = lax.broadcasted_iota(jnp.int32, (1, tq), 1).astype(F32)
    lane_ok = lane_f < float(n_valid_q)
    klim = jnp.minimum((jnp.floor(lane_f * (1.0 / CHUNK)) + 1.0) * CHUNK, float(last_valid))
    krow = lax.broadcasted_iota(jnp.int32, (TK, tq), 0).astype(F32)
    adm_last = krow < klim
    nkt_f = nkt.astype(F32) if nkt_static is None else float(nkt)
    n_adm = (nkt_f - 1.0) * TK + klim
    k_target = jnp.minimum(float(n_sel), n_adm)

    def query_rows(ref):
        x = ref[...]
        if n_valid_q == tq:
            return x
        return jnp.concatenate([x, jnp.zeros((tq - n_valid_q, x.shape[1]), x.dtype)], axis=0)

    iq = query_rows(iq_ref)
    w_t = query_rows(iw_ref).T[IW_OFF:IW_OFF + IDX_HEADS, :] * (IDX_HEADS ** -0.5 * IDX_DIM ** -0.5)
    iq_all = jnp.concatenate(
        [iq[:, h * IDX_DIM:(h + 1) * IDX_DIM].astype(BF16) for h in range(IDX_HEADS)], axis=0)

    def score_tile(kt, n_tiles=1):
        ki_t = ki_ref[kt] if n_tiles == 1 else jnp.concatenate(
            [ki_ref[kt + j] for j in range(n_tiles)], axis=0)
        s_all = lax.dot_general(ki_t, iq_all, NT_DIMS, preferred_element_type=F32)
        acc = jnp.maximum(s_all[:, :tq], 0.0) * w_t[0:1, :]
        for h in range(1, IDX_HEADS):
            acc = acc + jnp.maximum(s_all[:, h * tq:(h + 1) * tq], 0.0) * w_t[h:h + 1, :]
        return acc

    def fold_rows(x, op=jnp.add):
        parts = [x[i * CNT_ROWS:(i + 1) * CNT_ROWS] for i in range(x.shape[0] // CNT_ROWS)]
        while len(parts) > 1:
            parts = [op(a, b) for a, b in zip(parts[0::2], parts[1::2])]
        return parts[0]

    def p1_step(kt, n_tiles, carry):
        rmax, rmin = carry
        sc = score_tile(kt, n_tiles)
        for j in range(n_tiles):
            s_scr[kt + j] = sc[j * TK:(j + 1) * TK]
            sb_scr[kt + j] = sc[j * TK:(j + 1) * TK].astype(BF16)
        return (jnp.maximum(rmax, jnp.max(sc, axis=0, keepdims=True)),
                jnp.minimum(rmin, jnp.min(sc, axis=0, keepdims=True)))

    def run_steps(n, step, carry):
        n4 = n // 4
        carry = lax.fori_loop(0, n4, lambda i, c: step(4 * i, 4, c), carry)
        carry = lax.fori_loop(0, (n - 4 * n4) // 2, lambda i, c: step(4 * n4, 2, c), carry)
        return lax.fori_loop(0, n % 2, lambda i, c: step(n - 1, 1, c), carry)

    init = (jnp.full((1, tq), -jnp.inf, F32), jnp.full((1, tq), jnp.inf, F32))
    rmax, rmin = run_steps(nkt - 1, p1_step, init)
    sc = jnp.where(adm_last, score_tile(nkt - 1), -jnp.inf)
    s_scr[nkt - 1] = sc
    sb_scr[nkt - 1] = sc.astype(BF16)
    rmax = jnp.maximum(rmax, jnp.max(sc, axis=0, keepdims=True))
    rmin = jnp.minimum(rmin, jnp.min(jnp.where(adm_last, sc, jnp.inf), axis=0, keepdims=True))

    def tile_loop(n, body, carry):
        carry = lax.fori_loop(0, n // 2, lambda i, c: body(2 * i + 1, body(2 * i, c)), carry)
        return lax.fori_loop(0, n % 2, lambda i, c: body(n - 1, c), carry)

    def count_where(pred):
        def body(kt, c):
            return c + fold_rows(jnp.where(pred(kt), 1.0, 0.0))
        c = tile_loop(nkt, body, jnp.zeros((CNT_ROWS, tq), F32))
        return jnp.sum(c, axis=0, keepdims=True)

    def count_ge(thr):
        return count_where(lambda kt: s_scr[kt] >= thr)

    def count_ge_rounded(thr_b):
        one, zero = jnp.ones((), BF16), jnp.zeros((), BF16)

        def body(kt, c):
            ind = jnp.where(sb_scr[kt] >= thr_b, one, zero)
            parts = [ind[i * BF16_ROWS:(i + 1) * BF16_ROWS] for i in range(TK // BF16_ROWS)]
            while len(parts) > 1:
                parts = [a + b for a, b in zip(parts[0::2], parts[1::2])]
            return c + parts[0].astype(F32)
        c = tile_loop(nkt, body, jnp.zeros((BF16_ROWS, tq), F32))
        return jnp.sum(c, axis=0, keepdims=True)

    span = jnp.maximum(jnp.maximum(rmax - rmin, jnp.abs(rmax)), 1e-30)
    hi0 = rmax + span * (2.0 ** -10)

    def max_below(bound, n_tiles):
        def body(kt, m):
            s = s_scr[kt]
            return jnp.maximum(m, fold_rows(jnp.where(s < bound, s, -jnp.inf), jnp.maximum))
        m = tile_loop(n_tiles, body, jnp.full((CNT_ROWS, tq), -jnp.inf, F32))
        return jnp.max(m, axis=0, keepdims=True)

    def plan(it, lo, hi, c_lo, c_hi, done, want_flag=True):
        width = hi - lo
        frac = (c_lo - k_target - 0.5) / (c_lo - c_hi)
        interp = lax.convert_element_type((it + 1) % 2, F32)
        mid_i = lo + width * (0.5 + interp * (frac - 0.5))
        mid_b = lo + 0.5 * width
        mid = jnp.where((mid_i > lo) & (mid_i < hi), mid_i, mid_b)
        if not want_flag:
            return mid, None
        active = done == 0.0
        narrow = jnp.logical_not((mid > lo) & (mid < hi)) | (width <= span * NARROW)
        wide = active & jnp.logical_not(narrow) & (c_lo - c_hi > SMALL_BRACKET)
        flag = jnp.max(jnp.where(wide, 5.0, jnp.where(active, 3.0, 0.0)))
        return mid, flag

    def accept(mid, c, exact, lo, hi, c_lo, c_hi, done):
        active = done == 0.0
        ge = c >= k_target
        up = active & ge
        dn = active & jnp.logical_not(ge)
        lo = jnp.where(up, mid, lo)
        c_lo = jnp.where(up, c, c_lo)
        hi = jnp.where(dn, mid, hi)
        c_hi = jnp.where(dn, c, c_hi)
        finished = (c_lo == k_target) if exact is None else (c_lo == k_target) | (exact & up)
        return lo, hi, c_lo, c_hi, jnp.where(finished, 1.0, done)

    def blind_body(it, st):
        lo, hi, c_lo, c_hi, done = st
        mid, _ = plan(it, lo, hi, c_lo, c_hi, done, want_flag=False)
        return accept(mid, count_ge(mid), None, lo, hi, c_lo, c_hi, done)

    def search_body(st):
        it, flag, lo, hi, c_lo, c_hi, done, mid = st
        exact = (jnp.zeros((1, tq), F32) + flag) < 4.0
        top = max_below(hi, jnp.where(flag < 4.0, nkt, 0))
        mid = jnp.where(exact, top, mid)
        lo, hi, c_lo, c_hi, done = accept(mid, count_ge(mid), exact, lo, hi, c_lo, c_hi, done)
        mid, flag = plan(it + 1, lo, hi, c_lo, c_hi, done)
        return it + 1, flag, lo, hi, c_lo, c_hi, done, mid

    c_zero = count_ge_rounded(jnp.zeros((1, tq), BF16))
    c_pos = count_ge_rounded(jnp.full((1, tq), F32_TINY, BF16))
    pos_side = c_pos >= k_target
    at_zero = jnp.logical_not(pos_side) & (c_zero >= k_target)
    pick = lambda p, z, n: jnp.where(pos_side, p, jnp.where(at_zero, z, n))
    lo0 = pick(F32_TINY, 0.0, rmin)
    hi0 = pick(hi0, F32_TINY, 0.0)
    c_lo0 = pick(c_pos, c_zero, n_adm)
    c_hi0 = pick(0.0, c_pos, c_zero)
    done0 = jnp.where((n_adm == k_target) | at_zero | jnp.logical_not(lane_ok), 1.0, 0.0)

    def coarse_body(it, st):
        lo, hi, c_lo, c_hi, live = st
        width = hi - lo
        frac = (c_lo - k_target - 0.5) / (c_lo - c_hi)
        interp = lax.convert_element_type((it + 1) % 2, F32)
        grid = lambda x: x.astype(BF16).astype(F32)
        mid_i = grid(lo + width * (0.5 + interp * (frac - 0.5)))
        mid_b = grid(lo + 0.5 * width)
        mid = jnp.where((mid_i > lo) & (mid_i < hi), mid_i, mid_b)
        ok = (mid > lo) & (mid < hi) & (live > 0.0)
        c = count_ge_rounded(mid.astype(BF16))
        ge = c >= k_target
        up = ok & ge
        dn = ok & jnp.logical_not(ge)
        return (jnp.where(up, mid, lo), jnp.where(dn, mid, hi), jnp.where(up, c, c_lo),
                jnp.where(dn, c, c_hi), jnp.where(ok, live, 0.0))

    lo1, hi0, c_lo0, c_hi0, _ = lax.fori_loop(
        0, N_COARSE, coarse_body, (lo0, hi0, c_lo0, c_hi0, 1.0 - done0))
    lo0 = jnp.where(lo1 != lo0, lo1 - jnp.abs(lo1) * BF16_STEP, lo0)
    c_lo0 = count_ge(lo0)
    done0 = jnp.where(c_lo0 == k_target, 1.0, done0)
    lo0, hi0, c_lo0, c_hi0, done0 = lax.fori_loop(
        0, n_blind, blind_body, (lo0, hi0, c_lo0, c_hi0, done0))
    mid0, flag0 = plan(jnp.int32(n_blind), lo0, hi0, c_lo0, c_hi0, done0)
    st0 = (jnp.int32(n_blind), flag0, lo0, hi0, c_lo0, c_hi0, done0, mid0)
    _, _, lo, hi, cnt_lo, _, _, _ = lax.while_loop(
        lambda st: (st[0] < max_search) & (st[1] > 0.0), search_body, st0)

    excess = jnp.max(jnp.where(lane_ok, cnt_lo - k_target, 0.0))

    @pl.when(excess > 0.0)
    def _():
        need = k_target - count_ge(hi)
        r = lax.broadcasted_iota(jnp.int32, (TK, TK), 0)
        c = lax.broadcasted_iota(jnp.int32, (TK, TK), 1)
        prefix = jnp.where(r >= c, 1.0, 0.0).astype(BF16)

        def drop_step(kt, n_tiles, before):
            tiles = [s_scr[kt + j] for j in range(n_tiles)]
            ties = [(s >= lo) & (s < hi) for s in tiles]
            ranks = [jnp.dot(prefix, jnp.where(t, 1.0, 0.0).astype(BF16),
                             preferred_element_type=F32) for t in ties]
            for j in range(n_tiles):
                rank = before + ranks[j]
                s_scr[kt + j] = jnp.where(ties[j] & (rank > need), -jnp.inf, tiles[j])
                before = rank[TK - 1:TK, :]
            return before

        run_steps(nkt, drop_step, jnp.zeros((1, tq), F32))

    aq = query_rows(aq_ref) * (ATT_HD ** -0.5)
    q_heads = [aq[:, h * ATT_HD:(h + 1) * ATT_HD].astype(BF16) for h in range(ATT_HEADS)]

    m_scr[...] = jnp.full(m_scr.shape, NEG, F32)
    acc_scr[...] = jnp.zeros(acc_scr.shape, F32)

    def col_max(x):
        parts = [x[i * BF16_ROWS:(i + 1) * BF16_ROWS] for i in range(x.shape[0] // BF16_ROWS)]
        while len(parts) > 1:
            parts = [jnp.maximum(a, b) for a, b in zip(parts[0::2], parts[1::2])]
        return jnp.max(parts[0].astype(F32), axis=0, keepdims=True)

    cat = lambda xs, axis: xs[0] if len(xs) == 1 else jnp.concatenate(xs, axis=axis)

    def logits(kt, n_tiles, near):
        k_t = cat([k_ref[kt + j] for j in range(n_tiles)], 0)
        s_t = cat([s_scr[kt + j] for j in range(n_tiles)], 0)
        neg_mask = jnp.where(s_t >= lo, 0.0, NEG).astype(BF16)
        out = []
        for h in range(ATT_HEADS):
            g = h // HEADS_PER_KV
            lg = lax.dot_general(k_t[:, g * ATT_HD:(g + 1) * ATT_HD], q_heads[h], NT_DIMS,
                                 preferred_element_type=F32)
            if near:
                lg = lg + cat([bt_ref[h, 2 - n_tiles + j] for j in range(n_tiles)], 0)
            out.append(lg.astype(BF16) + neg_mask)
        return out

    def softmax_pv(kt, n_tiles, lgs):
        tiles = [kt + j for j in range(n_tiles)]
        rows = n_tiles * TK
        alphas = []
        for h in range(ATT_HEADS):
            g, r = divmod(h, HEADS_PER_KV)
            lg = lgs[h]
            m_old = m_scr[h:h + 1, :]
            m_new = jnp.maximum(m_old, col_max(lg))
            alphas.append(jnp.exp(m_old - m_new))
            p_scr[g, :rows, r * tq:(r + 1) * tq] = jnp.exp(lg - m_new.astype(BF16))
            m_scr[h:h + 1, :] = m_new
        for g in range(ATT_KV_HEADS):
            hs = range(g * HEADS_PER_KV, (g + 1) * HEADS_PER_KV)
            a_g = jnp.concatenate([alphas[h] for h in hs], axis=1)
            vt_g = cat([vt_ref[t, g * VT_ROWS:(g + 1) * VT_ROWS, :] for t in tiles], 1)
            acc_scr[g] = acc_scr[g] * a_g + jnp.dot(vt_g, p_scr[g, :rows, :],
                                                    preferred_element_type=F32)

    def attend(kt, n_tiles, near):
        softmax_pv(kt, n_tiles, logits(kt, n_tiles, near))

    n_far = jnp.maximum(nkt - 2, 0) if nkt_static is None else max(nkt - 2, 0)

    def far_body(i, c):
        attend(2 * i, 2, False)
        return c
    lax.fori_loop(0, n_far // 2, far_body, 0)

    def when(cond):
        return pl.when(cond) if nkt_static is None else (lambda f: f() if cond else None)

    @when(n_far % 2 == 1)
    def _():
        attend(n_far - 1, 1, False)

    @when(nkt >= 2)
    def _():
        attend(nkt - 2, 2, True)

    @when(nkt < 2)
    def _():
        attend(0, 1, True)

    outs = []
    for g in range(ATT_KV_HEADS):
        a = acc_scr[g]
        o_g = a[:ATT_HD] * (1.0 / a[ATT_HD:ATT_HD + 1])
        outs += [o_g[:, r * tq:(r + 1) * tq] for r in range(HEADS_PER_KV)]
    o_ref[...] = jnp.concatenate(outs, axis=0).T[:n_valid_q]


def _attention(aq_src, iq_src, iw, ki3, k3, vt3, bt, batch, n_qblk, tq, nkt_total, nkt_static,
               last_valid, n_valid_q, n_sel, aq_col, iq_col):
    kern = functools.partial(_attn_kernel, tq=tq, nkt_static=nkt_static, last_valid=last_valid,
                             n_valid_q=n_valid_q, n_sel=n_sel, max_search=MAX_SEARCH,
                             n_blind=max(N_BLIND, int(math.log2(nkt_total * TK)) - BLIND_SLACK))
    qspec = lambda c: pl.BlockSpec((n_valid_q, ATT_W), lambda b, q, c=c: (b * n_qblk + q, c))
    kspec = lambda a, c: pl.BlockSpec((nkt_total, a, c), lambda b, q: (b, 0, 0))
    return pl.pallas_call(
        kern,
        grid=(batch, n_qblk),
        in_specs=[qspec(aq_col), qspec(iq_col),
                  pl.BlockSpec((n_valid_q, LANE), lambda b, q: (b * n_qblk + q, 0)),
                  kspec(TK, IDX_DIM), kspec(TK, KV_W), kspec(VT_ALL, TK),
                  _const_spec(bt.shape)],
        out_specs=pl.BlockSpec((n_valid_q, ATT_W), lambda b, q: (b * n_qblk + q, 0)),
        out_shape=jax.ShapeDtypeStruct((batch * n_qblk * n_valid_q, ATT_W), F32),
        scratch_shapes=[pltpu.VMEM((nkt_total, TK, tq), F32),
                        pltpu.VMEM((nkt_total, TK, tq), BF16),
                        pltpu.VMEM((ATT_KV_HEADS, VT_ROWS, HEADS_PER_KV * tq), F32),
                        pltpu.VMEM((ATT_HEADS, tq), F32),
                        pltpu.VMEM((ATT_KV_HEADS, 2 * TK, HEADS_PER_KV * tq), BF16)],
        compiler_params=pltpu.CompilerParams(dimension_semantics=("parallel", "arbitrary"),
                                             vmem_limit_bytes=VMEM_LIMIT),
        name="attention",
    )(aq_src, iq_src, iw, ki3, k3, vt3, bt)


def _ffn_kernel(x_ref, oret_ref, oatt_ref, wout_ref, wup_ref, wdown_ref,
                gpost_ref, gpre_ref, gffn_ref, y_ref, *, ff_chunk):
    mix = jnp.dot(oret_ref[...].astype(BF16), wout_ref[:RET_W, :], preferred_element_type=F32)
    mix = mix + jnp.dot(oatt_ref[...].astype(BF16), wout_ref[RET_W:, :], preferred_element_type=F32)
    h = x_ref[...] + _rms(mix, gpost_ref[...])
    a = _rms(h, gpre_ref[...]).astype(BF16)
    f = jnp.zeros(h.shape, F32)
    for c in range(D_FF // ff_chunk):
        sl = slice(c * ff_chunk, (c + 1) * ff_chunk)
        u = jnp.dot(a, wup_ref[:, sl], preferred_element_type=F32)
        u = jnp.square(jnp.maximum(u, 0.0)).astype(BF16)
        f = f + jnp.dot(u, wdown_ref[sl, :], preferred_element_type=F32)
    y_ref[...] = h + _rms(f, gffn_ref[...])


def _out_ffn(x2, o_ret, o_att, w_out, w_up, w_down, g_post, g_pre, g_ffn, tm):
    rows = x2.shape[0]
    row_spec = lambda w: pl.BlockSpec((tm, w), lambda i: (i, 0))
    return pl.pallas_call(
        functools.partial(_ffn_kernel, ff_chunk=COL_CHUNK),
        grid=(rows // tm,),
        in_specs=[row_spec(D_MODEL), row_spec(RET_W), row_spec(ATT_W),
                  _const_spec(w_out.shape), _const_spec(w_up.shape), _const_spec(w_down.shape),
                  _const_spec((1, D_MODEL)), _const_spec((1, D_MODEL)), _const_spec((1, D_MODEL))],
        out_specs=row_spec(D_MODEL),
        out_shape=jax.ShapeDtypeStruct((rows, D_MODEL), F32),
        compiler_params=pltpu.CompilerParams(dimension_semantics=("parallel",),
                                             vmem_limit_bytes=VMEM_LIMIT),
        name="out_ffn",
    )(x2, o_ret, o_att, w_out, w_up, w_down, g_post, g_pre, g_ffn)


def _t5_bucket(rel):
    half = N_BUCKETS // 2
    max_exact = half // 2
    ret = jnp.where(rel > 0, half, 0)
    n = jnp.abs(rel)
    nf = jnp.maximum(n, 1).astype(F32)
    large = max_exact + (jnp.log(nf / max_exact) / math.log(MAX_DISTANCE / max_exact)
                         * (half - max_exact)).astype(jnp.int32)
    large = jnp.minimum(large, half - 1)
    return ret + jnp.where(n < max_exact, n, large)


def _bias_tiles(rel_bias, tq):
    period = 2 * TK + tq
    rel = jnp.arange(period, dtype=jnp.int32) - (TK + tq - 1)
    table = rel_bias.astype(F32)
    far = table[_t5_bucket(jnp.int32(-MAX_DISTANCE))]
    u = (table[_t5_bucket(jnp.clip(rel, -MAX_DISTANCE, MAX_DISTANCE))] - far).T
    n = jnp.tile(u, (1, tq))[:, :tq * (period - 1)].reshape(ATT_HEADS, tq, period - 1)
    m = n[:, :, tq - 1:tq - 1 + 2 * TK]
    return jnp.transpose(m.reshape(ATT_HEADS, tq, 2, TK), (0, 2, 3, 1))


def _rope_tables(pos):
    half = RET_DK // 2
    inv = ROPE_BASE ** (-jnp.arange(half, dtype=F32) / half)
    ang = pos.astype(F32)[:, None] * inv[None, :]
    cos, sin = jnp.cos(ang), jnp.sin(ang)
    cos_h = jnp.concatenate([cos, cos], axis=1)
    sin_h = jnp.concatenate([-sin, sin], axis=1)
    reps = LANE // RET_DK
    return jnp.tile(cos_h, (1, reps)), jnp.tile(sin_h, (1, reps))


def _permute_w_in(w_in):
    offs = np.cumsum([0, RET_W, RET_W, RET_W, RET_W, ATT_W, KV_W, KV_W, IDX_W, IDX_DIM, IDX_HEADS])
    seg = lambda i: w_in[:, offs[i]:offs[i + 1]]
    pad = jnp.zeros((D_MODEL, LANE - IDX_DIM - IDX_HEADS), w_in.dtype)
    return jnp.concatenate([seg(0), seg(1), seg(2), seg(3), seg(4), seg(7),
                            seg(5), seg(6), seg(8), seg(9), pad], axis=1).astype(BF16)


def _layer(x, s0, past, weights, bias_tiles):
    w_perm, w_out, w_up, w_down, g_pre_mix, g_post_mix, g_pre_ffn, g_post_ffn = weights
    batch, t_len, _ = x.shape
    rows = batch * t_len
    x2 = x.reshape(rows, D_MODEL)
    tm = min(ROW_TILE, rows)
    p_len = 0 if past is None else past[0].shape[1]
    pos = p_len + jnp.arange(t_len, dtype=jnp.int32)
    cos_t, sin_t = _rope_tables(pos)
    main, kc, vc, ic, iw, *key_tiles = _inproj(x2, g_pre_mix, w_perm, cos_t, sin_t, tm,
                                               key_tiles=past is None)

    chunk = min(CHUNK, t_len)
    sb = min(RET_BLOCK, t_len)
    o_ret, s_new = _retention(main, s0, batch, t_len, sb, chunk)

    l_all = p_len + t_len
    n_sel = min(TOPK_MAX, l_all // 4)
    if past is None:
        tq = TK
        n_qblk = t_len // tq
        nkt_total = t_len // TK
        bt = bias_tiles
        k3, ki3, vt3 = key_tiles
        o_att = _attention(main, main, iw, ki3, k3, vt3, bt, batch, n_qblk, tq, nkt_total, None,
                           TK, tq, n_sel, aq_col=4, iq_col=5)
    else:
        tq = LANE
        pk, pv, pi = past
        assert p_len % TK == 0 and t_len <= min(tq, TK), (p_len, t_len)
        nkt_total = p_len // TK + 1
        padk = nkt_total * TK - l_all
        cat = lambda old, new: jnp.concatenate(
            [old.astype(BF16), new.reshape(batch, t_len, -1).astype(BF16),
             jnp.zeros((batch, padk, old.shape[-1]), BF16)], axis=1)
        k_all = cat(pk.reshape(batch, p_len, KV_W), kc)
        v_all = cat(pv.reshape(batch, p_len, KV_W), vc)
        i_all = cat(pi, ic)
        v_t = jnp.transpose(v_all.reshape(batch, nkt_total, TK, ATT_KV_HEADS, ATT_HD), (0, 1, 3, 4, 2))
        ones = jnp.zeros((batch, nkt_total, ATT_KV_HEADS, VT_ROWS - ATT_HD, TK), BF16).at[:, :, :, 0, :].set(1.0)
        vt_s = jnp.concatenate([v_t, ones], axis=3).reshape(batch * nkt_total, VT_ALL, TK)
        k_s = k_all.reshape(batch * nkt_total, TK, KV_W)
        i_s = i_all.reshape(batch * nkt_total, TK, IDX_DIM)
        bt = bias_tiles[..., :tq]
        o_att = _attention(main, main, iw, i_s, k_s, vt_s, bt, batch, 1, tq, nkt_total, nkt_total,
                           l_all - p_len, t_len, n_sel, aq_col=4, iq_col=5)

    y = _out_ffn(x2, o_ret, o_att, w_out, w_up, w_down, g_post_mix, g_pre_ffn, g_post_ffn, tm)
    return (y.reshape(batch, t_len, D_MODEL), s_new,
            kc.reshape(batch, t_len, ATT_KV_HEADS, ATT_HD),
            vc.reshape(batch, t_len, ATT_KV_HEADS, ATT_HD),
            ic.reshape(batch, t_len, IDX_DIM))


def kernel(x_prompt, x_sample, state_ret, cache_k, cache_v, cache_kidx, w_in, w_out, w_up, w_down,
           g_pre_mix, g_post_mix, g_pre_ffn, g_post_ffn, rel_bias):
    depth = w_in.shape[0]
    bp = x_prompt.shape[0]
    zero_state = jnp.zeros((bp, RET_HEADS, RET_DK, RET_DV), F32)
    bias_tiles = _bias_tiles(rel_bias, TK)
    yp, ys = x_prompt, x_sample
    outs_p, outs_s = [], []
    for l in range(depth):
        row = lambda g: g[l].reshape(1, D_MODEL).astype(F32)
        weights = (_permute_w_in(w_in[l]), w_out[l].astype(BF16), w_up[l].astype(BF16),
                   w_down[l].astype(BF16), row(g_pre_mix), row(g_post_mix), row(g_pre_ffn),
                   row(g_post_ffn))
        yp, *rest_p = _layer(yp, zero_state, None, weights, bias_tiles)
        ys, *rest_s = _layer(ys, state_ret[l], (cache_k[l], cache_v[l], cache_kidx[l]), weights,
                             bias_tiles)
        outs_p.append(rest_p)
        outs_s.append(rest_s)
    stack = lambda outs, i: jnp.stack([o[i] for o in outs])
    return (yp, ys,
            stack(outs_p, 0), stack(outs_p, 1), stack(outs_p, 2), stack(outs_p, 3),
            stack(outs_s, 0), stack(outs_s, 1), stack(outs_s, 2), stack(outs_s, 3))
```
